```python
import math
import jax
import jax.numpy as jnp
from jax import lax
import numpy as np

D_MODEL = 1024
BATCH = 16
SEQ = 256
DEPTH = 2
DEC_BATCH = 4
DEC_SEQ = 2048
PAST_LEN = 256

GRID_W = 64
ROPE_THETA = 10000.0
Q_BLOCK = 128
N_BRANCH = 4
BRANCH_W = D_MODEL // 4
A_HEADS = 4
A_KV_HEADS = 2
A_HEAD_DIM = 64
B_HEADS = 4
B_HALF_DIM = 32
B_V_DIM = 2 * B_HALF_DIM
C_CONV_WIDTH = 31
D_GROUPS = 4
D_GROUP_W = BRANCH_W // D_GROUPS
POOL_WINDOWS = (2, 4, 8, 16)
N_EXPERTS = 32
TOP_K = 4
D_FF = D_MODEL
SWIGLU_LIMIT = 7.0
SWIGLU_ALPHA = 1.702
EPS = 1e-6

IN_SIZES = (A_HEADS * A_HEAD_DIM, A_KV_HEADS * A_HEAD_DIM, A_KV_HEADS * A_HEAD_DIM,
            B_HEADS * 2 * B_HALF_DIM, B_HEADS * 2 * B_HALF_DIM, B_HEADS * B_V_DIM,
            2 * BRANCH_W, BRANCH_W, N_BRANCH * D_MODEL)
IN_COLS = sum(IN_SIZES)
SPLIT_POINTS = tuple(int(s) for s in np.cumsum(IN_SIZES)[:-1])

kernel_name = "hybrid_flow_prefix_step"

F32 = jnp.float32


def rms_norm(x, g):
    xf = x.astype(F32)
    y = xf * lax.rsqrt(jnp.mean(xf * xf, axis=-1, keepdims=True) + EPS)
    return (y * g.astype(F32)).astype(x.dtype)


def axial_rope(T, dim):
    rows = T // GRID_W
    r, c = jnp.meshgrid(jnp.arange(rows), jnp.arange(GRID_W), indexing="ij")
    r = r.reshape(-1).astype(F32)
    c = c.reshape(-1).astype(F32)
    n_axis = dim // 4
    inv = ROPE_THETA ** (-jnp.arange(n_axis, dtype=F32) / n_axis)
    ang = jnp.concatenate([r[:, None] * inv, c[:, None] * inv], axis=-1)
    return jnp.cos(ang), jnp.sin(ang)


def apply_rope(x, cos, sin):
    half = x.shape[-1] // 2
    shape = (1, x.shape[1]) + (1,) * (x.ndim - 3) + (half,)
    cos = cos.reshape(shape)
    sin = sin.reshape(shape)
    xf = x.astype(F32)
    x1, x2 = xf[..., :half], xf[..., half:]
    return jnp.concatenate([x1 * cos - x2 * sin, x2 * cos + x1 * sin], axis=-1).astype(x.dtype)


def sweep_query_blocks(q, fn):
    B, T = q.shape[:2]
    nb = T // Q_BLOCK
    qb = jnp.moveaxis(q.reshape((B, nb, Q_BLOCK) + q.shape[2:]), 1, 0)
    ob = jnp.moveaxis(lax.map(fn, qb), 0, 1)
    return ob.reshape((B, T) + ob.shape[3:])


def gqa_attention(q, k, v):
    B, T = q.shape[:2]
    G = A_HEADS // A_KV_HEADS
    qg = q.reshape(B, T, A_KV_HEADS, G, A_HEAD_DIM)
    kf = k.astype(F32)
    vf = v.astype(F32)
    scale = A_HEAD_DIM ** -0.5

    def block(qb):
        s = jnp.einsum("bqkgd,bskd->bkgqs", qb.astype(F32), kf) * scale
        p = jax.nn.softmax(s, axis=-1)
        return jnp.einsum("bkgqs,bskd->bqkgd", p, vf)

    o = sweep_query_blocks(qg, block)
    return o.reshape(B, T, A_HEADS * A_HEAD_DIM).astype(q.dtype)


def diff_attention(q, k, v, lam, subln_g, lambda_init):
    B, T = q.shape[:2]
    kf = k.astype(F32)
    vf = v.astype(F32)
    scale = B_HALF_DIM ** -0.5

    def block(qb):
        s = jnp.einsum("bqhcd,bshcd->bchqs", qb.astype(F32), kf) * scale
        p = jax.nn.softmax(s, axis=-1)
        a = p[:, 0] - lam * p[:, 1]
        return jnp.einsum("bhqs,bshd->bqhd", a, vf)

    o = sweep_query_blocks(q, block)
    o = rms_norm(o, subln_g) * (1.0 - lambda_init)
    return o.reshape(B, T, B_HEADS * B_V_DIM).astype(q.dtype)


def conformer_conv(u, dw_w, dw_b, ln_g, ln_b):
    a, b = jnp.split(u, 2, axis=-1)
    h = a * jax.nn.sigmoid(b)
    h = lax.conv_general_dilated(h, dw_w[:, None, :].astype(h.dtype), window_strides=(1,),
                                 padding=[(C_CONV_WIDTH // 2, C_CONV_WIDTH // 2)],
                                 dimension_numbers=("NWC", "WIO", "NWC"),
                                 feature_group_count=BRANCH_W) + dw_b
    hf = h.astype(F32)
    mu = jnp.mean(hf, axis=-1, keepdims=True)
    var = jnp.mean(jnp.square(hf - mu), axis=-1, keepdims=True)
    hf = (hf - mu) * lax.rsqrt(var + EPS) * ln_g.astype(F32) + ln_b.astype(F32)
    return jax.nn.silu(hf).astype(u.dtype)


def multiscale_pool(u, w_group, scale):
    B, T, _ = u.shape
    uf = u.astype(F32).reshape(B, T, D_GROUPS, D_GROUP_W)
    cs = jnp.concatenate([jnp.zeros((B, 1, D_GROUPS, D_GROUP_W), F32), jnp.cumsum(uf, axis=1)], axis=1)
    t = jnp.arange(T)
    means = []
    for g, w in enumerate(POOL_WINDOWS):
        lo = jnp.clip(t - w // 2, 0, T - 1)
        hi = jnp.clip(t + (w - 1 - w // 2), 0, T - 1)
        csg = cs[:, :, g, :]
        means.append((csg[:, hi + 1] - csg[:, lo]) / (hi - lo + 1).astype(F32)[None, :, None])
    d = jnp.stack(means, axis=2) - uf
    y = jnp.einsum("btgc,gce->btge", d, w_group.astype(F32)).reshape(B, T, BRANCH_W)
    return (y * scale.astype(F32)).astype(u.dtype)


def token_mixer(h, lp, lambda_init, ctx):
    B, T, _ = h.shape
    z = h @ lp["w_in"]
    qa, ka, va, qb, kb, vb, zc, zd, zg = jnp.split(z, SPLIT_POINTS, axis=-1)
    qa = rms_norm(qa.reshape(B, T, A_HEADS, A_HEAD_DIM), lp["a_q_gain"])
    ka = rms_norm(ka.reshape(B, T, A_KV_HEADS, A_HEAD_DIM), lp["a_k_gain"])
    va = va.reshape(B, T, A_KV_HEADS, A_HEAD_DIM)
    qb = qb.reshape(B, T, B_HEADS, 2, B_HALF_DIM)
    kb = kb.reshape(B, T, B_HEADS, 2, B_HALF_DIM)
    vb = vb.reshape(B, T, B_HEADS, B_V_DIM)
    if ctx is None:
        new_ctx = (ka, va, kb, vb)
        ka_all, va_all, kb_all, vb_all = ka, va, kb, vb
    else:
        cos_a, sin_a = axial_rope(T, A_HEAD_DIM)
        cos_b, sin_b = axial_rope(T, B_HALF_DIM)
        qa = apply_rope(qa, cos_a, sin_a)
        ka = apply_rope(ka, cos_a, sin_a)
        qb = apply_rope(qb, cos_b, sin_b)
        kb = apply_rope(kb, cos_b, sin_b)
        ctx_ka, ctx_va, ctx_kb, ctx_vb = ctx
        ka_all = jnp.concatenate([ka, ctx_ka.astype(ka.dtype)], axis=1)
        va_all = jnp.concatenate([va, ctx_va.astype(va.dtype)], axis=1)
        kb_all = jnp.concatenate([kb, ctx_kb.astype(kb.dtype)], axis=1)
        vb_all = jnp.concatenate([vb, ctx_vb.astype(vb.dtype)], axis=1)
        new_ctx = None
    bl = lp["b_lambda"].astype(F32)
    lam = jnp.exp(jnp.sum(bl[0] * bl[1])) - jnp.exp(jnp.sum(bl[2] * bl[3])) + lambda_init
    o_a = gqa_attention(qa, ka_all, va_all)
    o_b = diff_attention(qb, kb_all, vb_all, lam, lp["b_subln_gain"], lambda_init)
    o_c = conformer_conv(zc, lp["c_dw_w"], lp["c_dw_b"], lp["c_ln_g"], lp["c_ln_b"])
    o_d = multiscale_pool(zd, lp["d_w_group"], lp["d_scale"])
    branches = jnp.einsum("btnc,ncd->btnd", jnp.stack([o_a, o_b, o_c, o_d], axis=2), lp["w_branch"])
    gates = jax.nn.sigmoid(zg.reshape(B, T, N_BRANCH, D_MODEL))
    merged = jnp.sum(gates * branches, axis=2)
    return merged @ lp["w_out"], new_ctx


def moe(h, r_w, r_b, w1, b1, w2, b2):
    B, T, D = h.shape
    x = h.reshape(B * T, D)
    logits = (x @ r_w + r_b).astype(F32)
    top_v, top_i = lax.top_k(logits, TOP_K)
    wts = jax.nn.softmax(top_v, axis=-1)
    gate = jnp.sum(jax.nn.one_hot(top_i, N_EXPERTS, dtype=F32) * wts[..., None], axis=1)
    out = jnp.zeros((B * T, D), F32)
    for e in range(N_EXPERTS):
        u = x @ w1[e] + b1[e]
        xg = jnp.minimum(u[:, :D_FF], SWIGLU_LIMIT)
        xl = jnp.clip(u[:, D_FF:], -SWIGLU_LIMIT, SWIGLU_LIMIT)
        act = (xl + 1.0) * (xg * jax.nn.sigmoid(SWIGLU_ALPHA * xg))
        out = out + gate[:, e:e + 1] * (act @ w2[e] + b2[e]).astype(F32)
    return out.reshape(B, T, D).astype(h.dtype)


def trunk_layer(x, cond, lp, lambda_init, ctx):
    m = jax.nn.silu(cond) @ lp["w_mod"] + lp["b_mod"]
    m = m.reshape(-1, 1, 6 * D_MODEL)
    sh1, sc1, g1, sh2, sc2, g2 = jnp.split(m, 6, axis=-1)
    h = rms_norm(x, lp["g_pre1"]) * (1.0 + sc1) + sh1
    y, new_ctx = token_mixer(h, lp, lambda_init, ctx)
    x = x + g1 * rms_norm(y, lp["g_post1"])
    h = rms_norm(x, lp["g_pre2"]) * (1.0 + sc2) + sh2
    y = moe(h, lp["r_w"], lp["r_b"], lp["e_w1"], lp["e_b1"], lp["e_w2"], lp["e_b2"])
    x = x + g2 * rms_norm(y, lp["g_post2"])
    return x, new_ctx


def setup_inputs(seed: int = 0) -> dict:
    key = jax.random.key(seed)
    ks = jax.random.split(key, 40)
    L, D = DEPTH, D_MODEL

    def nrm(k, shape, scale):
        return jax.random.normal(k, shape, F32) * scale

    return {
        "x_prompt": nrm(ks[0], (BATCH, SEQ, D), 1.0),
        "x_sample": nrm(ks[1], (DEC_BATCH, DEC_SEQ, D), 1.0),
        "cache_a_k": nrm(ks[2], (DEC_BATCH, L, PAST_LEN, A_KV_HEADS, A_HEAD_DIM), 1.0),
        "cache_a_v": nrm(ks[3], (DEC_BATCH, L, PAST_LEN, A_KV_HEADS, A_HEAD_DIM), 1.0),
        "cache_b_k": nrm(ks[4], (DEC_BATCH, L, PAST_LEN, B_HEADS, 2, B_HALF_DIM), 1.0),
        "cache_b_v": nrm(ks[5], (DEC_BATCH, L, PAST_LEN, B_HEADS, B_V_DIM), 1.0),
        "c": nrm(ks[6], (DEC_BATCH, D), 1.0),
        "c_ctx": nrm(ks[7], (D,), 1.0),
        "w_mod": nrm(ks[8], (L, D, 6 * D), 0.5 * D ** -0.5),
        "b_mod": nrm(ks[9], (L, 6 * D), 0.02),
        "g_pre1": 1.0 + nrm(ks[10], (L, D), 0.05),
        "g_post1": 1.0 + nrm(ks[11], (L, D), 0.05),
        "g_pre2": 1.0 + nrm(ks[12], (L, D), 0.05),
        "g_post2": 1.0 + nrm(ks[13], (L, D), 0.05),
        "w_in": nrm(ks[14], (L, D, IN_COLS), D ** -0.5),
        "a_q_gain": 1.0 + nrm(ks[15], (L, A_HEAD_DIM), 0.05),
        "a_k_gain": 1.0 + nrm(ks[16], (L, A_HEAD_DIM), 0.05),
        "b_lambda": nrm(ks[17], (L, 4, B_HALF_DIM), 0.1),
        "b_subln_gain": 1.0 + nrm(ks[18], (L, B_V_DIM), 0.05),
        "c_dw_w": nrm(ks[19], (L, C_CONV_WIDTH, BRANCH_W), C_CONV_WIDTH ** -0.5),
        "c_dw_b": nrm(ks[20], (L, BRANCH_W), 0.02),
        "c_ln_g": 1.0 + nrm(ks[21], (L, BRANCH_W), 0.05),
        "c_ln_b": nrm(ks[22], (L, BRANCH_W), 0.02),
        "d_w_group": nrm(ks[23], (L, D_GROUPS, D_GROUP_W, D_GROUP_W), D_GROUP_W ** -0.5),
        "d_scale": 1.0 + nrm(ks[24], (L, BRANCH_W), 0.1),
        "w_branch": nrm(ks[25], (L, N_BRANCH, BRANCH_W, D), BRANCH_W ** -0.5),
        "w_out": nrm(ks[26], (L, D, D), D ** -0.5),
        "r_w": nrm(ks[27], (L, D, N_EXPERTS), D ** -0.5),
        "r_b": nrm(ks[28], (L, N_EXPERTS), 0.01),
        "e_w1": nrm(ks[29], (L, N_EXPERTS, D, 2 * D_FF), D ** -0.5),
        "e_b1": nrm(ks[30], (L, N_EXPERTS, 2 * D_FF), 0.02),
        "e_w2": nrm(ks[31], (L, N_EXPERTS, D_FF, D), D_FF ** -0.5),
        "e_b2": nrm(ks[32], (L, N_EXPERTS, D), 0.02),
    }


def reference(x_prompt, x_sample, cache_a_k, cache_a_v, cache_b_k, cache_b_v, c, c_ctx,
              w_mod, b_mod, g_pre1, g_post1, g_pre2, g_post2, w_in, a_q_gain, a_k_gain,
              b_lambda, b_subln_gain, c_dw_w, c_dw_b, c_ln_g, c_ln_b, d_w_group, d_scale,
              w_branch, w_out, r_w, r_b, e_w1, e_b1, e_w2, e_b2):
    xp = x_prompt
    xs = x_sample
    new_ak, new_av, new_bk, new_bv = [], [], [], []
    for l in range(DEPTH):
        lp = {
            "w_mod": w_mod[l], "b_mod": b_mod[l],
            "g_pre1": g_pre1[l], "g_post1": g_post1[l], "g_pre2": g_pre2[l], "g_post2": g_post2[l],
            "w_in": w_in[l], "a_q_gain": a_q_gain[l], "a_k_gain": a_k_gain[l],
            "b_lambda": b_lambda[l], "b_subln_gain": b_subln_gain[l],
            "c_dw_w": c_dw_w[l], "c_dw_b": c_dw_b[l], "c_ln_g": c_ln_g[l], "c_ln_b": c_ln_b[l],
            "d_w_group": d_w_group[l], "d_scale": d_scale[l],
            "w_branch": w_branch[l], "w_out": w_out[l],
            "r_w": r_w[l], "r_b": r_b[l],
            "e_w1": e_w1[l], "e_b1": e_b1[l], "e_w2": e_w2[l], "e_b2": e_b2[l],
        }
        lambda_init = 0.8 - 0.6 * math.exp(-0.3 * l)
        xp, (ka, va, kb, vb) = trunk_layer(xp, c_ctx, lp, lambda_init, None)
        new_ak.append(ka)
        new_av.append(va)
        new_bk.append(kb)
        new_bv.append(vb)
        ctx = (cache_a_k[:, l], cache_a_v[:, l], cache_b_k[:, l], cache_b_v[:, l])
        xs, _ = trunk_layer(xs, c, lp, lambda_init, ctx)
    new_a_k = jnp.stack(new_ak, axis=1)
    new_a_v = jnp.stack(new_av, axis=1)
    new_b_k = jnp.stack(new_bk, axis=1)
    new_b_v = jnp.stack(new_bv, axis=1)
    return (xp, xs, new_a_k, new_a_v, new_b_k, new_b_v)
```

```python
import functools
import math

import jax
import jax.numpy as jnp
import numpy as np
from jax import lax
from jax.experimental import pallas as pl
from jax.experimental.pallas import tpu as pltpu

F32 = jnp.float32
BF16 = jnp.bfloat16

D_MODEL = 1024
BATCH = 16
SEQ = 256
DEPTH = 2
DEC_BATCH = 4
DEC_SEQ = 2048
PAST_LEN = 256
GRID_W = 64
ROPE_THETA = 10000.0
N_BRANCH = 4
BRANCH_W = D_MODEL // 4
A_HEADS = 4
A_KV_HEADS = 2
A_HEAD_DIM = 64
B_HEADS = 4
B_HALF_DIM = 32
B_V_DIM = 2 * B_HALF_DIM
C_CONV_WIDTH = 31
D_GROUPS = 4
D_GROUP_W = BRANCH_W // D_GROUPS
POOL_WINDOWS = (2, 4, 8, 16)
N_EXPERTS = 32
TOP_K = 4
D_FF = D_MODEL
SWIGLU_LIMIT = 7.0
SWIGLU_ALPHA = 1.702
EPS = 1e-6

N_CTX = BATCH * SEQ
N_LAT = DEC_BATCH * DEC_SEQ
N_TOK = N_CTX + N_LAT

COL_QA, COL_KA, COL_VA = 0, 256, 384
COL_QB, COL_KB, COL_VB = 512, 768, 1024
COL_ZC, COL_ZD = 1280, 1792
N_SMALL = 2048
N_GATE = N_BRANCH * D_MODEL

TM = 256
N_TILES = N_TOK // TM
CTX_TILES = N_CTX // TM
LAT_TILES_PER_SEQ = DEC_SEQ // TM
MOD_ROWS = 8

TE = 256
N_SLOTS = N_TOK * TOP_K
P_MAX = N_SLOTS + N_EXPERTS * TE
E_TILES = P_MAX // TE
OUT_LANES = 128

TQ = 256
CONV_PAD = 16
CONV_CHUNK = 128

VMEM_LIMIT = 56 * 1024 * 1024


def _sigmoid(x):
    return 1.0 / (1.0 + jnp.exp(-x))


def _split_bf16(a):
    hi = a.astype(BF16)
    lo = (a - hi.astype(F32)).astype(BF16)
    return hi, lo


def _dot(a, b):
    return jnp.dot(a, b, preferred_element_type=F32)


def _dot_nt(a, b):
    return lax.dot_general(a, b, (((1,), (1,)), ((), ())), preferred_element_type=F32)


def _dot3(a, b):
    ah, al = _split_bf16(a)
    bh, bl = _split_bf16(b)
    return _dot(ah, bh) + _dot(ah, bl) + _dot(al, bh)


def _rms(x, g):
    return x * lax.rsqrt(jnp.mean(x * x, axis=-1, keepdims=True) + EPS) * g


def _mod_row(i):
    return jnp.where(i < CTX_TILES, 0, 1 + (i - CTX_TILES) // LAT_TILES_PER_SEQ)


MOD_TN = 1536


def _mod_kernel(cond_ref, w_ref, b_ref, o_ref):
    c = cond_ref[...]
    s = c * _sigmoid(c)
    o_ref[0] = _dot3(s, w_ref[0]) + b_ref[0]


def _modulation(cond8, w_mod, b_mod):
    L = w_mod.shape[0]
    return pl.pallas_call(
        _mod_kernel,
        name="modulation",
        grid=(L, 6 * D_MODEL // MOD_TN),
        in_specs=[
            pl.BlockSpec((MOD_ROWS, D_MODEL), lambda l, j: (0, 0)),
            pl.BlockSpec((1, D_MODEL, MOD_TN), lambda l, j: (l, 0, j)),
            pl.BlockSpec((1, 1, MOD_TN), lambda l, j: (l, 0, j)),
        ],
        out_specs=pl.BlockSpec((1, MOD_ROWS, MOD_TN), lambda l, j: (l, 0, j)),
        out_shape=jax.ShapeDtypeStruct((L, MOD_ROWS, 6 * D_MODEL), F32),
        compiler_params=pltpu.CompilerParams(
            dimension_semantics=("arbitrary", "arbitrary"), vmem_limit_bytes=VMEM_LIMIT),
    )(cond8, w_mod, b_mod.reshape(L, 1, 6 * D_MODEL))


def _group_mean(sq, gmat):
    hi, lo = _split_bf16(sq)
    return _dot(hi, gmat) + _dot(lo, gmat)


def _swap_halves(x, half):
    w = x.shape[-1]
    lane = lax.broadcasted_iota(jnp.int32, x.shape, 1)
    first = (lane % (2 * half)) < half
    return jnp.where(first, pltpu.roll(x, w - half, 1), pltpu.roll(x, half, 1))


def _inproj_kernel(x_ref, mod_ref, gpre_ref, w_ref, gq_ref, gk_ref, gmat_ref,
                   cos_a_ref, sin_a_ref, cos_b_ref, sin_b_ref, z_ref, h_ref):
    i = pl.program_id(0)
    m = mod_ref[0]
    sh1 = m[:, 0:D_MODEL]
    sc1 = m[:, D_MODEL:2 * D_MODEL]
    h = _rms(x_ref[...], gpre_ref[...]) * (1.0 + sc1) + sh1
    hb = h.astype(BF16)
    h_ref[...] = hb
    z = _dot(hb, w_ref[...])

    gmat = gmat_ref[...]
    qa = z[:, COL_QA:COL_KA]
    ka = z[:, COL_KA:COL_VA]
    qa = qa * lax.rsqrt(_group_mean(qa * qa, gmat) + EPS) * gq_ref[...]
    ka = ka * lax.rsqrt(_group_mean(ka * ka, gmat[0:128, 0:128]) + EPS) * gk_ref[...]
    qb = z[:, COL_QB:COL_KB]
    kb = z[:, COL_KB:COL_VB]
    z_ref[:, COL_VA:COL_QB] = z[:, COL_VA:COL_QB]
    z_ref[:, COL_VB:N_SMALL] = z[:, COL_VB:N_SMALL]

    @pl.when(i < CTX_TILES)
    def _():
        z_ref[:, COL_QA:COL_KA] = qa
        z_ref[:, COL_KA:COL_VA] = ka
        z_ref[:, COL_QB:COL_KB] = qb
        z_ref[:, COL_KB:COL_VB] = kb

    @pl.when(i >= CTX_TILES)
    def _():
        cos_a = cos_a_ref[...]
        sin_a = sin_a_ref[...]
        cos_b = cos_b_ref[...]
        sin_b = sin_b_ref[...]
        ha, hb_ = A_HEAD_DIM // 2, B_HALF_DIM // 2
        z_ref[:, COL_QA:COL_KA] = qa * cos_a + _swap_halves(qa, ha) * sin_a
        z_ref[:, COL_KA:COL_VA] = ka * cos_a[:, 0:128] + _swap_halves(ka, ha) * sin_a[:, 0:128]
        z_ref[:, COL_QB:COL_KB] = qb * cos_b + _swap_halves(qb, hb_) * sin_b
        z_ref[:, COL_KB:COL_VB] = kb * cos_b + _swap_halves(kb, hb_) * sin_b


def _rope_block(i):
    return jnp.where(i < CTX_TILES, 0, (i - CTX_TILES) % LAT_TILES_PER_SEQ)


def _in_projection(x, mods, l, g_pre, w_small, gq, gk, gmat, rope):
    const = lambda i: (0, 0)
    rope_spec = pl.BlockSpec((TM, 256), lambda i: (_rope_block(i), 0))
    return pl.pallas_call(
        _inproj_kernel,
        name="in_projection",
        grid=(N_TILES,),
        in_specs=[
            pl.BlockSpec((TM, D_MODEL), lambda i: (i, 0)),
            pl.BlockSpec((1, 1, 6 * D_MODEL), lambda i: (l * MOD_ROWS + _mod_row(i), 0, 0)),
            pl.BlockSpec((1, D_MODEL), const),
            pl.BlockSpec((D_MODEL, N_SMALL), const),
            pl.BlockSpec((1, 256), const),
            pl.BlockSpec((1, 128), const),
            pl.BlockSpec((256, 256), const),
            rope_spec, rope_spec, rope_spec, rope_spec,
        ],
        out_specs=[
            pl.BlockSpec((TM, N_SMALL), lambda i: (i, 0)),
            pl.BlockSpec((TM, D_MODEL), lambda i: (i, 0)),
        ],
        out_shape=[
            jax.ShapeDtypeStruct((N_TOK, N_SMALL), F32),
            jax.ShapeDtypeStruct((N_TOK, D_MODEL), BF16),
        ],
        compiler_params=pltpu.CompilerParams(
            dimension_semantics=("arbitrary",), vmem_limit_bytes=VMEM_LIMIT),
    )(x, mods, g_pre, w_small, gq, gk, gmat, *rope)


def _softmax_parts(q, ks):
    ss = [_dot_nt(q, k) for k in ks]
    m = ss[0].max(axis=-1, keepdims=True)
    for s in ss[1:]:
        m = jnp.maximum(m, s.max(axis=-1, keepdims=True))
    ps = [jnp.exp(s - m) for s in ss]
    den = ps[0].sum(axis=-1, keepdims=True)
    for p in ps[1:]:
        den = den + p.sum(axis=-1, keepdims=True)
    return ps, den


def _attn_kernel(*refs, has_cache, lambda_init):
    if has_cache:
        (qa_ref, ka_ref, va_ref, qb_ref, kb_ref, vb_ref,
         cka_ref, cva_ref, ckb_ref, cvb_ref, lam_ref, subg_ref, oa_ref, ob_ref) = refs
    else:
        (qa_ref, ka_ref, va_ref, qb_ref, kb_ref, vb_ref,
         lam_ref, subg_ref, oa_ref, ob_ref) = refs

    def cols(ref, lo, hi):
        return ref[:, lo:hi].astype(BF16)

    def ccols(ref, lo, hi):
        return ref[0, 0, :, lo:hi].astype(BF16)

    scale_a = A_HEAD_DIM ** -0.5
    group = A_HEADS // A_KV_HEADS
    for hd in range(A_HEADS):
        g = hd // group
        lo, hi = g * A_HEAD_DIM, (g + 1) * A_HEAD_DIM
        q = (qa_ref[:, hd * A_HEAD_DIM:(hd + 1) * A_HEAD_DIM] * scale_a).astype(BF16)
        ks = [cols(ka_ref, lo, hi)]
        vs = [cols(va_ref, lo, hi)]
        if has_cache:
            ks.append(ccols(cka_ref, lo, hi))
            vs.append(ccols(cva_ref, lo, hi))
        ps, den = _softmax_parts(q, ks)
        o = _dot(ps[0].astype(BF16), vs[0])
        for p, v in zip(ps[1:], vs[1:]):
            o = o + _dot(p.astype(BF16), v)
        oa_ref[:, hd * A_HEAD_DIM:(hd + 1) * A_HEAD_DIM] = o / den

    bl = lam_ref[...]
    lam = (jnp.exp(jnp.sum(bl[0:1] * bl[1:2], axis=-1, keepdims=True))
           - jnp.exp(jnp.sum(bl[2:3] * bl[3:4], axis=-1, keepdims=True)) + lambda_init)
    scale_b = B_HALF_DIM ** -0.5
    subg = subg_ref[...]
    for hd in range(B_HEADS):
        base = hd * 2 * B_HALF_DIM
        vlo, vhi = hd * B_V_DIM, (hd + 1) * B_V_DIM
        vs = [cols(vb_ref, vlo, vhi)]
        if has_cache:
            vs.append(ccols(cvb_ref, vlo, vhi))
        parts = []
        for c in range(2):
            lo, hi = base + c * B_HALF_DIM, base + (c + 1) * B_HALF_DIM
            q = (qb_ref[:, lo:hi] * scale_b).astype(BF16)
            ks = [cols(kb_ref, lo, hi)]
            if has_cache:
                ks.append(ccols(ckb_ref, lo, hi))
            parts.append(_softmax_parts(q, ks))
        (p1, d1), (p2, d2) = parts
        r1 = 1.0 / d1
        r2 = lam / d2
        o = None
        for j, v in enumerate(vs):
            a = (p1[j] * r1 - p2[j] * r2).astype(BF16)
            t = _dot(a, v)
            o = t if o is None else o + t
        o = _rms(o, subg) * (1.0 - lambda_init)
        ob_ref[:, vlo:vhi] = o


def _attention_ctx(z, b_lambda, subg, lambda_init):
    def zspec(width, col):
        return pl.BlockSpec((SEQ, width), lambda b: (b, col // width))
    out_spec = pl.BlockSpec((SEQ, BRANCH_W), lambda b: (b, 0))
    return pl.pallas_call(
        functools.partial(_attn_kernel, has_cache=False, lambda_init=lambda_init),
        name="attention_ctx",
        grid=(BATCH,),
        in_specs=[
            zspec(256, COL_QA), zspec(128, COL_KA), zspec(128, COL_VA),
            zspec(256, COL_QB), zspec(256, COL_KB), zspec(256, COL_VB),
            pl.BlockSpec((4, B_HALF_DIM), lambda b: (0, 0)),
            pl.BlockSpec((1, B_V_DIM), lambda b: (0, 0)),
        ],
        out_specs=[out_spec, out_spec],
        out_shape=[jax.ShapeDtypeStruct((N_CTX, BRANCH_W), F32)] * 2,
        compiler_params=pltpu.CompilerParams(
            dimension_semantics=("arbitrary",), vmem_limit_bytes=VMEM_LIMIT),
    )(z, z, z, z, z, z, b_lambda, subg)


def _attention_lat(z, caches, l, b_lambda, subg, lambda_init):
    q_tiles = DEC_SEQ // TQ
    seq0 = N_CTX // DEC_SEQ

    def qspec(col):
        return pl.BlockSpec((TQ, 256), lambda b, t: (CTX_TILES + b * q_tiles + t, col // 256))

    def kspec(width, col):
        return pl.BlockSpec((DEC_SEQ, width), lambda b, t: (seq0 + b, col // width))

    def cspec(width):
        return pl.BlockSpec((1, 1, PAST_LEN, width), lambda b, t: (b, l, 0, 0))

    out_spec = pl.BlockSpec((TQ, BRANCH_W), lambda b, t: (b * q_tiles + t, 0))
    cka, cva, ckb, cvb = caches
    return pl.pallas_call(
        functools.partial(_attn_kernel, has_cache=True, lambda_init=lambda_init),
        name="attention_lat",
        grid=(DEC_BATCH, q_tiles),
        in_specs=[
            qspec(COL_QA), kspec(128, COL_KA), kspec(128, COL_VA),
            qspec(COL_QB), kspec(256, COL_KB), kspec(256, COL_VB),
            cspec(128), cspec(128), cspec(256), cspec(256),
            pl.BlockSpec((4, B_HALF_DIM), lambda b, t: (0, 0)),
            pl.BlockSpec((1, B_V_DIM), lambda b, t: (0, 0)),
        ],
        out_specs=[out_spec, out_spec],
        out_shape=[jax.ShapeDtypeStruct((N_LAT, BRANCH_W), F32)] * 2,
        compiler_params=pltpu.CompilerParams(
            dimension_semantics=("arbitrary", "arbitrary"), vmem_limit_bytes=VMEM_LIMIT),
    )(z, z, z, z, z, z, cka, cva, ckb, cvb, b_lambda, subg)


SUBLANES = 8


def _row_shifter(win):
    span = win.shape[0] - SUBLANES
    shifted = {}

    def rows(off):
        s = off % SUBLANES
        if s not in shifted:
            shifted[s] = win[s:s + span, :]
        base = off - s
        return shifted[s][base:base + CONV_CHUNK, :]

    return rows


def _convpool_kernel(a_ref, b_ref, d_ref, dww_ref, dwb_ref, lng_ref, lnb_ref, wbd_ref, dsc_ref,
                     oc_ref, od_ref, hpad, upad, *, seq_len):
    n_chunks = seq_len // CONV_CHUNK
    zeros = jnp.zeros((CONV_PAD, BRANCH_W), F32)
    hpad[0:CONV_PAD, :] = zeros
    hpad[seq_len + CONV_PAD:seq_len + 2 * CONV_PAD, :] = zeros
    upad[0:CONV_PAD, :] = zeros
    upad[seq_len + CONV_PAD:seq_len + 2 * CONV_PAD, :] = zeros

    def fill(c, carry):
        r = pl.multiple_of(c * CONV_CHUNK, CONV_CHUNK)
        a = a_ref[pl.ds(r, CONV_CHUNK), :]
        b = b_ref[pl.ds(r, CONV_CHUNK), :]
        hpad[pl.ds(r + CONV_PAD, CONV_CHUNK), :] = a * _sigmoid(b)
        upad[pl.ds(r + CONV_PAD, CONV_CHUNK), :] = d_ref[pl.ds(r, CONV_CHUNK), :]
        return carry

    lax.fori_loop(0, n_chunks, fill, 0)

    lane = lax.broadcasted_iota(jnp.int32, (CONV_CHUNK, BRANCH_W), 1)
    half = C_CONV_WIDTH // 2

    def body(c, carry):
        r = pl.multiple_of(c * CONV_CHUNK, CONV_CHUNK)
        hrows = _row_shifter(hpad[pl.ds(r, CONV_CHUNK + 2 * CONV_PAD), :])
        acc = jnp.zeros((CONV_CHUNK, BRANCH_W), F32)
        for k in range(C_CONV_WIDTH):
            acc = acc + dww_ref[k:k + 1, :] * hrows(CONV_PAD - half + k)
        acc = acc + dwb_ref[...]
        mu = jnp.mean(acc, axis=-1, keepdims=True)
        cen = acc - mu
        var = jnp.mean(cen * cen, axis=-1, keepdims=True)
        y = cen * lax.rsqrt(var + EPS) * lng_ref[...] + lnb_ref[...]
        oc_ref[pl.ds(r, CONV_CHUNK), :] = y * _sigmoid(y)

        urows = _row_shifter(upad[pl.ds(r, CONV_CHUNK + 2 * CONV_PAD), :])

        def ld(d):
            return urows(CONV_PAD + d)

        u = ld(0)
        sums = {}
        s = u + ld(-1)
        sums[2] = s
        s = s + ld(-2) + ld(1)
        sums[4] = s
        s = s + ld(-4) + ld(-3) + ld(2) + ld(3)
        sums[8] = s
        s = s + ld(-8) + ld(-7) + ld(-6) + ld(-5) + ld(4) + ld(5) + ld(6) + ld(7)
        sums[16] = s
        t = r + lax.broadcasted_iota(jnp.int32, (CONV_CHUNK, 1), 0)
        pooled = None
        for g, w in reversed(list(enumerate(POOL_WINDOWS))):
            lo = jnp.maximum(t - w // 2, 0)
            hi = jnp.minimum(t + (w - 1 - w // 2), seq_len - 1)
            mean = sums[w] / (hi - lo + 1).astype(F32)
            pooled = mean if pooled is None else jnp.where(lane < (g + 1) * D_GROUP_W, mean, pooled)
        dlt = (pooled - u).astype(BF16)
        od_ref[pl.ds(r, CONV_CHUNK), :] = _dot(dlt, wbd_ref[...]) * dsc_ref[...]
        return carry

    lax.fori_loop(0, n_chunks, body, 0)


def _conv_pool(z, seq_len, n_seq, row0, dww, dwb, lng, lnb, wbd, dsc):
    blk0 = row0 // seq_len

    def zspec(col):
        return pl.BlockSpec((seq_len, 256), lambda b: (blk0 + b, col // 256))

    def const(shape):
        return pl.BlockSpec(shape, lambda b: (0, 0))

    out_spec = pl.BlockSpec((seq_len, BRANCH_W), lambda b: (b, 0))
    return pl.pallas_call(
        functools.partial(_convpool_kernel, seq_len=seq_len),
        name=f"conv_pool_{seq_len}",
        grid=(n_seq,),
        in_specs=[
            zspec(COL_ZC), zspec(COL_ZC + BRANCH_W), zspec(COL_ZD),
            const((C_CONV_WIDTH, BRANCH_W)), const((1, BRANCH_W)), const((1, BRANCH_W)),
            const((1, BRANCH_W)), const((BRANCH_W, BRANCH_W)), const((1, BRANCH_W)),
        ],
        out_specs=[out_spec, out_spec],
        out_shape=[jax.ShapeDtypeStruct((n_seq * seq_len, BRANCH_W), F32)] * 2,
        scratch_shapes=[pltpu.VMEM((seq_len + 2 * CONV_PAD, BRANCH_W), F32)] * 2,
        compiler_params=pltpu.CompilerParams(
            dimension_semantics=("arbitrary",), vmem_limit_bytes=VMEM_LIMIT),
    )(z, z, z, dww, dwb, lng, lnb, wbd, dsc)


def _mix_kernel(x_ref, h_ref, oa_ref, ob_ref, oc_ref, od_ref, mod_ref, wg_ref, wbr_ref, wout_ref,
                gpost_ref, gpre2_ref, rw_ref, rb_ref,
                x1_ref, h2_ref, eid_ref, rank_ref, wts_ref, cnt_ref, carry):
    i = pl.program_id(0)

    @pl.when(i == 0)
    def _():
        carry[...] = jnp.zeros_like(carry)

    m = mod_ref[0]
    g1 = m[:, 2 * D_MODEL:3 * D_MODEL]
    sh2 = m[:, 3 * D_MODEL:4 * D_MODEL]
    sc2 = m[:, 4 * D_MODEL:5 * D_MODEL]

    hb = h_ref[...]
    merged = None
    for n, o_ref in enumerate((oa_ref, ob_ref, oc_ref, od_ref)):
        gate = _sigmoid(_dot(hb, wg_ref[:, n * D_MODEL:(n + 1) * D_MODEL]))
        br = _dot(o_ref[...].astype(BF16), wbr_ref[n])
        merged = gate * br if merged is None else merged + gate * br
    y = _dot(merged.astype(BF16), wout_ref[...])
    x1 = x_ref[...] + g1 * _rms(y, gpost_ref[...])
    x1_ref[...] = x1
    h2 = _rms(x1, gpre2_ref[...]) * (1.0 + sc2) + sh2
    h2_ref[...] = h2

    logits = _dot3(h2, rw_ref[...]) + rb_ref[...]
    iota_e = lax.broadcasted_iota(jnp.int32, (TM, N_EXPERTS), 1)
    rem = logits
    vals, idxs = [], []
    for _ in range(TOP_K):
        mx = jnp.max(rem, axis=-1, keepdims=True)
        idx = jnp.min(jnp.where(rem == mx, iota_e, N_EXPERTS), axis=-1, keepdims=True)
        vals.append(mx)
        idxs.append(idx)
        rem = jnp.where(iota_e == idx, -jnp.inf, rem)
    exps = [jnp.exp(v - vals[0]) for v in vals]
    den = exps[0]
    for e in exps[1:]:
        den = den + e

    sel = [iota_e == idx for idx in idxs]
    member = jnp.zeros((TM, N_EXPERTS), F32)
    for s in sel:
        member = member + jnp.where(s, 1.0, 0.0)
    row = lax.broadcasted_iota(jnp.int32, (TM, TM), 0)
    col = lax.broadcasted_iota(jnp.int32, (TM, TM), 1)
    before = jnp.where(col < row, 1.0, 0.0).astype(BF16)
    seen = _dot(before, member.astype(BF16)) + carry[...]
    carry[...] = carry[...] + jnp.sum(member, axis=0, keepdims=True)
    cnt_ref[...] = carry[...]

    lane = lax.broadcasted_iota(jnp.int32, (TM, OUT_LANES), 1)
    eid = jnp.zeros((TM, OUT_LANES), jnp.int32)
    rank = jnp.zeros((TM, OUT_LANES), jnp.int32)
    wts = jnp.zeros((TM, OUT_LANES), F32)
    for k in range(TOP_K):
        rk = jnp.sum(jnp.where(sel[k], seen, 0.0), axis=-1, keepdims=True).astype(jnp.int32)
        eid = jnp.where(lane == k, idxs[k], eid)
        rank = jnp.where(lane == k, rk, rank)
        wts = jnp.where(lane == k, exps[k] / den, wts)
    eid_ref[...] = eid
    rank_ref[...] = rank
    wts_ref[...] = wts


def _mix(x, h, o_a, o_b, o_c, o_d, mods, l, w_gate, w_branch, w_out, g_post, g_pre2, r_w, r_b):
    const2 = lambda i: (0, 0)
    tile = lambda width: pl.BlockSpec((TM, width), lambda i: (i, 0))
    single = pl.Buffered(1)
    return pl.pallas_call(
        _mix_kernel,
        name="mix_router",
        grid=(N_TILES,),
        in_specs=[
            tile(D_MODEL), tile(D_MODEL),
            tile(BRANCH_W), tile(BRANCH_W), tile(BRANCH_W), tile(BRANCH_W),
            pl.BlockSpec((1, 1, 6 * D_MODEL), lambda i: (l * MOD_ROWS + _mod_row(i), 0, 0)),
            pl.BlockSpec((D_MODEL, N_GATE), const2, pipeline_mode=single),
            pl.BlockSpec((N_BRANCH, BRANCH_W, D_MODEL), lambda i: (0, 0, 0), pipeline_mode=single),
            pl.BlockSpec((D_MODEL, D_MODEL), const2, pipeline_mode=single),
            pl.BlockSpec((1, D_MODEL), const2),
            pl.BlockSpec((1, D_MODEL), const2),
            pl.BlockSpec((D_MODEL, N_EXPERTS), const2),
            pl.BlockSpec((1, N_EXPERTS), const2),
        ],
        out_specs=[
            tile(D_MODEL), tile(D_MODEL),
            tile(OUT_LANES), tile(OUT_LANES), tile(OUT_LANES),
            pl.BlockSpec((1, N_EXPERTS), const2),
        ],
        out_shape=[
            jax.ShapeDtypeStruct((N_TOK, D_MODEL), F32),
            jax.ShapeDtypeStruct((N_TOK, D_MODEL), F32),
            jax.ShapeDtypeStruct((N_TOK, OUT_LANES), jnp.int32),
            jax.ShapeDtypeStruct((N_TOK, OUT_LANES), jnp.int32),
            jax.ShapeDtypeStruct((N_TOK, OUT_LANES), F32),
            jax.ShapeDtypeStruct((1, N_EXPERTS), F32),
        ],
        scratch_shapes=[pltpu.VMEM((1, N_EXPERTS), F32)],
        compiler_params=pltpu.CompilerParams(
            dimension_semantics=("arbitrary",), vmem_limit_bytes=VMEM_LIMIT),
    )(x, h, o_a, o_b, o_c, o_d, mods, w_gate, w_branch, w_out, g_post, g_pre2, r_w, r_b)


def _row_copy(src_ref, src_row, dst_ref, dst_row, sem):
    return pltpu.make_async_copy(src_ref.at[pl.ds(src_row, 1), :], dst_ref.at[pl.ds(dst_row, 1), :], sem)


def _dispatch_kernel(pos_ref, h2_ref, init_ref, xs_ref, sem):
    del init_ref

    def issue(t, carry):
        for k in range(TOP_K):
            _row_copy(h2_ref, t, xs_ref, pos_ref[t * TOP_K + k], sem).start()
        return carry

    lax.fori_loop(0, TM, issue, 0)

    def drain(t, carry):
        for k in range(TOP_K):
            _row_copy(h2_ref, t, xs_ref, pos_ref[t * TOP_K + k], sem).wait()
        return carry

    lax.fori_loop(0, TM, drain, 0)


def _dispatch(pos, h2, xs_init):
    return pl.pallas_call(
        _dispatch_kernel,
        name="dispatch",
        grid=(N_TILES,),
        in_specs=[
            pl.BlockSpec((TM * TOP_K,), lambda i: (i,), memory_space=pltpu.SMEM),
            pl.BlockSpec((TM, D_MODEL), lambda i: (i, 0)),
            pl.BlockSpec(memory_space=pl.ANY),
        ],
        out_specs=pl.BlockSpec(memory_space=pl.ANY),
        out_shape=jax.ShapeDtypeStruct((P_MAX, D_MODEL), F32),
        scratch_shapes=[pltpu.SemaphoreType.DMA(())],
        input_output_aliases={2: 0},
        compiler_params=pltpu.CompilerParams(
            dimension_semantics=("arbitrary",), vmem_limit_bytes=VMEM_LIMIT),
    )(pos, h2, xs_init)


W_CAST_ROWS = 128


def _moe_kernel(te_ref, valid_ref, x_ref, w1_ref, b1_ref, w2_ref, b2_ref, y_ref, w1b, w2b):
    j = pl.program_id(0)
    e = te_ref[j]
    prev = te_ref[jnp.maximum(j - 1, 0)]
    valid = valid_ref[j] == 1
    new_expert = jnp.logical_or(j == 0, e != prev)

    @pl.when(jnp.logical_and(valid, new_expert))
    def _():
        def cast(c, carry):
            r = pl.multiple_of(c * W_CAST_ROWS, W_CAST_ROWS)
            w1b[pl.ds(r, W_CAST_ROWS), :] = w1_ref[0, 0, pl.ds(r, W_CAST_ROWS), :].astype(BF16)
            w2b[pl.ds(r, W_CAST_ROWS), :] = w2_ref[0, 0, pl.ds(r, W_CAST_ROWS), :].astype(BF16)
            return carry
        lax.fori_loop(0, D_MODEL // W_CAST_ROWS, cast, 0)

    @pl.when(valid)
    def _():
        x = x_ref[...].astype(BF16)
        u = _dot(x, w1b[...]) + b1_ref[0, 0]
        xg = jnp.minimum(u[:, :D_FF], SWIGLU_LIMIT)
        xl = jnp.clip(u[:, D_FF:], -SWIGLU_LIMIT, SWIGLU_LIMIT)
        act = (xl + 1.0) * (xg * _sigmoid(SWIGLU_ALPHA * xg))
        y_ref[...] = _dot(act.astype(BF16), w2b[...]) + b2_ref[0, 0]

    @pl.when(jnp.logical_not(valid))
    def _():
        y_ref[...] = jnp.zeros_like(y_ref)


def _expert_ffn(tile_expert, tile_valid, xs, l, e_w1, e_b1, e_w2, e_b2):
    L = e_w1.shape[0]
    grid_spec = pltpu.PrefetchScalarGridSpec(
        num_scalar_prefetch=2,
        grid=(E_TILES,),
        in_specs=[
            pl.BlockSpec((TE, D_MODEL), lambda j, te, va: (j, 0)),
            pl.BlockSpec((1, 1, D_MODEL, 2 * D_FF), lambda j, te, va: (l, te[j], 0, 0)),
            pl.BlockSpec((1, 1, 1, 2 * D_FF), lambda j, te, va: (l, te[j], 0, 0)),
            pl.BlockSpec((1, 1, D_FF, D_MODEL), lambda j, te, va: (l, te[j], 0, 0)),
            pl.BlockSpec((1, 1, 1, D_MODEL), lambda j, te, va: (l, te[j], 0, 0)),
        ],
        out_specs=pl.BlockSpec((TE, D_MODEL), lambda j, te, va: (j, 0)),
        scratch_shapes=[pltpu.VMEM((D_MODEL, 2 * D_FF), BF16), pltpu.VMEM((D_FF, D_MODEL), BF16)],
    )
    return pl.pallas_call(
        _moe_kernel,
        name="expert_ffn",
        grid_spec=grid_spec,
        out_shape=jax.ShapeDtypeStruct((P_MAX, D_MODEL), F32),
        compiler_params=pltpu.CompilerParams(
            dimension_semantics=("arbitrary",), vmem_limit_bytes=VMEM_LIMIT),
    )(tile_expert, tile_valid, xs, e_w1,
      e_b1.reshape(L, N_EXPERTS, 1, 2 * D_FF), e_w2, e_b2.reshape(L, N_EXPERTS, 1, D_MODEL))


def _combine_kernel(pos_ref, ys_ref, wts_ref, x1_ref, mod_ref, gpost_ref, out_ref, buf, sem):
    def issue(t, carry):
        for k in range(TOP_K):
            _row_copy(ys_ref, pos_ref[t * TOP_K + k], buf.at[k], t, sem).start()
        return carry

    lax.fori_loop(0, TM, issue, 0)

    def drain(t, carry):
        for k in range(TOP_K):
            _row_copy(ys_ref, pos_ref[t * TOP_K + k], buf.at[k], t, sem).wait()
        return carry

    lax.fori_loop(0, TM, drain, 0)

    w = wts_ref[...]
    y = w[:, 0:1] * buf[0]
    for k in range(1, TOP_K):
        y = y + w[:, k:k + 1] * buf[k]
    g2 = mod_ref[0][:, 5 * D_MODEL:6 * D_MODEL]
    out_ref[...] = x1_ref[...] + g2 * _rms(y, gpost_ref[...])


def _combine(pos, ys, wts, x1, mods, l, g_post2):
    return pl.pallas_call(
        _combine_kernel,
        name="combine",
        grid=(N_TILES,),
        in_specs=[
            pl.BlockSpec((TM * TOP_K,), lambda i: (i,), memory_space=pltpu.SMEM),
            pl.BlockSpec(memory_space=pl.ANY),
            pl.BlockSpec((TM, OUT_LANES), lambda i: (i, 0)),
            pl.BlockSpec((TM, D_MODEL), lambda i: (i, 0)),
            pl.BlockSpec((1, 1, 6 * D_MODEL), lambda i: (l * MOD_ROWS + _mod_row(i), 0, 0)),
            pl.BlockSpec((1, D_MODEL), lambda i: (0, 0)),
        ],
        out_specs=pl.BlockSpec((TM, D_MODEL), lambda i: (i, 0)),
        out_shape=jax.ShapeDtypeStruct((N_TOK, D_MODEL), F32),
        scratch_shapes=[pltpu.VMEM((TOP_K, TM, D_MODEL), F32), pltpu.SemaphoreType.DMA(())],
        compiler_params=pltpu.CompilerParams(
            dimension_semantics=("arbitrary",), vmem_limit_bytes=VMEM_LIMIT),
    )(pos, ys, wts, x1, mods, g_post2)


def _rope_tables():
    t = np.arange(DEC_SEQ)
    r = (t // GRID_W).astype(np.float32)
    c = (t % GRID_W).astype(np.float32)

    def table(dim, reps):
        n_axis = dim // 4
        inv = (ROPE_THETA ** (-np.arange(n_axis, dtype=np.float32) / n_axis)).astype(np.float32)
        ang = np.concatenate([r[:, None] * inv, c[:, None] * inv], axis=-1).astype(np.float32)
        cos, sin = np.cos(ang), np.sin(ang)
        return (np.tile(np.concatenate([cos, cos], -1), (1, reps)),
                np.tile(np.concatenate([-sin, sin], -1), (1, reps)))

    cos_a, sin_a = table(A_HEAD_DIM, 256 // A_HEAD_DIM)
    cos_b, sin_b = table(B_HALF_DIM, 256 // B_HALF_DIM)
    return tuple(jnp.asarray(a, F32) for a in (cos_a, sin_a, cos_b, sin_b))


def _group_mean_matrix():
    idx = np.arange(256) // A_HEAD_DIM
    return jnp.asarray((idx[:, None] == idx[None, :]).astype(np.float32) / A_HEAD_DIM, BF16)


def _block_diag(w):
    out = jnp.zeros((BRANCH_W, BRANCH_W), w.dtype)
    for g in range(D_GROUPS):
        out = out.at[g * D_GROUP_W:(g + 1) * D_GROUP_W, g * D_GROUP_W:(g + 1) * D_GROUP_W].set(w[g])
    return out


def _routing_tables(eid, rank, cnt):
    cnt = cnt.reshape(N_EXPERTS).astype(jnp.int32)
    padded = ((cnt + TE - 1) // TE) * TE
    ends = jnp.cumsum(padded)
    starts = ends - padded
    pos = (starts[eid[:, :TOP_K]] + rank[:, :TOP_K]).reshape(N_SLOTS)
    tile_row = jnp.arange(E_TILES, dtype=jnp.int32) * TE
    tile_expert = jnp.minimum(jnp.searchsorted(ends, tile_row, side="right"), N_EXPERTS - 1)
    tile_valid = (tile_row < ends[-1]).astype(jnp.int32)
    return pos.astype(jnp.int32), tile_expert.astype(jnp.int32), tile_valid


def kernel(x_prompt, x_sample, cache_a_k, cache_a_v, cache_b_k, cache_b_v, c, c_ctx, w_mod, b_mod, g_pre1, g_post1, g_pre2, g_post2, w_in, a_q_gain, a_k_gain, b_lambda, b_subln_gain, c_dw_w, c_dw_b, c_ln_g, c_ln_b, d_w_group, d_scale, w_branch, w_out, r_w, r_b, e_w1, e_b1, e_w2, e_b2):
    x = jnp.concatenate([x_prompt.reshape(N_CTX, D_MODEL), x_sample.reshape(N_LAT, D_MODEL)], axis=0)
    cond8 = jnp.concatenate(
        [c_ctx[None, :], c, jnp.zeros((MOD_ROWS - 1 - DEC_BATCH, D_MODEL), F32)], axis=0)
    mods = _modulation(cond8, w_mod, b_mod).reshape(DEPTH * MOD_ROWS, 1, 6 * D_MODEL)

    rope = _rope_tables()
    gmat = _group_mean_matrix()
    caches = (
        cache_a_k.reshape(DEC_BATCH, DEPTH, PAST_LEN, A_KV_HEADS * A_HEAD_DIM),
        cache_a_v.reshape(DEC_BATCH, DEPTH, PAST_LEN, A_KV_HEADS * A_HEAD_DIM),
        cache_b_k.reshape(DEC_BATCH, DEPTH, PAST_LEN, B_HEADS * 2 * B_HALF_DIM),
        cache_b_v.reshape(DEC_BATCH, DEPTH, PAST_LEN, B_HEADS * B_V_DIM),
    )
    xs_init = jnp.zeros((P_MAX, D_MODEL), F32)

    new_ak, new_av, new_bk, new_bv = [], [], [], []
    row = lambda v: v.reshape(1, -1)
    for l in range(DEPTH):
        lambda_init = 0.8 - 0.6 * math.exp(-0.3 * l)
        w_small = w_in[l, :, :N_SMALL].astype(BF16)
        w_gate = w_in[l, :, N_SMALL:].astype(BF16)
        z, h = _in_projection(
            x, mods, l, row(g_pre1[l]), w_small,
            row(jnp.tile(a_q_gain[l], A_HEADS)), row(jnp.tile(a_k_gain[l], A_KV_HEADS)), gmat, rope)

        zc = z[:N_CTX]
        new_ak.append(zc[:, COL_KA:COL_VA].reshape(BATCH, SEQ, A_KV_HEADS, A_HEAD_DIM))
        new_av.append(zc[:, COL_VA:COL_QB].reshape(BATCH, SEQ, A_KV_HEADS, A_HEAD_DIM))
        new_bk.append(zc[:, COL_KB:COL_VB].reshape(BATCH, SEQ, B_HEADS, 2, B_HALF_DIM))
        new_bv.append(zc[:, COL_VB:COL_ZC].reshape(BATCH, SEQ, B_HEADS, B_V_DIM))

        subg = row(b_subln_gain[l])
        oa_c, ob_c = _attention_ctx(z, b_lambda[l], subg, lambda_init)
        oa_l, ob_l = _attention_lat(z, caches, l, b_lambda[l], subg, lambda_init)
        conv_args = (c_dw_w[l], row(c_dw_b[l]), row(c_ln_g[l]), row(c_ln_b[l]),
                     _block_diag(d_w_group[l]).astype(BF16), row(d_scale[l]))
        oc_c, od_c = _conv_pool(z, SEQ, BATCH, 0, *conv_args)
        oc_l, od_l = _conv_pool(z, DEC_SEQ, DEC_BATCH, N_CTX, *conv_args)
        cat = lambda a, b: jnp.concatenate([a, b], axis=0)

        x1, h2, eid, rank, wts, cnt = _mix(
            x, h, cat(oa_c, oa_l), cat(ob_c, ob_l), cat(oc_c, oc_l), cat(od_c, od_l), mods, l,
            w_gate, w_branch[l].astype(BF16), w_out[l].astype(BF16),
            row(g_post1[l]), row(g_pre2[l]), r_w[l], row(r_b[l]))

        pos, tile_expert, tile_valid = _routing_tables(eid, rank, cnt)
        xs = _dispatch(pos, h2, xs_init)
        ys = _expert_ffn(tile_expert, tile_valid, xs, l, e_w1, e_b1, e_w2, e_b2)
        x = _combine(pos, ys, wts, x1, mods, l, row(g_post2[l]))

    y_prompt = x[:N_CTX].reshape(BATCH, SEQ, D_MODEL)
    y_sample = x[N_CTX:].reshape(DEC_BATCH, DEC_SEQ, D_MODEL)
    return (y_prompt, y_sample, jnp.stack(new_ak, axis=1), jnp.stack(new_av, axis=1),
            jnp.stack(new_bk, axis=1), jnp.stack(new_bv, axis=1))
```

```python
import functools
import math

import jax
import jax.numpy as jnp
import numpy as np
from jax import lax
from jax.experimental import pallas as pl
from jax.experimental.pallas import tpu as pltpu

F32 = jnp.float32
BF16 = jnp.bfloat16

D_MODEL = 1024
BATCH = 16
SEQ = 256
DEPTH = 2
DEC_BATCH = 4
DEC_SEQ = 2048
PAST_LEN = 256
GRID_W = 64
ROPE_THETA = 10000.0
N_BRANCH = 4
BRANCH_W = D_MODEL // 4
A_HEADS = 4
A_KV_HEADS = 2
A_HEAD_DIM = 64
B_HEADS = 4
B_HALF_DIM = 32
B_V_DIM = 2 * B_HALF_DIM
C_CONV_WIDTH = 31
D_GROUPS = 4
D_GROUP_W = BRANCH_W // D_GROUPS
POOL_WINDOWS = (2, 4, 8, 16)
N_EXPERTS = 32
TOP_K = 4
D_FF = D_MODEL
SWIGLU_LIMIT = 7.0
SWIGLU_ALPHA = 1.702
EPS = 1e-6

N_CTX = BATCH * SEQ
N_LAT = DEC_BATCH * DEC_SEQ
N_TOK = N_CTX + N_LAT

COL_QA, COL_KA, COL_VA = 0, 256, 384
COL_QB, COL_KB, COL_VB = 512, 768, 1024
COL_ZC, COL_ZD = 1280, 1792
N_SMALL = 2048
N_GATE = N_BRANCH * D_MODEL

TM = 256
N_TILES = N_TOK // TM
CTX_TILES = N_CTX // TM
LAT_TILES_PER_SEQ = DEC_SEQ // TM
MOD_ROWS = 8

TE = 512
N_SLOTS = N_TOK * TOP_K
P_MAX = N_SLOTS + N_EXPERTS * TE
E_TILES = P_MAX // TE
OUT_LANES = 128

TQ = 256
CONV_PAD = 16
CONV_CHUNK = 128

VMEM_LIMIT = 56 * 1024 * 1024


def _sigmoid(x):
    return 1.0 / (1.0 + jnp.exp(-x))


def _split_bf16(a):
    hi = a.astype(BF16)
    lo = (a - hi.astype(F32)).astype(BF16)
    return hi, lo


def _dot(a, b):
    return jnp.dot(a, b, preferred_element_type=F32)


def _dot_nt(a, b):
    return lax.dot_general(a, b, (((1,), (1,)), ((), ())), preferred_element_type=F32)


def _dot3(a, b):
    ah, al = _split_bf16(a)
    bh, bl = _split_bf16(b)
    return _dot(ah, bh) + _dot(ah, bl) + _dot(al, bh)


def _rms(x, g):
    return x * lax.rsqrt(jnp.mean(x * x, axis=-1, keepdims=True) + EPS) * g


def _mod_row(i):
    return jnp.where(i < CTX_TILES, 0, 1 + (i - CTX_TILES) // LAT_TILES_PER_SEQ)


MOD_TN = 1536


def _mod_kernel(cond_ref, w_ref, b_ref, o_ref):
    c = cond_ref[...]
    s = c * _sigmoid(c)
    o_ref[0] = _dot3(s, w_ref[0]) + b_ref[0]


def _modulation(cond8, w_mod, b_mod):
    L = w_mod.shape[0]
    return pl.pallas_call(
        _mod_kernel,
        name="modulation",
        grid=(L, 6 * D_MODEL // MOD_TN),
        in_specs=[
            pl.BlockSpec((MOD_ROWS, D_MODEL), lambda l, j: (0, 0)),
            pl.BlockSpec((1, D_MODEL, MOD_TN), lambda l, j: (l, 0, j)),
            pl.BlockSpec((1, 1, MOD_TN), lambda l, j: (l, 0, j)),
        ],
        out_specs=pl.BlockSpec((1, MOD_ROWS, MOD_TN), lambda l, j: (l, 0, j)),
        out_shape=jax.ShapeDtypeStruct((L, MOD_ROWS, 6 * D_MODEL), F32),
        compiler_params=pltpu.CompilerParams(
            dimension_semantics=("arbitrary", "arbitrary"), vmem_limit_bytes=VMEM_LIMIT),
    )(cond8, w_mod, b_mod.reshape(L, 1, 6 * D_MODEL))


def _group_mean(sq, gmat):
    hi, lo = _split_bf16(sq)
    return _dot(hi, gmat) + _dot(lo, gmat)


def _swap_halves(x, half):
    w = x.shape[-1]
    lane = lax.broadcasted_iota(jnp.int32, x.shape, 1)
    first = (lane % (2 * half)) < half
    return jnp.where(first, pltpu.roll(x, w - half, 1), pltpu.roll(x, half, 1))


def _inproj_kernel(x_ref, mod_ref, gpre_ref, w_ref, gq_ref, gk_ref, gmat_ref,
                   cos_a_ref, sin_a_ref, cos_b_ref, sin_b_ref, z_ref, h_ref):
    i = pl.program_id(0)
    m = mod_ref[0]
    sh1 = m[:, 0:D_MODEL]
    sc1 = m[:, D_MODEL:2 * D_MODEL]
    h = _rms(x_ref[...], gpre_ref[...]) * (1.0 + sc1) + sh1
    hb = h.astype(BF16)
    h_ref[...] = hb
    z = _dot(hb, w_ref[...])

    gmat = gmat_ref[...]
    qa = z[:, COL_QA:COL_KA]
    ka = z[:, COL_KA:COL_VA]
    qa = qa * lax.rsqrt(_group_mean(qa * qa, gmat) + EPS) * gq_ref[...]
    ka = ka * lax.rsqrt(_group_mean(ka * ka, gmat[0:128, 0:128]) + EPS) * gk_ref[...]
    qb = z[:, COL_QB:COL_KB]
    kb = z[:, COL_KB:COL_VB]
    z_ref[:, COL_VA:COL_QB] = z[:, COL_VA:COL_QB]
    z_ref[:, COL_VB:N_SMALL] = z[:, COL_VB:N_SMALL]

    @pl.when(i < CTX_TILES)
    def _():
        z_ref[:, COL_QA:COL_KA] = qa
        z_ref[:, COL_KA:COL_VA] = ka
        z_ref[:, COL_QB:COL_KB] = qb
        z_ref[:, COL_KB:COL_VB] = kb

    @pl.when(i >= CTX_TILES)
    def _():
        cos_a = cos_a_ref[...]
        sin_a = sin_a_ref[...]
        cos_b = cos_b_ref[...]
        sin_b = sin_b_ref[...]
        ha, hb_ = A_HEAD_DIM // 2, B_HALF_DIM // 2
        z_ref[:, COL_QA:COL_KA] = qa * cos_a + _swap_halves(qa, ha) * sin_a
        z_ref[:, COL_KA:COL_VA] = ka * cos_a[:, 0:128] + _swap_halves(ka, ha) * sin_a[:, 0:128]
        z_ref[:, COL_QB:COL_KB] = qb * cos_b + _swap_halves(qb, hb_) * sin_b
        z_ref[:, COL_KB:COL_VB] = kb * cos_b + _swap_halves(kb, hb_) * sin_b


def _rope_block(i):
    return jnp.where(i < CTX_TILES, 0, (i - CTX_TILES) % LAT_TILES_PER_SEQ)


def _in_projection(x, mods, l, g_pre, w_small, gq, gk, gmat, rope):
    const = lambda i: (0, 0)
    rope_spec = pl.BlockSpec((TM, 256), lambda i: (_rope_block(i), 0))
    return pl.pallas_call(
        _inproj_kernel,
        name="in_projection",
        grid=(N_TILES,),
        in_specs=[
            pl.BlockSpec((TM, D_MODEL), lambda i: (i, 0)),
            pl.BlockSpec((1, 1, 6 * D_MODEL), lambda i: (l * MOD_ROWS + _mod_row(i), 0, 0)),
            pl.BlockSpec((1, D_MODEL), const),
            pl.BlockSpec((D_MODEL, N_SMALL), const),
            pl.BlockSpec((1, 256), const),
            pl.BlockSpec((1, 128), const),
            pl.BlockSpec((256, 256), const),
            rope_spec, rope_spec, rope_spec, rope_spec,
        ],
        out_specs=[
            pl.BlockSpec((TM, N_SMALL), lambda i: (i, 0)),
            pl.BlockSpec((TM, D_MODEL), lambda i: (i, 0)),
        ],
        out_shape=[
            jax.ShapeDtypeStruct((N_TOK, N_SMALL), F32),
            jax.ShapeDtypeStruct((N_TOK, D_MODEL), BF16),
        ],
        compiler_params=pltpu.CompilerParams(
            dimension_semantics=("arbitrary",), vmem_limit_bytes=VMEM_LIMIT),
    )(x, mods, g_pre, w_small, gq, gk, gmat, *rope)


def _softmax_parts(q, ks):
    ss = [_dot_nt(q, k) for k in ks]
    m = ss[0].max(axis=-1, keepdims=True)
    for s in ss[1:]:
        m = jnp.maximum(m, s.max(axis=-1, keepdims=True))
    ps = [jnp.exp(s - m) for s in ss]
    den = ps[0].sum(axis=-1, keepdims=True)
    for p in ps[1:]:
        den = den + p.sum(axis=-1, keepdims=True)
    return ps, den


def _attn_kernel(*refs, has_cache, lambda_init):
    if has_cache:
        (qa_ref, ka_ref, va_ref, qb_ref, kb_ref, vb_ref,
         cka_ref, cva_ref, ckb_ref, cvb_ref, lam_ref, subg_ref, oa_ref, ob_ref) = refs
    else:
        (qa_ref, ka_ref, va_ref, qb_ref, kb_ref, vb_ref,
         lam_ref, subg_ref, oa_ref, ob_ref) = refs

    def cols(ref, lo, hi):
        return ref[:, lo:hi].astype(BF16)

    def ccols(ref, lo, hi):
        return ref[0, 0, :, lo:hi].astype(BF16)

    scale_a = A_HEAD_DIM ** -0.5
    group = A_HEADS // A_KV_HEADS
    for hd in range(A_HEADS):
        g = hd // group
        lo, hi = g * A_HEAD_DIM, (g + 1) * A_HEAD_DIM
        q = (qa_ref[:, hd * A_HEAD_DIM:(hd + 1) * A_HEAD_DIM] * scale_a).astype(BF16)
        ks = [cols(ka_ref, lo, hi)]
        vs = [cols(va_ref, lo, hi)]
        if has_cache:
            ks.append(ccols(cka_ref, lo, hi))
            vs.append(ccols(cva_ref, lo, hi))
        ps, den = _softmax_parts(q, ks)
        o = _dot(ps[0].astype(BF16), vs[0])
        for p, v in zip(ps[1:], vs[1:]):
            o = o + _dot(p.astype(BF16), v)
        oa_ref[:, hd * A_HEAD_DIM:(hd + 1) * A_HEAD_DIM] = o / den

    bl = lam_ref[...]
    lam = (jnp.exp(jnp.sum(bl[0:1] * bl[1:2], axis=-1, keepdims=True))
           - jnp.exp(jnp.sum(bl[2:3] * bl[3:4], axis=-1, keepdims=True)) + lambda_init)
    scale_b = B_HALF_DIM ** -0.5
    subg = subg_ref[...]
    for hd in range(B_HEADS):
        base = hd * 2 * B_HALF_DIM
        vlo, vhi = hd * B_V_DIM, (hd + 1) * B_V_DIM
        vs = [cols(vb_ref, vlo, vhi)]
        if has_cache:
            vs.append(ccols(cvb_ref, vlo, vhi))
        parts = []
        for c in range(2):
            lo, hi = base + c * B_HALF_DIM, base + (c + 1) * B_HALF_DIM
            q = (qb_ref[:, lo:hi] * scale_b).astype(BF16)
            ks = [cols(kb_ref, lo, hi)]
            if has_cache:
                ks.append(ccols(ckb_ref, lo, hi))
            parts.append(_softmax_parts(q, ks))
        (p1, d1), (p2, d2) = parts
        r1 = 1.0 / d1
        r2 = lam / d2
        o = None
        for j, v in enumerate(vs):
            a = (p1[j] * r1 - p2[j] * r2).astype(BF16)
            t = _dot(a, v)
            o = t if o is None else o + t
        o = _rms(o, subg) * (1.0 - lambda_init)
        ob_ref[:, vlo:vhi] = o


def _attention_ctx(z, b_lambda, subg, lambda_init):
    def zspec(width, col):
        return pl.BlockSpec((SEQ, width), lambda b: (b, col // width))
    out_spec = pl.BlockSpec((SEQ, BRANCH_W), lambda b: (b, 0))
    return pl.pallas_call(
        functools.partial(_attn_kernel, has_cache=False, lambda_init=lambda_init),
        name="attention_ctx",
        grid=(BATCH,),
        in_specs=[
            zspec(256, COL_QA), zspec(128, COL_KA), zspec(128, COL_VA),
            zspec(256, COL_QB), zspec(256, COL_KB), zspec(256, COL_VB),
            pl.BlockSpec((4, B_HALF_DIM), lambda b: (0, 0)),
            pl.BlockSpec((1, B_V_DIM), lambda b: (0, 0)),
        ],
        out_specs=[out_spec, out_spec],
        out_shape=[jax.ShapeDtypeStruct((N_CTX, BRANCH_W), F32)] * 2,
        compiler_params=pltpu.CompilerParams(
            dimension_semantics=("arbitrary",), vmem_limit_bytes=VMEM_LIMIT),
    )(z, z, z, z, z, z, b_lambda, subg)


def _attention_lat(z, caches, l, b_lambda, subg, lambda_init):
    q_tiles = DEC_SEQ // TQ
    seq0 = N_CTX // DEC_SEQ

    def qspec(col):
        return pl.BlockSpec((TQ, 256), lambda b, t: (CTX_TILES + b * q_tiles + t, col // 256))

    def kspec(width, col):
        return pl.BlockSpec((DEC_SEQ, width), lambda b, t: (seq0 + b, col // width))

    def cspec(width):
        return pl.BlockSpec((1, 1, PAST_LEN, width), lambda b, t: (b, l, 0, 0))

    out_spec = pl.BlockSpec((TQ, BRANCH_W), lambda b, t: (b * q_tiles + t, 0))
    cka, cva, ckb, cvb = caches
    return pl.pallas_call(
        functools.partial(_attn_kernel, has_cache=True, lambda_init=lambda_init),
        name="attention_lat",
        grid=(DEC_BATCH, q_tiles),
        in_specs=[
            qspec(COL_QA), kspec(128, COL_KA), kspec(128, COL_VA),
            qspec(COL_QB), kspec(256, COL_KB), kspec(256, COL_VB),
            cspec(128), cspec(128), cspec(256), cspec(256),
            pl.BlockSpec((4, B_HALF_DIM), lambda b, t: (0, 0)),
            pl.BlockSpec((1, B_V_DIM), lambda b, t: (0, 0)),
        ],
        out_specs=[out_spec, out_spec],
        out_shape=[jax.ShapeDtypeStruct((N_LAT, BRANCH_W), F32)] * 2,
        compiler_params=pltpu.CompilerParams(
            dimension_semantics=("arbitrary", "arbitrary"), vmem_limit_bytes=VMEM_LIMIT),
    )(z, z, z, z, z, z, cka, cva, ckb, cvb, b_lambda, subg)


SUBLANES = 8


def _row_shifter(win):
    span = win.shape[0] - SUBLANES
    shifted = {}

    def rows(off):
        s = off % SUBLANES
        if s not in shifted:
            shifted[s] = win[s:s + span, :]
        base = off - s
        return shifted[s][base:base + CONV_CHUNK, :]

    return rows


def _convpool_kernel(a_ref, b_ref, d_ref, dww_ref, dwb_ref, lng_ref, lnb_ref, wbd_ref, dsc_ref,
                     oc_ref, od_ref, hpad, upad, *, seq_len):
    n_chunks = seq_len // CONV_CHUNK
    zeros = jnp.zeros((CONV_PAD, BRANCH_W), F32)
    hpad[0:CONV_PAD, :] = zeros
    hpad[seq_len + CONV_PAD:seq_len + 2 * CONV_PAD, :] = zeros
    upad[0:CONV_PAD, :] = zeros
    upad[seq_len + CONV_PAD:seq_len + 2 * CONV_PAD, :] = zeros

    def fill(c, carry):
        r = pl.multiple_of(c * CONV_CHUNK, CONV_CHUNK)
        a = a_ref[pl.ds(r, CONV_CHUNK), :]
        b = b_ref[pl.ds(r, CONV_CHUNK), :]
        hpad[pl.ds(r + CONV_PAD, CONV_CHUNK), :] = a * _sigmoid(b)
        upad[pl.ds(r + CONV_PAD, CONV_CHUNK), :] = d_ref[pl.ds(r, CONV_CHUNK), :]
        return carry

    lax.fori_loop(0, n_chunks, fill, 0)

    lane = lax.broadcasted_iota(jnp.int32, (CONV_CHUNK, BRANCH_W), 1)
    half = C_CONV_WIDTH // 2

    def body(c, carry):
        r = pl.multiple_of(c * CONV_CHUNK, CONV_CHUNK)
        hrows = _row_shifter(hpad[pl.ds(r, CONV_CHUNK + 2 * CONV_PAD), :])
        acc = jnp.zeros((CONV_CHUNK, BRANCH_W), F32)
        for k in range(C_CONV_WIDTH):
            acc = acc + dww_ref[k:k + 1, :] * hrows(CONV_PAD - half + k)
        acc = acc + dwb_ref[...]
        mu = jnp.mean(acc, axis=-1, keepdims=True)
        cen = acc - mu
        var = jnp.mean(cen * cen, axis=-1, keepdims=True)
        y = cen * lax.rsqrt(var + EPS) * lng_ref[...] + lnb_ref[...]
        oc_ref[pl.ds(r, CONV_CHUNK), :] = y * _sigmoid(y)

        urows = _row_shifter(upad[pl.ds(r, CONV_CHUNK + 2 * CONV_PAD), :])

        def ld(d):
            return urows(CONV_PAD + d)

        u = ld(0)
        sums = {}
        s = u + ld(-1)
        sums[2] = s
        s = s + ld(-2) + ld(1)
        sums[4] = s
        s = s + ld(-4) + ld(-3) + ld(2) + ld(3)
        sums[8] = s
        s = s + ld(-8) + ld(-7) + ld(-6) + ld(-5) + ld(4) + ld(5) + ld(6) + ld(7)
        sums[16] = s
        t = r + lax.broadcasted_iota(jnp.int32, (CONV_CHUNK, 1), 0)
        pooled = None
        for g, w in reversed(list(enumerate(POOL_WINDOWS))):
            lo = jnp.maximum(t - w // 2, 0)
            hi = jnp.minimum(t + (w - 1 - w // 2), seq_len - 1)
            mean = sums[w] / (hi - lo + 1).astype(F32)
            pooled = mean if pooled is None else jnp.where(lane < (g + 1) * D_GROUP_W, mean, pooled)
        dlt = (pooled - u).astype(BF16)
        od_ref[pl.ds(r, CONV_CHUNK), :] = _dot(dlt, wbd_ref[...]) * dsc_ref[...]
        return carry

    lax.fori_loop(0, n_chunks, body, 0)


def _conv_pool(z, seq_len, n_seq, row0, dww, dwb, lng, lnb, wbd, dsc):
    blk0 = row0 // seq_len

    def zspec(col):
        return pl.BlockSpec((seq_len, 256), lambda b: (blk0 + b, col // 256))

    def const(shape):
        return pl.BlockSpec(shape, lambda b: (0, 0))

    out_spec = pl.BlockSpec((seq_len, BRANCH_W), lambda b: (b, 0))
    return pl.pallas_call(
        functools.partial(_convpool_kernel, seq_len=seq_len),
        name=f"conv_pool_{seq_len}",
        grid=(n_seq,),
        in_specs=[
            zspec(COL_ZC), zspec(COL_ZC + BRANCH_W), zspec(COL_ZD),
            const((C_CONV_WIDTH, BRANCH_W)), const((1, BRANCH_W)), const((1, BRANCH_W)),
            const((1, BRANCH_W)), const((BRANCH_W, BRANCH_W)), const((1, BRANCH_W)),
        ],
        out_specs=[out_spec, out_spec],
        out_shape=[jax.ShapeDtypeStruct((n_seq * seq_len, BRANCH_W), F32)] * 2,
        scratch_shapes=[pltpu.VMEM((seq_len + 2 * CONV_PAD, BRANCH_W), F32)] * 2,
        compiler_params=pltpu.CompilerParams(
            dimension_semantics=("arbitrary",), vmem_limit_bytes=VMEM_LIMIT),
    )(z, z, z, dww, dwb, lng, lnb, wbd, dsc)


def _mix_kernel(x_ref, h_ref, oa_c, ob_c, oc_c, od_c, oa_l, ob_l, oc_l, od_l,
                mod_ref, wg_ref, wbr_ref, wout_ref, gpost_ref, gpre2_ref, rw_ref, rb_ref,
                x1_ref, h2_ref, eid_ref, rank_ref, wts_ref, cnt_ref, carry):
    i = pl.program_id(0)
    is_ctx = i < CTX_TILES

    @pl.when(i == 0)
    def _():
        carry[...] = jnp.zeros_like(carry)

    m = mod_ref[0]
    g1 = m[:, 2 * D_MODEL:3 * D_MODEL]
    sh2 = m[:, 3 * D_MODEL:4 * D_MODEL]
    sc2 = m[:, 4 * D_MODEL:5 * D_MODEL]

    hb = h_ref[...]
    merged = None
    for n, (c_ref, l_ref) in enumerate(((oa_c, oa_l), (ob_c, ob_l), (oc_c, oc_l), (od_c, od_l))):
        gate = _sigmoid(_dot(hb, wg_ref[:, n * D_MODEL:(n + 1) * D_MODEL]))
        o = jnp.where(is_ctx, c_ref[...], l_ref[...])
        br = _dot(o.astype(BF16), wbr_ref[n])
        merged = gate * br if merged is None else merged + gate * br
    y = _dot(merged.astype(BF16), wout_ref[...])
    x1 = x_ref[...] + g1 * _rms(y, gpost_ref[...])
    x1_ref[...] = x1
    h2 = _rms(x1, gpre2_ref[...]) * (1.0 + sc2) + sh2
    h2_ref[...] = h2

    logits = _dot3(h2, rw_ref[...]) + rb_ref[...]
    iota_e = lax.broadcasted_iota(jnp.int32, (TM, N_EXPERTS), 1)
    rem = logits
    vals, idxs = [], []
    for _ in range(TOP_K):
        mx = jnp.max(rem, axis=-1, keepdims=True)
        idx = jnp.min(jnp.where(rem == mx, iota_e, N_EXPERTS), axis=-1, keepdims=True)
        vals.append(mx)
        idxs.append(idx)
        rem = jnp.where(iota_e == idx, -jnp.inf, rem)
    exps = [jnp.exp(v - vals[0]) for v in vals]
    den = exps[0]
    for e in exps[1:]:
        den = den + e

    sel = [iota_e == idx for idx in idxs]
    member = jnp.zeros((TM, N_EXPERTS), F32)
    for s in sel:
        member = member + jnp.where(s, 1.0, 0.0)
    row = lax.broadcasted_iota(jnp.int32, (TM, TM), 0)
    col = lax.broadcasted_iota(jnp.int32, (TM, TM), 1)
    before = jnp.where(col < row, 1.0, 0.0).astype(BF16)
    seen = _dot(before, member.astype(BF16)) + carry[...]
    carry[...] = carry[...] + jnp.sum(member, axis=0, keepdims=True)
    cnt_ref[...] = carry[...]

    lane = lax.broadcasted_iota(jnp.int32, (TM, OUT_LANES), 1)
    eid = jnp.zeros((TM, OUT_LANES), jnp.int32)
    rank = jnp.zeros((TM, OUT_LANES), jnp.int32)
    wts = jnp.zeros((TM, OUT_LANES), F32)
    for k in range(TOP_K):
        rk = jnp.sum(jnp.where(sel[k], seen, 0.0), axis=-1, keepdims=True).astype(jnp.int32)
        eid = jnp.where(lane == k, idxs[k], eid)
        rank = jnp.where(lane == k, rk, rank)
        wts = jnp.where(lane == k, exps[k] / den, wts)
    eid_ref[...] = eid
    rank_ref[...] = rank
    wts_ref[...] = wts


def _mix(x, h, o_ctx, o_lat, mods, l, w_gate, w_branch, w_out, g_post, g_pre2, r_w, r_b):
    const2 = lambda i: (0, 0)
    tile = lambda width: pl.BlockSpec((TM, width), lambda i: (i, 0))
    ctx_tile = pl.BlockSpec((TM, BRANCH_W), lambda i: (jnp.minimum(i, CTX_TILES - 1), 0))
    lat_tile = pl.BlockSpec((TM, BRANCH_W), lambda i: (jnp.maximum(i - CTX_TILES, 0), 0))
    single = pl.Buffered(1)
    return pl.pallas_call(
        _mix_kernel,
        name="mix_router",
        grid=(N_TILES,),
        in_specs=[
            tile(D_MODEL), tile(D_MODEL),
            ctx_tile, ctx_tile, ctx_tile, ctx_tile, lat_tile, lat_tile, lat_tile, lat_tile,
            pl.BlockSpec((1, 1, 6 * D_MODEL), lambda i: (l * MOD_ROWS + _mod_row(i), 0, 0)),
            pl.BlockSpec((D_MODEL, N_GATE), const2, pipeline_mode=single),
            pl.BlockSpec((N_BRANCH, BRANCH_W, D_MODEL), lambda i: (0, 0, 0), pipeline_mode=single),
            pl.BlockSpec((D_MODEL, D_MODEL), const2, pipeline_mode=single),
            pl.BlockSpec((1, D_MODEL), const2),
            pl.BlockSpec((1, D_MODEL), const2),
            pl.BlockSpec((D_MODEL, N_EXPERTS), const2),
            pl.BlockSpec((1, N_EXPERTS), const2),
        ],
        out_specs=[
            tile(D_MODEL), tile(D_MODEL),
            tile(OUT_LANES), tile(OUT_LANES), tile(OUT_LANES),
            pl.BlockSpec((1, N_EXPERTS), const2),
        ],
        out_shape=[
            jax.ShapeDtypeStruct((N_TOK, D_MODEL), F32),
            jax.ShapeDtypeStruct((N_TOK, D_MODEL), F32),
            jax.ShapeDtypeStruct((N_TOK, OUT_LANES), jnp.int32),
            jax.ShapeDtypeStruct((N_TOK, OUT_LANES), jnp.int32),
            jax.ShapeDtypeStruct((N_TOK, OUT_LANES), F32),
            jax.ShapeDtypeStruct((1, N_EXPERTS), F32),
        ],
        scratch_shapes=[pltpu.VMEM((1, N_EXPERTS), F32)],
        compiler_params=pltpu.CompilerParams(
            dimension_semantics=("arbitrary",), vmem_limit_bytes=VMEM_LIMIT),
    )(x, h, *o_ctx, *o_lat, mods, w_gate, w_branch, w_out, g_post, g_pre2, r_w, r_b)


def _row_copy(src_ref, src_row, dst_ref, dst_row, sem):
    return pltpu.make_async_copy(src_ref.at[pl.ds(src_row, 1), :], dst_ref.at[pl.ds(dst_row, 1), :], sem)


DISPATCH_ROWS = 1024
DISPATCH_STEPS = N_TOK // DISPATCH_ROWS
DMA_UNROLL = 8


def _dispatch_kernel(starts_ref, padded_ref, pos_ref, h2_ref, xs_ref, zbuf, sem, zsem, tsem):
    i = pl.program_id(0)

    def pad_copy(e):
        last = pl.multiple_of(starts_ref[e] + padded_ref[e] - TE, TE)
        return pltpu.make_async_copy(zbuf, xs_ref.at[pl.ds(last, TE), :], zsem)

    first_tail = (starts_ref[N_EXPERTS - 1] + padded_ref[N_EXPERTS - 1]) // TE

    def tail_copy(j):
        return pltpu.make_async_copy(zbuf, xs_ref.at[pl.ds(pl.multiple_of(j * TE, TE), TE), :], tsem)

    def tail_start(j, carry):
        tail_copy(j).start()
        return carry

    def tail_wait(j, carry):
        tail_copy(j).wait()
        return carry

    @pl.when(i == 0)
    def _():
        zbuf[...] = jnp.zeros_like(zbuf)
        for e in range(N_EXPERTS):
            @pl.when(padded_ref[e] > 0)
            def _():
                pad_copy(e).start()
        lax.fori_loop(first_tail, E_TILES, tail_start, 0)
        for e in range(N_EXPERTS):
            @pl.when(padded_ref[e] > 0)
            def _():
                pad_copy(e).wait()

    @pl.when(i == DISPATCH_STEPS - 1)
    def _():
        lax.fori_loop(first_tail, E_TILES, tail_wait, 0)

    base = i * DISPATCH_ROWS

    def issue(t, carry):
        for k in range(TOP_K):
            _row_copy(h2_ref, base + t, xs_ref, pos_ref[t * TOP_K + k], sem).start()
        return carry

    lax.fori_loop(0, DISPATCH_ROWS, issue, 0, unroll=DMA_UNROLL)

    def step_wait():
        for _ in range(TOP_K):
            pltpu.make_async_copy(h2_ref.at[pl.ds(0, DISPATCH_ROWS), :],
                                  xs_ref.at[pl.ds(0, DISPATCH_ROWS), :], sem).wait()

    @pl.when(i > 0)
    def _():
        step_wait()

    @pl.when(i == DISPATCH_STEPS - 1)
    def _():
        step_wait()


def _dispatch(starts, padded, pos, h2):
    grid_spec = pltpu.PrefetchScalarGridSpec(
        num_scalar_prefetch=2,
        grid=(DISPATCH_STEPS,),
        in_specs=[
            pl.BlockSpec((DISPATCH_ROWS * TOP_K,), lambda i, st, pd: (i,), memory_space=pltpu.SMEM),
            pl.BlockSpec(memory_space=pl.ANY),
        ],
        out_specs=pl.BlockSpec(memory_space=pl.ANY),
        scratch_shapes=[pltpu.VMEM((TE, D_MODEL), F32),
                        pltpu.SemaphoreType.DMA(()), pltpu.SemaphoreType.DMA(()),
                        pltpu.SemaphoreType.DMA(())],
    )
    return pl.pallas_call(
        _dispatch_kernel,
        name="dispatch",
        grid_spec=grid_spec,
        out_shape=jax.ShapeDtypeStruct((P_MAX, D_MODEL), F32),
        compiler_params=pltpu.CompilerParams(
            dimension_semantics=("arbitrary",), vmem_limit_bytes=VMEM_LIMIT),
    )(starts, padded, pos, h2)


W_CAST_ROWS = 128


def _moe_kernel(te_ref, valid_ref, x_ref, w1_ref, b1_ref, w2_ref, b2_ref, y_ref, w1b, w2b):
    j = pl.program_id(0)
    e = te_ref[j]
    prev = te_ref[jnp.maximum(j - 1, 0)]
    valid = valid_ref[j] == 1
    new_expert = jnp.logical_or(j == 0, e != prev)

    @pl.when(jnp.logical_and(valid, new_expert))
    def _():
        def cast(c, carry):
            r = pl.multiple_of(c * W_CAST_ROWS, W_CAST_ROWS)
            w1b[pl.ds(r, W_CAST_ROWS), :] = w1_ref[0, 0, pl.ds(r, W_CAST_ROWS), :].astype(BF16)
            w2b[pl.ds(r, W_CAST_ROWS), :] = w2_ref[0, 0, pl.ds(r, W_CAST_ROWS), :].astype(BF16)
            return carry
        lax.fori_loop(0, D_MODEL // W_CAST_ROWS, cast, 0)

    @pl.when(valid)
    def _():
        x = x_ref[...].astype(BF16)
        u = _dot(x, w1b[...]) + b1_ref[0, 0]
        xg = jnp.minimum(u[:, :D_FF], SWIGLU_LIMIT)
        xl = jnp.clip(u[:, D_FF:], -SWIGLU_LIMIT, SWIGLU_LIMIT)
        act = (xl + 1.0) * (xg * _sigmoid(SWIGLU_ALPHA * xg))
        y_ref[...] = _dot(act.astype(BF16), w2b[...]) + b2_ref[0, 0]

    @pl.when(jnp.logical_not(valid))
    def _():
        y_ref[...] = jnp.zeros_like(y_ref)


def _expert_ffn(tile_expert, tile_valid, xs, l, e_w1, e_b1, e_w2, e_b2):
    L = e_w1.shape[0]
    grid_spec = pltpu.PrefetchScalarGridSpec(
        num_scalar_prefetch=2,
        grid=(E_TILES,),
        in_specs=[
            pl.BlockSpec((TE, D_MODEL), lambda j, te, va: (j * va[j], 0)),
            pl.BlockSpec((1, 1, D_MODEL, 2 * D_FF), lambda j, te, va: (l, te[j], 0, 0)),
            pl.BlockSpec((1, 1, 1, 2 * D_FF), lambda j, te, va: (l, te[j], 0, 0)),
            pl.BlockSpec((1, 1, D_FF, D_MODEL), lambda j, te, va: (l, te[j], 0, 0)),
            pl.BlockSpec((1, 1, 1, D_MODEL), lambda j, te, va: (l, te[j], 0, 0)),
        ],
        out_specs=pl.BlockSpec((TE, D_MODEL), lambda j, te, va: (j, 0)),
        scratch_shapes=[pltpu.VMEM((D_MODEL, 2 * D_FF), BF16), pltpu.VMEM((D_FF, D_MODEL), BF16)],
    )
    return pl.pallas_call(
        _moe_kernel,
        name="expert_ffn",
        grid_spec=grid_spec,
        out_shape=jax.ShapeDtypeStruct((P_MAX, D_MODEL), F32),
        compiler_params=pltpu.CompilerParams(
            dimension_semantics=("arbitrary",), vmem_limit_bytes=VMEM_LIMIT),
    )(tile_expert, tile_valid, xs, e_w1,
      e_b1.reshape(L, N_EXPERTS, 1, 2 * D_FF), e_w2, e_b2.reshape(L, N_EXPERTS, 1, D_MODEL))


def _combine_kernel(pos_ref, pos_next_ref, ys_ref, wts_ref, x1_ref, mod_ref, gpost_ref, out_ref,
                    buf, sem):
    i = pl.program_id(0)
    slot = i % 2

    def gather(p_ref, s):
        def issue(t, carry):
            for k in range(TOP_K):
                _row_copy(ys_ref, p_ref[t * TOP_K + k], buf.at[s, k], t, sem.at[s]).start()
            return carry
        lax.fori_loop(0, TM, issue, 0, unroll=DMA_UNROLL)

    @pl.when(i == 0)
    def _():
        gather(pos_ref, 0)

    @pl.when(i + 1 < N_TILES)
    def _():
        gather(pos_next_ref, 1 - slot)

    for k in range(TOP_K):
        pltpu.make_async_copy(ys_ref.at[pl.ds(0, TM), :], buf.at[slot, k], sem.at[slot]).wait()

    w = wts_ref[...]
    y = w[:, 0:1] * buf[slot, 0]
    for k in range(1, TOP_K):
        y = y + w[:, k:k + 1] * buf[slot, k]
    g2 = mod_ref[0][:, 5 * D_MODEL:6 * D_MODEL]
    out_ref[...] = x1_ref[...] + g2 * _rms(y, gpost_ref[...])


def _combine(pos, ys, wts, x1, mods, l, g_post2):
    return pl.pallas_call(
        _combine_kernel,
        name="combine",
        grid=(N_TILES,),
        in_specs=[
            pl.BlockSpec((TM * TOP_K,), lambda i: (i,), memory_space=pltpu.SMEM),
            pl.BlockSpec((TM * TOP_K,), lambda i: (jnp.minimum(i + 1, N_TILES - 1),),
                         memory_space=pltpu.SMEM),
            pl.BlockSpec(memory_space=pl.ANY),
            pl.BlockSpec((TM, OUT_LANES), lambda i: (i, 0)),
            pl.BlockSpec((TM, D_MODEL), lambda i: (i, 0)),
            pl.BlockSpec((1, 1, 6 * D_MODEL), lambda i: (l * MOD_ROWS + _mod_row(i), 0, 0)),
            pl.BlockSpec((1, D_MODEL), lambda i: (0, 0)),
        ],
        out_specs=pl.BlockSpec((TM, D_MODEL), lambda i: (i, 0)),
        out_shape=jax.ShapeDtypeStruct((N_TOK, D_MODEL), F32),
        scratch_shapes=[pltpu.VMEM((2, TOP_K, TM, D_MODEL), F32), pltpu.SemaphoreType.DMA((2,))],
        compiler_params=pltpu.CompilerParams(
            dimension_semantics=("arbitrary",), vmem_limit_bytes=VMEM_LIMIT),
    )(pos, pos, ys, wts, x1, mods, g_post2)


def _rope_tables():
    t = np.arange(DEC_SEQ)
    r = (t // GRID_W).astype(np.float32)
    c = (t % GRID_W).astype(np.float32)

    def table(dim, reps):
        n_axis = dim // 4
        inv = (ROPE_THETA ** (-np.arange(n_axis, dtype=np.float32) / n_axis)).astype(np.float32)
        ang = np.concatenate([r[:, None] * inv, c[:, None] * inv], axis=-1).astype(np.float32)
        cos, sin = np.cos(ang), np.sin(ang)
        return (np.tile(np.concatenate([cos, cos], -1), (1, reps)),
                np.tile(np.concatenate([-sin, sin], -1), (1, reps)))

    cos_a, sin_a = table(A_HEAD_DIM, 256 // A_HEAD_DIM)
    cos_b, sin_b = table(B_HALF_DIM, 256 // B_HALF_DIM)
    return tuple(jnp.asarray(a, F32) for a in (cos_a, sin_a, cos_b, sin_b))


def _group_mean_matrix():
    idx = np.arange(256) // A_HEAD_DIM
    return jnp.asarray((idx[:, None] == idx[None, :]).astype(np.float32) / A_HEAD_DIM, BF16)


def _block_diag(w):
    out = jnp.zeros((BRANCH_W, BRANCH_W), w.dtype)
    for g in range(D_GROUPS):
        out = out.at[g * D_GROUP_W:(g + 1) * D_GROUP_W, g * D_GROUP_W:(g + 1) * D_GROUP_W].set(w[g])
    return out


def _routing_tables(eid, rank, cnt):
    cnt = cnt.reshape(N_EXPERTS).astype(jnp.int32)
    padded = ((cnt + TE - 1) // TE) * TE
    ends = jnp.cumsum(padded)
    starts = ends - padded
    experts = jnp.arange(N_EXPERTS, dtype=jnp.int32)
    start_of = jnp.sum(jnp.where(eid[:, :TOP_K, None] == experts, starts, 0), axis=-1)
    pos = (start_of + rank[:, :TOP_K]).reshape(N_SLOTS)
    tile_row = jnp.arange(E_TILES, dtype=jnp.int32) * TE
    tile_expert = jnp.sum((tile_row[:, None] >= ends[None, :]).astype(jnp.int32), axis=-1)
    tile_expert = jnp.minimum(tile_expert, N_EXPERTS - 1)
    tile_valid = (tile_row < ends[-1]).astype(jnp.int32)
    return (pos.astype(jnp.int32), tile_expert.astype(jnp.int32), tile_valid,
            starts.astype(jnp.int32), padded.astype(jnp.int32))


def kernel(x_prompt, x_sample, cache_a_k, cache_a_v, cache_b_k, cache_b_v, c, c_ctx, w_mod, b_mod, g_pre1, g_post1, g_pre2, g_post2, w_in, a_q_gain, a_k_gain, b_lambda, b_subln_gain, c_dw_w, c_dw_b, c_ln_g, c_ln_b, d_w_group, d_scale, w_branch, w_out, r_w, r_b, e_w1, e_b1, e_w2, e_b2):
    x = jnp.concatenate([x_prompt.reshape(N_CTX, D_MODEL), x_sample.reshape(N_LAT, D_MODEL)], axis=0)
    cond8 = jnp.concatenate(
        [c_ctx[None, :], c, jnp.zeros((MOD_ROWS - 1 - DEC_BATCH, D_MODEL), F32)], axis=0)
    mods = _modulation(cond8, w_mod, b_mod).reshape(DEPTH * MOD_ROWS, 1, 6 * D_MODEL)

    rope = _rope_tables()
    gmat = _group_mean_matrix()
    caches = (
        cache_a_k.reshape(DEC_BATCH, DEPTH, PAST_LEN, A_KV_HEADS * A_HEAD_DIM),
        cache_a_v.reshape(DEC_BATCH, DEPTH, PAST_LEN, A_KV_HEADS * A_HEAD_DIM),
        cache_b_k.reshape(DEC_BATCH, DEPTH, PAST_LEN, B_HEADS * 2 * B_HALF_DIM),
        cache_b_v.reshape(DEC_BATCH, DEPTH, PAST_LEN, B_HEADS * B_V_DIM),
    )

    new_ak, new_av, new_bk, new_bv = [], [], [], []
    row = lambda v: v.reshape(1, -1)
    for l in range(DEPTH):
        lambda_init = 0.8 - 0.6 * math.exp(-0.3 * l)
        w_small = w_in[l, :, :N_SMALL].astype(BF16)
        w_gate = w_in[l, :, N_SMALL:].astype(BF16)
        z, h = _in_projection(
            x, mods, l, row(g_pre1[l]), w_small,
            row(jnp.tile(a_q_gain[l], A_HEADS)), row(jnp.tile(a_k_gain[l], A_KV_HEADS)), gmat, rope)

        zc = z[:N_CTX]
        new_ak.append(zc[:, COL_KA:COL_VA].reshape(BATCH, SEQ, A_KV_HEADS, A_HEAD_DIM))
        new_av.append(zc[:, COL_VA:COL_QB].reshape(BATCH, SEQ, A_KV_HEADS, A_HEAD_DIM))
        new_bk.append(zc[:, COL_KB:COL_VB].reshape(BATCH, SEQ, B_HEADS, 2, B_HALF_DIM))
        new_bv.append(zc[:, COL_VB:COL_ZC].reshape(BATCH, SEQ, B_HEADS, B_V_DIM))

        subg = row(b_subln_gain[l])
        oa_c, ob_c = _attention_ctx(z, b_lambda[l], subg, lambda_init)
        oa_l, ob_l = _attention_lat(z, caches, l, b_lambda[l], subg, lambda_init)
        conv_args = (c_dw_w[l], row(c_dw_b[l]), row(c_ln_g[l]), row(c_ln_b[l]),
                     _block_diag(d_w_group[l]).astype(BF16), row(d_scale[l]))
        oc_c, od_c = _conv_pool(z, SEQ, BATCH, 0, *conv_args)
        oc_l, od_l = _conv_pool(z, DEC_SEQ, DEC_BATCH, N_CTX, *conv_args)

        x1, h2, eid, rank, wts, cnt = _mix(
            x, h, (oa_c, ob_c, oc_c, od_c), (oa_l, ob_l, oc_l, od_l), mods, l,
            w_gate, w_branch[l].astype(BF16), w_out[l].astype(BF16),
            row(g_post1[l]), row(g_pre2[l]), r_w[l], row(r_b[l]))

        pos, tile_expert, tile_valid, starts, padded = _routing_tables(eid, rank, cnt)
        xs = _dispatch(starts, padded, pos, h2)
        ys = _expert_ffn(tile_expert, tile_valid, xs, l, e_w1, e_b1, e_w2, e_b2)
        x = _combine(pos, ys, wts, x1, mods, l, row(g_post2[l]))

    y_prompt = x[:N_CTX].reshape(BATCH, SEQ, D_MODEL)
    y_sample = x[N_CTX:].reshape(DEC_BATCH, DEC_SEQ, D_MODEL)
    return (y_prompt, y_sample, jnp.stack(new_ak, axis=1), jnp.stack(new_av, axis=1),
            jnp.stack(new_bk, axis=1), jnp.stack(new_bv, axis=1))
```

```python
import functools
import math

import jax
import jax.numpy as jnp
import numpy as np
from jax import lax
from jax.experimental import pallas as pl
from jax.experimental.pallas import tpu as pltpu

F32 = jnp.float32
BF16 = jnp.bfloat16

D_MODEL = 1024
BATCH = 16
SEQ = 256
DEPTH = 2
DEC_BATCH = 4
DEC_SEQ = 2048
PAST_LEN = 256
GRID_W = 64
ROPE_THETA = 10000.0
N_BRANCH = 4
BRANCH_W = D_MODEL // 4
A_HEADS = 4
A_KV_HEADS = 2
A_HEAD_DIM = 64
B_HEADS = 4
B_HALF_DIM = 32
B_V_DIM = 2 * B_HALF_DIM
C_CONV_WIDTH = 31
D_GROUPS = 4
D_GROUP_W = BRANCH_W // D_GROUPS
POOL_WINDOWS = (2, 4, 8, 16)
N_EXPERTS = 32
TOP_K = 4
D_FF = D_MODEL
SWIGLU_LIMIT = 7.0
SWIGLU_ALPHA = 1.702
EPS = 1e-6

N_CTX = BATCH * SEQ
N_LAT = DEC_BATCH * DEC_SEQ
N_TOK = N_CTX + N_LAT

COL_QA, COL_KA, COL_VA = 0, 256, 384
COL_QB, COL_KB, COL_VB = 512, 768, 1024
COL_ZC, COL_ZD = 1280, 1792
N_SMALL = 2048
N_GATE = N_BRANCH * D_MODEL

TM = 256
N_TILES = N_TOK // TM
CTX_TILES = N_CTX // TM
LAT_TILES_PER_SEQ = DEC_SEQ // TM
MOD_ROWS = 8

TE = 512
N_SLOTS = N_TOK * TOP_K
P_MAX = N_SLOTS + N_EXPERTS * TE
E_TILES = P_MAX // TE
OUT_LANES = 128

TQ = 256
CONV_PAD = 16
CONV_CHUNK = 128

VMEM_LIMIT = 56 * 1024 * 1024


def _sigmoid(x):
    return 1.0 / (1.0 + jnp.exp(-x))


def _split_bf16(a):
    hi = a.astype(BF16)
    lo = (a - hi.astype(F32)).astype(BF16)
    return hi, lo


def _dot(a, b):
    return jnp.dot(a, b, preferred_element_type=F32)


def _dot_nt(a, b):
    return lax.dot_general(a, b, (((1,), (1,)), ((), ())), preferred_element_type=F32)


def _dot3(a, b):
    ah, al = _split_bf16(a)
    bh, bl = _split_bf16(b)
    return _dot(ah, bh) + _dot(ah, bl) + _dot(al, bh)


def _rms(x, g):
    return x * lax.rsqrt(jnp.mean(x * x, axis=-1, keepdims=True) + EPS) * g


def _mod_row(i):
    return jnp.where(i < CTX_TILES, 0, 1 + (i - CTX_TILES) // LAT_TILES_PER_SEQ)


MOD_TN = 1536


def _mod_kernel(cond_ref, w_ref, b_ref, o_ref):
    c = cond_ref[...]
    s = c * _sigmoid(c)
    o_ref[0] = _dot3(s, w_ref[0]) + b_ref[0]


def _modulation(cond8, w_mod, b_mod):
    L = w_mod.shape[0]
    return pl.pallas_call(
        _mod_kernel,
        name="modulation",
        grid=(L, 6 * D_MODEL // MOD_TN),
        in_specs=[
            pl.BlockSpec((MOD_ROWS, D_MODEL), lambda l, j: (0, 0)),
            pl.BlockSpec((1, D_MODEL, MOD_TN), lambda l, j: (l, 0, j)),
            pl.BlockSpec((1, 1, MOD_TN), lambda l, j: (l, 0, j)),
        ],
        out_specs=pl.BlockSpec((1, MOD_ROWS, MOD_TN), lambda l, j: (l, 0, j)),
        out_shape=jax.ShapeDtypeStruct((L, MOD_ROWS, 6 * D_MODEL), F32),
        compiler_params=pltpu.CompilerParams(
            dimension_semantics=("arbitrary", "arbitrary"), vmem_limit_bytes=VMEM_LIMIT),
    )(cond8, w_mod, b_mod.reshape(L, 1, 6 * D_MODEL))


def _group_mean(sq, gmat):
    hi, lo = _split_bf16(sq)
    return _dot(hi, gmat) + _dot(lo, gmat)


def _swap_halves(x, half):
    w = x.shape[-1]
    lane = lax.broadcasted_iota(jnp.int32, x.shape, 1)
    first = (lane % (2 * half)) < half
    return jnp.where(first, pltpu.roll(x, w - half, 1), pltpu.roll(x, half, 1))


def _inproj_kernel(x_ref, mod_ref, gpre_ref, w_ref, gq_ref, gk_ref, gmat_ref,
                   cos_a_ref, sin_a_ref, cos_b_ref, sin_b_ref, z_ref, h_ref):
    i = pl.program_id(0)
    m = mod_ref[0]
    sh1 = m[:, 0:D_MODEL]
    sc1 = m[:, D_MODEL:2 * D_MODEL]
    h = _rms(x_ref[...], gpre_ref[...]) * (1.0 + sc1) + sh1
    hb = h.astype(BF16)
    h_ref[...] = hb
    z = _dot(hb, w_ref[...])

    gmat = gmat_ref[...]
    qa = z[:, COL_QA:COL_KA]
    ka = z[:, COL_KA:COL_VA]
    qa = qa * lax.rsqrt(_group_mean(qa * qa, gmat) + EPS) * gq_ref[...]
    ka = ka * lax.rsqrt(_group_mean(ka * ka, gmat[0:128, 0:128]) + EPS) * gk_ref[...]
    qb = z[:, COL_QB:COL_KB]
    kb = z[:, COL_KB:COL_VB]
    z_ref[:, COL_VA:COL_QB] = z[:, COL_VA:COL_QB]
    z_ref[:, COL_VB:N_SMALL] = z[:, COL_VB:N_SMALL]

    @pl.when(i < CTX_TILES)
    def _():
        z_ref[:, COL_QA:COL_KA] = qa
        z_ref[:, COL_KA:COL_VA] = ka
        z_ref[:, COL_QB:COL_KB] = qb
        z_ref[:, COL_KB:COL_VB] = kb

    @pl.when(i >= CTX_TILES)
    def _():
        cos_a = cos_a_ref[...]
        sin_a = sin_a_ref[...]
        cos_b = cos_b_ref[...]
        sin_b = sin_b_ref[...]
        ha, hb_ = A_HEAD_DIM // 2, B_HALF_DIM // 2
        z_ref[:, COL_QA:COL_KA] = qa * cos_a + _swap_halves(qa, ha) * sin_a
        z_ref[:, COL_KA:COL_VA] = ka * cos_a[:, 0:128] + _swap_halves(ka, ha) * sin_a[:, 0:128]
        z_ref[:, COL_QB:COL_KB] = qb * cos_b + _swap_halves(qb, hb_) * sin_b
        z_ref[:, COL_KB:COL_VB] = kb * cos_b + _swap_halves(kb, hb_) * sin_b


def _rope_block(i):
    return jnp.where(i < CTX_TILES, 0, (i - CTX_TILES) % LAT_TILES_PER_SEQ)


def _in_projection(x, mods, l, g_pre, w_small, gq, gk, gmat, rope):
    const = lambda i: (0, 0)
    rope_spec = pl.BlockSpec((TM, 256), lambda i: (_rope_block(i), 0))
    return pl.pallas_call(
        _inproj_kernel,
        name="in_projection",
        grid=(N_TILES,),
        in_specs=[
            pl.BlockSpec((TM, D_MODEL), lambda i: (i, 0)),
            pl.BlockSpec((1, 1, 6 * D_MODEL), lambda i: (l * MOD_ROWS + _mod_row(i), 0, 0)),
            pl.BlockSpec((1, D_MODEL), const),
            pl.BlockSpec((D_MODEL, N_SMALL), const),
            pl.BlockSpec((1, 256), const),
            pl.BlockSpec((1, 128), const),
            pl.BlockSpec((256, 256), const),
            rope_spec, rope_spec, rope_spec, rope_spec,
        ],
        out_specs=[
            pl.BlockSpec((TM, N_SMALL), lambda i: (i, 0)),
            pl.BlockSpec((TM, D_MODEL), lambda i: (i, 0)),
        ],
        out_shape=[
            jax.ShapeDtypeStruct((N_TOK, N_SMALL), F32),
            jax.ShapeDtypeStruct((N_TOK, D_MODEL), BF16),
        ],
        compiler_params=pltpu.CompilerParams(
            dimension_semantics=("arbitrary",), vmem_limit_bytes=VMEM_LIMIT),
    )(x, mods, g_pre, w_small, gq, gk, gmat, *rope)


def _softmax_parts(q, ks):
    ss = [_dot_nt(q, k) for k in ks]
    m = ss[0].max(axis=-1, keepdims=True)
    for s in ss[1:]:
        m = jnp.maximum(m, s.max(axis=-1, keepdims=True))
    ps = [jnp.exp(s - m) for s in ss]
    den = ps[0].sum(axis=-1, keepdims=True)
    for p in ps[1:]:
        den = den + p.sum(axis=-1, keepdims=True)
    return ps, den


def _attn_kernel(*refs, has_cache, lambda_init):
    if has_cache:
        (qa_ref, ka_ref, va_ref, qb_ref, kb_ref, vb_ref,
         cka_ref, cva_ref, ckb_ref, cvb_ref, lam_ref, subg_ref, oa_ref, ob_ref) = refs
    else:
        (qa_ref, ka_ref, va_ref, qb_ref, kb_ref, vb_ref,
         lam_ref, subg_ref, oa_ref, ob_ref) = refs

    def cols(ref, lo, hi):
        return ref[:, lo:hi].astype(BF16)

    def ccols(ref, lo, hi):
        return ref[0, 0, :, lo:hi].astype(BF16)

    scale_a = A_HEAD_DIM ** -0.5
    group = A_HEADS // A_KV_HEADS
    for hd in range(A_HEADS):
        g = hd // group
        lo, hi = g * A_HEAD_DIM, (g + 1) * A_HEAD_DIM
        q = (qa_ref[:, hd * A_HEAD_DIM:(hd + 1) * A_HEAD_DIM] * scale_a).astype(BF16)
        ks = [cols(ka_ref, lo, hi)]
        vs = [cols(va_ref, lo, hi)]
        if has_cache:
            ks.append(ccols(cka_ref, lo, hi))
            vs.append(ccols(cva_ref, lo, hi))
        ps, den = _softmax_parts(q, ks)
        o = _dot(ps[0].astype(BF16), vs[0])
        for p, v in zip(ps[1:], vs[1:]):
            o = o + _dot(p.astype(BF16), v)
        oa_ref[:, hd * A_HEAD_DIM:(hd + 1) * A_HEAD_DIM] = o / den

    bl = lam_ref[...]
    lam = (jnp.exp(jnp.sum(bl[0:1] * bl[1:2], axis=-1, keepdims=True))
           - jnp.exp(jnp.sum(bl[2:3] * bl[3:4], axis=-1, keepdims=True)) + lambda_init)
    scale_b = B_HALF_DIM ** -0.5
    subg = subg_ref[...]
    for hd in range(B_HEADS):
        base = hd * 2 * B_HALF_DIM
        vlo, vhi = hd * B_V_DIM, (hd + 1) * B_V_DIM
        vs = [cols(vb_ref, vlo, vhi)]
        if has_cache:
            vs.append(ccols(cvb_ref, vlo, vhi))
        parts = []
        for c in range(2):
            lo, hi = base + c * B_HALF_DIM, base + (c + 1) * B_HALF_DIM
            q = (qb_ref[:, lo:hi] * scale_b).astype(BF16)
            ks = [cols(kb_ref, lo, hi)]
            if has_cache:
                ks.append(ccols(ckb_ref, lo, hi))
            parts.append(_softmax_parts(q, ks))
        (p1, d1), (p2, d2) = parts
        r1 = 1.0 / d1
        r2 = lam / d2
        o = None
        for j, v in enumerate(vs):
            a = (p1[j] * r1 - p2[j] * r2).astype(BF16)
            t = _dot(a, v)
            o = t if o is None else o + t
        o = _rms(o, subg) * (1.0 - lambda_init)
        ob_ref[:, vlo:vhi] = o


def _attention_ctx(z, b_lambda, subg, lambda_init):
    def zspec(width, col):
        return pl.BlockSpec((SEQ, width), lambda b: (b, col // width))
    out_spec = pl.BlockSpec((SEQ, BRANCH_W), lambda b: (b, 0))
    return pl.pallas_call(
        functools.partial(_attn_kernel, has_cache=False, lambda_init=lambda_init),
        name="attention_ctx",
        grid=(BATCH,),
        in_specs=[
            zspec(256, COL_QA), zspec(128, COL_KA), zspec(128, COL_VA),
            zspec(256, COL_QB), zspec(256, COL_KB), zspec(256, COL_VB),
            pl.BlockSpec((4, B_HALF_DIM), lambda b: (0, 0)),
            pl.BlockSpec((1, B_V_DIM), lambda b: (0, 0)),
        ],
        out_specs=[out_spec, out_spec],
        out_shape=[jax.ShapeDtypeStruct((N_CTX, BRANCH_W), F32)] * 2,
        compiler_params=pltpu.CompilerParams(
            dimension_semantics=("arbitrary",), vmem_limit_bytes=VMEM_LIMIT),
    )(z, z, z, z, z, z, b_lambda, subg)


def _attention_lat(z, caches, l, b_lambda, subg, lambda_init):
    q_tiles = DEC_SEQ // TQ
    seq0 = N_CTX // DEC_SEQ

    def qspec(col):
        return pl.BlockSpec((TQ, 256), lambda b, t: (CTX_TILES + b * q_tiles + t, col // 256))

    def kspec(width, col):
        return pl.BlockSpec((DEC_SEQ, width), lambda b, t: (seq0 + b, col // width))

    def cspec(width):
        return pl.BlockSpec((1, 1, PAST_LEN, width), lambda b, t: (b, l, 0, 0))

    out_spec = pl.BlockSpec((TQ, BRANCH_W), lambda b, t: (b * q_tiles + t, 0))
    cka, cva, ckb, cvb = caches
    return pl.pallas_call(
        functools.partial(_attn_kernel, has_cache=True, lambda_init=lambda_init),
        name="attention_lat",
        grid=(DEC_BATCH, q_tiles),
        in_specs=[
            qspec(COL_QA), kspec(128, COL_KA), kspec(128, COL_VA),
            qspec(COL_QB), kspec(256, COL_KB), kspec(256, COL_VB),
            cspec(128), cspec(128), cspec(256), cspec(256),
            pl.BlockSpec((4, B_HALF_DIM), lambda b, t: (0, 0)),
            pl.BlockSpec((1, B_V_DIM), lambda b, t: (0, 0)),
        ],
        out_specs=[out_spec, out_spec],
        out_shape=[jax.ShapeDtypeStruct((N_LAT, BRANCH_W), F32)] * 2,
        compiler_params=pltpu.CompilerParams(
            dimension_semantics=("arbitrary", "arbitrary"), vmem_limit_bytes=VMEM_LIMIT),
    )(z, z, z, z, z, z, cka, cva, ckb, cvb, b_lambda, subg)


SUBLANES = 8


def _row_shifter(win):
    span = win.shape[0] - SUBLANES
    shifted = {}

    def rows(off):
        s = off % SUBLANES
        if s not in shifted:
            shifted[s] = win[s:s + span, :]
        base = off - s
        return shifted[s][base:base + CONV_CHUNK, :]

    return rows


def _convpool_kernel(a_ref, b_ref, d_ref, dww_ref, dwb_ref, lng_ref, lnb_ref, wbd_ref, dsc_ref,
                     oc_ref, od_ref, hpad, upad, *, seq_len):
    n_chunks = seq_len // CONV_CHUNK
    zeros = jnp.zeros((CONV_PAD, BRANCH_W), F32)
    hpad[0:CONV_PAD, :] = zeros
    hpad[seq_len + CONV_PAD:seq_len + 2 * CONV_PAD, :] = zeros
    upad[0:CONV_PAD, :] = zeros
    upad[seq_len + CONV_PAD:seq_len + 2 * CONV_PAD, :] = zeros

    def fill(c, carry):
        r = pl.multiple_of(c * CONV_CHUNK, CONV_CHUNK)
        a = a_ref[pl.ds(r, CONV_CHUNK), :]
        b = b_ref[pl.ds(r, CONV_CHUNK), :]
        hpad[pl.ds(r + CONV_PAD, CONV_CHUNK), :] = a * _sigmoid(b)
        upad[pl.ds(r + CONV_PAD, CONV_CHUNK), :] = d_ref[pl.ds(r, CONV_CHUNK), :]
        return carry

    lax.fori_loop(0, n_chunks, fill, 0)

    lane = lax.broadcasted_iota(jnp.int32, (CONV_CHUNK, BRANCH_W), 1)
    half = C_CONV_WIDTH // 2

    def body(c, carry):
        r = pl.multiple_of(c * CONV_CHUNK, CONV_CHUNK)
        hrows = _row_shifter(hpad[pl.ds(r, CONV_CHUNK + 2 * CONV_PAD), :])
        acc = jnp.zeros((CONV_CHUNK, BRANCH_W), F32)
        for k in range(C_CONV_WIDTH):
            acc = acc + dww_ref[k:k + 1, :] * hrows(CONV_PAD - half + k)
        acc = acc + dwb_ref[...]
        mu = jnp.mean(acc, axis=-1, keepdims=True)
        cen = acc - mu
        var = jnp.mean(cen * cen, axis=-1, keepdims=True)
        y = cen * lax.rsqrt(var + EPS) * lng_ref[...] + lnb_ref[...]
        oc_ref[pl.ds(r, CONV_CHUNK), :] = y * _sigmoid(y)

        urows = _row_shifter(upad[pl.ds(r, CONV_CHUNK + 2 * CONV_PAD), :])

        def ld(d):
            return urows(CONV_PAD + d)

        u = ld(0)
        sums = {}
        s = u + ld(-1)
        sums[2] = s
        s = s + ld(-2) + ld(1)
        sums[4] = s
        s = s + ld(-4) + ld(-3) + ld(2) + ld(3)
        sums[8] = s
        s = s + ld(-8) + ld(-7) + ld(-6) + ld(-5) + ld(4) + ld(5) + ld(6) + ld(7)
        sums[16] = s
        t = r + lax.broadcasted_iota(jnp.int32, (CONV_CHUNK, 1), 0)
        pooled = None
        for g, w in reversed(list(enumerate(POOL_WINDOWS))):
            lo = jnp.maximum(t - w // 2, 0)
            hi = jnp.minimum(t + (w - 1 - w // 2), seq_len - 1)
            mean = sums[w] / (hi - lo + 1).astype(F32)
            pooled = mean if pooled is None else jnp.where(lane < (g + 1) * D_GROUP_W, mean, pooled)
        dlt = (pooled - u).astype(BF16)
        od_ref[pl.ds(r, CONV_CHUNK), :] = _dot(dlt, wbd_ref[...]) * dsc_ref[...]
        return carry

    lax.fori_loop(0, n_chunks, body, 0)


def _conv_pool(z, seq_len, n_seq, row0, dww, dwb, lng, lnb, wbd, dsc):
    blk0 = row0 // seq_len

    def zspec(col):
        return pl.BlockSpec((seq_len, 256), lambda b: (blk0 + b, col // 256))

    def const(shape):
        return pl.BlockSpec(shape, lambda b: (0, 0))

    out_spec = pl.BlockSpec((seq_len, BRANCH_W), lambda b: (b, 0))
    return pl.pallas_call(
        functools.partial(_convpool_kernel, seq_len=seq_len),
        name=f"conv_pool_{seq_len}",
        grid=(n_seq,),
        in_specs=[
            zspec(COL_ZC), zspec(COL_ZC + BRANCH_W), zspec(COL_ZD),
            const((C_CONV_WIDTH, BRANCH_W)), const((1, BRANCH_W)), const((1, BRANCH_W)),
            const((1, BRANCH_W)), const((BRANCH_W, BRANCH_W)), const((1, BRANCH_W)),
        ],
        out_specs=[out_spec, out_spec],
        out_shape=[jax.ShapeDtypeStruct((n_seq * seq_len, BRANCH_W), F32)] * 2,
        scratch_shapes=[pltpu.VMEM((seq_len + 2 * CONV_PAD, BRANCH_W), F32)] * 2,
        compiler_params=pltpu.CompilerParams(
            dimension_semantics=("arbitrary",), vmem_limit_bytes=VMEM_LIMIT),
    )(z, z, z, dww, dwb, lng, lnb, wbd, dsc)


def _mix_kernel(x_ref, h_ref, oa_c, ob_c, oc_c, od_c, oa_l, ob_l, oc_l, od_l,
                mod_ref, wg_ref, wbr_ref, wout_ref, gpost_ref, gpre2_ref, rw_ref, rb_ref,
                x1_ref, h2_ref, eid_ref, rank_ref, wts_ref, cnt_ref, carry):
    i = pl.program_id(0)
    is_ctx = i < CTX_TILES

    @pl.when(i == 0)
    def _():
        carry[...] = jnp.zeros_like(carry)

    m = mod_ref[0]
    g1 = m[:, 2 * D_MODEL:3 * D_MODEL]
    sh2 = m[:, 3 * D_MODEL:4 * D_MODEL]
    sc2 = m[:, 4 * D_MODEL:5 * D_MODEL]

    hb = h_ref[...]
    merged = None
    for n, (c_ref, l_ref) in enumerate(((oa_c, oa_l), (ob_c, ob_l), (oc_c, oc_l), (od_c, od_l))):
        gate = _sigmoid(_dot(hb, wg_ref[:, n * D_MODEL:(n + 1) * D_MODEL]))
        o = jnp.where(is_ctx, c_ref[...], l_ref[...])
        br = _dot(o.astype(BF16), wbr_ref[n])
        merged = gate * br if merged is None else merged + gate * br
    y = _dot(merged.astype(BF16), wout_ref[...])
    x1 = x_ref[...] + g1 * _rms(y, gpost_ref[...])
    x1_ref[...] = x1
    h2 = _rms(x1, gpre2_ref[...]) * (1.0 + sc2) + sh2
    h2_ref[...] = h2

    logits = _dot3(h2, rw_ref[...]) + rb_ref[...]
    iota_e = lax.broadcasted_iota(jnp.int32, (TM, N_EXPERTS), 1)
    rem = logits
    vals, idxs = [], []
    for _ in range(TOP_K):
        mx = jnp.max(rem, axis=-1, keepdims=True)
        idx = jnp.min(jnp.where(rem == mx, iota_e, N_EXPERTS), axis=-1, keepdims=True)
        vals.append(mx)
        idxs.append(idx)
        rem = jnp.where(iota_e == idx, -jnp.inf, rem)
    exps = [jnp.exp(v - vals[0]) for v in vals]
    den = exps[0]
    for e in exps[1:]:
        den = den + e

    sel = [iota_e == idx for idx in idxs]
    member = jnp.zeros((TM, N_EXPERTS), F32)
    for s in sel:
        member = member + jnp.where(s, 1.0, 0.0)
    row = lax.broadcasted_iota(jnp.int32, (TM, TM), 0)
    col = lax.broadcasted_iota(jnp.int32, (TM, TM), 1)
    before = jnp.where(col < row, 1.0, 0.0).astype(BF16)
    seen = _dot(before, member.astype(BF16)) + carry[...]
    carry[...] = carry[...] + jnp.sum(member, axis=0, keepdims=True)
    cnt_ref[...] = carry[...]

    lane = lax.broadcasted_iota(jnp.int32, (TM, OUT_LANES), 1)
    eid = jnp.zeros((TM, OUT_LANES), jnp.int32)
    rank = jnp.zeros((TM, OUT_LANES), jnp.int32)
    wts = jnp.zeros((TM, OUT_LANES), F32)
    for k in range(TOP_K):
        rk = jnp.sum(jnp.where(sel[k], seen, 0.0), axis=-1, keepdims=True).astype(jnp.int32)
        eid = jnp.where(lane == k, idxs[k], eid)
        rank = jnp.where(lane == k, rk, rank)
        wts = jnp.where(lane == k, exps[k] / den, wts)
    eid_ref[...] = eid
    rank_ref[...] = rank
    wts_ref[...] = wts


def _mix(x, h, o_ctx, o_lat, mods, l, w_gate, w_branch, w_out, g_post, g_pre2, r_w, r_b):
    const2 = lambda i: (0, 0)
    tile = lambda width: pl.BlockSpec((TM, width), lambda i: (i, 0))
    ctx_tile = pl.BlockSpec((TM, BRANCH_W), lambda i: (jnp.minimum(i, CTX_TILES - 1), 0))
    lat_tile = pl.BlockSpec((TM, BRANCH_W), lambda i: (jnp.maximum(i - CTX_TILES, 0), 0))
    single = pl.Buffered(1)
    return pl.pallas_call(
        _mix_kernel,
        name="mix_router",
        grid=(N_TILES,),
        in_specs=[
            tile(D_MODEL), tile(D_MODEL),
            ctx_tile, ctx_tile, ctx_tile, ctx_tile, lat_tile, lat_tile, lat_tile, lat_tile,
            pl.BlockSpec((1, 1, 6 * D_MODEL), lambda i: (l * MOD_ROWS + _mod_row(i), 0, 0)),
            pl.BlockSpec((D_MODEL, N_GATE), const2, pipeline_mode=single),
            pl.BlockSpec((N_BRANCH, BRANCH_W, D_MODEL), lambda i: (0, 0, 0), pipeline_mode=single),
            pl.BlockSpec((D_MODEL, D_MODEL), const2, pipeline_mode=single),
            pl.BlockSpec((1, D_MODEL), const2),
            pl.BlockSpec((1, D_MODEL), const2),
            pl.BlockSpec((D_MODEL, N_EXPERTS), const2),
            pl.BlockSpec((1, N_EXPERTS), const2),
        ],
        out_specs=[
            tile(D_MODEL), tile(D_MODEL),
            tile(OUT_LANES), tile(OUT_LANES), tile(OUT_LANES),
            pl.BlockSpec((1, N_EXPERTS), const2),
        ],
        out_shape=[
            jax.ShapeDtypeStruct((N_TOK, D_MODEL), F32),
            jax.ShapeDtypeStruct((N_TOK, D_MODEL), F32),
            jax.ShapeDtypeStruct((N_TOK, OUT_LANES), jnp.int32),
            jax.ShapeDtypeStruct((N_TOK, OUT_LANES), jnp.int32),
            jax.ShapeDtypeStruct((N_TOK, OUT_LANES), F32),
            jax.ShapeDtypeStruct((1, N_EXPERTS), F32),
        ],
        scratch_shapes=[pltpu.VMEM((1, N_EXPERTS), F32)],
        compiler_params=pltpu.CompilerParams(
            dimension_semantics=("arbitrary",), vmem_limit_bytes=VMEM_LIMIT),
    )(x, h, *o_ctx, *o_lat, mods, w_gate, w_branch, w_out, g_post, g_pre2, r_w, r_b)


def _row_copy(src_ref, src_row, dst_ref, dst_row, sem):
    return pltpu.make_async_copy(src_ref.at[pl.ds(src_row, 1), :], dst_ref.at[pl.ds(dst_row, 1), :], sem)


DISPATCH_ROWS = 512
DISPATCH_STEPS = N_TOK // DISPATCH_ROWS
DMA_UNROLL = 8


def _dispatch_kernel(starts_ref, padded_ref, pos_ref, h2_ref, xs_ref, zbuf, sem, zsem, tsem):
    i = pl.program_id(0)

    def pad_copy(e):
        last = pl.multiple_of(starts_ref[e] + padded_ref[e] - TE, TE)
        return pltpu.make_async_copy(zbuf, xs_ref.at[pl.ds(last, TE), :], zsem)

    first_tail = (starts_ref[N_EXPERTS - 1] + padded_ref[N_EXPERTS - 1]) // TE

    def tail_copy(j):
        return pltpu.make_async_copy(zbuf, xs_ref.at[pl.ds(pl.multiple_of(j * TE, TE), TE), :], tsem)

    def tail_start(j, carry):
        tail_copy(j).start()
        return carry

    def tail_wait(j, carry):
        tail_copy(j).wait()
        return carry

    @pl.when(i == 0)
    def _():
        zbuf[...] = jnp.zeros_like(zbuf)
        for e in range(N_EXPERTS):
            @pl.when(padded_ref[e] > 0)
            def _():
                pad_copy(e).start()
        lax.fori_loop(first_tail, E_TILES, tail_start, 0)
        for e in range(N_EXPERTS):
            @pl.when(padded_ref[e] > 0)
            def _():
                pad_copy(e).wait()

    @pl.when(i == DISPATCH_STEPS - 1)
    def _():
        lax.fori_loop(first_tail, E_TILES, tail_wait, 0)

    def issue(t, carry):
        for k in range(TOP_K):
            _row_copy(h2_ref, t, xs_ref, pos_ref[t * TOP_K + k], sem).start(priority=k % 2)
        return carry

    lax.fori_loop(0, DISPATCH_ROWS, issue, 0, unroll=DMA_UNROLL)

    for _ in range(TOP_K):
        pltpu.make_async_copy(h2_ref, xs_ref.at[pl.ds(0, DISPATCH_ROWS), :], sem).wait()


def _dispatch(starts, padded, pos, h2):
    grid_spec = pltpu.PrefetchScalarGridSpec(
        num_scalar_prefetch=2,
        grid=(DISPATCH_STEPS,),
        in_specs=[
            pl.BlockSpec((DISPATCH_ROWS * TOP_K,), lambda i, st, pd: (i,), memory_space=pltpu.SMEM),
            pl.BlockSpec((DISPATCH_ROWS, D_MODEL), lambda i, st, pd: (i, 0)),
        ],
        out_specs=pl.BlockSpec(memory_space=pl.ANY),
        scratch_shapes=[pltpu.VMEM((TE, D_MODEL), F32),
                        pltpu.SemaphoreType.DMA(()), pltpu.SemaphoreType.DMA(()),
                        pltpu.SemaphoreType.DMA(())],
    )
    return pl.pallas_call(
        _dispatch_kernel,
        name="dispatch",
        grid_spec=grid_spec,
        out_shape=jax.ShapeDtypeStruct((P_MAX, D_MODEL), F32),
        compiler_params=pltpu.CompilerParams(
            dimension_semantics=("arbitrary",), vmem_limit_bytes=VMEM_LIMIT),
    )(starts, padded, pos, h2)


W_CAST_ROWS = 128


def _moe_kernel(te_ref, valid_ref, x_ref, w1_ref, b1_ref, w2_ref, b2_ref, y_ref, w1b, w2b):
    j = pl.program_id(0)
    e = te_ref[j]
    prev = te_ref[jnp.maximum(j - 1, 0)]
    valid = valid_ref[j] == 1
    new_expert = jnp.logical_or(j == 0, e != prev)

    @pl.when(jnp.logical_and(valid, new_expert))
    def _():
        def cast(c, carry):
            r = pl.multiple_of(c * W_CAST_ROWS, W_CAST_ROWS)
            w1b[pl.ds(r, W_CAST_ROWS), :] = w1_ref[0, 0, pl.ds(r, W_CAST_ROWS), :].astype(BF16)
            w2b[pl.ds(r, W_CAST_ROWS), :] = w2_ref[0, 0, pl.ds(r, W_CAST_ROWS), :].astype(BF16)
            return carry
        lax.fori_loop(0, D_MODEL // W_CAST_ROWS, cast, 0)

    @pl.when(valid)
    def _():
        x = x_ref[...].astype(BF16)
        u = _dot(x, w1b[...]) + b1_ref[0, 0]
        xg = jnp.minimum(u[:, :D_FF], SWIGLU_LIMIT)
        xl = jnp.clip(u[:, D_FF:], -SWIGLU_LIMIT, SWIGLU_LIMIT)
        act = (xl + 1.0) * (xg * _sigmoid(SWIGLU_ALPHA * xg))
        y_ref[...] = _dot(act.astype(BF16), w2b[...]) + b2_ref[0, 0]

    @pl.when(jnp.logical_not(valid))
    def _():
        y_ref[...] = jnp.zeros_like(y_ref)


def _expert_ffn(tile_expert, tile_valid, xs, l, e_w1, e_b1, e_w2, e_b2):
    L = e_w1.shape[0]
    grid_spec = pltpu.PrefetchScalarGridSpec(
        num_scalar_prefetch=2,
        grid=(E_TILES,),
        in_specs=[
            pl.BlockSpec((TE, D_MODEL), lambda j, te, va: (j * va[j], 0)),
            pl.BlockSpec((1, 1, D_MODEL, 2 * D_FF), lambda j, te, va: (l, te[j], 0, 0)),
            pl.BlockSpec((1, 1, 1, 2 * D_FF), lambda j, te, va: (l, te[j], 0, 0)),
            pl.BlockSpec((1, 1, D_FF, D_MODEL), lambda j, te, va: (l, te[j], 0, 0)),
            pl.BlockSpec((1, 1, 1, D_MODEL), lambda j, te, va: (l, te[j], 0, 0)),
        ],
        out_specs=pl.BlockSpec((TE, D_MODEL), lambda j, te, va: (j, 0)),
        scratch_shapes=[pltpu.VMEM((D_MODEL, 2 * D_FF), BF16), pltpu.VMEM((D_FF, D_MODEL), BF16)],
    )
    return pl.pallas_call(
        _moe_kernel,
        name="expert_ffn",
        grid_spec=grid_spec,
        out_shape=jax.ShapeDtypeStruct((P_MAX, D_MODEL), F32),
        compiler_params=pltpu.CompilerParams(
            dimension_semantics=("arbitrary",), vmem_limit_bytes=VMEM_LIMIT),
    )(tile_expert, tile_valid, xs, e_w1,
      e_b1.reshape(L, N_EXPERTS, 1, 2 * D_FF), e_w2, e_b2.reshape(L, N_EXPERTS, 1, D_MODEL))


def _combine_kernel(pos_ref, pos_next_ref, ys_ref, wts_ref, x1_ref, mod_ref, gpost_ref, out_ref,
                    buf, sem):
    i = pl.program_id(0)
    slot = i % 2

    def gather(p_ref, s):
        def issue(t, carry):
            for k in range(TOP_K):
                _row_copy(ys_ref, p_ref[t * TOP_K + k], buf.at[s, k], t, sem.at[s]).start(priority=k % 2)
            return carry
        lax.fori_loop(0, TM, issue, 0, unroll=DMA_UNROLL)

    @pl.when(i == 0)
    def _():
        gather(pos_ref, 0)

    @pl.when(i + 1 < N_TILES)
    def _():
        gather(pos_next_ref, 1 - slot)

    for k in range(TOP_K):
        pltpu.make_async_copy(ys_ref.at[pl.ds(0, TM), :], buf.at[slot, k], sem.at[slot]).wait()

    w = wts_ref[...]
    y = w[:, 0:1] * buf[slot, 0]
    for k in range(1, TOP_K):
        y = y + w[:, k:k + 1] * buf[slot, k]
    g2 = mod_ref[0][:, 5 * D_MODEL:6 * D_MODEL]
    out_ref[...] = x1_ref[...] + g2 * _rms(y, gpost_ref[...])


def _combine(pos, ys, wts, x1, mods, l, g_post2):
    return pl.pallas_call(
        _combine_kernel,
        name="combine",
        grid=(N_TILES,),
        in_specs=[
            pl.BlockSpec((TM * TOP_K,), lambda i: (i,), memory_space=pltpu.SMEM),
            pl.BlockSpec((TM * TOP_K,), lambda i: (jnp.minimum(i + 1, N_TILES - 1),),
                         memory_space=pltpu.SMEM),
            pl.BlockSpec(memory_space=pl.ANY),
            pl.BlockSpec((TM, OUT_LANES), lambda i: (i, 0)),
            pl.BlockSpec((TM, D_MODEL), lambda i: (i, 0)),
            pl.BlockSpec((1, 1, 6 * D_MODEL), lambda i: (l * MOD_ROWS + _mod_row(i), 0, 0)),
            pl.BlockSpec((1, D_MODEL), lambda i: (0, 0)),
        ],
        out_specs=pl.BlockSpec((TM, D_MODEL), lambda i: (i, 0)),
        out_shape=jax.ShapeDtypeStruct((N_TOK, D_MODEL), F32),
        scratch_shapes=[pltpu.VMEM((2, TOP_K, TM, D_MODEL), F32), pltpu.SemaphoreType.DMA((2,))],
        compiler_params=pltpu.CompilerParams(
            dimension_semantics=("arbitrary",), vmem_limit_bytes=VMEM_LIMIT),
    )(pos, pos, ys, wts, x1, mods, g_post2)


def _rope_tables():
    t = np.arange(DEC_SEQ)
    r = (t // GRID_W).astype(np.float32)
    c = (t % GRID_W).astype(np.float32)

    def table(dim, reps):
        n_axis = dim // 4
        inv = (ROPE_THETA ** (-np.arange(n_axis, dtype=np.float32) / n_axis)).astype(np.float32)
        ang = np.concatenate([r[:, None] * inv, c[:, None] * inv], axis=-1).astype(np.float32)
        cos, sin = np.cos(ang), np.sin(ang)
        return (np.tile(np.concatenate([cos, cos], -1), (1, reps)),
                np.tile(np.concatenate([-sin, sin], -1), (1, reps)))

    cos_a, sin_a = table(A_HEAD_DIM, 256 // A_HEAD_DIM)
    cos_b, sin_b = table(B_HALF_DIM, 256 // B_HALF_DIM)
    return tuple(jnp.asarray(a, F32) for a in (cos_a, sin_a, cos_b, sin_b))


def _group_mean_matrix():
    idx = np.arange(256) // A_HEAD_DIM
    return jnp.asarray((idx[:, None] == idx[None, :]).astype(np.float32) / A_HEAD_DIM, BF16)


def _block_diag(w):
    out = jnp.zeros((BRANCH_W, BRANCH_W), w.dtype)
    for g in range(D_GROUPS):
        out = out.at[g * D_GROUP_W:(g + 1) * D_GROUP_W, g * D_GROUP_W:(g + 1) * D_GROUP_W].set(w[g])
    return out


def _routing_tables(eid, rank, cnt):
    cnt = cnt.reshape(N_EXPERTS).astype(jnp.int32)
    padded = ((cnt + TE - 1) // TE) * TE
    ends = jnp.cumsum(padded)
    starts = ends - padded
    experts = jnp.arange(N_EXPERTS, dtype=jnp.int32)
    start_of = jnp.sum(jnp.where(eid[:, :TOP_K, None] == experts, starts, 0), axis=-1)
    pos = (start_of + rank[:, :TOP_K]).reshape(N_SLOTS)
    tile_row = jnp.arange(E_TILES, dtype=jnp.int32) * TE
    tile_expert = jnp.sum((tile_row[:, None] >= ends[None, :]).astype(jnp.int32), axis=-1)
    tile_expert = jnp.minimum(tile_expert, N_EXPERTS - 1)
    tile_valid = (tile_row < ends[-1]).astype(jnp.int32)
    return (pos.astype(jnp.int32), tile_expert.astype(jnp.int32), tile_valid,
            starts.astype(jnp.int32), padded.astype(jnp.int32))


def kernel(x_prompt, x_sample, cache_a_k, cache_a_v, cache_b_k, cache_b_v, c, c_ctx, w_mod, b_mod, g_pre1, g_post1, g_pre2, g_post2, w_in, a_q_gain, a_k_gain, b_lambda, b_subln_gain, c_dw_w, c_dw_b, c_ln_g, c_ln_b, d_w_group, d_scale, w_branch, w_out, r_w, r_b, e_w1, e_b1, e_w2, e_b2):
    x = jnp.concatenate([x_prompt.reshape(N_CTX, D_MODEL), x_sample.reshape(N_LAT, D_MODEL)], axis=0)
    cond8 = jnp.concatenate(
        [c_ctx[None, :], c, jnp.zeros((MOD_ROWS - 1 - DEC_BATCH, D_MODEL), F32)], axis=0)
    mods = _modulation(cond8, w_mod, b_mod).reshape(DEPTH * MOD_ROWS, 1, 6 * D_MODEL)

    rope = _rope_tables()
    gmat = _group_mean_matrix()
    caches = (
        cache_a_k.reshape(DEC_BATCH, DEPTH, PAST_LEN, A_KV_HEADS * A_HEAD_DIM),
        cache_a_v.reshape(DEC_BATCH, DEPTH, PAST_LEN, A_KV_HEADS * A_HEAD_DIM),
        cache_b_k.reshape(DEC_BATCH, DEPTH, PAST_LEN, B_HEADS * 2 * B_HALF_DIM),
        cache_b_v.reshape(DEC_BATCH, DEPTH, PAST_LEN, B_HEADS * B_V_DIM),
    )

    new_ak, new_av, new_bk, new_bv = [], [], [], []
    row = lambda v: v.reshape(1, -1)
    for l in range(DEPTH):
        lambda_init = 0.8 - 0.6 * math.exp(-0.3 * l)
        w_small = w_in[l, :, :N_SMALL].astype(BF16)
        w_gate = w_in[l, :, N_SMALL:].astype(BF16)
        z, h = _in_projection(
            x, mods, l, row(g_pre1[l]), w_small,
            row(jnp.tile(a_q_gain[l], A_HEADS)), row(jnp.tile(a_k_gain[l], A_KV_HEADS)), gmat, rope)

        zc = z[:N_CTX]
        new_ak.append(zc[:, COL_KA:COL_VA].reshape(BATCH, SEQ, A_KV_HEADS, A_HEAD_DIM))
        new_av.append(zc[:, COL_VA:COL_QB].reshape(BATCH, SEQ, A_KV_HEADS, A_HEAD_DIM))
        new_bk.append(zc[:, COL_KB:COL_VB].reshape(BATCH, SEQ, B_HEADS, 2, B_HALF_DIM))
        new_bv.append(zc[:, COL_VB:COL_ZC].reshape(BATCH, SEQ, B_HEADS, B_V_DIM))

        subg = row(b_subln_gain[l])
        oa_c, ob_c = _attention_ctx(z, b_lambda[l], subg, lambda_init)
        oa_l, ob_l = _attention_lat(z, caches, l, b_lambda[l], subg, lambda_init)
        conv_args = (c_dw_w[l], row(c_dw_b[l]), row(c_ln_g[l]), row(c_ln_b[l]),
                     _block_diag(d_w_group[l]).astype(BF16), row(d_scale[l]))
        oc_c, od_c = _conv_pool(z, SEQ, BATCH, 0, *conv_args)
        oc_l, od_l = _conv_pool(z, DEC_SEQ, DEC_BATCH, N_CTX, *conv_args)

        x1, h2, eid, rank, wts, cnt = _mix(
            x, h, (oa_c, ob_c, oc_c, od_c), (oa_l, ob_l, oc_l, od_l), mods, l,
            w_gate, w_branch[l].astype(BF16), w_out[l].astype(BF16),
            row(g_post1[l]), row(g_pre2[l]), r_w[l], row(r_b[l]))

        pos, tile_expert, tile_valid, starts, padded = _routing_tables(eid, rank, cnt)
        xs = _dispatch(starts, padded, pos, h2)
        ys = _expert_ffn(tile_expert, tile_valid, xs, l, e_w1, e_b1, e_w2, e_b2)
        x = _combine(pos, ys, wts, x1, mods, l, row(g_post2[l]))

    y_prompt = x[:N_CTX].reshape(BATCH, SEQ, D_MODEL)
    y_sample = x[N_CTX:].reshape(DEC_BATCH, DEC_SEQ, D_MODEL)
    return (y_prompt, y_sample, jnp.stack(new_ak, axis=1), jnp.stack(new_av, axis=1),
            jnp.stack(new_bk, axis=1), jnp.stack(new_bv, axis=1))
```

```python
import functools
import math

import jax
import jax.numpy as jnp
import numpy as np
from jax import lax
from jax.experimental import pallas as pl
from jax.experimental.pallas import tpu as pltpu

F32 = jnp.float32
BF16 = jnp.bfloat16

D_MODEL = 1024
BATCH = 16
SEQ = 256
DEPTH = 2
DEC_BATCH = 4
DEC_SEQ = 2048
PAST_LEN = 256
GRID_W = 64
ROPE_THETA = 10000.0
N_BRANCH = 4
BRANCH_W = D_MODEL // 4
A_HEADS = 4
A_KV_HEADS = 2
A_HEAD_DIM = 64
B_HEADS = 4
B_HALF_DIM = 32
B_V_DIM = 2 * B_HALF_DIM
C_CONV_WIDTH = 31
D_GROUPS = 4
D_GROUP_W = BRANCH_W // D_GROUPS
POOL_WINDOWS = (2, 4, 8, 16)
N_EXPERTS = 32
TOP_K = 4
D_FF = D_MODEL
SWIGLU_LIMIT = 7.0
SWIGLU_ALPHA = 1.702
EPS = 1e-6

N_CTX = BATCH * SEQ
N_LAT = DEC_BATCH * DEC_SEQ
N_TOK = N_CTX + N_LAT

COL_QA, COL_KA, COL_VA = 0, 256, 384
COL_QB, COL_KB, COL_VB = 512, 768, 1024
COL_ZC, COL_ZD = 1280, 1792
N_SMALL = 2048
N_GATE = N_BRANCH * D_MODEL

TM = 256
N_TILES = N_TOK // TM
CTX_TILES = N_CTX // TM
LAT_TILES_PER_SEQ = DEC_SEQ // TM
MOD_ROWS = 8

TE = 512
N_SLOTS = N_TOK * TOP_K
P_MAX = N_SLOTS + N_EXPERTS * TE
E_TILES = P_MAX // TE

TQ = 256
CONV_PAD = 16
CONV_CHUNK = 128

VMEM_LIMIT = 56 * 1024 * 1024


def _sigmoid(x):
    return 1.0 / (1.0 + jnp.exp(-x))


def _split_bf16(a):
    hi = a.astype(BF16)
    lo = (a - hi.astype(F32)).astype(BF16)
    return hi, lo


def _dot(a, b):
    return jnp.dot(a, b, preferred_element_type=F32)


def _dot_nt(a, b):
    return lax.dot_general(a, b, (((1,), (1,)), ((), ())), preferred_element_type=F32)


def _dot3(a, b):
    ah, al = _split_bf16(a)
    bh, bl = _split_bf16(b)
    return _dot(ah, bh) + _dot(ah, bl) + _dot(al, bh)


def _rms(x, g):
    return x * lax.rsqrt(jnp.mean(x * x, axis=-1, keepdims=True) + EPS) * g


def _mod_row(i, ctx_tiles=CTX_TILES, tiles_per_seq=LAT_TILES_PER_SEQ):
    return jnp.where(i < ctx_tiles, 0, 1 + (i - ctx_tiles) // tiles_per_seq)


MOD_TN = 1536


def _mod_kernel(cond_ref, w_ref, b_ref, o_ref):
    c = cond_ref[...]
    s = c * _sigmoid(c)
    o_ref[0] = _dot3(s, w_ref[0]) + b_ref[0]


def _modulation(cond8, w_mod, b_mod):
    L = w_mod.shape[0]
    return pl.pallas_call(
        _mod_kernel,
        name="modulation",
        grid=(L, 6 * D_MODEL // MOD_TN),
        in_specs=[
            pl.BlockSpec((MOD_ROWS, D_MODEL), lambda l, j: (0, 0)),
            pl.BlockSpec((1, D_MODEL, MOD_TN), lambda l, j: (l, 0, j)),
            pl.BlockSpec((1, 1, MOD_TN), lambda l, j: (l, 0, j)),
        ],
        out_specs=pl.BlockSpec((1, MOD_ROWS, MOD_TN), lambda l, j: (l, 0, j)),
        out_shape=jax.ShapeDtypeStruct((L, MOD_ROWS, 6 * D_MODEL), F32),
        compiler_params=pltpu.CompilerParams(
            dimension_semantics=("arbitrary", "arbitrary"), vmem_limit_bytes=VMEM_LIMIT),
    )(cond8, w_mod, b_mod.reshape(L, 1, 6 * D_MODEL))


def _group_mean(sq, gmat):
    hi, lo = _split_bf16(sq)
    return _dot(hi, gmat) + _dot(lo, gmat)


def _swap_halves(x, half):
    w = x.shape[-1]
    lane = lax.broadcasted_iota(jnp.int32, x.shape, 1)
    first = (lane % (2 * half)) < half
    return jnp.where(first, pltpu.roll(x, w - half, 1), pltpu.roll(x, half, 1))


def _inproj_kernel(x_ref, mod_ref, gpre_ref, w_ref, gq_ref, gk_ref, gmat_ref,
                   cos_a_ref, sin_a_ref, cos_b_ref, sin_b_ref, z_ref, h_ref):
    i = pl.program_id(0)
    m = mod_ref[0]
    sh1 = m[:, 0:D_MODEL]
    sc1 = m[:, D_MODEL:2 * D_MODEL]
    h = _rms(x_ref[...], gpre_ref[...]) * (1.0 + sc1) + sh1
    hb = h.astype(BF16)
    h_ref[...] = hb
    z = _dot(hb, w_ref[...])

    gmat = gmat_ref[...]
    qa = z[:, COL_QA:COL_KA]
    ka = z[:, COL_KA:COL_VA]
    qa = qa * lax.rsqrt(_group_mean(qa * qa, gmat) + EPS) * gq_ref[...]
    ka = ka * lax.rsqrt(_group_mean(ka * ka, gmat[0:128, 0:128]) + EPS) * gk_ref[...]
    qb = z[:, COL_QB:COL_KB]
    kb = z[:, COL_KB:COL_VB]
    z_ref[:, COL_VA:COL_QB] = z[:, COL_VA:COL_QB]
    z_ref[:, COL_VB:N_SMALL] = z[:, COL_VB:N_SMALL]

    @pl.when(i < CTX_TILES)
    def _():
        z_ref[:, COL_QA:COL_KA] = qa
        z_ref[:, COL_KA:COL_VA] = ka
        z_ref[:, COL_QB:COL_KB] = qb
        z_ref[:, COL_KB:COL_VB] = kb

    @pl.when(i >= CTX_TILES)
    def _():
        cos_a = cos_a_ref[...]
        sin_a = sin_a_ref[...]
        cos_b = cos_b_ref[...]
        sin_b = sin_b_ref[...]
        ha, hb_ = A_HEAD_DIM // 2, B_HALF_DIM // 2
        z_ref[:, COL_QA:COL_KA] = qa * cos_a + _swap_halves(qa, ha) * sin_a
        z_ref[:, COL_KA:COL_VA] = ka * cos_a[:, 0:128] + _swap_halves(ka, ha) * sin_a[:, 0:128]
        z_ref[:, COL_QB:COL_KB] = qb * cos_b + _swap_halves(qb, hb_) * sin_b
        z_ref[:, COL_KB:COL_VB] = kb * cos_b + _swap_halves(kb, hb_) * sin_b


def _rope_block(i):
    return jnp.where(i < CTX_TILES, 0, (i - CTX_TILES) % LAT_TILES_PER_SEQ)


def _in_projection(x, mods, l, g_pre, w_small, gq, gk, gmat, rope):
    const = lambda i: (0, 0)
    rope_spec = pl.BlockSpec((TM, 256), lambda i: (_rope_block(i), 0))
    return pl.pallas_call(
        _inproj_kernel,
        name="in_projection",
        grid=(N_TILES,),
        in_specs=[
            pl.BlockSpec((TM, D_MODEL), lambda i: (i, 0)),
            pl.BlockSpec((1, 1, 6 * D_MODEL), lambda i: (l * MOD_ROWS + _mod_row(i), 0, 0)),
            pl.BlockSpec((1, D_MODEL), const),
            pl.BlockSpec((D_MODEL, N_SMALL), const),
            pl.BlockSpec((1, 256), const),
            pl.BlockSpec((1, 128), const),
            pl.BlockSpec((256, 256), const),
            rope_spec, rope_spec, rope_spec, rope_spec,
        ],
        out_specs=[
            pl.BlockSpec((TM, N_SMALL), lambda i: (i, 0)),
            pl.BlockSpec((TM, D_MODEL), lambda i: (i, 0)),
        ],
        out_shape=[
            jax.ShapeDtypeStruct((N_TOK, N_SMALL), F32),
            jax.ShapeDtypeStruct((N_TOK, D_MODEL), BF16),
        ],
        compiler_params=pltpu.CompilerParams(
            dimension_semantics=("arbitrary",), vmem_limit_bytes=VMEM_LIMIT),
    )(x, mods, g_pre, w_small, gq, gk, gmat, *rope)


LOG2E = 1.4426950408889634
HALF_LANES = 64


def _attend(q, ks, vexts):
    ss = [_dot_nt(q, k) for k in ks]
    m = ss[0].max(axis=-1, keepdims=True)
    for s in ss[1:]:
        m = jnp.maximum(m, s.max(axis=-1, keepdims=True))
    o = None
    for s, v in zip(ss, vexts):
        t = _dot(jnp.exp2(s - m).astype(BF16), v)
        o = t if o is None else o + t
    return o


def _value_ext(v128, upper):
    lane = lax.broadcasted_iota(jnp.int32, v128.shape, 1)
    keep = (lane >= HALF_LANES) if upper else (lane < HALF_LANES)
    return jnp.where(keep, v128, 1.0).astype(BF16)


def _normalised(o, upper):
    if upper:
        return o[:, HALF_LANES:] / o[:, 0:1]
    return o[:, :HALF_LANES] / o[:, HALF_LANES:HALF_LANES + 1]


def _attn_kernel(*refs, has_cache, lambda_init):
    if has_cache:
        (qa_ref, ka_ref, va_ref, qb_ref, kb_ref, vb_ref,
         cka_ref, cva_ref, ckb_ref, cvb_ref, lam_ref, subg_ref, oa_ref, ob_ref) = refs
    else:
        (qa_ref, ka_ref, va_ref, qb_ref, kb_ref, vb_ref,
         lam_ref, subg_ref, oa_ref, ob_ref) = refs

    def cols(ref, lo, hi):
        return ref[:, lo:hi].astype(BF16)

    def ccols(ref, lo, hi):
        return ref[0, 0, :, lo:hi].astype(BF16)

    scale_a = A_HEAD_DIM ** -0.5 * LOG2E
    group = A_HEADS // A_KV_HEADS
    for g in range(A_KV_HEADS):
        lo, hi = g * A_HEAD_DIM, (g + 1) * A_HEAD_DIM
        upper = g % 2 == 1
        ks = [cols(ka_ref, lo, hi)]
        vexts = [_value_ext(va_ref[...], upper)]
        if has_cache:
            ks.append(ccols(cka_ref, lo, hi))
            vexts.append(_value_ext(cva_ref[0, 0], upper))
        for hd in range(g * group, (g + 1) * group):
            q = (qa_ref[:, hd * A_HEAD_DIM:(hd + 1) * A_HEAD_DIM] * scale_a).astype(BF16)
            o = _attend(q, ks, vexts)
            oa_ref[:, hd * A_HEAD_DIM:(hd + 1) * A_HEAD_DIM] = _normalised(o, upper)

    bl = lam_ref[...]
    lam = (jnp.exp(jnp.sum(bl[0:1] * bl[1:2], axis=-1, keepdims=True))
           - jnp.exp(jnp.sum(bl[2:3] * bl[3:4], axis=-1, keepdims=True)) + lambda_init)
    scale_b = B_HALF_DIM ** -0.5 * LOG2E
    subg = subg_ref[...]
    for hd in range(B_HEADS):
        base = hd * 2 * B_HALF_DIM
        vlo, vhi = hd * B_V_DIM, (hd + 1) * B_V_DIM
        blk = (hd // 2) * 2 * B_V_DIM
        upper = hd % 2 == 1
        vexts = [_value_ext(vb_ref[:, blk:blk + 2 * B_V_DIM], upper)]
        if has_cache:
            vexts.append(_value_ext(cvb_ref[0, 0, :, blk:blk + 2 * B_V_DIM], upper))
        outs = []
        for c in range(2):
            lo, hi = base + c * B_HALF_DIM, base + (c + 1) * B_HALF_DIM
            q = (qb_ref[:, lo:hi] * scale_b).astype(BF16)
            ks = [cols(kb_ref, lo, hi)]
            if has_cache:
                ks.append(ccols(ckb_ref, lo, hi))
            outs.append(_normalised(_attend(q, ks, vexts), upper))
        o = outs[0] - lam * outs[1]
        o = _rms(o, subg) * (1.0 - lambda_init)
        ob_ref[:, vlo:vhi] = o


def _attention_ctx(z, b_lambda, subg, lambda_init):
    def zspec(width, col):
        return pl.BlockSpec((SEQ, width), lambda b: (b, col // width))
    out_spec = pl.BlockSpec((SEQ, BRANCH_W), lambda b: (b, 0))
    return pl.pallas_call(
        functools.partial(_attn_kernel, has_cache=False, lambda_init=lambda_init),
        name="attention_ctx",
        grid=(BATCH,),
        in_specs=[
            zspec(256, COL_QA), zspec(128, COL_KA), zspec(128, COL_VA),
            zspec(256, COL_QB), zspec(256, COL_KB), zspec(256, COL_VB),
            pl.BlockSpec((4, B_HALF_DIM), lambda b: (0, 0)),
            pl.BlockSpec((1, B_V_DIM), lambda b: (0, 0)),
        ],
        out_specs=[out_spec, out_spec],
        out_shape=[jax.ShapeDtypeStruct((N_CTX, BRANCH_W), F32)] * 2,
        compiler_params=pltpu.CompilerParams(
            dimension_semantics=("arbitrary",), vmem_limit_bytes=VMEM_LIMIT),
    )(z, z, z, z, z, z, b_lambda, subg)


def _attention_lat(z, caches, l, b_lambda, subg, lambda_init):
    q_tiles = DEC_SEQ // TQ
    seq0 = N_CTX // DEC_SEQ

    def qspec(col):
        return pl.BlockSpec((TQ, 256), lambda b, t: (CTX_TILES + b * q_tiles + t, col // 256))

    def kspec(width, col):
        return pl.BlockSpec((DEC_SEQ, width), lambda b, t: (seq0 + b, col // width))

    def cspec(width):
        return pl.BlockSpec((1, 1, PAST_LEN, width), lambda b, t: (b, l, 0, 0))

    out_spec = pl.BlockSpec((TQ, BRANCH_W), lambda b, t: (b * q_tiles + t, 0))
    cka, cva, ckb, cvb = caches
    return pl.pallas_call(
        functools.partial(_attn_kernel, has_cache=True, lambda_init=lambda_init),
        name="attention_lat",
        grid=(DEC_BATCH, q_tiles),
        in_specs=[
            qspec(COL_QA), kspec(128, COL_KA), kspec(128, COL_VA),
            qspec(COL_QB), kspec(256, COL_KB), kspec(256, COL_VB),
            cspec(128), cspec(128), cspec(256), cspec(256),
            pl.BlockSpec((4, B_HALF_DIM), lambda b, t: (0, 0)),
            pl.BlockSpec((1, B_V_DIM), lambda b, t: (0, 0)),
        ],
        out_specs=[out_spec, out_spec],
        out_shape=[jax.ShapeDtypeStruct((N_LAT, BRANCH_W), F32)] * 2,
        compiler_params=pltpu.CompilerParams(
            dimension_semantics=("arbitrary", "arbitrary"), vmem_limit_bytes=VMEM_LIMIT),
    )(z, z, z, z, z, z, cka, cva, ckb, cvb, b_lambda, subg)


SUBLANES = 8


def _row_shifter(win):
    span = win.shape[0] - SUBLANES
    shifted = {}

    def rows(off):
        s = off % SUBLANES
        if s not in shifted:
            shifted[s] = win[s:s + span, :]
        base = off - s
        return shifted[s][base:base + CONV_CHUNK, :]

    return rows


def _convpool_kernel(a_ref, b_ref, d_ref, dww_ref, dwb_ref, lng_ref, lnb_ref, wbd_ref, dsc_ref,
                     oc_ref, od_ref, hpad, upad, *, seq_len):
    n_chunks = seq_len // CONV_CHUNK
    zeros = jnp.zeros((CONV_PAD, BRANCH_W), F32)
    hpad[0:CONV_PAD, :] = zeros
    hpad[seq_len + CONV_PAD:seq_len + 2 * CONV_PAD, :] = zeros
    upad[0:CONV_PAD, :] = zeros
    upad[seq_len + CONV_PAD:seq_len + 2 * CONV_PAD, :] = zeros

    def fill(c, carry):
        r = pl.multiple_of(c * CONV_CHUNK, CONV_CHUNK)
        a = a_ref[pl.ds(r, CONV_CHUNK), :]
        b = b_ref[pl.ds(r, CONV_CHUNK), :]
        hpad[pl.ds(r + CONV_PAD, CONV_CHUNK), :] = a * _sigmoid(b)
        upad[pl.ds(r + CONV_PAD, CONV_CHUNK), :] = d_ref[pl.ds(r, CONV_CHUNK), :]
        return carry

    lax.fori_loop(0, n_chunks, fill, 0)

    lane = lax.broadcasted_iota(jnp.int32, (CONV_CHUNK, BRANCH_W), 1)
    half = C_CONV_WIDTH // 2

    def body(c, carry):
        r = pl.multiple_of(c * CONV_CHUNK, CONV_CHUNK)
        hrows = _row_shifter(hpad[pl.ds(r, CONV_CHUNK + 2 * CONV_PAD), :])
        acc = jnp.zeros((CONV_CHUNK, BRANCH_W), F32)
        for k in range(C_CONV_WIDTH):
            acc = acc + dww_ref[k:k + 1, :] * hrows(CONV_PAD - half + k)
        acc = acc + dwb_ref[...]
        mu = jnp.mean(acc, axis=-1, keepdims=True)
        cen = acc - mu
        var = jnp.mean(cen * cen, axis=-1, keepdims=True)
        y = cen * lax.rsqrt(var + EPS) * lng_ref[...] + lnb_ref[...]
        oc_ref[pl.ds(r, CONV_CHUNK), :] = y * _sigmoid(y)

        urows = _row_shifter(upad[pl.ds(r, CONV_CHUNK + 2 * CONV_PAD), :])

        def ld(d):
            return urows(CONV_PAD + d)

        u = ld(0)
        sums = {}
        s = u + ld(-1)
        sums[2] = s
        s = s + ld(-2) + ld(1)
        sums[4] = s
        s = s + ld(-4) + ld(-3) + ld(2) + ld(3)
        sums[8] = s
        s = s + ld(-8) + ld(-7) + ld(-6) + ld(-5) + ld(4) + ld(5) + ld(6) + ld(7)
        sums[16] = s
        t = r + lax.broadcasted_iota(jnp.int32, (CONV_CHUNK, 1), 0)
        pooled = None
        for g, w in reversed(list(enumerate(POOL_WINDOWS))):
            lo = jnp.maximum(t - w // 2, 0)
            hi = jnp.minimum(t + (w - 1 - w // 2), seq_len - 1)
            mean = sums[w] / (hi - lo + 1).astype(F32)
            pooled = mean if pooled is None else jnp.where(lane < (g + 1) * D_GROUP_W, mean, pooled)
        dlt = (pooled - u).astype(BF16)
        od_ref[pl.ds(r, CONV_CHUNK), :] = _dot(dlt, wbd_ref[...]) * dsc_ref[...]
        return carry

    lax.fori_loop(0, n_chunks, body, 0)


def _conv_pool(z, seq_len, n_seq, row0, dww, dwb, lng, lnb, wbd, dsc):
    blk0 = row0 // seq_len

    def zspec(col):
        return pl.BlockSpec((seq_len, 256), lambda b: (blk0 + b, col // 256))

    def const(shape):
        return pl.BlockSpec(shape, lambda b: (0, 0))

    out_spec = pl.BlockSpec((seq_len, BRANCH_W), lambda b: (b, 0))
    return pl.pallas_call(
        functools.partial(_convpool_kernel, seq_len=seq_len),
        name=f"conv_pool_{seq_len}",
        grid=(n_seq,),
        in_specs=[
            zspec(COL_ZC), zspec(COL_ZC + BRANCH_W), zspec(COL_ZD),
            const((C_CONV_WIDTH, BRANCH_W)), const((1, BRANCH_W)), const((1, BRANCH_W)),
            const((1, BRANCH_W)), const((BRANCH_W, BRANCH_W)), const((1, BRANCH_W)),
        ],
        out_specs=[out_spec, out_spec],
        out_shape=[jax.ShapeDtypeStruct((n_seq * seq_len, BRANCH_W), F32)] * 2,
        scratch_shapes=[pltpu.VMEM((seq_len + 2 * CONV_PAD, BRANCH_W), F32)] * 2,
        compiler_params=pltpu.CompilerParams(
            dimension_semantics=("arbitrary",), vmem_limit_bytes=VMEM_LIMIT),
    )(z, z, z, dww, dwb, lng, lnb, wbd, dsc)


MIX_TM = 512
MIX_TILES = N_TOK // MIX_TM
MIX_CTX_TILES = N_CTX // MIX_TM
MIX_LAT_PER_SEQ = DEC_SEQ // MIX_TM
ROUTE_ROWS = 8


def _mix_kernel(x_ref, h_ref, oa_c, ob_c, oc_c, od_c, oa_l, ob_l, oc_l, od_l,
                mod_ref, wg_ref, wbr_ref, wout_ref, gpost_ref, gpre2_ref, rwt_ref, rb_ref,
                x1_ref, h2_ref, eid_ref, rank_ref, wts_ref, cnt_ref, carry):
    i = pl.program_id(0)
    is_ctx = i < MIX_CTX_TILES

    @pl.when(i == 0)
    def _():
        carry[...] = jnp.zeros_like(carry)

    m = mod_ref[0]
    g1 = m[:, 2 * D_MODEL:3 * D_MODEL]
    sh2 = m[:, 3 * D_MODEL:4 * D_MODEL]
    sc2 = m[:, 4 * D_MODEL:5 * D_MODEL]

    hb = h_ref[...]
    merged = None
    for n, (c_ref, l_ref) in enumerate(((oa_c, oa_l), (ob_c, ob_l), (oc_c, oc_l), (od_c, od_l))):
        gate = _sigmoid(_dot(hb, wg_ref[:, n * D_MODEL:(n + 1) * D_MODEL]))
        o = jnp.where(is_ctx, c_ref[...], l_ref[...])
        br = _dot(o.astype(BF16), wbr_ref[n])
        merged = gate * br if merged is None else merged + gate * br
    y = _dot(merged.astype(BF16), wout_ref[...])
    x1 = x_ref[...] + g1 * _rms(y, gpost_ref[...])
    x1_ref[...] = x1
    h2 = _rms(x1, gpre2_ref[...]) * (1.0 + sc2) + sh2
    h2_ref[...] = h2

    rh, rl = _split_bf16(rwt_ref[...])
    hh, hl = _split_bf16(h2)
    logits = _dot_nt(rh, hh) + _dot_nt(rh, hl) + _dot_nt(rl, hh) + rb_ref[...]

    iota_e = lax.broadcasted_iota(jnp.int32, (N_EXPERTS, MIX_TM), 0)
    rem = logits
    vals, idxs = [], []
    for _ in range(TOP_K):
        mx = jnp.max(rem, axis=0, keepdims=True)
        idx = jnp.min(jnp.where(rem == mx, iota_e, N_EXPERTS), axis=0, keepdims=True)
        vals.append(mx)
        idxs.append(idx)
        rem = jnp.where(iota_e == idx, -jnp.inf, rem)
    exps = [jnp.exp(v - vals[0]) for v in vals]
    den = exps[0]
    for e in exps[1:]:
        den = den + e

    sel = [iota_e == idx for idx in idxs]
    member = jnp.zeros((N_EXPERTS, MIX_TM), F32)
    for s in sel:
        member = member + jnp.where(s, 1.0, 0.0)
    row = lax.broadcasted_iota(jnp.int32, (MIX_TM, MIX_TM), 0)
    col = lax.broadcasted_iota(jnp.int32, (MIX_TM, MIX_TM), 1)
    earlier = jnp.where(row < col, 1.0, 0.0).astype(BF16)
    seen = _dot(member.astype(BF16), earlier) + carry[...]
    carry[...] = carry[...] + jnp.sum(member, axis=1, keepdims=True)
    cnt_ref[...] = carry[...]

    sub = lax.broadcasted_iota(jnp.int32, (ROUTE_ROWS, MIX_TM), 0)
    eid = jnp.zeros((ROUTE_ROWS, MIX_TM), jnp.int32)
    rank = jnp.zeros((ROUTE_ROWS, MIX_TM), jnp.int32)
    wts = jnp.zeros((ROUTE_ROWS, MIX_TM), F32)
    for k in range(TOP_K):
        rk = jnp.sum(jnp.where(sel[k], seen, 0.0), axis=0, keepdims=True).astype(jnp.int32)
        eid = jnp.where(sub == k, idxs[k], eid)
        rank = jnp.where(sub == k, rk, rank)
        wts = jnp.where(sub == k, exps[k] / den, wts)
    eid_ref[...] = eid
    rank_ref[...] = rank
    wts_ref[...] = wts


def _mix(x, h, o_ctx, o_lat, mods, l, w_gate, w_branch, w_out, g_post, g_pre2, r_wt, r_b):
    const2 = lambda i: (0, 0)
    tile = lambda width: pl.BlockSpec((MIX_TM, width), lambda i: (i, 0))
    route = pl.BlockSpec((ROUTE_ROWS, MIX_TM), lambda i: (0, i))
    ctx_tile = pl.BlockSpec((MIX_TM, BRANCH_W), lambda i: (jnp.minimum(i, MIX_CTX_TILES - 1), 0))
    lat_tile = pl.BlockSpec((MIX_TM, BRANCH_W), lambda i: (jnp.maximum(i - MIX_CTX_TILES, 0), 0))
    mod_row = lambda i: _mod_row(i, MIX_CTX_TILES, MIX_LAT_PER_SEQ)
    single = pl.Buffered(1)
    return pl.pallas_call(
        _mix_kernel,
        name="mix_router",
        grid=(MIX_TILES,),
        in_specs=[
            tile(D_MODEL), tile(D_MODEL),
            ctx_tile, ctx_tile, ctx_tile, ctx_tile, lat_tile, lat_tile, lat_tile, lat_tile,
            pl.BlockSpec((1, 1, 6 * D_MODEL), lambda i: (l * MOD_ROWS + mod_row(i), 0, 0)),
            pl.BlockSpec((D_MODEL, N_GATE), const2, pipeline_mode=single),
            pl.BlockSpec((N_BRANCH, BRANCH_W, D_MODEL), lambda i: (0, 0, 0), pipeline_mode=single),
            pl.BlockSpec((D_MODEL, D_MODEL), const2, pipeline_mode=single),
            pl.BlockSpec((1, D_MODEL), const2),
            pl.BlockSpec((1, D_MODEL), const2),
            pl.BlockSpec((N_EXPERTS, D_MODEL), const2),
            pl.BlockSpec((N_EXPERTS, 1), const2),
        ],
        out_specs=[
            tile(D_MODEL), tile(D_MODEL),
            route, route, route,
            pl.BlockSpec((N_EXPERTS, 1), const2),
        ],
        out_shape=[
            jax.ShapeDtypeStruct((N_TOK, D_MODEL), F32),
            jax.ShapeDtypeStruct((N_TOK, D_MODEL), F32),
            jax.ShapeDtypeStruct((ROUTE_ROWS, N_TOK), jnp.int32),
            jax.ShapeDtypeStruct((ROUTE_ROWS, N_TOK), jnp.int32),
            jax.ShapeDtypeStruct((ROUTE_ROWS, N_TOK), F32),
            jax.ShapeDtypeStruct((N_EXPERTS, 1), F32),
        ],
        scratch_shapes=[pltpu.VMEM((N_EXPERTS, 1), F32)],
        compiler_params=pltpu.CompilerParams(
            dimension_semantics=("arbitrary",), vmem_limit_bytes=VMEM_LIMIT),
    )(x, h, *o_ctx, *o_lat, mods, w_gate, w_branch, w_out, g_post, g_pre2, r_wt, r_b)


def _row_copy(src_ref, src_row, dst_ref, dst_row, sem):
    return pltpu.make_async_copy(src_ref.at[pl.ds(src_row, 1), :], dst_ref.at[pl.ds(dst_row, 1), :], sem)


DISPATCH_ROWS = 512
DISPATCH_STEPS = N_TOK // DISPATCH_ROWS
DMA_UNROLL = 8


def _dispatch_kernel(starts_ref, padded_ref, pos_ref, h2_ref, xs_ref, zbuf, sem, zsem, tsem):
    i = pl.program_id(0)

    def pad_copy(e):
        last = pl.multiple_of(starts_ref[e] + padded_ref[e] - TE, TE)
        return pltpu.make_async_copy(zbuf, xs_ref.at[pl.ds(last, TE), :], zsem)

    first_tail = (starts_ref[N_EXPERTS - 1] + padded_ref[N_EXPERTS - 1]) // TE

    def tail_copy(j):
        return pltpu.make_async_copy(zbuf, xs_ref.at[pl.ds(pl.multiple_of(j * TE, TE), TE), :], tsem)

    def tail_start(j, carry):
        tail_copy(j).start()
        return carry

    def tail_wait(j, carry):
        tail_copy(j).wait()
        return carry

    @pl.when(i == 0)
    def _():
        zbuf[...] = jnp.zeros_like(zbuf)
        for e in range(N_EXPERTS):
            @pl.when(padded_ref[e] > 0)
            def _():
                pad_copy(e).start()
        lax.fori_loop(first_tail, E_TILES, tail_start, 0)
        for e in range(N_EXPERTS):
            @pl.when(padded_ref[e] > 0)
            def _():
                pad_copy(e).wait()

    @pl.when(i == DISPATCH_STEPS - 1)
    def _():
        lax.fori_loop(first_tail, E_TILES, tail_wait, 0)

    def issue(t, carry):
        for k in range(TOP_K):
            _row_copy(h2_ref, t, xs_ref, pos_ref[t * TOP_K + k], sem).start(priority=k % 2)
        return carry

    lax.fori_loop(0, DISPATCH_ROWS, issue, 0, unroll=DMA_UNROLL)

    for _ in range(TOP_K):
        pltpu.make_async_copy(h2_ref, xs_ref.at[pl.ds(0, DISPATCH_ROWS), :], sem).wait()


def _dispatch(starts, padded, pos, h2):
    grid_spec = pltpu.PrefetchScalarGridSpec(
        num_scalar_prefetch=2,
        grid=(DISPATCH_STEPS,),
        in_specs=[
            pl.BlockSpec((DISPATCH_ROWS * TOP_K,), lambda i, st, pd: (i,), memory_space=pltpu.SMEM),
            pl.BlockSpec((DISPATCH_ROWS, D_MODEL), lambda i, st, pd: (i, 0)),
        ],
        out_specs=pl.BlockSpec(memory_space=pl.ANY),
        scratch_shapes=[pltpu.VMEM((TE, D_MODEL), F32),
                        pltpu.SemaphoreType.DMA(()), pltpu.SemaphoreType.DMA(()),
                        pltpu.SemaphoreType.DMA(())],
    )
    return pl.pallas_call(
        _dispatch_kernel,
        name="dispatch",
        grid_spec=grid_spec,
        out_shape=jax.ShapeDtypeStruct((P_MAX, D_MODEL), F32),
        compiler_params=pltpu.CompilerParams(
            dimension_semantics=("arbitrary",), vmem_limit_bytes=VMEM_LIMIT),
    )(starts, padded, pos, h2)


W_CAST_ROWS = 128


def _moe_kernel(te_ref, valid_ref, x_ref, w1_ref, b1_ref, w2_ref, b2_ref, y_ref, w1b, w2b):
    j = pl.program_id(0)
    e = te_ref[j]
    prev = te_ref[jnp.maximum(j - 1, 0)]
    valid = valid_ref[j] == 1
    new_expert = jnp.logical_or(j == 0, e != prev)

    @pl.when(jnp.logical_and(valid, new_expert))
    def _():
        def cast(c, carry):
            r = pl.multiple_of(c * W_CAST_ROWS, W_CAST_ROWS)
            w1b[pl.ds(r, W_CAST_ROWS), :] = w1_ref[0, 0, pl.ds(r, W_CAST_ROWS), :].astype(BF16)
            w2b[pl.ds(r, W_CAST_ROWS), :] = w2_ref[0, 0, pl.ds(r, W_CAST_ROWS), :].astype(BF16)
            return carry
        lax.fori_loop(0, D_MODEL // W_CAST_ROWS, cast, 0)

    @pl.when(valid)
    def _():
        x = x_ref[...].astype(BF16)
        u = _dot(x, w1b[...]) + b1_ref[0, 0]
        xg = jnp.minimum(u[:, :D_FF], SWIGLU_LIMIT)
        xl = jnp.clip(u[:, D_FF:], -SWIGLU_LIMIT, SWIGLU_LIMIT)
        act = (xl + 1.0) * (xg * _sigmoid(SWIGLU_ALPHA * xg))
        y_ref[...] = _dot(act.astype(BF16), w2b[...]) + b2_ref[0, 0]

    @pl.when(jnp.logical_not(valid))
    def _():
        y_ref[...] = jnp.zeros_like(y_ref)


def _expert_ffn(tile_expert, tile_valid, xs, l, e_w1, e_b1, e_w2, e_b2):
    L = e_w1.shape[0]
    grid_spec = pltpu.PrefetchScalarGridSpec(
        num_scalar_prefetch=2,
        grid=(E_TILES,),
        in_specs=[
            pl.BlockSpec((TE, D_MODEL), lambda j, te, va: (j * va[j], 0)),
            pl.BlockSpec((1, 1, D_MODEL, 2 * D_FF), lambda j, te, va: (l, te[j], 0, 0)),
            pl.BlockSpec((1, 1, 1, 2 * D_FF), lambda j, te, va: (l, te[j], 0, 0)),
            pl.BlockSpec((1, 1, D_FF, D_MODEL), lambda j, te, va: (l, te[j], 0, 0)),
            pl.BlockSpec((1, 1, 1, D_MODEL), lambda j, te, va: (l, te[j], 0, 0)),
        ],
        out_specs=pl.BlockSpec((TE, D_MODEL), lambda j, te, va: (j, 0)),
        scratch_shapes=[pltpu.VMEM((D_MODEL, 2 * D_FF), BF16), pltpu.VMEM((D_FF, D_MODEL), BF16)],
    )
    return pl.pallas_call(
        _moe_kernel,
        name="expert_ffn",
        grid_spec=grid_spec,
        out_shape=jax.ShapeDtypeStruct((P_MAX, D_MODEL), F32),
        compiler_params=pltpu.CompilerParams(
            dimension_semantics=("arbitrary",), vmem_limit_bytes=VMEM_LIMIT),
    )(tile_expert, tile_valid, xs, e_w1,
      e_b1.reshape(L, N_EXPERTS, 1, 2 * D_FF), e_w2, e_b2.reshape(L, N_EXPERTS, 1, D_MODEL))


def _combine_kernel(pos_ref, pos_next_ref, ys_ref, wts_ref, x1_ref, mod_ref, gpost_ref, out_ref,
                    buf, sem):
    i = pl.program_id(0)
    slot = i % 2

    def gather(p_ref, s):
        def issue(t, carry):
            for k in range(TOP_K):
                _row_copy(ys_ref, p_ref[t * TOP_K + k], buf.at[s, k], t, sem.at[s]).start(priority=k % 2)
            return carry
        lax.fori_loop(0, TM, issue, 0, unroll=DMA_UNROLL)

    @pl.when(i == 0)
    def _():
        gather(pos_ref, 0)

    @pl.when(i + 1 < N_TILES)
    def _():
        gather(pos_next_ref, 1 - slot)

    for k in range(TOP_K):
        pltpu.make_async_copy(ys_ref.at[pl.ds(0, TM), :], buf.at[slot, k], sem.at[slot]).wait()

    w = wts_ref[...]
    y = w[:, 0:1] * buf[slot, 0]
    for k in range(1, TOP_K):
        y = y + w[:, k:k + 1] * buf[slot, k]
    g2 = mod_ref[0][:, 5 * D_MODEL:6 * D_MODEL]
    out_ref[...] = x1_ref[...] + g2 * _rms(y, gpost_ref[...])


def _combine(pos, ys, wts, x1, mods, l, g_post2):
    return pl.pallas_call(
        _combine_kernel,
        name="combine",
        grid=(N_TILES,),
        in_specs=[
            pl.BlockSpec((TM * TOP_K,), lambda i: (i,), memory_space=pltpu.SMEM),
            pl.BlockSpec((TM * TOP_K,), lambda i: (jnp.minimum(i + 1, N_TILES - 1),),
                         memory_space=pltpu.SMEM),
            pl.BlockSpec(memory_space=pl.ANY),
            pl.BlockSpec((TM, TOP_K), lambda i: (i, 0)),
            pl.BlockSpec((TM, D_MODEL), lambda i: (i, 0)),
            pl.BlockSpec((1, 1, 6 * D_MODEL), lambda i: (l * MOD_ROWS + _mod_row(i), 0, 0)),
            pl.BlockSpec((1, D_MODEL), lambda i: (0, 0)),
        ],
        out_specs=pl.BlockSpec((TM, D_MODEL), lambda i: (i, 0)),
        out_shape=jax.ShapeDtypeStruct((N_TOK, D_MODEL), F32),
        scratch_shapes=[pltpu.VMEM((2, TOP_K, TM, D_MODEL), F32), pltpu.SemaphoreType.DMA((2,))],
        compiler_params=pltpu.CompilerParams(
            dimension_semantics=("arbitrary",), vmem_limit_bytes=VMEM_LIMIT),
    )(pos, pos, ys, wts, x1, mods, g_post2)


def _rope_tables():
    t = np.arange(DEC_SEQ)
    r = (t // GRID_W).astype(np.float32)
    c = (t % GRID_W).astype(np.float32)

    def table(dim, reps):
        n_axis = dim // 4
        inv = (ROPE_THETA ** (-np.arange(n_axis, dtype=np.float32) / n_axis)).astype(np.float32)
        ang = np.concatenate([r[:, None] * inv, c[:, None] * inv], axis=-1).astype(np.float32)
        cos, sin = np.cos(ang), np.sin(ang)
        return (np.tile(np.concatenate([cos, cos], -1), (1, reps)),
                np.tile(np.concatenate([-sin, sin], -1), (1, reps)))

    cos_a, sin_a = table(A_HEAD_DIM, 256 // A_HEAD_DIM)
    cos_b, sin_b = table(B_HALF_DIM, 256 // B_HALF_DIM)
    return tuple(jnp.asarray(a, F32) for a in (cos_a, sin_a, cos_b, sin_b))


def _group_mean_matrix():
    idx = np.arange(256) // A_HEAD_DIM
    return jnp.asarray((idx[:, None] == idx[None, :]).astype(np.float32) / A_HEAD_DIM, BF16)


def _block_diag(w):
    out = jnp.zeros((BRANCH_W, BRANCH_W), w.dtype)
    for g in range(D_GROUPS):
        out = out.at[g * D_GROUP_W:(g + 1) * D_GROUP_W, g * D_GROUP_W:(g + 1) * D_GROUP_W].set(w[g])
    return out


def _routing_tables(eid, rank, cnt):
    cnt = cnt.reshape(N_EXPERTS).astype(jnp.int32)
    padded = ((cnt + TE - 1) // TE) * TE
    ends = jnp.cumsum(padded)
    starts = ends - padded
    experts = jnp.arange(N_EXPERTS, dtype=jnp.int32)
    start_of = jnp.sum(jnp.where(eid[:TOP_K, :, None] == experts, starts, 0), axis=-1)
    pos = (start_of + rank[:TOP_K]).T.reshape(N_SLOTS)
    tile_row = jnp.arange(E_TILES, dtype=jnp.int32) * TE
    tile_expert = jnp.sum((tile_row[:, None] >= ends[None, :]).astype(jnp.int32), axis=-1)
    tile_expert = jnp.minimum(tile_expert, N_EXPERTS - 1)
    tile_valid = (tile_row < ends[-1]).astype(jnp.int32)
    return (pos.astype(jnp.int32), tile_expert.astype(jnp.int32), tile_valid,
            starts.astype(jnp.int32), padded.astype(jnp.int32))


def kernel(x_prompt, x_sample, cache_a_k, cache_a_v, cache_b_k, cache_b_v, c, c_ctx, w_mod, b_mod, g_pre1, g_post1, g_pre2, g_post2, w_in, a_q_gain, a_k_gain, b_lambda, b_subln_gain, c_dw_w, c_dw_b, c_ln_g, c_ln_b, d_w_group, d_scale, w_branch, w_out, r_w, r_b, e_w1, e_b1, e_w2, e_b2):
    x = jnp.concatenate([x_prompt.reshape(N_CTX, D_MODEL), x_sample.reshape(N_LAT, D_MODEL)], axis=0)
    cond8 = jnp.concatenate(
        [c_ctx[None, :], c, jnp.zeros((MOD_ROWS - 1 - DEC_BATCH, D_MODEL), F32)], axis=0)
    mods = _modulation(cond8, w_mod, b_mod).reshape(DEPTH * MOD_ROWS, 1, 6 * D_MODEL)

    rope = _rope_tables()
    gmat = _group_mean_matrix()
    caches = (
        cache_a_k.reshape(DEC_BATCH, DEPTH, PAST_LEN, A_KV_HEADS * A_HEAD_DIM),
        cache_a_v.reshape(DEC_BATCH, DEPTH, PAST_LEN, A_KV_HEADS * A_HEAD_DIM),
        cache_b_k.reshape(DEC_BATCH, DEPTH, PAST_LEN, B_HEADS * 2 * B_HALF_DIM),
        cache_b_v.reshape(DEC_BATCH, DEPTH, PAST_LEN, B_HEADS * B_V_DIM),
    )

    new_ak, new_av, new_bk, new_bv = [], [], [], []
    row = lambda v: v.reshape(1, -1)
    for l in range(DEPTH):
        lambda_init = 0.8 - 0.6 * math.exp(-0.3 * l)
        w_small = w_in[l, :, :N_SMALL].astype(BF16)
        w_gate = w_in[l, :, N_SMALL:].astype(BF16)
        z, h = _in_projection(
            x, mods, l, row(g_pre1[l]), w_small,
            row(jnp.tile(a_q_gain[l], A_HEADS)), row(jnp.tile(a_k_gain[l], A_KV_HEADS)), gmat, rope)

        zc = z[:N_CTX]
        new_ak.append(zc[:, COL_KA:COL_VA].reshape(BATCH, SEQ, A_KV_HEADS, A_HEAD_DIM))
        new_av.append(zc[:, COL_VA:COL_QB].reshape(BATCH, SEQ, A_KV_HEADS, A_HEAD_DIM))
        new_bk.append(zc[:, COL_KB:COL_VB].reshape(BATCH, SEQ, B_HEADS, 2, B_HALF_DIM))
        new_bv.append(zc[:, COL_VB:COL_ZC].reshape(BATCH, SEQ, B_HEADS, B_V_DIM))

        subg = row(b_subln_gain[l])
        oa_c, ob_c = _attention_ctx(z, b_lambda[l], subg, lambda_init)
        oa_l, ob_l = _attention_lat(z, caches, l, b_lambda[l], subg, lambda_init)
        conv_args = (c_dw_w[l], row(c_dw_b[l]), row(c_ln_g[l]), row(c_ln_b[l]),
                     _block_diag(d_w_group[l]).astype(BF16), row(d_scale[l]))
        oc_c, od_c = _conv_pool(z, SEQ, BATCH, 0, *conv_args)
        oc_l, od_l = _conv_pool(z, DEC_SEQ, DEC_BATCH, N_CTX, *conv_args)

        x1, h2, eid, rank, wts, cnt = _mix(
            x, h, (oa_c, ob_c, oc_c, od_c), (oa_l, ob_l, oc_l, od_l), mods, l,
            w_gate, w_branch[l].astype(BF16), w_out[l].astype(BF16),
            row(g_post1[l]), row(g_pre2[l]), r_w[l].T, r_b[l].reshape(N_EXPERTS, 1))
        wts = wts[:TOP_K].T

        pos, tile_expert, tile_valid, starts, padded = _routing_tables(eid, rank, cnt)
        xs = _dispatch(starts, padded, pos, h2)
        ys = _expert_ffn(tile_expert, tile_valid, xs, l, e_w1, e_b1, e_w2, e_b2)
        x = _combine(pos, ys, wts, x1, mods, l, row(g_post2[l]))

    y_prompt = x[:N_CTX].reshape(BATCH, SEQ, D_MODEL)
    y_sample = x[N_CTX:].reshape(DEC_BATCH, DEC_SEQ, D_MODEL)
    return (y_prompt, y_sample, jnp.stack(new_ak, axis=1), jnp.stack(new_av, axis=1),
            jnp.stack(new_bk, axis=1), jnp.stack(new_bv, axis=1))
```

```python
import functools
import math

import jax
import jax.numpy as jnp
import numpy as np
from jax import lax
from jax.experimental import pallas as pl
from jax.experimental.pallas import tpu as pltpu

F32 = jnp.float32
BF16 = jnp.bfloat16

D_MODEL = 1024
BATCH = 16
SEQ = 256
DEPTH = 2
DEC_BATCH = 4
DEC_SEQ = 2048
PAST_LEN = 256
GRID_W = 64
ROPE_THETA = 10000.0
N_BRANCH = 4
BRANCH_W = D_MODEL // 4
A_HEADS = 4
A_KV_HEADS = 2
A_HEAD_DIM = 64
B_HEADS = 4
B_HALF_DIM = 32
B_V_DIM = 2 * B_HALF_DIM
C_CONV_WIDTH = 31
D_GROUPS = 4
D_GROUP_W = BRANCH_W // D_GROUPS
POOL_WINDOWS = (2, 4, 8, 16)
N_EXPERTS = 32
TOP_K = 4
D_FF = D_MODEL
SWIGLU_LIMIT = 7.0
SWIGLU_ALPHA = 1.702
EPS = 1e-6

N_CTX = BATCH * SEQ
N_LAT = DEC_BATCH * DEC_SEQ
N_TOK = N_CTX + N_LAT

COL_QA, COL_KA, COL_VA = 0, 256, 384
COL_QB, COL_KB, COL_VB = 512, 768, 1024
COL_ZC, COL_ZD = 1280, 1792
N_SMALL = 2048
N_GATE = N_BRANCH * D_MODEL

TM = 256
N_TILES = N_TOK // TM
CTX_TILES = N_CTX // TM
LAT_TILES_PER_SEQ = DEC_SEQ // TM
MOD_ROWS = 8

TE = 512
N_SLOTS = N_TOK * TOP_K
P_MAX = N_SLOTS + N_EXPERTS * TE
E_TILES = P_MAX // TE

TQ = 256
CONV_PAD = 16
CONV_CHUNK = 128

VMEM_LIMIT = 56 * 1024 * 1024


def _sigmoid(x):
    return 1.0 / (1.0 + jnp.exp(-x))


def _split_bf16(a):
    hi = a.astype(BF16)
    lo = (a - hi.astype(F32)).astype(BF16)
    return hi, lo


def _dot(a, b):
    return jnp.dot(a, b, preferred_element_type=F32)


def _dot_nt(a, b):
    return lax.dot_general(a, b, (((1,), (1,)), ((), ())), preferred_element_type=F32)


def _dot3(a, b):
    ah, al = _split_bf16(a)
    bh, bl = _split_bf16(b)
    return _dot(ah, bh) + _dot(ah, bl) + _dot(al, bh)


def _rms(x, g):
    return x * lax.rsqrt(jnp.mean(x * x, axis=-1, keepdims=True) + EPS) * g


def _mod_row(i, ctx_tiles=CTX_TILES, tiles_per_seq=LAT_TILES_PER_SEQ):
    return jnp.where(i < ctx_tiles, 0, 1 + (i - ctx_tiles) // tiles_per_seq)


MOD_TN = 1536


def _mod_kernel(cond_ref, w_ref, b_ref, o_ref):
    c = cond_ref[...]
    s = c * _sigmoid(c)
    o_ref[0] = _dot3(s, w_ref[0]) + b_ref[0]


def _modulation(cond8, w_mod, b_mod):
    L = w_mod.shape[0]
    return pl.pallas_call(
        _mod_kernel,
        name="modulation",
        grid=(L, 6 * D_MODEL // MOD_TN),
        in_specs=[
            pl.BlockSpec((MOD_ROWS, D_MODEL), lambda l, j: (0, 0)),
            pl.BlockSpec((1, D_MODEL, MOD_TN), lambda l, j: (l, 0, j)),
            pl.BlockSpec((1, 1, MOD_TN), lambda l, j: (l, 0, j)),
        ],
        out_specs=pl.BlockSpec((1, MOD_ROWS, MOD_TN), lambda l, j: (l, 0, j)),
        out_shape=jax.ShapeDtypeStruct((L, MOD_ROWS, 6 * D_MODEL), F32),
        compiler_params=pltpu.CompilerParams(
            dimension_semantics=("arbitrary", "arbitrary"), vmem_limit_bytes=VMEM_LIMIT),
    )(cond8, w_mod, b_mod.reshape(L, 1, 6 * D_MODEL))


def _group_mean(sq, gmat):
    hi, lo = _split_bf16(sq)
    return _dot(hi, gmat) + _dot(lo, gmat)


def _swap_halves(x, half):
    w = x.shape[-1]
    lane = lax.broadcasted_iota(jnp.int32, x.shape, 1)
    first = (lane % (2 * half)) < half
    return jnp.where(first, pltpu.roll(x, w - half, 1), pltpu.roll(x, half, 1))


def _inproj_kernel(x_ref, mod_ref, gpre_ref, w_ref, gq_ref, gk_ref, gmat_ref,
                   cos_a_ref, sin_a_ref, cos_b_ref, sin_b_ref, z_ref, h_ref):
    i = pl.program_id(0)
    m = mod_ref[0]
    sh1 = m[:, 0:D_MODEL]
    sc1 = m[:, D_MODEL:2 * D_MODEL]
    h = _rms(x_ref[...], gpre_ref[...]) * (1.0 + sc1) + sh1
    hb = h.astype(BF16)
    h_ref[...] = hb
    z = _dot(hb, w_ref[...])

    gmat = gmat_ref[...]
    qa = z[:, COL_QA:COL_KA]
    ka = z[:, COL_KA:COL_VA]
    qa = qa * lax.rsqrt(_group_mean(qa * qa, gmat) + EPS) * gq_ref[...]
    ka = ka * lax.rsqrt(_group_mean(ka * ka, gmat[0:128, 0:128]) + EPS) * gk_ref[...]
    qb = z[:, COL_QB:COL_KB]
    kb = z[:, COL_KB:COL_VB]
    z_ref[:, COL_VA:COL_QB] = z[:, COL_VA:COL_QB]
    z_ref[:, COL_VB:N_SMALL] = z[:, COL_VB:N_SMALL]

    @pl.when(i < CTX_TILES)
    def _():
        z_ref[:, COL_QA:COL_KA] = qa
        z_ref[:, COL_KA:COL_VA] = ka
        z_ref[:, COL_QB:COL_KB] = qb
        z_ref[:, COL_KB:COL_VB] = kb

    @pl.when(i >= CTX_TILES)
    def _():
        cos_a = cos_a_ref[...]
        sin_a = sin_a_ref[...]
        cos_b = cos_b_ref[...]
        sin_b = sin_b_ref[...]
        ha, hb_ = A_HEAD_DIM // 2, B_HALF_DIM // 2
        z_ref[:, COL_QA:COL_KA] = qa * cos_a + _swap_halves(qa, ha) * sin_a
        z_ref[:, COL_KA:COL_VA] = ka * cos_a[:, 0:128] + _swap_halves(ka, ha) * sin_a[:, 0:128]
        z_ref[:, COL_QB:COL_KB] = qb * cos_b + _swap_halves(qb, hb_) * sin_b
        z_ref[:, COL_KB:COL_VB] = kb * cos_b + _swap_halves(kb, hb_) * sin_b


def _rope_block(i):
    return jnp.where(i < CTX_TILES, 0, (i - CTX_TILES) % LAT_TILES_PER_SEQ)


def _in_projection(x, mods, l, g_pre, w_small, gq, gk, gmat, rope):
    const = lambda i: (0, 0)
    rope_spec = pl.BlockSpec((TM, 256), lambda i: (_rope_block(i), 0))
    return pl.pallas_call(
        _inproj_kernel,
        name="in_projection",
        grid=(N_TILES,),
        in_specs=[
            pl.BlockSpec((TM, D_MODEL), lambda i: (i, 0)),
            pl.BlockSpec((1, 1, 6 * D_MODEL), lambda i: (l * MOD_ROWS + _mod_row(i), 0, 0)),
            pl.BlockSpec((1, D_MODEL), const),
            pl.BlockSpec((D_MODEL, N_SMALL), const),
            pl.BlockSpec((1, 256), const),
            pl.BlockSpec((1, 128), const),
            pl.BlockSpec((256, 256), const),
            rope_spec, rope_spec, rope_spec, rope_spec,
        ],
        out_specs=[
            pl.BlockSpec((TM, N_SMALL), lambda i: (i, 0)),
            pl.BlockSpec((TM, D_MODEL), lambda i: (i, 0)),
        ],
        out_shape=[
            jax.ShapeDtypeStruct((N_TOK, N_SMALL), F32),
            jax.ShapeDtypeStruct((N_TOK, D_MODEL), BF16),
        ],
        compiler_params=pltpu.CompilerParams(
            dimension_semantics=("arbitrary",), vmem_limit_bytes=VMEM_LIMIT),
    )(x, mods, g_pre, w_small, gq, gk, gmat, *rope)


LOG2E = 1.4426950408889634
HALF_LANES = 64


def _attend(q, ks, vexts):
    ss = [_dot_nt(q, k) for k in ks]
    m = ss[0].max(axis=-1, keepdims=True)
    for s in ss[1:]:
        m = jnp.maximum(m, s.max(axis=-1, keepdims=True))
    o = None
    for s, v in zip(ss, vexts):
        t = _dot(jnp.exp2(s - m).astype(BF16), v)
        o = t if o is None else o + t
    return o


def _value_ext(v128, upper):
    lane = lax.broadcasted_iota(jnp.int32, v128.shape, 1)
    keep = (lane >= HALF_LANES) if upper else (lane < HALF_LANES)
    return jnp.where(keep, v128, 1.0).astype(BF16)


def _normalised(o, upper):
    if upper:
        return o[:, HALF_LANES:] / o[:, 0:1]
    return o[:, :HALF_LANES] / o[:, HALF_LANES:HALF_LANES + 1]


def _attn_kernel(*refs, has_cache, lambda_init):
    if has_cache:
        (qa_ref, ka_ref, va_ref, qb_ref, kb_ref, vb_ref,
         cka_ref, cva_ref, ckb_ref, cvb_ref, lam_ref, subg_ref, oa_ref, ob_ref) = refs
    else:
        (qa_ref, ka_ref, va_ref, qb_ref, kb_ref, vb_ref,
         lam_ref, subg_ref, oa_ref, ob_ref) = refs

    def cols(ref, lo, hi):
        return ref[:, lo:hi].astype(BF16)

    def ccols(ref, lo, hi):
        return ref[0, 0, :, lo:hi].astype(BF16)

    scale_a = A_HEAD_DIM ** -0.5 * LOG2E
    group = A_HEADS // A_KV_HEADS
    for g in range(A_KV_HEADS):
        lo, hi = g * A_HEAD_DIM, (g + 1) * A_HEAD_DIM
        upper = g % 2 == 1
        ks = [cols(ka_ref, lo, hi)]
        vexts = [_value_ext(va_ref[...], upper)]
        if has_cache:
            ks.append(ccols(cka_ref, lo, hi))
            vexts.append(_value_ext(cva_ref[0, 0], upper))
        for hd in range(g * group, (g + 1) * group):
            q = (qa_ref[:, hd * A_HEAD_DIM:(hd + 1) * A_HEAD_DIM] * scale_a).astype(BF16)
            o = _attend(q, ks, vexts)
            oa_ref[:, hd * A_HEAD_DIM:(hd + 1) * A_HEAD_DIM] = _normalised(o, upper)

    bl = lam_ref[...]
    lam = (jnp.exp(jnp.sum(bl[0:1] * bl[1:2], axis=-1, keepdims=True))
           - jnp.exp(jnp.sum(bl[2:3] * bl[3:4], axis=-1, keepdims=True)) + lambda_init)
    scale_b = B_HALF_DIM ** -0.5 * LOG2E
    subg = subg_ref[...]
    for hd in range(B_HEADS):
        base = hd * 2 * B_HALF_DIM
        vlo, vhi = hd * B_V_DIM, (hd + 1) * B_V_DIM
        blk = (hd // 2) * 2 * B_V_DIM
        upper = hd % 2 == 1
        vexts = [_value_ext(vb_ref[:, blk:blk + 2 * B_V_DIM], upper)]
        if has_cache:
            vexts.append(_value_ext(cvb_ref[0, 0, :, blk:blk + 2 * B_V_DIM], upper))
        outs = []
        for c in range(2):
            lo, hi = base + c * B_HALF_DIM, base + (c + 1) * B_HALF_DIM
            q = (qb_ref[:, lo:hi] * scale_b).astype(BF16)
            ks = [cols(kb_ref, lo, hi)]
            if has_cache:
                ks.append(ccols(ckb_ref, lo, hi))
            outs.append(_normalised(_attend(q, ks, vexts), upper))
        o = outs[0] - lam * outs[1]
        o = _rms(o, subg) * (1.0 - lambda_init)
        ob_ref[:, vlo:vhi] = o


def _attention_ctx(z, b_lambda, subg, lambda_init):
    def zspec(width, col):
        return pl.BlockSpec((SEQ, width), lambda b: (b, col // width))
    out_spec = pl.BlockSpec((SEQ, BRANCH_W), lambda b: (b, 0))
    return pl.pallas_call(
        functools.partial(_attn_kernel, has_cache=False, lambda_init=lambda_init),
        name="attention_ctx",
        grid=(BATCH,),
        in_specs=[
            zspec(256, COL_QA), zspec(128, COL_KA), zspec(128, COL_VA),
            zspec(256, COL_QB), zspec(256, COL_KB), zspec(256, COL_VB),
            pl.BlockSpec((4, B_HALF_DIM), lambda b: (0, 0)),
            pl.BlockSpec((1, B_V_DIM), lambda b: (0, 0)),
        ],
        out_specs=[out_spec, out_spec],
        out_shape=[jax.ShapeDtypeStruct((N_CTX, BRANCH_W), F32)] * 2,
        compiler_params=pltpu.CompilerParams(
            dimension_semantics=("arbitrary",), vmem_limit_bytes=VMEM_LIMIT),
    )(z, z, z, z, z, z, b_lambda, subg)


def _attention_lat(z, caches, l, b_lambda, subg, lambda_init):
    q_tiles = DEC_SEQ // TQ
    seq0 = N_CTX // DEC_SEQ

    def qspec(col):
        return pl.BlockSpec((TQ, 256), lambda b, t: (CTX_TILES + b * q_tiles + t, col // 256))

    def kspec(width, col):
        return pl.BlockSpec((DEC_SEQ, width), lambda b, t: (seq0 + b, col // width))

    def cspec(width):
        return pl.BlockSpec((1, 1, PAST_LEN, width), lambda b, t: (b, l, 0, 0))

    out_spec = pl.BlockSpec((TQ, BRANCH_W), lambda b, t: (b * q_tiles + t, 0))
    cka, cva, ckb, cvb = caches
    return pl.pallas_call(
        functools.partial(_attn_kernel, has_cache=True, lambda_init=lambda_init),
        name="attention_lat",
        grid=(DEC_BATCH, q_tiles),
        in_specs=[
            qspec(COL_QA), kspec(128, COL_KA), kspec(128, COL_VA),
            qspec(COL_QB), kspec(256, COL_KB), kspec(256, COL_VB),
            cspec(128), cspec(128), cspec(256), cspec(256),
            pl.BlockSpec((4, B_HALF_DIM), lambda b, t: (0, 0)),
            pl.BlockSpec((1, B_V_DIM), lambda b, t: (0, 0)),
        ],
        out_specs=[out_spec, out_spec],
        out_shape=[jax.ShapeDtypeStruct((N_LAT, BRANCH_W), F32)] * 2,
        compiler_params=pltpu.CompilerParams(
            dimension_semantics=("arbitrary", "arbitrary"), vmem_limit_bytes=VMEM_LIMIT),
    )(z, z, z, z, z, z, cka, cva, ckb, cvb, b_lambda, subg)


SUBLANES = 8


def _row_shifter(win):
    span = win.shape[0] - SUBLANES
    shifted = {}

    def rows(off):
        s = off % SUBLANES
        if s not in shifted:
            shifted[s] = win[s:s + span, :]
        base = off - s
        return shifted[s][base:base + CONV_CHUNK, :]

    return rows


def _convpool_kernel(a_ref, b_ref, d_ref, dww_ref, dwb_ref, lng_ref, lnb_ref, wbd_ref, dsc_ref,
                     oc_ref, od_ref, hpad, upad, *, seq_len):
    n_chunks = seq_len // CONV_CHUNK
    zeros = jnp.zeros((CONV_PAD, BRANCH_W), F32)
    hpad[0:CONV_PAD, :] = zeros
    hpad[seq_len + CONV_PAD:seq_len + 2 * CONV_PAD, :] = zeros
    upad[0:CONV_PAD, :] = zeros
    upad[seq_len + CONV_PAD:seq_len + 2 * CONV_PAD, :] = zeros

    def fill(c, carry):
        r = pl.multiple_of(c * CONV_CHUNK, CONV_CHUNK)
        a = a_ref[pl.ds(r, CONV_CHUNK), :]
        b = b_ref[pl.ds(r, CONV_CHUNK), :]
        hpad[pl.ds(r + CONV_PAD, CONV_CHUNK), :] = a * _sigmoid(b)
        upad[pl.ds(r + CONV_PAD, CONV_CHUNK), :] = d_ref[pl.ds(r, CONV_CHUNK), :]
        return carry

    lax.fori_loop(0, n_chunks, fill, 0)

    lane = lax.broadcasted_iota(jnp.int32, (CONV_CHUNK, BRANCH_W), 1)
    half = C_CONV_WIDTH // 2

    def body(c, carry):
        r = pl.multiple_of(c * CONV_CHUNK, CONV_CHUNK)
        hrows = _row_shifter(hpad[pl.ds(r, CONV_CHUNK + 2 * CONV_PAD), :])
        acc = jnp.zeros((CONV_CHUNK, BRANCH_W), F32)
        for k in range(C_CONV_WIDTH):
            acc = acc + dww_ref[k:k + 1, :] * hrows(CONV_PAD - half + k)
        acc = acc + dwb_ref[...]
        mu = jnp.mean(acc, axis=-1, keepdims=True)
        cen = acc - mu
        var = jnp.mean(cen * cen, axis=-1, keepdims=True)
        y = cen * lax.rsqrt(var + EPS) * lng_ref[...] + lnb_ref[...]
        oc_ref[pl.ds(r, CONV_CHUNK), :] = y * _sigmoid(y)

        urows = _row_shifter(upad[pl.ds(r, CONV_CHUNK + 2 * CONV_PAD), :])

        def ld(d):
            return urows(CONV_PAD + d)

        u = ld(0)
        sums = {}
        s = u + ld(-1)
        sums[2] = s
        s = s + ld(-2) + ld(1)
        sums[4] = s
        s = s + ld(-4) + ld(-3) + ld(2) + ld(3)
        sums[8] = s
        s = s + ld(-8) + ld(-7) + ld(-6) + ld(-5) + ld(4) + ld(5) + ld(6) + ld(7)
        sums[16] = s
        t = r + lax.broadcasted_iota(jnp.int32, (CONV_CHUNK, 1), 0)
        pooled = None
        for g, w in reversed(list(enumerate(POOL_WINDOWS))):
            lo = jnp.maximum(t - w // 2, 0)
            hi = jnp.minimum(t + (w - 1 - w // 2), seq_len - 1)
            mean = sums[w] / (hi - lo + 1).astype(F32)
            pooled = mean if pooled is None else jnp.where(lane < (g + 1) * D_GROUP_W, mean, pooled)
        dlt = (pooled - u).astype(BF16)
        od_ref[pl.ds(r, CONV_CHUNK), :] = _dot(dlt, wbd_ref[...]) * dsc_ref[...]
        return carry

    lax.fori_loop(0, n_chunks, body, 0)


def _conv_pool(z, seq_len, n_seq, row0, dww, dwb, lng, lnb, wbd, dsc):
    blk0 = row0 // seq_len

    def zspec(col):
        return pl.BlockSpec((seq_len, 256), lambda b: (blk0 + b, col // 256))

    def const(shape):
        return pl.BlockSpec(shape, lambda b: (0, 0))

    out_spec = pl.BlockSpec((seq_len, BRANCH_W), lambda b: (b, 0))
    return pl.pallas_call(
        functools.partial(_convpool_kernel, seq_len=seq_len),
        name=f"conv_pool_{seq_len}",
        grid=(n_seq,),
        in_specs=[
            zspec(COL_ZC), zspec(COL_ZC + BRANCH_W), zspec(COL_ZD),
            const((C_CONV_WIDTH, BRANCH_W)), const((1, BRANCH_W)), const((1, BRANCH_W)),
            const((1, BRANCH_W)), const((BRANCH_W, BRANCH_W)), const((1, BRANCH_W)),
        ],
        out_specs=[out_spec, out_spec],
        out_shape=[jax.ShapeDtypeStruct((n_seq * seq_len, BRANCH_W), F32)] * 2,
        scratch_shapes=[pltpu.VMEM((seq_len + 2 * CONV_PAD, BRANCH_W), F32)] * 2,
        compiler_params=pltpu.CompilerParams(
            dimension_semantics=("arbitrary",), vmem_limit_bytes=VMEM_LIMIT),
    )(z, z, z, dww, dwb, lng, lnb, wbd, dsc)


MIX_TM = 512
MIX_TILES = N_TOK // MIX_TM
MIX_CTX_TILES = N_CTX // MIX_TM
MIX_LAT_PER_SEQ = DEC_SEQ // MIX_TM
ROUTE_ROWS = 8


def _mix_kernel(x_ref, h_ref, oa_c, ob_c, oc_c, od_c, oa_l, ob_l, oc_l, od_l,
                mod_ref, wg_ref, wbr_ref, wout_ref, gpost_ref, gpre2_ref, rwt_ref, rb_ref,
                x1_ref, h2_ref, eid_ref, rank_ref, wts_ref, cnt_ref, carry):
    i = pl.program_id(0)
    is_ctx = i < MIX_CTX_TILES

    @pl.when(i == 0)
    def _():
        carry[...] = jnp.zeros_like(carry)

    m = mod_ref[0]
    g1 = m[:, 2 * D_MODEL:3 * D_MODEL]
    sh2 = m[:, 3 * D_MODEL:4 * D_MODEL]
    sc2 = m[:, 4 * D_MODEL:5 * D_MODEL]

    hb = h_ref[...]
    merged = None
    for n, (c_ref, l_ref) in enumerate(((oa_c, oa_l), (ob_c, ob_l), (oc_c, oc_l), (od_c, od_l))):
        gate = _sigmoid(_dot(hb, wg_ref[:, n * D_MODEL:(n + 1) * D_MODEL]))
        o = jnp.where(is_ctx, c_ref[...], l_ref[...])
        br = _dot(o.astype(BF16), wbr_ref[n])
        merged = gate * br if merged is None else merged + gate * br
    y = _dot(merged.astype(BF16), wout_ref[...])
    x1 = x_ref[...] + g1 * _rms(y, gpost_ref[...])
    x1_ref[...] = x1
    h2 = _rms(x1, gpre2_ref[...]) * (1.0 + sc2) + sh2
    h2_ref[...] = h2

    rh, rl = _split_bf16(rwt_ref[...])
    hh, hl = _split_bf16(h2)
    logits = _dot_nt(rh, hh) + _dot_nt(rh, hl) + _dot_nt(rl, hh) + rb_ref[...]

    iota_e = lax.broadcasted_iota(jnp.int32, (N_EXPERTS, MIX_TM), 0)
    rem = logits
    vals, idxs = [], []
    for _ in range(TOP_K):
        mx = jnp.max(rem, axis=0, keepdims=True)
        idx = jnp.min(jnp.where(rem == mx, iota_e, N_EXPERTS), axis=0, keepdims=True)
        vals.append(mx)
        idxs.append(idx)
        rem = jnp.where(iota_e == idx, -jnp.inf, rem)
    exps = [jnp.exp(v - vals[0]) for v in vals]
    den = exps[0]
    for e in exps[1:]:
        den = den + e

    sel = [iota_e == idx for idx in idxs]
    member = jnp.zeros((N_EXPERTS, MIX_TM), F32)
    for s in sel:
        member = member + jnp.where(s, 1.0, 0.0)
    row = lax.broadcasted_iota(jnp.int32, (MIX_TM, MIX_TM), 0)
    col = lax.broadcasted_iota(jnp.int32, (MIX_TM, MIX_TM), 1)
    earlier = jnp.where(row < col, 1.0, 0.0).astype(BF16)
    seen = _dot(member.astype(BF16), earlier) + carry[...]
    carry[...] = carry[...] + jnp.sum(member, axis=1, keepdims=True)
    cnt_ref[...] = carry[...]

    sub = lax.broadcasted_iota(jnp.int32, (ROUTE_ROWS, MIX_TM), 0)
    eid = jnp.zeros((ROUTE_ROWS, MIX_TM), jnp.int32)
    rank = jnp.zeros((ROUTE_ROWS, MIX_TM), jnp.int32)
    wts = jnp.zeros((ROUTE_ROWS, MIX_TM), F32)
    for k in range(TOP_K):
        rk = jnp.sum(jnp.where(sel[k], seen, 0.0), axis=0, keepdims=True).astype(jnp.int32)
        eid = jnp.where(sub == k, idxs[k], eid)
        rank = jnp.where(sub == k, rk, rank)
        wts = jnp.where(sub == k, exps[k] / den, wts)
    eid_ref[...] = eid
    rank_ref[...] = rank
    wts_ref[...] = wts


def _mix(x, h, o_ctx, o_lat, mods, l, w_gate, w_branch, w_out, g_post, g_pre2, r_wt, r_b):
    const2 = lambda i: (0, 0)
    tile = lambda width: pl.BlockSpec((MIX_TM, width), lambda i: (i, 0))
    route = pl.BlockSpec((ROUTE_ROWS, MIX_TM), lambda i: (0, i))
    ctx_tile = pl.BlockSpec((MIX_TM, BRANCH_W), lambda i: (jnp.minimum(i, MIX_CTX_TILES - 1), 0))
    lat_tile = pl.BlockSpec((MIX_TM, BRANCH_W), lambda i: (jnp.maximum(i - MIX_CTX_TILES, 0), 0))
    mod_row = lambda i: _mod_row(i, MIX_CTX_TILES, MIX_LAT_PER_SEQ)
    single = pl.Buffered(1)
    return pl.pallas_call(
        _mix_kernel,
        name="mix_router",
        grid=(MIX_TILES,),
        in_specs=[
            tile(D_MODEL), tile(D_MODEL),
            ctx_tile, ctx_tile, ctx_tile, ctx_tile, lat_tile, lat_tile, lat_tile, lat_tile,
            pl.BlockSpec((1, 1, 6 * D_MODEL), lambda i: (l * MOD_ROWS + mod_row(i), 0, 0)),
            pl.BlockSpec((D_MODEL, N_GATE), const2, pipeline_mode=single),
            pl.BlockSpec((N_BRANCH, BRANCH_W, D_MODEL), lambda i: (0, 0, 0), pipeline_mode=single),
            pl.BlockSpec((D_MODEL, D_MODEL), const2, pipeline_mode=single),
            pl.BlockSpec((1, D_MODEL), const2),
            pl.BlockSpec((1, D_MODEL), const2),
            pl.BlockSpec((N_EXPERTS, D_MODEL), const2),
            pl.BlockSpec((N_EXPERTS, 1), const2),
        ],
        out_specs=[
            tile(D_MODEL), tile(D_MODEL),
            route, route, route,
            pl.BlockSpec((N_EXPERTS, 1), const2),
        ],
        out_shape=[
            jax.ShapeDtypeStruct((N_TOK, D_MODEL), F32),
            jax.ShapeDtypeStruct((N_TOK, D_MODEL), F32),
            jax.ShapeDtypeStruct((ROUTE_ROWS, N_TOK), jnp.int32),
            jax.ShapeDtypeStruct((ROUTE_ROWS, N_TOK), jnp.int32),
            jax.ShapeDtypeStruct((ROUTE_ROWS, N_TOK), F32),
            jax.ShapeDtypeStruct((N_EXPERTS, 1), F32),
        ],
        scratch_shapes=[pltpu.VMEM((N_EXPERTS, 1), F32)],
        compiler_params=pltpu.CompilerParams(
            dimension_semantics=("arbitrary",), vmem_limit_bytes=VMEM_LIMIT),
    )(x, h, *o_ctx, *o_lat, mods, w_gate, w_branch, w_out, g_post, g_pre2, r_wt, r_b)


DMA_UNROLL = 8


def _row_copy(src_ref, src_row, dst_ref, dst_row, sem):
    return pltpu.make_async_copy(src_ref.at[pl.ds(src_row, 1), :], dst_ref.at[pl.ds(dst_row, 1), :], sem)


W_CAST_ROWS = 128


def _moe_kernel(te_ref, valid_ref, src_ref, src_next_ref, h2_ref, w1_ref, b1_ref, w2_ref, b2_ref,
                y_ref, w1b, w2b, xbuf, sem):
    j = pl.program_id(0)
    e = te_ref[j]
    prev = te_ref[jnp.maximum(j - 1, 0)]
    valid = valid_ref[j] == 1
    new_expert = jnp.logical_or(j == 0, e != prev)
    slot = j % 2

    def gather(s_ref, s):
        def issue(c, carry):
            for u in range(2):
                r = 2 * c + u
                _row_copy(h2_ref, s_ref[r], xbuf.at[s], r, sem.at[s]).start(priority=u)
            return carry
        lax.fori_loop(0, TE // 2, issue, 0, unroll=DMA_UNROLL)

    @pl.when(j == 0)
    def _():
        gather(src_ref, 0)

    nxt = jnp.minimum(j + 1, E_TILES - 1)

    @pl.when(jnp.logical_and(j + 1 < E_TILES, valid_ref[nxt] == 1))
    def _():
        gather(src_next_ref, 1 - slot)

    @pl.when(valid)
    def _():
        pltpu.make_async_copy(h2_ref.at[pl.ds(0, TE), :], xbuf.at[slot], sem.at[slot]).wait()

    @pl.when(jnp.logical_and(valid, new_expert))
    def _():
        def cast(c, carry):
            r = pl.multiple_of(c * W_CAST_ROWS, W_CAST_ROWS)
            w1b[pl.ds(r, W_CAST_ROWS), :] = w1_ref[0, 0, pl.ds(r, W_CAST_ROWS), :].astype(BF16)
            w2b[pl.ds(r, W_CAST_ROWS), :] = w2_ref[0, 0, pl.ds(r, W_CAST_ROWS), :].astype(BF16)
            return carry
        lax.fori_loop(0, D_MODEL // W_CAST_ROWS, cast, 0)

    @pl.when(valid)
    def _():
        x = xbuf[slot].astype(BF16)
        u = _dot(x, w1b[...]) + b1_ref[0, 0]
        xg = jnp.minimum(u[:, :D_FF], SWIGLU_LIMIT)
        xl = jnp.clip(u[:, D_FF:], -SWIGLU_LIMIT, SWIGLU_LIMIT)
        act = (xl + 1.0) * (xg * _sigmoid(SWIGLU_ALPHA * xg))
        y_ref[...] = _dot(act.astype(BF16), w2b[...]) + b2_ref[0, 0]

    @pl.when(jnp.logical_not(valid))
    def _():
        y_ref[...] = jnp.zeros_like(y_ref)


def _expert_ffn(tile_expert, tile_valid, src, h2, l, e_w1, e_b1, e_w2, e_b2):
    L = e_w1.shape[0]
    grid_spec = pltpu.PrefetchScalarGridSpec(
        num_scalar_prefetch=2,
        grid=(E_TILES,),
        in_specs=[
            pl.BlockSpec((TE,), lambda j, te, va: (j,), memory_space=pltpu.SMEM),
            pl.BlockSpec((TE,), lambda j, te, va: (jnp.minimum(j + 1, E_TILES - 1),),
                         memory_space=pltpu.SMEM),
            pl.BlockSpec(memory_space=pl.ANY),
            pl.BlockSpec((1, 1, D_MODEL, 2 * D_FF), lambda j, te, va: (l, te[j], 0, 0)),
            pl.BlockSpec((1, 1, 1, 2 * D_FF), lambda j, te, va: (l, te[j], 0, 0)),
            pl.BlockSpec((1, 1, D_FF, D_MODEL), lambda j, te, va: (l, te[j], 0, 0)),
            pl.BlockSpec((1, 1, 1, D_MODEL), lambda j, te, va: (l, te[j], 0, 0)),
        ],
        out_specs=pl.BlockSpec((TE, D_MODEL), lambda j, te, va: (j, 0)),
        scratch_shapes=[pltpu.VMEM((D_MODEL, 2 * D_FF), BF16), pltpu.VMEM((D_FF, D_MODEL), BF16),
                        pltpu.VMEM((2, TE, D_MODEL), F32), pltpu.SemaphoreType.DMA((2,))],
    )
    return pl.pallas_call(
        _moe_kernel,
        name="expert_ffn",
        grid_spec=grid_spec,
        out_shape=jax.ShapeDtypeStruct((P_MAX, D_MODEL), F32),
        compiler_params=pltpu.CompilerParams(
            dimension_semantics=("arbitrary",), vmem_limit_bytes=VMEM_LIMIT),
    )(tile_expert, tile_valid, src, src, h2, e_w1,
      e_b1.reshape(L, N_EXPERTS, 1, 2 * D_FF), e_w2, e_b2.reshape(L, N_EXPERTS, 1, D_MODEL))


def _combine_kernel(pos_ref, pos_next_ref, ys_ref, wts_ref, x1_ref, mod_ref, gpost_ref, out_ref,
                    buf, sem):
    i = pl.program_id(0)
    slot = i % 2

    def gather(p_ref, s):
        def issue(t, carry):
            for k in range(TOP_K):
                _row_copy(ys_ref, p_ref[t * TOP_K + k], buf.at[s, k], t, sem.at[s]).start(priority=k % 2)
            return carry
        lax.fori_loop(0, TM, issue, 0, unroll=DMA_UNROLL)

    @pl.when(i == 0)
    def _():
        gather(pos_ref, 0)

    @pl.when(i + 1 < N_TILES)
    def _():
        gather(pos_next_ref, 1 - slot)

    for k in range(TOP_K):
        pltpu.make_async_copy(ys_ref.at[pl.ds(0, TM), :], buf.at[slot, k], sem.at[slot]).wait()

    w = wts_ref[...]
    y = w[:, 0:1] * buf[slot, 0]
    for k in range(1, TOP_K):
        y = y + w[:, k:k + 1] * buf[slot, k]
    g2 = mod_ref[0][:, 5 * D_MODEL:6 * D_MODEL]
    out_ref[...] = x1_ref[...] + g2 * _rms(y, gpost_ref[...])


def _combine(pos, ys, wts, x1, mods, l, g_post2):
    return pl.pallas_call(
        _combine_kernel,
        name="combine",
        grid=(N_TILES,),
        in_specs=[
            pl.BlockSpec((TM * TOP_K,), lambda i: (i,), memory_space=pltpu.SMEM),
            pl.BlockSpec((TM * TOP_K,), lambda i: (jnp.minimum(i + 1, N_TILES - 1),),
                         memory_space=pltpu.SMEM),
            pl.BlockSpec(memory_space=pl.ANY),
            pl.BlockSpec((TM, TOP_K), lambda i: (i, 0)),
            pl.BlockSpec((TM, D_MODEL), lambda i: (i, 0)),
            pl.BlockSpec((1, 1, 6 * D_MODEL), lambda i: (l * MOD_ROWS + _mod_row(i), 0, 0)),
            pl.BlockSpec((1, D_MODEL), lambda i: (0, 0)),
        ],
        out_specs=pl.BlockSpec((TM, D_MODEL), lambda i: (i, 0)),
        out_shape=jax.ShapeDtypeStruct((N_TOK, D_MODEL), F32),
        scratch_shapes=[pltpu.VMEM((2, TOP_K, TM, D_MODEL), F32), pltpu.SemaphoreType.DMA((2,))],
        compiler_params=pltpu.CompilerParams(
            dimension_semantics=("arbitrary",), vmem_limit_bytes=VMEM_LIMIT),
    )(pos, pos, ys, wts, x1, mods, g_post2)


def _rope_tables():
    t = np.arange(DEC_SEQ)
    r = (t // GRID_W).astype(np.float32)
    c = (t % GRID_W).astype(np.float32)

    def table(dim, reps):
        n_axis = dim // 4
        inv = (ROPE_THETA ** (-np.arange(n_axis, dtype=np.float32) / n_axis)).astype(np.float32)
        ang = np.concatenate([r[:, None] * inv, c[:, None] * inv], axis=-1).astype(np.float32)
        cos, sin = np.cos(ang), np.sin(ang)
        return (np.tile(np.concatenate([cos, cos], -1), (1, reps)),
                np.tile(np.concatenate([-sin, sin], -1), (1, reps)))

    cos_a, sin_a = table(A_HEAD_DIM, 256 // A_HEAD_DIM)
    cos_b, sin_b = table(B_HALF_DIM, 256 // B_HALF_DIM)
    return tuple(jnp.asarray(a, F32) for a in (cos_a, sin_a, cos_b, sin_b))


def _group_mean_matrix():
    idx = np.arange(256) // A_HEAD_DIM
    return jnp.asarray((idx[:, None] == idx[None, :]).astype(np.float32) / A_HEAD_DIM, BF16)


def _block_diag(w):
    out = jnp.zeros((BRANCH_W, BRANCH_W), w.dtype)
    for g in range(D_GROUPS):
        out = out.at[g * D_GROUP_W:(g + 1) * D_GROUP_W, g * D_GROUP_W:(g + 1) * D_GROUP_W].set(w[g])
    return out


def _routing_tables(eid, rank, cnt):
    cnt = cnt.reshape(N_EXPERTS).astype(jnp.int32)
    padded = ((cnt + TE - 1) // TE) * TE
    ends = jnp.cumsum(padded)
    starts = ends - padded
    experts = jnp.arange(N_EXPERTS, dtype=jnp.int32)
    start_of = jnp.sum(jnp.where(eid[:TOP_K, :, None] == experts, starts, 0), axis=-1)
    pos = (start_of + rank[:TOP_K]).T.reshape(N_SLOTS)
    tile_row = jnp.arange(E_TILES, dtype=jnp.int32) * TE
    tile_expert = jnp.sum((tile_row[:, None] >= ends[None, :]).astype(jnp.int32), axis=-1)
    tile_expert = jnp.minimum(tile_expert, N_EXPERTS - 1)
    tile_valid = (tile_row < ends[-1]).astype(jnp.int32)
    pos = pos.astype(jnp.int32)
    token = jnp.repeat(jnp.arange(N_TOK, dtype=jnp.int32), TOP_K)
    src = jnp.zeros((P_MAX,), jnp.int32).at[pos].set(token, unique_indices=True)
    return pos, src, tile_expert.astype(jnp.int32), tile_valid


def kernel(x_prompt, x_sample, cache_a_k, cache_a_v, cache_b_k, cache_b_v, c, c_ctx, w_mod, b_mod, g_pre1, g_post1, g_pre2, g_post2, w_in, a_q_gain, a_k_gain, b_lambda, b_subln_gain, c_dw_w, c_dw_b, c_ln_g, c_ln_b, d_w_group, d_scale, w_branch, w_out, r_w, r_b, e_w1, e_b1, e_w2, e_b2):
    x = jnp.concatenate([x_prompt.reshape(N_CTX, D_MODEL), x_sample.reshape(N_LAT, D_MODEL)], axis=0)
    cond8 = jnp.concatenate(
        [c_ctx[None, :], c, jnp.zeros((MOD_ROWS - 1 - DEC_BATCH, D_MODEL), F32)], axis=0)
    mods = _modulation(cond8, w_mod, b_mod).reshape(DEPTH * MOD_ROWS, 1, 6 * D_MODEL)

    rope = _rope_tables()
    gmat = _group_mean_matrix()
    caches = (
        cache_a_k.reshape(DEC_BATCH, DEPTH, PAST_LEN, A_KV_HEADS * A_HEAD_DIM),
        cache_a_v.reshape(DEC_BATCH, DEPTH, PAST_LEN, A_KV_HEADS * A_HEAD_DIM),
        cache_b_k.reshape(DEC_BATCH, DEPTH, PAST_LEN, B_HEADS * 2 * B_HALF_DIM),
        cache_b_v.reshape(DEC_BATCH, DEPTH, PAST_LEN, B_HEADS * B_V_DIM),
    )

    new_ak, new_av, new_bk, new_bv = [], [], [], []
    row = lambda v: v.reshape(1, -1)
    for l in range(DEPTH):
        lambda_init = 0.8 - 0.6 * math.exp(-0.3 * l)
        w_small = w_in[l, :, :N_SMALL].astype(BF16)
        w_gate = w_in[l, :, N_SMALL:].astype(BF16)
        z, h = _in_projection(
            x, mods, l, row(g_pre1[l]), w_small,
            row(jnp.tile(a_q_gain[l], A_HEADS)), row(jnp.tile(a_k_gain[l], A_KV_HEADS)), gmat, rope)

        zc = z[:N_CTX]
        new_ak.append(zc[:, COL_KA:COL_VA].reshape(BATCH, SEQ, A_KV_HEADS, A_HEAD_DIM))
        new_av.append(zc[:, COL_VA:COL_QB].reshape(BATCH, SEQ, A_KV_HEADS, A_HEAD_DIM))
        new_bk.append(zc[:, COL_KB:COL_VB].reshape(BATCH, SEQ, B_HEADS, 2, B_HALF_DIM))
        new_bv.append(zc[:, COL_VB:COL_ZC].reshape(BATCH, SEQ, B_HEADS, B_V_DIM))

        subg = row(b_subln_gain[l])
        oa_c, ob_c = _attention_ctx(z, b_lambda[l], subg, lambda_init)
        oa_l, ob_l = _attention_lat(z, caches, l, b_lambda[l], subg, lambda_init)
        conv_args = (c_dw_w[l], row(c_dw_b[l]), row(c_ln_g[l]), row(c_ln_b[l]),
                     _block_diag(d_w_group[l]).astype(BF16), row(d_scale[l]))
        oc_c, od_c = _conv_pool(z, SEQ, BATCH, 0, *conv_args)
        oc_l, od_l = _conv_pool(z, DEC_SEQ, DEC_BATCH, N_CTX, *conv_args)

        x1, h2, eid, rank, wts, cnt = _mix(
            x, h, (oa_c, ob_c, oc_c, od_c), (oa_l, ob_l, oc_l, od_l), mods, l,
            w_gate, w_branch[l].astype(BF16), w_out[l].astype(BF16),
            row(g_post1[l]), row(g_pre2[l]), r_w[l].T, r_b[l].reshape(N_EXPERTS, 1))
        wts = wts[:TOP_K].T

        pos, src, tile_expert, tile_valid = _routing_tables(eid, rank, cnt)
        ys = _expert_ffn(tile_expert, tile_valid, src, h2, l, e_w1, e_b1, e_w2, e_b2)
        x = _combine(pos, ys, wts, x1, mods, l, row(g_post2[l]))

    y_prompt = x[:N_CTX].reshape(BATCH, SEQ, D_MODEL)
    y_sample = x[N_CTX:].reshape(DEC_BATCH, DEC_SEQ, D_MODEL)
    return (y_prompt, y_sample, jnp.stack(new_ak, axis=1), jnp.stack(new_av, axis=1),
            jnp.stack(new_bk, axis=1), jnp.stack(new_bv, axis=1))
```

```python
import functools
import math

import jax
import jax.numpy as jnp
import numpy as np
from jax import lax
from jax.experimental import pallas as pl
from jax.experimental.pallas import tpu as pltpu

F32 = jnp.float32
BF16 = jnp.bfloat16

D_MODEL = 1024
BATCH = 16
SEQ = 256
DEPTH = 2
DEC_BATCH = 4
DEC_SEQ = 2048
PAST_LEN = 256
GRID_W = 64
ROPE_THETA = 10000.0
N_BRANCH = 4
BRANCH_W = D_MODEL // 4
A_HEADS = 4
A_KV_HEADS = 2
A_HEAD_DIM = 64
B_HEADS = 4
B_HALF_DIM = 32
B_V_DIM = 2 * B_HALF_DIM
C_CONV_WIDTH = 31
D_GROUPS = 4
D_GROUP_W = BRANCH_W // D_GROUPS
POOL_WINDOWS = (2, 4, 8, 16)
N_EXPERTS = 32
TOP_K = 4
D_FF = D_MODEL
SWIGLU_LIMIT = 7.0
SWIGLU_ALPHA = 1.702
EPS = 1e-6

N_CTX = BATCH * SEQ
N_LAT = DEC_BATCH * DEC_SEQ
N_TOK = N_CTX + N_LAT

COL_QA, COL_KA, COL_VA = 0, 256, 384
COL_QB, COL_KB, COL_VB = 512, 768, 1024
COL_ZC, COL_ZD = 1280, 1792
N_SMALL = 2048
N_GATE = N_BRANCH * D_MODEL

TM = 256
N_TILES = N_TOK // TM
CTX_TILES = N_CTX // TM
LAT_TILES_PER_SEQ = DEC_SEQ // TM
MOD_ROWS = 8

TE = 512
N_SLOTS = N_TOK * TOP_K
P_MAX = N_SLOTS + N_EXPERTS * TE
E_TILES = P_MAX // TE

TQ = 256
CONV_PAD = 16
CONV_CHUNK = 128

VMEM_LIMIT = 56 * 1024 * 1024


def _sigmoid(x):
    return 1.0 / (1.0 + jnp.exp(-x))


def _split_bf16(a):
    hi = a.astype(BF16)
    lo = (a - hi.astype(F32)).astype(BF16)
    return hi, lo


def _dot(a, b):
    return jnp.dot(a, b, preferred_element_type=F32)


def _dot_nt(a, b):
    return lax.dot_general(a, b, (((1,), (1,)), ((), ())), preferred_element_type=F32)


def _dot3(a, b):
    ah, al = _split_bf16(a)
    bh, bl = _split_bf16(b)
    return _dot(ah, bh) + _dot(ah, bl) + _dot(al, bh)


def _rms(x, g):
    return x * lax.rsqrt(jnp.mean(x * x, axis=-1, keepdims=True) + EPS) * g


def _mod_row(i, ctx_tiles=CTX_TILES, tiles_per_seq=LAT_TILES_PER_SEQ):
    return jnp.where(i < ctx_tiles, 0, 1 + (i - ctx_tiles) // tiles_per_seq)


def _ctx_rows(width, tile=TM):
    last = N_CTX // tile - 1
    return pl.BlockSpec((tile, width), lambda i: (jnp.minimum(i, last), 0))


def _lat_rows(width, tile=TM):
    first = N_CTX // tile
    return pl.BlockSpec((tile, width), lambda i: (jnp.maximum(i - first, 0), 0))


MOD_TN = 1536


def _mod_kernel(cond_ref, w_ref, b_ref, o_ref):
    c = cond_ref[...]
    s = c * _sigmoid(c)
    o_ref[0] = _dot3(s, w_ref[0]) + b_ref[0]


def _modulation(cond8, w_mod, b_mod):
    L = w_mod.shape[0]
    return pl.pallas_call(
        _mod_kernel,
        name="modulation",
        grid=(L, 6 * D_MODEL // MOD_TN),
        in_specs=[
            pl.BlockSpec((MOD_ROWS, D_MODEL), lambda l, j: (0, 0)),
            pl.BlockSpec((1, D_MODEL, MOD_TN), lambda l, j: (l, 0, j)),
            pl.BlockSpec((1, 1, MOD_TN), lambda l, j: (l, 0, j)),
        ],
        out_specs=pl.BlockSpec((1, MOD_ROWS, MOD_TN), lambda l, j: (l, 0, j)),
        out_shape=jax.ShapeDtypeStruct((L, MOD_ROWS, 6 * D_MODEL), F32),
        compiler_params=pltpu.CompilerParams(
            dimension_semantics=("arbitrary", "arbitrary"), vmem_limit_bytes=VMEM_LIMIT),
    )(cond8, w_mod, b_mod.reshape(L, 1, 6 * D_MODEL))


def _group_mean(sq, gmat):
    hi, lo = _split_bf16(sq)
    return _dot(hi, gmat) + _dot(lo, gmat)


def _swap_halves(x, half):
    w = x.shape[-1]
    lane = lax.broadcasted_iota(jnp.int32, x.shape, 1)
    first = (lane % (2 * half)) < half
    return jnp.where(first, pltpu.roll(x, w - half, 1), pltpu.roll(x, half, 1))


def _inproj_kernel(xc_ref, xl_ref, mod_ref, gpre_ref, w_ref, gq_ref, gk_ref, gmat_ref,
                   cos_a_ref, sin_a_ref, cos_b_ref, sin_b_ref, z_ref, h_ref):
    i = pl.program_id(0)
    m = mod_ref[0]
    sh1 = m[:, 0:D_MODEL]
    sc1 = m[:, D_MODEL:2 * D_MODEL]
    x = jnp.where(i < CTX_TILES, xc_ref[...], xl_ref[...])
    h = _rms(x, gpre_ref[...]) * (1.0 + sc1) + sh1
    hb = h.astype(BF16)
    h_ref[...] = hb
    z = _dot(hb, w_ref[0])

    gmat = gmat_ref[...]
    qa = z[:, COL_QA:COL_KA]
    ka = z[:, COL_KA:COL_VA]
    qa = qa * lax.rsqrt(_group_mean(qa * qa, gmat) + EPS) * gq_ref[...]
    ka = ka * lax.rsqrt(_group_mean(ka * ka, gmat[0:128, 0:128]) + EPS) * gk_ref[...]
    qb = z[:, COL_QB:COL_KB]
    kb = z[:, COL_KB:COL_VB]
    z_ref[:, COL_VA:COL_QB] = z[:, COL_VA:COL_QB]
    z_ref[:, COL_VB:N_SMALL] = z[:, COL_VB:N_SMALL]

    @pl.when(i < CTX_TILES)
    def _():
        z_ref[:, COL_QA:COL_KA] = qa
        z_ref[:, COL_KA:COL_VA] = ka
        z_ref[:, COL_QB:COL_KB] = qb
        z_ref[:, COL_KB:COL_VB] = kb

    @pl.when(i >= CTX_TILES)
    def _():
        cos_a = cos_a_ref[...]
        sin_a = sin_a_ref[...]
        cos_b = cos_b_ref[...]
        sin_b = sin_b_ref[...]
        ha, hb_ = A_HEAD_DIM // 2, B_HALF_DIM // 2
        z_ref[:, COL_QA:COL_KA] = qa * cos_a + _swap_halves(qa, ha) * sin_a
        z_ref[:, COL_KA:COL_VA] = ka * cos_a[:, 0:128] + _swap_halves(ka, ha) * sin_a[:, 0:128]
        z_ref[:, COL_QB:COL_KB] = qb * cos_b + _swap_halves(qb, hb_) * sin_b
        z_ref[:, COL_KB:COL_VB] = kb * cos_b + _swap_halves(kb, hb_) * sin_b


def _rope_block(i):
    return jnp.where(i < CTX_TILES, 0, (i - CTX_TILES) % LAT_TILES_PER_SEQ)


def _in_projection(x_ctx, x_lat, mods, l, g_pre, w_in_bf, gq, gk, gmat, rope):
    const = lambda i: (0, 0)
    rope_spec = pl.BlockSpec((TM, 256), lambda i: (_rope_block(i), 0))
    return pl.pallas_call(
        _inproj_kernel,
        name="in_projection",
        grid=(N_TILES,),
        in_specs=[
            _ctx_rows(D_MODEL), _lat_rows(D_MODEL),
            pl.BlockSpec((1, 1, 6 * D_MODEL), lambda i: (l * MOD_ROWS + _mod_row(i), 0, 0)),
            pl.BlockSpec((1, D_MODEL), const),
            pl.BlockSpec((1, D_MODEL, N_SMALL), lambda i: (l, 0, 0)),
            pl.BlockSpec((1, 256), const),
            pl.BlockSpec((1, 128), const),
            pl.BlockSpec((256, 256), const),
            rope_spec, rope_spec, rope_spec, rope_spec,
        ],
        out_specs=[
            pl.BlockSpec((TM, N_SMALL), lambda i: (i, 0)),
            pl.BlockSpec((TM, D_MODEL), lambda i: (i, 0)),
        ],
        out_shape=[
            jax.ShapeDtypeStruct((N_TOK, N_SMALL), F32),
            jax.ShapeDtypeStruct((N_TOK, D_MODEL), BF16),
        ],
        compiler_params=pltpu.CompilerParams(
            dimension_semantics=("arbitrary",), vmem_limit_bytes=VMEM_LIMIT),
    )(x_ctx, x_lat, mods, g_pre, w_in_bf, gq, gk, gmat, *rope)


LOG2E = 1.4426950408889634
HALF_LANES = 64


def _attend(q, ks, vexts):
    ss = [_dot_nt(q, k) for k in ks]
    m = ss[0].max(axis=-1, keepdims=True)
    for s in ss[1:]:
        m = jnp.maximum(m, s.max(axis=-1, keepdims=True))
    o = None
    for s, v in zip(ss, vexts):
        t = _dot(jnp.exp2(s - m).astype(BF16), v)
        o = t if o is None else o + t
    return o


def _value_ext(v128, upper):
    lane = lax.broadcasted_iota(jnp.int32, v128.shape, 1)
    keep = (lane >= HALF_LANES) if upper else (lane < HALF_LANES)
    return jnp.where(keep, v128, 1.0).astype(BF16)


def _normalised(o, upper):
    if upper:
        return o[:, HALF_LANES:] / o[:, 0:1]
    return o[:, :HALF_LANES] / o[:, HALF_LANES:HALF_LANES + 1]


def _attn_kernel(*refs, has_cache, lambda_init, n_seq=1):
    if n_seq > 1:
        rows = refs[0].shape[0] // n_seq
        for s in range(n_seq):
            sub = [r.at[pl.ds(s * rows, rows), :] for r in refs[:6]]
            outs = [r.at[pl.ds(s * rows, rows), :] for r in refs[-2:]]
            _attn_kernel(*sub, *refs[6:-2], *outs, has_cache=has_cache, lambda_init=lambda_init)
        return
    if has_cache:
        (qa_ref, ka_ref, va_ref, qb_ref, kb_ref, vb_ref,
         cka_ref, cva_ref, ckb_ref, cvb_ref, lam_ref, subg_ref, oa_ref, ob_ref) = refs
    else:
        (qa_ref, ka_ref, va_ref, qb_ref, kb_ref, vb_ref,
         lam_ref, subg_ref, oa_ref, ob_ref) = refs

    def cols(ref, lo, hi):
        return ref[:, lo:hi].astype(BF16)

    def ccols(ref, lo, hi):
        return ref[0, 0, :, lo:hi].astype(BF16)

    scale_a = A_HEAD_DIM ** -0.5 * LOG2E
    group = A_HEADS // A_KV_HEADS
    for g in range(A_KV_HEADS):
        lo, hi = g * A_HEAD_DIM, (g + 1) * A_HEAD_DIM
        upper = g % 2 == 1
        ks = [cols(ka_ref, lo, hi)]
        vexts = [_value_ext(va_ref[...], upper)]
        if has_cache:
            ks.append(ccols(cka_ref, lo, hi))
            vexts.append(_value_ext(cva_ref[0, 0], upper))
        for hd in range(g * group, (g + 1) * group):
            q = (qa_ref[:, hd * A_HEAD_DIM:(hd + 1) * A_HEAD_DIM] * scale_a).astype(BF16)
            o = _attend(q, ks, vexts)
            oa_ref[:, hd * A_HEAD_DIM:(hd + 1) * A_HEAD_DIM] = _normalised(o, upper)

    bl = lam_ref[...]
    lam = (jnp.exp(jnp.sum(bl[0:1] * bl[1:2], axis=-1, keepdims=True))
           - jnp.exp(jnp.sum(bl[2:3] * bl[3:4], axis=-1, keepdims=True)) + lambda_init)
    scale_b = B_HALF_DIM ** -0.5 * LOG2E
    subg = subg_ref[...]
    for hd in range(B_HEADS):
        base = hd * 2 * B_HALF_DIM
        vlo, vhi = hd * B_V_DIM, (hd + 1) * B_V_DIM
        blk = (hd // 2) * 2 * B_V_DIM
        upper = hd % 2 == 1
        vexts = [_value_ext(vb_ref[:, blk:blk + 2 * B_V_DIM], upper)]
        if has_cache:
            vexts.append(_value_ext(cvb_ref[0, 0, :, blk:blk + 2 * B_V_DIM], upper))
        outs = []
        for c in range(2):
            lo, hi = base + c * B_HALF_DIM, base + (c + 1) * B_HALF_DIM
            q = (qb_ref[:, lo:hi] * scale_b).astype(BF16)
            ks = [cols(kb_ref, lo, hi)]
            if has_cache:
                ks.append(ccols(ckb_ref, lo, hi))
            outs.append(_normalised(_attend(q, ks, vexts), upper))
        o = outs[0] - lam * outs[1]
        o = _rms(o, subg) * (1.0 - lambda_init)
        ob_ref[:, vlo:vhi] = o


CTX_SEQ_PER_STEP = 4


def _attention_ctx(z, b_lambda, subg, lambda_init):
    rows = CTX_SEQ_PER_STEP * SEQ

    def zspec(width, col):
        return pl.BlockSpec((rows, width), lambda b: (b, col // width))
    out_spec = pl.BlockSpec((rows, BRANCH_W), lambda b: (b, 0))
    return pl.pallas_call(
        functools.partial(_attn_kernel, has_cache=False, lambda_init=lambda_init,
                          n_seq=CTX_SEQ_PER_STEP),
        name="attention_ctx",
        grid=(BATCH // CTX_SEQ_PER_STEP,),
        in_specs=[
            zspec(256, COL_QA), zspec(128, COL_KA), zspec(128, COL_VA),
            zspec(256, COL_QB), zspec(256, COL_KB), zspec(256, COL_VB),
            pl.BlockSpec((4, B_HALF_DIM), lambda b: (0, 0)),
            pl.BlockSpec((1, B_V_DIM), lambda b: (0, 0)),
        ],
        out_specs=[out_spec, out_spec],
        out_shape=[jax.ShapeDtypeStruct((N_CTX, BRANCH_W), F32)] * 2,
        compiler_params=pltpu.CompilerParams(
            dimension_semantics=("arbitrary",), vmem_limit_bytes=VMEM_LIMIT),
    )(z, z, z, z, z, z, b_lambda, subg)


def _attention_lat(z, caches, l, b_lambda, subg, lambda_init):
    q_tiles = DEC_SEQ // TQ
    seq0 = N_CTX // DEC_SEQ

    def qspec(col):
        return pl.BlockSpec((TQ, 256), lambda b, t: (CTX_TILES + b * q_tiles + t, col // 256))

    def kspec(width, col):
        return pl.BlockSpec((DEC_SEQ, width), lambda b, t: (seq0 + b, col // width))

    def cspec(width):
        return pl.BlockSpec((1, 1, PAST_LEN, width), lambda b, t: (b, l, 0, 0))

    out_spec = pl.BlockSpec((TQ, BRANCH_W), lambda b, t: (b * q_tiles + t, 0))
    cka, cva, ckb, cvb = caches
    return pl.pallas_call(
        functools.partial(_attn_kernel, has_cache=True, lambda_init=lambda_init),
        name="attention_lat",
        grid=(DEC_BATCH, q_tiles),
        in_specs=[
            qspec(COL_QA), kspec(128, COL_KA), kspec(128, COL_VA),
            qspec(COL_QB), kspec(256, COL_KB), kspec(256, COL_VB),
            cspec(128), cspec(128), cspec(256), cspec(256),
            pl.BlockSpec((4, B_HALF_DIM), lambda b, t: (0, 0)),
            pl.BlockSpec((1, B_V_DIM), lambda b, t: (0, 0)),
        ],
        out_specs=[out_spec, out_spec],
        out_shape=[jax.ShapeDtypeStruct((N_LAT, BRANCH_W), F32)] * 2,
        compiler_params=pltpu.CompilerParams(
            dimension_semantics=("arbitrary", "arbitrary"), vmem_limit_bytes=VMEM_LIMIT),
    )(z, z, z, z, z, z, cka, cva, ckb, cvb, b_lambda, subg)


SUBLANES = 8


def _row_shifter(win):
    span = win.shape[0] - SUBLANES
    shifted = {}

    def rows(off):
        s = off % SUBLANES
        if s not in shifted:
            shifted[s] = win[s:s + span, :]
        base = off - s
        return shifted[s][base:base + CONV_CHUNK, :]

    return rows


def _convpool_kernel(a_ref, b_ref, d_ref, dww_ref, dwb_ref, lng_ref, lnb_ref, wbd_ref, dsc_ref,
                     oc_ref, od_ref, hpad, upad, *, seq_len):
    n_chunks = seq_len // CONV_CHUNK
    zeros = jnp.zeros((CONV_PAD, BRANCH_W), F32)
    hpad[0:CONV_PAD, :] = zeros
    hpad[seq_len + CONV_PAD:seq_len + 2 * CONV_PAD, :] = zeros
    upad[0:CONV_PAD, :] = zeros
    upad[seq_len + CONV_PAD:seq_len + 2 * CONV_PAD, :] = zeros

    def fill(c, carry):
        r = pl.multiple_of(c * CONV_CHUNK, CONV_CHUNK)
        a = a_ref[pl.ds(r, CONV_CHUNK), :]
        b = b_ref[pl.ds(r, CONV_CHUNK), :]
        hpad[pl.ds(r + CONV_PAD, CONV_CHUNK), :] = a * _sigmoid(b)
        upad[pl.ds(r + CONV_PAD, CONV_CHUNK), :] = d_ref[pl.ds(r, CONV_CHUNK), :]
        return carry

    lax.fori_loop(0, n_chunks, fill, 0)

    lane = lax.broadcasted_iota(jnp.int32, (CONV_CHUNK, BRANCH_W), 1)
    half = C_CONV_WIDTH // 2

    def body(c, carry):
        r = pl.multiple_of(c * CONV_CHUNK, CONV_CHUNK)
        hrows = _row_shifter(hpad[pl.ds(r, CONV_CHUNK + 2 * CONV_PAD), :])
        acc = jnp.zeros((CONV_CHUNK, BRANCH_W), F32)
        for k in range(C_CONV_WIDTH):
            acc = acc + dww_ref[k:k + 1, :] * hrows(CONV_PAD - half + k)
        acc = acc + dwb_ref[...]
        mu = jnp.mean(acc, axis=-1, keepdims=True)
        cen = acc - mu
        var = jnp.mean(cen * cen, axis=-1, keepdims=True)
        y = cen * lax.rsqrt(var + EPS) * lng_ref[...] + lnb_ref[...]
        oc_ref[pl.ds(r, CONV_CHUNK), :] = y * _sigmoid(y)

        urows = _row_shifter(upad[pl.ds(r, CONV_CHUNK + 2 * CONV_PAD), :])

        def ld(d):
            return urows(CONV_PAD + d)

        u = ld(0)
        sums = {}
        s = u + ld(-1)
        sums[2] = s
        s = s + ld(-2) + ld(1)
        sums[4] = s
        s = s + ld(-4) + ld(-3) + ld(2) + ld(3)
        sums[8] = s
        s = s + ld(-8) + ld(-7) + ld(-6) + ld(-5) + ld(4) + ld(5) + ld(6) + ld(7)
        sums[16] = s
        t = r + lax.broadcasted_iota(jnp.int32, (CONV_CHUNK, 1), 0)
        pooled = None
        for g, w in reversed(list(enumerate(POOL_WINDOWS))):
            lo = jnp.maximum(t - w // 2, 0)
            hi = jnp.minimum(t + (w - 1 - w // 2), seq_len - 1)
            mean = sums[w] / (hi - lo + 1).astype(F32)
            pooled = mean if pooled is None else jnp.where(lane < (g + 1) * D_GROUP_W, mean, pooled)
        dlt = (pooled - u).astype(BF16)
        od_ref[pl.ds(r, CONV_CHUNK), :] = _dot(dlt, wbd_ref[...]) * dsc_ref[...]
        return carry

    lax.fori_loop(0, n_chunks, body, 0)


def _conv_pool(z, seq_len, n_seq, row0, dww, dwb, lng, lnb, wbd, dsc):
    blk0 = row0 // seq_len

    def zspec(col):
        return pl.BlockSpec((seq_len, 256), lambda b: (blk0 + b, col // 256))

    def const(shape):
        return pl.BlockSpec(shape, lambda b: (0, 0))

    out_spec = pl.BlockSpec((seq_len, BRANCH_W), lambda b: (b, 0))
    return pl.pallas_call(
        functools.partial(_convpool_kernel, seq_len=seq_len),
        name=f"conv_pool_{seq_len}",
        grid=(n_seq,),
        in_specs=[
            zspec(COL_ZC), zspec(COL_ZC + BRANCH_W), zspec(COL_ZD),
            const((C_CONV_WIDTH, BRANCH_W)), const((1, BRANCH_W)), const((1, BRANCH_W)),
            const((1, BRANCH_W)), const((BRANCH_W, BRANCH_W)), const((1, BRANCH_W)),
        ],
        out_specs=[out_spec, out_spec],
        out_shape=[jax.ShapeDtypeStruct((n_seq * seq_len, BRANCH_W), F32)] * 2,
        scratch_shapes=[pltpu.VMEM((seq_len + 2 * CONV_PAD, BRANCH_W), F32)] * 2,
        compiler_params=pltpu.CompilerParams(
            dimension_semantics=("arbitrary",), vmem_limit_bytes=VMEM_LIMIT),
    )(z, z, z, dww, dwb, lng, lnb, wbd, dsc)


MIX_TM = 512
MIX_TILES = N_TOK // MIX_TM
MIX_CTX_TILES = N_CTX // MIX_TM
MIX_LAT_PER_SEQ = DEC_SEQ // MIX_TM
ROUTE_ROWS = 8


def _mix_kernel(xc_ref, xl_ref, h_ref, oa_c, ob_c, oc_c, od_c, oa_l, ob_l, oc_l, od_l,
                mod_ref, wg01_ref, wg23_ref, wbr_ref, wout_ref, gpost_ref, gpre2_ref, rwt_ref, rb_ref,
                x1_ref, h2_ref, eid_ref, rank_ref, wts_ref, cnt_ref, carry):
    i = pl.program_id(0)
    is_ctx = i < MIX_CTX_TILES

    @pl.when(i == 0)
    def _():
        carry[...] = jnp.zeros_like(carry)

    m = mod_ref[0]
    g1 = m[:, 2 * D_MODEL:3 * D_MODEL]
    sh2 = m[:, 3 * D_MODEL:4 * D_MODEL]
    sc2 = m[:, 4 * D_MODEL:5 * D_MODEL]

    hb = h_ref[...]
    merged = None
    for n, (c_ref, l_ref) in enumerate(((oa_c, oa_l), (ob_c, ob_l), (oc_c, oc_l), (od_c, od_l))):
        wg_ref = wg01_ref if n < 2 else wg23_ref
        gate = _sigmoid(_dot(hb, wg_ref[0, :, (n % 2) * D_MODEL:(n % 2 + 1) * D_MODEL]))
        o = jnp.where(is_ctx, c_ref[...], l_ref[...])
        br = _dot(o.astype(BF16), wbr_ref[0, n])
        merged = gate * br if merged is None else merged + gate * br
    y = _dot(merged.astype(BF16), wout_ref[0])
    x = jnp.where(is_ctx, xc_ref[...], xl_ref[...])
    x1 = x + g1 * _rms(y, gpost_ref[...])
    x1_ref[...] = x1
    h2 = _rms(x1, gpre2_ref[...]) * (1.0 + sc2) + sh2
    h2_ref[...] = h2

    rh, rl = _split_bf16(rwt_ref[...])
    hh, hl = _split_bf16(h2)
    logits = _dot_nt(rh, hh) + _dot_nt(rh, hl) + _dot_nt(rl, hh) + rb_ref[...]

    iota_e = lax.broadcasted_iota(jnp.int32, (N_EXPERTS, MIX_TM), 0)
    rem = logits
    vals, idxs = [], []
    for _ in range(TOP_K):
        mx = jnp.max(rem, axis=0, keepdims=True)
        idx = jnp.min(jnp.where(rem == mx, iota_e, N_EXPERTS), axis=0, keepdims=True)
        vals.append(mx)
        idxs.append(idx)
        rem = jnp.where(iota_e == idx, -jnp.inf, rem)
    exps = [jnp.exp(v - vals[0]) for v in vals]
    den = exps[0]
    for e in exps[1:]:
        den = den + e

    sel = [iota_e == idx for idx in idxs]
    member = jnp.zeros((N_EXPERTS, MIX_TM), F32)
    for s in sel:
        member = member + jnp.where(s, 1.0, 0.0)
    row = lax.broadcasted_iota(jnp.int32, (MIX_TM, MIX_TM), 0)
    col = lax.broadcasted_iota(jnp.int32, (MIX_TM, MIX_TM), 1)
    earlier = jnp.where(row < col, 1.0, 0.0).astype(BF16)
    seen = _dot(member.astype(BF16), earlier) + carry[...]
    carry[...] = carry[...] + jnp.sum(member, axis=1, keepdims=True)
    cnt_ref[...] = carry[...]

    sub = lax.broadcasted_iota(jnp.int32, (ROUTE_ROWS, MIX_TM), 0)
    eid = jnp.zeros((ROUTE_ROWS, MIX_TM), jnp.int32)
    rank = jnp.zeros((ROUTE_ROWS, MIX_TM), jnp.int32)
    wts = jnp.zeros((ROUTE_ROWS, MIX_TM), F32)
    for k in range(TOP_K):
        rk = jnp.sum(jnp.where(sel[k], seen, 0.0), axis=0, keepdims=True).astype(jnp.int32)
        eid = jnp.where(sub == k, idxs[k], eid)
        rank = jnp.where(sub == k, rk, rank)
        wts = jnp.where(sub == k, exps[k] / den, wts)
    eid_ref[...] = eid
    rank_ref[...] = rank
    wts_ref[...] = wts


def _mix(x_ctx, x_lat, h, o_ctx, o_lat, mods, l, w_in_bf, w_branch_bf, w_out_bf,
         g_post, g_pre2, r_wt, r_b):
    const2 = lambda i: (0, 0)
    tile = lambda width: pl.BlockSpec((MIX_TM, width), lambda i: (i, 0))
    route = pl.BlockSpec((ROUTE_ROWS, MIX_TM), lambda i: (0, i))
    ctx_tile = _ctx_rows(BRANCH_W, MIX_TM)
    lat_tile = _lat_rows(BRANCH_W, MIX_TM)
    mod_row = lambda i: _mod_row(i, MIX_CTX_TILES, MIX_LAT_PER_SEQ)
    single = pl.Buffered(1)
    return pl.pallas_call(
        _mix_kernel,
        name="mix_router",
        grid=(MIX_TILES,),
        in_specs=[
            _ctx_rows(D_MODEL, MIX_TM), _lat_rows(D_MODEL, MIX_TM), tile(D_MODEL),
            ctx_tile, ctx_tile, ctx_tile, ctx_tile, lat_tile, lat_tile, lat_tile, lat_tile,
            pl.BlockSpec((1, 1, 6 * D_MODEL), lambda i: (l * MOD_ROWS + mod_row(i), 0, 0)),
            pl.BlockSpec((1, D_MODEL, N_SMALL), lambda i: (l, 0, 1), pipeline_mode=single),
            pl.BlockSpec((1, D_MODEL, N_SMALL), lambda i: (l, 0, 2), pipeline_mode=single),
            pl.BlockSpec((1, N_BRANCH, BRANCH_W, D_MODEL), lambda i: (l, 0, 0, 0),
                         pipeline_mode=single),
            pl.BlockSpec((1, D_MODEL, D_MODEL), lambda i: (l, 0, 0), pipeline_mode=single),
            pl.BlockSpec((1, D_MODEL), const2),
            pl.BlockSpec((1, D_MODEL), const2),
            pl.BlockSpec((N_EXPERTS, D_MODEL), const2),
            pl.BlockSpec((N_EXPERTS, 1), const2),
        ],
        out_specs=[
            tile(D_MODEL), tile(D_MODEL),
            route, route, route,
            pl.BlockSpec((N_EXPERTS, 1), const2),
        ],
        out_shape=[
            jax.ShapeDtypeStruct((N_TOK, D_MODEL), F32),
            jax.ShapeDtypeStruct((N_TOK, D_MODEL), F32),
            jax.ShapeDtypeStruct((ROUTE_ROWS, N_TOK), jnp.int32),
            jax.ShapeDtypeStruct((ROUTE_ROWS, N_TOK), jnp.int32),
            jax.ShapeDtypeStruct((ROUTE_ROWS, N_TOK), F32),
            jax.ShapeDtypeStruct((N_EXPERTS, 1), F32),
        ],
        scratch_shapes=[pltpu.VMEM((N_EXPERTS, 1), F32)],
        compiler_params=pltpu.CompilerParams(
            dimension_semantics=("arbitrary",), vmem_limit_bytes=VMEM_LIMIT),
    )(x_ctx, x_lat, h, *o_ctx, *o_lat, mods, w_in_bf, w_in_bf, w_branch_bf, w_out_bf,
      g_post, g_pre2, r_wt, r_b)


def _row_copy(src_ref, src_row, dst_ref, dst_row, sem):
    return pltpu.make_async_copy(src_ref.at[pl.ds(src_row, 1), :], dst_ref.at[pl.ds(dst_row, 1), :], sem)


DISPATCH_ROWS = 512
DISPATCH_STEPS = N_TOK // DISPATCH_ROWS
DMA_UNROLL = 8


def _dispatch_kernel(starts_ref, padded_ref, pos_ref, h2_ref, xs_ref, zbuf, sem, zsem, tsem):
    i = pl.program_id(0)

    def pad_copy(e):
        last = pl.multiple_of(starts_ref[e] + padded_ref[e] - TE, TE)
        return pltpu.make_async_copy(zbuf, xs_ref.at[pl.ds(last, TE), :], zsem)

    first_tail = (starts_ref[N_EXPERTS - 1] + padded_ref[N_EXPERTS - 1]) // TE

    def tail_copy(j):
        return pltpu.make_async_copy(zbuf, xs_ref.at[pl.ds(pl.multiple_of(j * TE, TE), TE), :], tsem)

    def tail_start(j, carry):
        tail_copy(j).start()
        return carry

    def tail_wait(j, carry):
        tail_copy(j).wait()
        return carry

    @pl.when(i == 0)
    def _():
        zbuf[...] = jnp.zeros_like(zbuf)
        for e in range(N_EXPERTS):
            @pl.when(padded_ref[e] > 0)
            def _():
                pad_copy(e).start()
        lax.fori_loop(first_tail, E_TILES, tail_start, 0)
        for e in range(N_EXPERTS):
            @pl.when(padded_ref[e] > 0)
            def _():
                pad_copy(e).wait()

    @pl.when(i == DISPATCH_STEPS - 1)
    def _():
        lax.fori_loop(first_tail, E_TILES, tail_wait, 0)

    def issue(t, carry):
        for k in range(TOP_K):
            _row_copy(h2_ref, t, xs_ref, pos_ref[t * TOP_K + k], sem).start(priority=k % 2)
        return carry

    lax.fori_loop(0, DISPATCH_ROWS, issue, 0, unroll=DMA_UNROLL)

    for _ in range(TOP_K):
        pltpu.make_async_copy(h2_ref, xs_ref.at[pl.ds(0, DISPATCH_ROWS), :], sem).wait()


def _dispatch(starts, padded, pos, h2):
    grid_spec = pltpu.PrefetchScalarGridSpec(
        num_scalar_prefetch=2,
        grid=(DISPATCH_STEPS,),
        in_specs=[
            pl.BlockSpec((DISPATCH_ROWS * TOP_K,), lambda i, st, pd: (i,), memory_space=pltpu.SMEM),
            pl.BlockSpec((DISPATCH_ROWS, D_MODEL), lambda i, st, pd: (i, 0)),
        ],
        out_specs=pl.BlockSpec(memory_space=pl.ANY),
        scratch_shapes=[pltpu.VMEM((TE, D_MODEL), F32),
                        pltpu.SemaphoreType.DMA(()), pltpu.SemaphoreType.DMA(()),
                        pltpu.SemaphoreType.DMA(())],
    )
    return pl.pallas_call(
        _dispatch_kernel,
        name="dispatch",
        grid_spec=grid_spec,
        out_shape=jax.ShapeDtypeStruct((P_MAX, D_MODEL), F32),
        compiler_params=pltpu.CompilerParams(
            dimension_semantics=("arbitrary",), vmem_limit_bytes=VMEM_LIMIT),
    )(starts, padded, pos, h2)


W_CAST_ROWS = 128
TE_PART = 256


def _moe_kernel(te_ref, rows_ref, x_ref, w1_ref, b1_ref, w2_ref, b2_ref, y_ref, w1b, w2b):
    j = pl.program_id(0)
    e = te_ref[j]
    prev = te_ref[jnp.maximum(j - 1, 0)]
    n_rows = rows_ref[j]
    valid = n_rows > 0
    new_expert = jnp.logical_or(j == 0, e != prev)

    @pl.when(jnp.logical_and(valid, new_expert))
    def _():
        def cast(c, carry):
            r = pl.multiple_of(c * W_CAST_ROWS, W_CAST_ROWS)
            w1b[pl.ds(r, W_CAST_ROWS), :] = w1_ref[0, 0, pl.ds(r, W_CAST_ROWS), :].astype(BF16)
            w2b[pl.ds(r, W_CAST_ROWS), :] = w2_ref[0, 0, pl.ds(r, W_CAST_ROWS), :].astype(BF16)
            return carry
        lax.fori_loop(0, D_MODEL // W_CAST_ROWS, cast, 0)

    for part in range(TE // TE_PART):
        rows = pl.ds(part * TE_PART, TE_PART)
        used = n_rows > part * TE_PART

        @pl.when(used)
        def _():
            x = x_ref[rows, :].astype(BF16)
            u = _dot(x, w1b[...]) + b1_ref[0, 0]
            xg = jnp.minimum(u[:, :D_FF], SWIGLU_LIMIT)
            xl = jnp.clip(u[:, D_FF:], -SWIGLU_LIMIT, SWIGLU_LIMIT)
            act = (xl + 1.0) * (xg * _sigmoid(SWIGLU_ALPHA * xg))
            y_ref[rows, :] = _dot(act.astype(BF16), w2b[...]) + b2_ref[0, 0]

        @pl.when(jnp.logical_not(used))
        def _():
            y_ref[rows, :] = jnp.zeros((TE_PART, D_MODEL), F32)


def _expert_ffn(tile_expert, tile_rows, xs, l, e_w1, e_b1, e_w2, e_b2):
    L = e_w1.shape[0]
    grid_spec = pltpu.PrefetchScalarGridSpec(
        num_scalar_prefetch=2,
        grid=(E_TILES,),
        in_specs=[
            pl.BlockSpec((TE, D_MODEL), lambda j, te, nr: (jnp.where(nr[j] > 0, j, 0), 0)),
            pl.BlockSpec((1, 1, D_MODEL, 2 * D_FF), lambda j, te, nr: (l, te[j], 0, 0)),
            pl.BlockSpec((1, 1, 1, 2 * D_FF), lambda j, te, nr: (l, te[j], 0, 0)),
            pl.BlockSpec((1, 1, D_FF, D_MODEL), lambda j, te, nr: (l, te[j], 0, 0)),
            pl.BlockSpec((1, 1, 1, D_MODEL), lambda j, te, nr: (l, te[j], 0, 0)),
        ],
        out_specs=pl.BlockSpec((TE, D_MODEL), lambda j, te, nr: (j, 0)),
        scratch_shapes=[pltpu.VMEM((D_MODEL, 2 * D_FF), BF16), pltpu.VMEM((D_FF, D_MODEL), BF16)],
    )
    return pl.pallas_call(
        _moe_kernel,
        name="expert_ffn",
        grid_spec=grid_spec,
        out_shape=jax.ShapeDtypeStruct((P_MAX, D_MODEL), F32),
        compiler_params=pltpu.CompilerParams(
            dimension_semantics=("arbitrary",), vmem_limit_bytes=VMEM_LIMIT),
    )(tile_expert, tile_rows, xs, e_w1,
      e_b1.reshape(L, N_EXPERTS, 1, 2 * D_FF), e_w2, e_b2.reshape(L, N_EXPERTS, 1, D_MODEL))


def _combine_kernel(pos_ref, pos_next_ref, ys_ref, wts_ref, x1_ref, mod_ref, gpost_ref,
                    out_c_ref, out_l_ref, buf, sem):
    i = pl.program_id(0)
    slot = i % 2

    def gather(p_ref, s):
        def issue(t, carry):
            for k in range(TOP_K):
                _row_copy(ys_ref, p_ref[t * TOP_K + k], buf.at[s, k], t, sem.at[s]).start(priority=k % 2)
            return carry
        lax.fori_loop(0, TM, issue, 0, unroll=DMA_UNROLL)

    @pl.when(i == 0)
    def _():
        gather(pos_ref, 0)

    @pl.when(i + 1 < N_TILES)
    def _():
        gather(pos_next_ref, 1 - slot)

    for k in range(TOP_K):
        pltpu.make_async_copy(ys_ref.at[pl.ds(0, TM), :], buf.at[slot, k], sem.at[slot]).wait()

    w = wts_ref[...]
    y = w[:, 0:1] * buf[slot, 0]
    for k in range(1, TOP_K):
        y = y + w[:, k:k + 1] * buf[slot, k]
    g2 = mod_ref[0][:, 5 * D_MODEL:6 * D_MODEL]
    res = x1_ref[...] + g2 * _rms(y, gpost_ref[...])

    @pl.when(i < CTX_TILES)
    def _():
        out_c_ref[...] = res

    @pl.when(i >= CTX_TILES)
    def _():
        out_l_ref[...] = res


def _combine(pos, ys, wts, x1, mods, l, g_post2):
    return pl.pallas_call(
        _combine_kernel,
        name="combine",
        grid=(N_TILES,),
        in_specs=[
            pl.BlockSpec((TM * TOP_K,), lambda i: (i,), memory_space=pltpu.SMEM),
            pl.BlockSpec((TM * TOP_K,), lambda i: (jnp.minimum(i + 1, N_TILES - 1),),
                         memory_space=pltpu.SMEM),
            pl.BlockSpec(memory_space=pl.ANY),
            pl.BlockSpec((TM, TOP_K), lambda i: (i, 0)),
            pl.BlockSpec((TM, D_MODEL), lambda i: (i, 0)),
            pl.BlockSpec((1, 1, 6 * D_MODEL), lambda i: (l * MOD_ROWS + _mod_row(i), 0, 0)),
            pl.BlockSpec((1, D_MODEL), lambda i: (0, 0)),
        ],
        out_specs=[_ctx_rows(D_MODEL), _lat_rows(D_MODEL)],
        out_shape=[jax.ShapeDtypeStruct((N_CTX, D_MODEL), F32),
                   jax.ShapeDtypeStruct((N_LAT, D_MODEL), F32)],
        scratch_shapes=[pltpu.VMEM((2, TOP_K, TM, D_MODEL), F32), pltpu.SemaphoreType.DMA((2,))],
        compiler_params=pltpu.CompilerParams(
            dimension_semantics=("arbitrary",), vmem_limit_bytes=VMEM_LIMIT),
    )(pos, pos, ys, wts, x1, mods, g_post2)


def _rope_tables():
    t = np.arange(DEC_SEQ)
    r = (t // GRID_W).astype(np.float32)
    c = (t % GRID_W).astype(np.float32)

    def table(dim, reps):
        n_axis = dim // 4
        inv = (ROPE_THETA ** (-np.arange(n_axis, dtype=np.float32) / n_axis)).astype(np.float32)
        ang = np.concatenate([r[:, None] * inv, c[:, None] * inv], axis=-1).astype(np.float32)
        cos, sin = np.cos(ang), np.sin(ang)
        return (np.tile(np.concatenate([cos, cos], -1), (1, reps)),
                np.tile(np.concatenate([-sin, sin], -1), (1, reps)))

    cos_a, sin_a = table(A_HEAD_DIM, 256 // A_HEAD_DIM)
    cos_b, sin_b = table(B_HALF_DIM, 256 // B_HALF_DIM)
    return tuple(jnp.asarray(a, F32) for a in (cos_a, sin_a, cos_b, sin_b))


def _group_mean_matrix():
    idx = np.arange(256) // A_HEAD_DIM
    return jnp.asarray((idx[:, None] == idx[None, :]).astype(np.float32) / A_HEAD_DIM, BF16)


def _block_diag(w):
    out = jnp.zeros((BRANCH_W, BRANCH_W), w.dtype)
    for g in range(D_GROUPS):
        out = out.at[g * D_GROUP_W:(g + 1) * D_GROUP_W, g * D_GROUP_W:(g + 1) * D_GROUP_W].set(w[g])
    return out


def _routing_tables(eid, rank, cnt):
    cnt = cnt.reshape(N_EXPERTS).astype(jnp.int32)
    padded = ((cnt + TE - 1) // TE) * TE
    ends = jnp.cumsum(padded)
    starts = ends - padded
    experts = jnp.arange(N_EXPERTS, dtype=jnp.int32)
    start_of = jnp.sum(jnp.where(eid[:TOP_K, :, None] == experts, starts, 0), axis=-1)
    pos = (start_of + rank[:TOP_K]).T.reshape(N_SLOTS)
    tile_row = jnp.arange(E_TILES, dtype=jnp.int32) * TE
    tile_expert = jnp.sum((tile_row[:, None] >= ends[None, :]).astype(jnp.int32), axis=-1)
    tile_expert = jnp.minimum(tile_expert, N_EXPERTS - 1)
    used_end = jnp.sum(jnp.where(tile_expert[:, None] == experts, starts + cnt, 0), axis=-1)
    tile_rows = jnp.where(tile_row < ends[-1], jnp.clip(used_end - tile_row, 0, TE), 0)
    return (pos.astype(jnp.int32), tile_expert.astype(jnp.int32), tile_rows.astype(jnp.int32),
            starts.astype(jnp.int32), padded.astype(jnp.int32))


def kernel(x_prompt, x_sample, cache_a_k, cache_a_v, cache_b_k, cache_b_v, c, c_ctx, w_mod, b_mod, g_pre1, g_post1, g_pre2, g_post2, w_in, a_q_gain, a_k_gain, b_lambda, b_subln_gain, c_dw_w, c_dw_b, c_ln_g, c_ln_b, d_w_group, d_scale, w_branch, w_out, r_w, r_b, e_w1, e_b1, e_w2, e_b2):
    x_ctx = x_prompt.reshape(N_CTX, D_MODEL)
    x_lat = x_sample.reshape(N_LAT, D_MODEL)
    cond8 = jnp.concatenate(
        [c_ctx[None, :], c, jnp.zeros((MOD_ROWS - 1 - DEC_BATCH, D_MODEL), F32)], axis=0)
    mods = _modulation(cond8, w_mod, b_mod).reshape(DEPTH * MOD_ROWS, 1, 6 * D_MODEL)

    rope = _rope_tables()
    gmat = _group_mean_matrix()
    caches = (
        cache_a_k.reshape(DEC_BATCH, DEPTH, PAST_LEN, A_KV_HEADS * A_HEAD_DIM),
        cache_a_v.reshape(DEC_BATCH, DEPTH, PAST_LEN, A_KV_HEADS * A_HEAD_DIM),
        cache_b_k.reshape(DEC_BATCH, DEPTH, PAST_LEN, B_HEADS * 2 * B_HALF_DIM),
        cache_b_v.reshape(DEC_BATCH, DEPTH, PAST_LEN, B_HEADS * B_V_DIM),
    )

    w_in_bf = w_in.astype(BF16)
    w_branch_bf = w_branch.astype(BF16)
    w_out_bf = w_out.astype(BF16)

    new_ak, new_av, new_bk, new_bv = [], [], [], []
    row = lambda v: v.reshape(1, -1)
    for l in range(DEPTH):
        lambda_init = 0.8 - 0.6 * math.exp(-0.3 * l)
        z, h = _in_projection(
            x_ctx, x_lat, mods, l, row(g_pre1[l]), w_in_bf,
            row(jnp.tile(a_q_gain[l], A_HEADS)), row(jnp.tile(a_k_gain[l], A_KV_HEADS)), gmat, rope)

        zc = z[:N_CTX]
        new_ak.append(zc[:, COL_KA:COL_VA].reshape(BATCH, SEQ, A_KV_HEADS, A_HEAD_DIM))
        new_av.append(zc[:, COL_VA:COL_QB].reshape(BATCH, SEQ, A_KV_HEADS, A_HEAD_DIM))
        new_bk.append(zc[:, COL_KB:COL_VB].reshape(BATCH, SEQ, B_HEADS, 2, B_HALF_DIM))
        new_bv.append(zc[:, COL_VB:COL_ZC].reshape(BATCH, SEQ, B_HEADS, B_V_DIM))

        subg = row(b_subln_gain[l])
        oa_c, ob_c = _attention_ctx(z, b_lambda[l], subg, lambda_init)
        oa_l, ob_l = _attention_lat(z, caches, l, b_lambda[l], subg, lambda_init)
        conv_args = (c_dw_w[l], row(c_dw_b[l]), row(c_ln_g[l]), row(c_ln_b[l]),
                     _block_diag(d_w_group[l]).astype(BF16), row(d_scale[l]))
        oc_c, od_c = _conv_pool(z, SEQ, BATCH, 0, *conv_args)
        oc_l, od_l = _conv_pool(z, DEC_SEQ, DEC_BATCH, N_CTX, *conv_args)

        x1, h2, eid, rank, wts, cnt = _mix(
            x_ctx, x_lat, h, (oa_c, ob_c, oc_c, od_c), (oa_l, ob_l, oc_l, od_l), mods, l,
            w_in_bf, w_branch_bf, w_out_bf,
            row(g_post1[l]), row(g_pre2[l]), r_w[l].T, r_b[l].reshape(N_EXPERTS, 1))
        wts = wts[:TOP_K].T

        pos, tile_expert, tile_rows, starts, padded = _routing_tables(eid, rank, cnt)
        xs = _dispatch(starts, padded, pos, h2)
        ys = _expert_ffn(tile_expert, tile_rows, xs, l, e_w1, e_b1, e_w2, e_b2)
        x_ctx, x_lat = _combine(pos, ys, wts, x1, mods, l, row(g_post2[l]))

    y_prompt = x_ctx.reshape(BATCH, SEQ, D_MODEL)
    y_sample = x_lat.reshape(DEC_BATCH, DEC_SEQ, D_MODEL)
    return (y_prompt, y_sample, jnp.stack(new_ak, axis=1), jnp.stack(new_av, axis=1),
            jnp.stack(new_bk, axis=1), jnp.stack(new_bv, axis=1))
```

```python
import functools
import math

import jax
import jax.numpy as jnp
import numpy as np
from jax import lax
from jax.experimental import pallas as pl
from jax.experimental.pallas import tpu as pltpu

F32 = jnp.float32
BF16 = jnp.bfloat16

D_MODEL = 1024
BATCH = 16
SEQ = 256
DEPTH = 2
DEC_BATCH = 4
DEC_SEQ = 2048
PAST_LEN = 256
GRID_W = 64
ROPE_THETA = 10000.0
N_BRANCH = 4
BRANCH_W = D_MODEL // 4
A_HEADS = 4
A_KV_HEADS = 2
A_HEAD_DIM = 64
B_HEADS = 4
B_HALF_DIM = 32
B_V_DIM = 2 * B_HALF_DIM
C_CONV_WIDTH = 31
D_GROUPS = 4
D_GROUP_W = BRANCH_W // D_GROUPS
POOL_WINDOWS = (2, 4, 8, 16)
N_EXPERTS = 32
TOP_K = 4
D_FF = D_MODEL
SWIGLU_LIMIT = 7.0
SWIGLU_ALPHA = 1.702
EPS = 1e-6

N_CTX = BATCH * SEQ
N_LAT = DEC_BATCH * DEC_SEQ
N_TOK = N_CTX + N_LAT

COL_QA, COL_KA, COL_VA = 0, 256, 384
COL_QB, COL_KB, COL_VB = 512, 768, 1024
COL_ZC, COL_ZD = 1280, 1792
N_SMALL = 2048
N_GATE = N_BRANCH * D_MODEL

TM = 256
N_TILES = N_TOK // TM
CTX_TILES = N_CTX // TM
LAT_TILES_PER_SEQ = DEC_SEQ // TM
MOD_ROWS = 8

TE = 512
N_SLOTS = N_TOK * TOP_K
P_MAX = N_SLOTS + N_EXPERTS * TE
E_TILES = P_MAX // TE

TQ = 512
CONV_PAD = 16
CONV_CHUNK = 128

VMEM_LIMIT = 56 * 1024 * 1024


def _sigmoid(x):
    return 1.0 / (1.0 + jnp.exp(-x))


def _split_bf16(a):
    hi = a.astype(BF16)
    lo = (a - hi.astype(F32)).astype(BF16)
    return hi, lo


def _dot(a, b):
    return jnp.dot(a, b, preferred_element_type=F32)


def _dot_nt(a, b):
    return lax.dot_general(a, b, (((1,), (1,)), ((), ())), preferred_element_type=F32)


def _dot3(a, b):
    ah, al = _split_bf16(a)
    bh, bl = _split_bf16(b)
    return _dot(ah, bh) + _dot(ah, bl) + _dot(al, bh)


def _rms(x, g):
    return x * lax.rsqrt(jnp.mean(x * x, axis=-1, keepdims=True) + EPS) * g


def _mod_row(i, ctx_tiles=CTX_TILES, tiles_per_seq=LAT_TILES_PER_SEQ):
    return jnp.where(i < ctx_tiles, 0, 1 + (i - ctx_tiles) // tiles_per_seq)


def _ctx_rows(width, tile=TM):
    last = N_CTX // tile - 1
    return pl.BlockSpec((tile, width), lambda i: (jnp.minimum(i, last), 0))


def _lat_rows(width, tile=TM):
    first = N_CTX // tile
    return pl.BlockSpec((tile, width), lambda i: (jnp.maximum(i - first, 0), 0))


MOD_TN = 1536


def _mod_kernel(cond_ref, w_ref, b_ref, o_ref):
    c = cond_ref[...]
    s = c * _sigmoid(c)
    o_ref[0] = _dot3(s, w_ref[0]) + b_ref[0]


def _modulation(cond8, w_mod, b_mod):
    L = w_mod.shape[0]
    return pl.pallas_call(
        _mod_kernel,
        name="modulation",
        grid=(L, 6 * D_MODEL // MOD_TN),
        in_specs=[
            pl.BlockSpec((MOD_ROWS, D_MODEL), lambda l, j: (0, 0)),
            pl.BlockSpec((1, D_MODEL, MOD_TN), lambda l, j: (l, 0, j)),
            pl.BlockSpec((1, 1, MOD_TN), lambda l, j: (l, 0, j)),
        ],
        out_specs=pl.BlockSpec((1, MOD_ROWS, MOD_TN), lambda l, j: (l, 0, j)),
        out_shape=jax.ShapeDtypeStruct((L, MOD_ROWS, 6 * D_MODEL), F32),
        compiler_params=pltpu.CompilerParams(
            dimension_semantics=("arbitrary", "arbitrary"), vmem_limit_bytes=VMEM_LIMIT),
    )(cond8, w_mod, b_mod.reshape(L, 1, 6 * D_MODEL))


def _group_mean(sq, gmat):
    hi, lo = _split_bf16(sq)
    return _dot(hi, gmat) + _dot(lo, gmat)


def _swap_halves(x, half):
    w = x.shape[-1]
    lane = lax.broadcasted_iota(jnp.int32, x.shape, 1)
    first = (lane % (2 * half)) < half
    return jnp.where(first, pltpu.roll(x, w - half, 1), pltpu.roll(x, half, 1))


def _inproj_kernel(xc_ref, xl_ref, mod_ref, gpre_ref, w_ref, gq_ref, gk_ref, gmat_ref,
                   cos_a_ref, sin_a_ref, cos_b_ref, sin_b_ref, z_ref, h_ref):
    i = pl.program_id(0)
    m = mod_ref[0]
    sh1 = m[:, 0:D_MODEL]
    sc1 = m[:, D_MODEL:2 * D_MODEL]
    x = jnp.where(i < CTX_TILES, xc_ref[...], xl_ref[...])
    h = _rms(x, gpre_ref[...]) * (1.0 + sc1) + sh1
    hb = h.astype(BF16)
    h_ref[...] = hb
    z = _dot(hb, w_ref[0])

    gmat = gmat_ref[...]
    qa = z[:, COL_QA:COL_KA]
    ka = z[:, COL_KA:COL_VA]
    qa = qa * lax.rsqrt(_group_mean(qa * qa, gmat) + EPS) * gq_ref[...]
    ka = ka * lax.rsqrt(_group_mean(ka * ka, gmat[0:128, 0:128]) + EPS) * gk_ref[...]
    qb = z[:, COL_QB:COL_KB]
    kb = z[:, COL_KB:COL_VB]
    z_ref[:, COL_VA:COL_QB] = z[:, COL_VA:COL_QB]
    z_ref[:, COL_VB:N_SMALL] = z[:, COL_VB:N_SMALL]

    @pl.when(i < CTX_TILES)
    def _():
        z_ref[:, COL_QA:COL_KA] = qa
        z_ref[:, COL_KA:COL_VA] = ka
        z_ref[:, COL_QB:COL_KB] = qb
        z_ref[:, COL_KB:COL_VB] = kb

    @pl.when(i >= CTX_TILES)
    def _():
        cos_a = cos_a_ref[...]
        sin_a = sin_a_ref[...]
        cos_b = cos_b_ref[...]
        sin_b = sin_b_ref[...]
        ha, hb_ = A_HEAD_DIM // 2, B_HALF_DIM // 2
        z_ref[:, COL_QA:COL_KA] = qa * cos_a + _swap_halves(qa, ha) * sin_a
        z_ref[:, COL_KA:COL_VA] = ka * cos_a[:, 0:128] + _swap_halves(ka, ha) * sin_a[:, 0:128]
        z_ref[:, COL_QB:COL_KB] = qb * cos_b + _swap_halves(qb, hb_) * sin_b
        z_ref[:, COL_KB:COL_VB] = kb * cos_b + _swap_halves(kb, hb_) * sin_b


def _rope_block(i):
    return jnp.where(i < CTX_TILES, 0, (i - CTX_TILES) % LAT_TILES_PER_SEQ)


def _in_projection(x_ctx, x_lat, mods, l, g_pre, w_in_bf, gq, gk, gmat, rope):
    const = lambda i: (0, 0)
    rope_spec = pl.BlockSpec((TM, 256), lambda i: (_rope_block(i), 0))
    return pl.pallas_call(
        _inproj_kernel,
        name="in_projection",
        grid=(N_TILES,),
        in_specs=[
            _ctx_rows(D_MODEL), _lat_rows(D_MODEL),
            pl.BlockSpec((1, 1, 6 * D_MODEL), lambda i: (l * MOD_ROWS + _mod_row(i), 0, 0)),
            pl.BlockSpec((1, D_MODEL), const),
            pl.BlockSpec((1, D_MODEL, N_SMALL), lambda i: (l, 0, 0)),
            pl.BlockSpec((1, 256), const),
            pl.BlockSpec((1, 128), const),
            pl.BlockSpec((256, 256), const),
            rope_spec, rope_spec, rope_spec, rope_spec,
        ],
        out_specs=[
            pl.BlockSpec((TM, N_SMALL), lambda i: (i, 0)),
            pl.BlockSpec((TM, D_MODEL), lambda i: (i, 0)),
        ],
        out_shape=[
            jax.ShapeDtypeStruct((N_TOK, N_SMALL), F32),
            jax.ShapeDtypeStruct((N_TOK, D_MODEL), BF16),
        ],
        compiler_params=pltpu.CompilerParams(
            dimension_semantics=("arbitrary",), vmem_limit_bytes=VMEM_LIMIT),
    )(x_ctx, x_lat, mods, g_pre, w_in_bf, gq, gk, gmat, *rope)


LOG2E = 1.4426950408889634
HALF_LANES = 64


def _attend(q, ks, vexts):
    ss = [_dot_nt(q, k) for k in ks]
    m = ss[0].max(axis=-1, keepdims=True)
    for s in ss[1:]:
        m = jnp.maximum(m, s.max(axis=-1, keepdims=True))
    o = None
    for s, v in zip(ss, vexts):
        t = _dot(jnp.exp2(s - m).astype(BF16), v)
        o = t if o is None else o + t
    return o


def _value_ext(v128, upper):
    lane = lax.broadcasted_iota(jnp.int32, v128.shape, 1)
    keep = (lane >= HALF_LANES) if upper else (lane < HALF_LANES)
    return jnp.where(keep, v128, 1.0).astype(BF16)


def _normalised(o, upper):
    if upper:
        return o[:, HALF_LANES:] / o[:, 0:1]
    return o[:, :HALF_LANES] / o[:, HALF_LANES:HALF_LANES + 1]


def _attn_kernel(*refs, has_cache, lambda_init, n_seq=1):
    if n_seq > 1:
        rows = refs[0].shape[0] // n_seq
        for s in range(n_seq):
            sub = [r.at[pl.ds(s * rows, rows), :] for r in refs[:6]]
            outs = [r.at[pl.ds(s * rows, rows), :] for r in refs[-2:]]
            _attn_kernel(*sub, *refs[6:-2], *outs, has_cache=has_cache, lambda_init=lambda_init)
        return
    if has_cache:
        (qa_ref, ka_ref, va_ref, qb_ref, kb_ref, vb_ref,
         cka_ref, cva_ref, ckb_ref, cvb_ref, lam_ref, subg_ref, oa_ref, ob_ref) = refs
    else:
        (qa_ref, ka_ref, va_ref, qb_ref, kb_ref, vb_ref,
         lam_ref, subg_ref, oa_ref, ob_ref) = refs

    def cols(ref, lo, hi):
        return ref[:, lo:hi].astype(BF16)

    def ccols(ref, lo, hi):
        return ref[0, 0, :, lo:hi].astype(BF16)

    scale_a = A_HEAD_DIM ** -0.5 * LOG2E
    group = A_HEADS // A_KV_HEADS
    for g in range(A_KV_HEADS):
        lo, hi = g * A_HEAD_DIM, (g + 1) * A_HEAD_DIM
        upper = g % 2 == 1
        ks = [cols(ka_ref, lo, hi)]
        vexts = [_value_ext(va_ref[...], upper)]
        if has_cache:
            ks.append(ccols(cka_ref, lo, hi))
            vexts.append(_value_ext(cva_ref[0, 0], upper))
        for hd in range(g * group, (g + 1) * group):
            q = (qa_ref[:, hd * A_HEAD_DIM:(hd + 1) * A_HEAD_DIM] * scale_a).astype(BF16)
            o = _attend(q, ks, vexts)
            oa_ref[:, hd * A_HEAD_DIM:(hd + 1) * A_HEAD_DIM] = _normalised(o, upper)

    bl = lam_ref[...]
    lam = (jnp.exp(jnp.sum(bl[0:1] * bl[1:2], axis=-1, keepdims=True))
           - jnp.exp(jnp.sum(bl[2:3] * bl[3:4], axis=-1, keepdims=True)) + lambda_init)
    scale_b = B_HALF_DIM ** -0.5 * LOG2E
    subg = subg_ref[...]
    for hd in range(B_HEADS):
        base = hd * 2 * B_HALF_DIM
        vlo, vhi = hd * B_V_DIM, (hd + 1) * B_V_DIM
        blk = (hd // 2) * 2 * B_V_DIM
        upper = hd % 2 == 1
        vexts = [_value_ext(vb_ref[:, blk:blk + 2 * B_V_DIM], upper)]
        if has_cache:
            vexts.append(_value_ext(cvb_ref[0, 0, :, blk:blk + 2 * B_V_DIM], upper))
        outs = []
        for c in range(2):
            lo, hi = base + c * B_HALF_DIM, base + (c + 1) * B_HALF_DIM
            q = (qb_ref[:, lo:hi] * scale_b).astype(BF16)
            ks = [cols(kb_ref, lo, hi)]
            if has_cache:
                ks.append(ccols(ckb_ref, lo, hi))
            outs.append(_normalised(_attend(q, ks, vexts), upper))
        o = outs[0] - lam * outs[1]
        o = _rms(o, subg) * (1.0 - lambda_init)
        ob_ref[:, vlo:vhi] = o


CTX_SEQ_PER_STEP = 1


def _attention_ctx(z, b_lambda, subg, lambda_init):
    rows = CTX_SEQ_PER_STEP * SEQ

    def zspec(width, col):
        return pl.BlockSpec((rows, width), lambda b: (b, col // width))
    out_spec = pl.BlockSpec((rows, BRANCH_W), lambda b: (b, 0))
    return pl.pallas_call(
        functools.partial(_attn_kernel, has_cache=False, lambda_init=lambda_init,
                          n_seq=CTX_SEQ_PER_STEP),
        name="attention_ctx",
        grid=(BATCH // CTX_SEQ_PER_STEP,),
        in_specs=[
            zspec(256, COL_QA), zspec(128, COL_KA), zspec(128, COL_VA),
            zspec(256, COL_QB), zspec(256, COL_KB), zspec(256, COL_VB),
            pl.BlockSpec((4, B_HALF_DIM), lambda b: (0, 0)),
            pl.BlockSpec((1, B_V_DIM), lambda b: (0, 0)),
        ],
        out_specs=[out_spec, out_spec],
        out_shape=[jax.ShapeDtypeStruct((N_CTX, BRANCH_W), F32)] * 2,
        compiler_params=pltpu.CompilerParams(
            dimension_semantics=("arbitrary",), vmem_limit_bytes=VMEM_LIMIT),
    )(z, z, z, z, z, z, b_lambda, subg)


def _attention_lat(z, caches, l, b_lambda, subg, lambda_init):
    q_tiles = DEC_SEQ // TQ
    seq0 = N_CTX // DEC_SEQ

    def qspec(col):
        return pl.BlockSpec((TQ, 256), lambda b, t: (N_CTX // TQ + b * q_tiles + t, col // 256))

    def kspec(width, col):
        return pl.BlockSpec((DEC_SEQ, width), lambda b, t: (seq0 + b, col // width))

    def cspec(width):
        return pl.BlockSpec((1, 1, PAST_LEN, width), lambda b, t: (b, l, 0, 0))

    out_spec = pl.BlockSpec((TQ, BRANCH_W), lambda b, t: (b * q_tiles + t, 0))
    cka, cva, ckb, cvb = caches
    return pl.pallas_call(
        functools.partial(_attn_kernel, has_cache=True, lambda_init=lambda_init),
        name="attention_lat",
        grid=(DEC_BATCH, q_tiles),
        in_specs=[
            qspec(COL_QA), kspec(128, COL_KA), kspec(128, COL_VA),
            qspec(COL_QB), kspec(256, COL_KB), kspec(256, COL_VB),
            cspec(128), cspec(128), cspec(256), cspec(256),
            pl.BlockSpec((4, B_HALF_DIM), lambda b, t: (0, 0)),
            pl.BlockSpec((1, B_V_DIM), lambda b, t: (0, 0)),
        ],
        out_specs=[out_spec, out_spec],
        out_shape=[jax.ShapeDtypeStruct((N_LAT, BRANCH_W), F32)] * 2,
        compiler_params=pltpu.CompilerParams(
            dimension_semantics=("arbitrary", "arbitrary"), vmem_limit_bytes=VMEM_LIMIT),
    )(z, z, z, z, z, z, cka, cva, ckb, cvb, b_lambda, subg)


SUBLANES = 8


def _row_shifter(win):
    span = win.shape[0] - SUBLANES
    shifted = {}

    def rows(off):
        s = off % SUBLANES
        if s not in shifted:
            shifted[s] = win[s:s + span, :]
        base = off - s
        return shifted[s][base:base + CONV_CHUNK, :]

    return rows


def _convpool_kernel(a_ref, b_ref, d_ref, dww_ref, dwb_ref, lng_ref, lnb_ref, wbd_ref, dsc_ref,
                     oc_ref, od_ref, hpad, upad, *, seq_len):
    n_chunks = seq_len // CONV_CHUNK
    zeros = jnp.zeros((CONV_PAD, BRANCH_W), F32)
    hpad[0:CONV_PAD, :] = zeros
    hpad[seq_len + CONV_PAD:seq_len + 2 * CONV_PAD, :] = zeros
    upad[0:CONV_PAD, :] = zeros
    upad[seq_len + CONV_PAD:seq_len + 2 * CONV_PAD, :] = zeros

    def fill(c, carry):
        r = pl.multiple_of(c * CONV_CHUNK, CONV_CHUNK)
        a = a_ref[pl.ds(r, CONV_CHUNK), :]
        b = b_ref[pl.ds(r, CONV_CHUNK), :]
        hpad[pl.ds(r + CONV_PAD, CONV_CHUNK), :] = a * _sigmoid(b)
        upad[pl.ds(r + CONV_PAD, CONV_CHUNK), :] = d_ref[pl.ds(r, CONV_CHUNK), :]
        return carry

    lax.fori_loop(0, n_chunks, fill, 0)

    lane = lax.broadcasted_iota(jnp.int32, (CONV_CHUNK, BRANCH_W), 1)
    half = C_CONV_WIDTH // 2

    def body(c, carry):
        r = pl.multiple_of(c * CONV_CHUNK, CONV_CHUNK)
        hrows = _row_shifter(hpad[pl.ds(r, CONV_CHUNK + 2 * CONV_PAD), :])
        acc = jnp.zeros((CONV_CHUNK, BRANCH_W), F32)
        for k in range(C_CONV_WIDTH):
            acc = acc + dww_ref[k:k + 1, :] * hrows(CONV_PAD - half + k)
        acc = acc + dwb_ref[...]
        mu = jnp.mean(acc, axis=-1, keepdims=True)
        cen = acc - mu
        var = jnp.mean(cen * cen, axis=-1, keepdims=True)
        y = cen * lax.rsqrt(var + EPS) * lng_ref[...] + lnb_ref[...]
        oc_ref[pl.ds(r, CONV_CHUNK), :] = y * _sigmoid(y)

        urows = _row_shifter(upad[pl.ds(r, CONV_CHUNK + 2 * CONV_PAD), :])

        def ld(d):
            return urows(CONV_PAD + d)

        u = ld(0)
        sums = {}
        s = u + ld(-1)
        sums[2] = s
        s = s + ld(-2) + ld(1)
        sums[4] = s
        s = s + ld(-4) + ld(-3) + ld(2) + ld(3)
        sums[8] = s
        s = s + ld(-8) + ld(-7) + ld(-6) + ld(-5) + ld(4) + ld(5) + ld(6) + ld(7)
        sums[16] = s
        t = r + lax.broadcasted_iota(jnp.int32, (CONV_CHUNK, 1), 0)
        pooled = None
        for g, w in reversed(list(enumerate(POOL_WINDOWS))):
            lo = jnp.maximum(t - w // 2, 0)
            hi = jnp.minimum(t + (w - 1 - w // 2), seq_len - 1)
            mean = sums[w] / (hi - lo + 1).astype(F32)
            pooled = mean if pooled is None else jnp.where(lane < (g + 1) * D_GROUP_W, mean, pooled)
        dlt = (pooled - u).astype(BF16)
        od_ref[pl.ds(r, CONV_CHUNK), :] = _dot(dlt, wbd_ref[...]) * dsc_ref[...]
        return carry

    lax.fori_loop(0, n_chunks, body, 0)


def _conv_pool(z, seq_len, n_seq, row0, dww, dwb, lng, lnb, wbd, dsc):
    blk0 = row0 // seq_len

    def zspec(col):
        return pl.BlockSpec((seq_len, 256), lambda b: (blk0 + b, col // 256))

    def const(shape):
        return pl.BlockSpec(shape, lambda b: (0, 0))

    out_spec = pl.BlockSpec((seq_len, BRANCH_W), lambda b: (b, 0))
    return pl.pallas_call(
        functools.partial(_convpool_kernel, seq_len=seq_len),
        name=f"conv_pool_{seq_len}",
        grid=(n_seq,),
        in_specs=[
            zspec(COL_ZC), zspec(COL_ZC + BRANCH_W), zspec(COL_ZD),
            const((C_CONV_WIDTH, BRANCH_W)), const((1, BRANCH_W)), const((1, BRANCH_W)),
            const((1, BRANCH_W)), const((BRANCH_W, BRANCH_W)), const((1, BRANCH_W)),
        ],
        out_specs=[out_spec, out_spec],
        out_shape=[jax.ShapeDtypeStruct((n_seq * seq_len, BRANCH_W), F32)] * 2,
        scratch_shapes=[pltpu.VMEM((seq_len + 2 * CONV_PAD, BRANCH_W), F32)] * 2,
        compiler_params=pltpu.CompilerParams(
            dimension_semantics=("arbitrary",), vmem_limit_bytes=VMEM_LIMIT),
    )(z, z, z, dww, dwb, lng, lnb, wbd, dsc)


MIX_TM = 512
MIX_TILES = N_TOK // MIX_TM
MIX_CTX_TILES = N_CTX // MIX_TM
MIX_LAT_PER_SEQ = DEC_SEQ // MIX_TM
ROUTE_ROWS = 8


def _mix_kernel(xc_ref, xl_ref, h_ref, oa_c, ob_c, oc_c, od_c, oa_l, ob_l, oc_l, od_l,
                mod_ref, wg01_ref, wg23_ref, wbr_ref, wout_ref, gpost_ref, gpre2_ref, rwt_ref, rb_ref,
                x1_ref, h2_ref, eid_ref, rank_ref, wts_ref, cnt_ref, carry):
    i = pl.program_id(0)
    is_ctx = i < MIX_CTX_TILES

    @pl.when(i == 0)
    def _():
        carry[...] = jnp.zeros_like(carry)

    m = mod_ref[0]
    g1 = m[:, 2 * D_MODEL:3 * D_MODEL]
    sh2 = m[:, 3 * D_MODEL:4 * D_MODEL]
    sc2 = m[:, 4 * D_MODEL:5 * D_MODEL]

    hb = h_ref[...]
    merged = None
    for n, (c_ref, l_ref) in enumerate(((oa_c, oa_l), (ob_c, ob_l), (oc_c, oc_l), (od_c, od_l))):
        wg_ref = wg01_ref if n < 2 else wg23_ref
        gate = _sigmoid(_dot(hb, wg_ref[0, :, (n % 2) * D_MODEL:(n % 2 + 1) * D_MODEL]))
        o = jnp.where(is_ctx, c_ref[...], l_ref[...])
        br = _dot(o.astype(BF16), wbr_ref[0, n])
        merged = gate * br if merged is None else merged + gate * br
    y = _dot(merged.astype(BF16), wout_ref[0])
    x = jnp.where(is_ctx, xc_ref[...], xl_ref[...])
    x1 = x + g1 * _rms(y, gpost_ref[...])
    x1_ref[...] = x1
    h2 = _rms(x1, gpre2_ref[...]) * (1.0 + sc2) + sh2
    h2_ref[...] = h2

    rh, rl = _split_bf16(rwt_ref[...])
    hh, hl = _split_bf16(h2)
    logits = _dot_nt(rh, hh) + _dot_nt(rh, hl) + _dot_nt(rl, hh) + rb_ref[...]

    iota_e = lax.broadcasted_iota(jnp.int32, (N_EXPERTS, MIX_TM), 0)
    rem = logits
    vals, idxs = [], []
    for _ in range(TOP_K):
        mx = jnp.max(rem, axis=0, keepdims=True)
        idx = jnp.min(jnp.where(rem == mx, iota_e, N_EXPERTS), axis=0, keepdims=True)
        vals.append(mx)
        idxs.append(idx)
        rem = jnp.where(iota_e == idx, -jnp.inf, rem)
    exps = [jnp.exp(v - vals[0]) for v in vals]
    den = exps[0]
    for e in exps[1:]:
        den = den + e

    sel = [iota_e == idx for idx in idxs]
    member = jnp.zeros((N_EXPERTS, MIX_TM), F32)
    for s in sel:
        member = member + jnp.where(s, 1.0, 0.0)
    row = lax.broadcasted_iota(jnp.int32, (MIX_TM, MIX_TM), 0)
    col = lax.broadcasted_iota(jnp.int32, (MIX_TM, MIX_TM), 1)
    earlier = jnp.where(row < col, 1.0, 0.0).astype(BF16)
    seen = _dot(member.astype(BF16), earlier) + carry[...]
    carry[...] = carry[...] + jnp.sum(member, axis=1, keepdims=True)
    cnt_ref[...] = carry[...]

    sub = lax.broadcasted_iota(jnp.int32, (ROUTE_ROWS, MIX_TM), 0)
    eid = jnp.zeros((ROUTE_ROWS, MIX_TM), jnp.int32)
    rank = jnp.zeros((ROUTE_ROWS, MIX_TM), jnp.int32)
    wts = jnp.zeros((ROUTE_ROWS, MIX_TM), F32)
    for k in range(TOP_K):
        rk = jnp.sum(jnp.where(sel[k], seen, 0.0), axis=0, keepdims=True).astype(jnp.int32)
        eid = jnp.where(sub == k, idxs[k], eid)
        rank = jnp.where(sub == k, rk, rank)
        wts = jnp.where(sub == k, exps[k] / den, wts)
    eid_ref[...] = eid
    rank_ref[...] = rank
    wts_ref[...] = wts


def _mix(x_ctx, x_lat, h, o_ctx, o_lat, mods, l, w_in_bf, w_branch_bf, w_out_bf,
         g_post, g_pre2, r_wt, r_b):
    const2 = lambda i: (0, 0)
    tile = lambda width: pl.BlockSpec((MIX_TM, width), lambda i: (i, 0))
    route = pl.BlockSpec((ROUTE_ROWS, MIX_TM), lambda i: (0, i))
    ctx_tile = _ctx_rows(BRANCH_W, MIX_TM)
    lat_tile = _lat_rows(BRANCH_W, MIX_TM)
    mod_row = lambda i: _mod_row(i, MIX_CTX_TILES, MIX_LAT_PER_SEQ)
    single = pl.Buffered(1)
    return pl.pallas_call(
        _mix_kernel,
        name="mix_router",
        grid=(MIX_TILES,),
        in_specs=[
            _ctx_rows(D_MODEL, MIX_TM), _lat_rows(D_MODEL, MIX_TM), tile(D_MODEL),
            ctx_tile, ctx_tile, ctx_tile, ctx_tile, lat_tile, lat_tile, lat_tile, lat_tile,
            pl.BlockSpec((1, 1, 6 * D_MODEL), lambda i: (l * MOD_ROWS + mod_row(i), 0, 0)),
            pl.BlockSpec((1, D_MODEL, N_SMALL), lambda i: (l, 0, 1), pipeline_mode=single),
            pl.BlockSpec((1, D_MODEL, N_SMALL), lambda i: (l, 0, 2), pipeline_mode=single),
            pl.BlockSpec((1, N_BRANCH, BRANCH_W, D_MODEL), lambda i: (l, 0, 0, 0),
                         pipeline_mode=single),
            pl.BlockSpec((1, D_MODEL, D_MODEL), lambda i: (l, 0, 0), pipeline_mode=single),
            pl.BlockSpec((1, D_MODEL), const2),
            pl.BlockSpec((1, D_MODEL), const2),
            pl.BlockSpec((N_EXPERTS, D_MODEL), const2),
            pl.BlockSpec((N_EXPERTS, 1), const2),
        ],
        out_specs=[
            tile(D_MODEL), tile(D_MODEL),
            route, route, route,
            pl.BlockSpec((N_EXPERTS, 1), const2),
        ],
        out_shape=[
            jax.ShapeDtypeStruct((N_TOK, D_MODEL), F32),
            jax.ShapeDtypeStruct((N_TOK, D_MODEL), F32),
            jax.ShapeDtypeStruct((ROUTE_ROWS, N_TOK), jnp.int32),
            jax.ShapeDtypeStruct((ROUTE_ROWS, N_TOK), jnp.int32),
            jax.ShapeDtypeStruct((ROUTE_ROWS, N_TOK), F32),
            jax.ShapeDtypeStruct((N_EXPERTS, 1), F32),
        ],
        scratch_shapes=[pltpu.VMEM((N_EXPERTS, 1), F32)],
        compiler_params=pltpu.CompilerParams(
            dimension_semantics=("arbitrary",), vmem_limit_bytes=VMEM_LIMIT),
    )(x_ctx, x_lat, h, *o_ctx, *o_lat, mods, w_in_bf, w_in_bf, w_branch_bf, w_out_bf,
      g_post, g_pre2, r_wt, r_b)


def _row_copy(src_ref, src_row, dst_ref, dst_row, sem):
    return pltpu.make_async_copy(src_ref.at[pl.ds(src_row, 1), :], dst_ref.at[pl.ds(dst_row, 1), :], sem)


DISPATCH_ROWS = 512
DISPATCH_STEPS = N_TOK // DISPATCH_ROWS
DMA_UNROLL = 8


def _dispatch_kernel(starts_ref, padded_ref, pos_ref, h2_ref, xs_ref, zbuf, sem, zsem, tsem):
    i = pl.program_id(0)

    def pad_copy(e):
        last = pl.multiple_of(starts_ref[e] + padded_ref[e] - TE, TE)
        return pltpu.make_async_copy(zbuf, xs_ref.at[pl.ds(last, TE), :], zsem)

    first_tail = (starts_ref[N_EXPERTS - 1] + padded_ref[N_EXPERTS - 1]) // TE

    def tail_copy(j):
        return pltpu.make_async_copy(zbuf, xs_ref.at[pl.ds(pl.multiple_of(j * TE, TE), TE), :], tsem)

    def tail_start(j, carry):
        tail_copy(j).start()
        return carry

    def tail_wait(j, carry):
        tail_copy(j).wait()
        return carry

    @pl.when(i == 0)
    def _():
        zbuf[...] = jnp.zeros_like(zbuf)
        for e in range(N_EXPERTS):
            @pl.when(padded_ref[e] > 0)
            def _():
                pad_copy(e).start()
        lax.fori_loop(first_tail, E_TILES, tail_start, 0)
        for e in range(N_EXPERTS):
            @pl.when(padded_ref[e] > 0)
            def _():
                pad_copy(e).wait()

    @pl.when(i == DISPATCH_STEPS - 1)
    def _():
        lax.fori_loop(first_tail, E_TILES, tail_wait, 0)

    def issue(t, carry):
        for k in range(TOP_K):
            _row_copy(h2_ref, t, xs_ref, pos_ref[t * TOP_K + k], sem).start(priority=k % 2)
        return carry

    lax.fori_loop(0, DISPATCH_ROWS, issue, 0, unroll=DMA_UNROLL)

    for _ in range(TOP_K):
        pltpu.make_async_copy(h2_ref, xs_ref.at[pl.ds(0, DISPATCH_ROWS), :], sem).wait()


def _dispatch(starts, padded, pos, h2):
    grid_spec = pltpu.PrefetchScalarGridSpec(
        num_scalar_prefetch=2,
        grid=(DISPATCH_STEPS,),
        in_specs=[
            pl.BlockSpec((DISPATCH_ROWS * TOP_K,), lambda i, st, pd: (i,), memory_space=pltpu.SMEM),
            pl.BlockSpec((DISPATCH_ROWS, D_MODEL), lambda i, st, pd: (i, 0)),
        ],
        out_specs=pl.BlockSpec(memory_space=pl.ANY),
        scratch_shapes=[pltpu.VMEM((TE, D_MODEL), F32),
                        pltpu.SemaphoreType.DMA(()), pltpu.SemaphoreType.DMA(()),
                        pltpu.SemaphoreType.DMA(())],
    )
    return pl.pallas_call(
        _dispatch_kernel,
        name="dispatch",
        grid_spec=grid_spec,
        out_shape=jax.ShapeDtypeStruct((P_MAX, D_MODEL), F32),
        compiler_params=pltpu.CompilerParams(
            dimension_semantics=("arbitrary",), vmem_limit_bytes=VMEM_LIMIT),
    )(starts, padded, pos, h2)


W_CAST_ROWS = 128


def _moe_kernel(te_ref, rows_ref, slot_ref, next_ref, x_ref, w1_hbm, b1_ref, w2_hbm, b2_ref, y_ref,
                w1f, w2f, w1b, w2b, sem, *, layer):
    j = pl.program_id(0)
    e = te_ref[j]
    prev = te_ref[jnp.maximum(j - 1, 0)]
    valid = rows_ref[j] > 0
    first = jnp.logical_and(valid, jnp.logical_or(j == 0, e != prev))
    s = slot_ref[j]

    def weight_copies(ex, sl):
        return (pltpu.make_async_copy(w1_hbm.at[layer, ex], w1f.at[sl], sem.at[0, sl]),
                pltpu.make_async_copy(w2_hbm.at[layer, ex], w2f.at[sl], sem.at[1, sl]))

    @pl.when(j == 0)
    def _():
        for cp in weight_copies(e, s):
            cp.start()

    @pl.when(first)
    def _():
        for cp in weight_copies(e, s):
            cp.wait()
        nxt = next_ref[j]

        @pl.when(nxt >= 0)
        def _():
            for cp in weight_copies(nxt, 1 - s):
                cp.start()

        def cast(c, carry):
            r = pl.multiple_of(c * W_CAST_ROWS, W_CAST_ROWS)
            w1b[pl.ds(r, W_CAST_ROWS), :] = w1f[s, pl.ds(r, W_CAST_ROWS), :].astype(BF16)
            w2b[pl.ds(r, W_CAST_ROWS), :] = w2f[s, pl.ds(r, W_CAST_ROWS), :].astype(BF16)
            return carry
        lax.fori_loop(0, D_MODEL // W_CAST_ROWS, cast, 0)

    @pl.when(valid)
    def _():
        x = x_ref[...].astype(BF16)
        u = _dot(x, w1b[...]) + b1_ref[0, 0]
        xg = jnp.minimum(u[:, :D_FF], SWIGLU_LIMIT)
        xl = jnp.clip(u[:, D_FF:], -SWIGLU_LIMIT, SWIGLU_LIMIT)
        act = (xl + 1.0) * (xg * _sigmoid(SWIGLU_ALPHA * xg))
        y_ref[...] = _dot(act.astype(BF16), w2b[...]) + b2_ref[0, 0]

    @pl.when(jnp.logical_not(valid))
    def _():
        y_ref[...] = jnp.zeros_like(y_ref)


def _expert_ffn(tile_expert, tile_rows, tile_slot, tile_next, xs, l, e_w1, e_b1, e_w2, e_b2):
    L = e_w1.shape[0]
    grid_spec = pltpu.PrefetchScalarGridSpec(
        num_scalar_prefetch=4,
        grid=(E_TILES,),
        in_specs=[
            pl.BlockSpec((TE, D_MODEL), lambda j, te, nr, sl, nx: (jnp.where(nr[j] > 0, j, 0), 0)),
            pl.BlockSpec(memory_space=pl.ANY),
            pl.BlockSpec((1, 1, 1, 2 * D_FF), lambda j, te, nr, sl, nx: (l, te[j], 0, 0)),
            pl.BlockSpec(memory_space=pl.ANY),
            pl.BlockSpec((1, 1, 1, D_MODEL), lambda j, te, nr, sl, nx: (l, te[j], 0, 0)),
        ],
        out_specs=pl.BlockSpec((TE, D_MODEL), lambda j, te, nr, sl, nx: (j, 0)),
        scratch_shapes=[pltpu.VMEM((2, D_MODEL, 2 * D_FF), F32), pltpu.VMEM((2, D_FF, D_MODEL), F32),
                        pltpu.VMEM((D_MODEL, 2 * D_FF), BF16), pltpu.VMEM((D_FF, D_MODEL), BF16),
                        pltpu.SemaphoreType.DMA((2, 2))],
    )
    return pl.pallas_call(
        functools.partial(_moe_kernel, layer=l),
        name="expert_ffn",
        grid_spec=grid_spec,
        out_shape=jax.ShapeDtypeStruct((P_MAX, D_MODEL), F32),
        compiler_params=pltpu.CompilerParams(
            dimension_semantics=("arbitrary",), vmem_limit_bytes=VMEM_LIMIT),
    )(tile_expert, tile_rows, tile_slot, tile_next, xs, e_w1,
      e_b1.reshape(L, N_EXPERTS, 1, 2 * D_FF), e_w2, e_b2.reshape(L, N_EXPERTS, 1, D_MODEL))


def _combine_kernel(pos_ref, pos_next_ref, ys_ref, wts_ref, x1_ref, mod_ref, gpost_ref,
                    out_c_ref, out_l_ref, buf, sem):
    i = pl.program_id(0)
    slot = i % 2

    def gather(p_ref, s):
        def issue(t, carry):
            for k in range(TOP_K):
                _row_copy(ys_ref, p_ref[t * TOP_K + k], buf.at[s, k], t, sem.at[s]).start(priority=k % 2)
            return carry
        lax.fori_loop(0, TM, issue, 0, unroll=DMA_UNROLL)

    @pl.when(i == 0)
    def _():
        gather(pos_ref, 0)

    @pl.when(i + 1 < N_TILES)
    def _():
        gather(pos_next_ref, 1 - slot)

    for k in range(TOP_K):
        pltpu.make_async_copy(ys_ref.at[pl.ds(0, TM), :], buf.at[slot, k], sem.at[slot]).wait()

    w = wts_ref[...]
    y = w[:, 0:1] * buf[slot, 0]
    for k in range(1, TOP_K):
        y = y + w[:, k:k + 1] * buf[slot, k]
    g2 = mod_ref[0][:, 5 * D_MODEL:6 * D_MODEL]
    res = x1_ref[...] + g2 * _rms(y, gpost_ref[...])

    @pl.when(i < CTX_TILES)
    def _():
        out_c_ref[...] = res

    @pl.when(i >= CTX_TILES)
    def _():
        out_l_ref[...] = res


def _combine(pos, ys, wts, x1, mods, l, g_post2):
    return pl.pallas_call(
        _combine_kernel,
        name="combine",
        grid=(N_TILES,),
        in_specs=[
            pl.BlockSpec((TM * TOP_K,), lambda i: (i,), memory_space=pltpu.SMEM),
            pl.BlockSpec((TM * TOP_K,), lambda i: (jnp.minimum(i + 1, N_TILES - 1),),
                         memory_space=pltpu.SMEM),
            pl.BlockSpec(memory_space=pl.ANY),
            pl.BlockSpec((TM, TOP_K), lambda i: (i, 0)),
            pl.BlockSpec((TM, D_MODEL), lambda i: (i, 0)),
            pl.BlockSpec((1, 1, 6 * D_MODEL), lambda i: (l * MOD_ROWS + _mod_row(i), 0, 0)),
            pl.BlockSpec((1, D_MODEL), lambda i: (0, 0)),
        ],
        out_specs=[_ctx_rows(D_MODEL), _lat_rows(D_MODEL)],
        out_shape=[jax.ShapeDtypeStruct((N_CTX, D_MODEL), F32),
                   jax.ShapeDtypeStruct((N_LAT, D_MODEL), F32)],
        scratch_shapes=[pltpu.VMEM((2, TOP_K, TM, D_MODEL), F32), pltpu.SemaphoreType.DMA((2,))],
        compiler_params=pltpu.CompilerParams(
            dimension_semantics=("arbitrary",), vmem_limit_bytes=VMEM_LIMIT),
    )(pos, pos, ys, wts, x1, mods, g_post2)


def _rope_tables():
    t = np.arange(DEC_SEQ)
    r = (t // GRID_W).astype(np.float32)
    c = (t % GRID_W).astype(np.float32)

    def table(dim, reps):
        n_axis = dim // 4
        inv = (ROPE_THETA ** (-np.arange(n_axis, dtype=np.float32) / n_axis)).astype(np.float32)
        ang = np.concatenate([r[:, None] * inv, c[:, None] * inv], axis=-1).astype(np.float32)
        cos, sin = np.cos(ang), np.sin(ang)
        return (np.tile(np.concatenate([cos, cos], -1), (1, reps)),
                np.tile(np.concatenate([-sin, sin], -1), (1, reps)))

    cos_a, sin_a = table(A_HEAD_DIM, 256 // A_HEAD_DIM)
    cos_b, sin_b = table(B_HALF_DIM, 256 // B_HALF_DIM)
    return tuple(jnp.asarray(a, F32) for a in (cos_a, sin_a, cos_b, sin_b))


def _group_mean_matrix():
    idx = np.arange(256) // A_HEAD_DIM
    return jnp.asarray((idx[:, None] == idx[None, :]).astype(np.float32) / A_HEAD_DIM, BF16)


def _block_diag(w):
    out = jnp.zeros((BRANCH_W, BRANCH_W), w.dtype)
    for g in range(D_GROUPS):
        out = out.at[g * D_GROUP_W:(g + 1) * D_GROUP_W, g * D_GROUP_W:(g + 1) * D_GROUP_W].set(w[g])
    return out


def _routing_tables(eid, rank, cnt):
    cnt = cnt.reshape(N_EXPERTS).astype(jnp.int32)
    padded = ((cnt + TE - 1) // TE) * TE
    ends = jnp.cumsum(padded)
    starts = ends - padded
    experts = jnp.arange(N_EXPERTS, dtype=jnp.int32)
    start_of = jnp.sum(jnp.where(eid[:TOP_K, :, None] == experts, starts, 0), axis=-1)
    pos = (start_of + rank[:TOP_K]).T.reshape(N_SLOTS)
    tile_row = jnp.arange(E_TILES, dtype=jnp.int32) * TE
    tile_expert = jnp.sum((tile_row[:, None] >= ends[None, :]).astype(jnp.int32), axis=-1)
    tile_expert = jnp.minimum(tile_expert, N_EXPERTS - 1)
    used_end = jnp.sum(jnp.where(tile_expert[:, None] == experts, starts + cnt, 0), axis=-1)
    tile_rows = jnp.where(tile_row < ends[-1], jnp.clip(used_end - tile_row, 0, TE), 0)
    valid = tile_rows > 0
    changed = jnp.concatenate([jnp.ones((1,), bool), tile_expert[1:] != tile_expert[:-1]])
    tile_slot = (jnp.cumsum(jnp.logical_and(valid, changed).astype(jnp.int32)) + 1) % 2
    group_end = jnp.sum(jnp.where(tile_expert[:, None] == experts, ends, 0), axis=-1) // TE
    tiles = jnp.arange(E_TILES, dtype=jnp.int32)
    expert_at = jnp.where(valid, tile_expert, -1)
    tile_next = jnp.sum(jnp.where(group_end[:, None] == tiles, expert_at + 1, 0), axis=-1) - 1
    as_i32 = lambda a: a.astype(jnp.int32)
    return (as_i32(pos), as_i32(tile_expert), as_i32(tile_rows), as_i32(tile_slot), as_i32(tile_next),
            as_i32(starts), as_i32(padded))


def kernel(x_prompt, x_sample, cache_a_k, cache_a_v, cache_b_k, cache_b_v, c, c_ctx, w_mod, b_mod, g_pre1, g_post1, g_pre2, g_post2, w_in, a_q_gain, a_k_gain, b_lambda, b_subln_gain, c_dw_w, c_dw_b, c_ln_g, c_ln_b, d_w_group, d_scale, w_branch, w_out, r_w, r_b, e_w1, e_b1, e_w2, e_b2):
    x_ctx = x_prompt.reshape(N_CTX, D_MODEL)
    x_lat = x_sample.reshape(N_LAT, D_MODEL)
    cond8 = jnp.concatenate(
        [c_ctx[None, :], c, jnp.zeros((MOD_ROWS - 1 - DEC_BATCH, D_MODEL), F32)], axis=0)
    mods = _modulation(cond8, w_mod, b_mod).reshape(DEPTH * MOD_ROWS, 1, 6 * D_MODEL)

    rope = _rope_tables()
    gmat = _group_mean_matrix()
    caches = (
        cache_a_k.reshape(DEC_BATCH, DEPTH, PAST_LEN, A_KV_HEADS * A_HEAD_DIM),
        cache_a_v.reshape(DEC_BATCH, DEPTH, PAST_LEN, A_KV_HEADS * A_HEAD_DIM),
        cache_b_k.reshape(DEC_BATCH, DEPTH, PAST_LEN, B_HEADS * 2 * B_HALF_DIM),
        cache_b_v.reshape(DEC_BATCH, DEPTH, PAST_LEN, B_HEADS * B_V_DIM),
    )

    w_in_bf = w_in.astype(BF16)
    w_branch_bf = w_branch.astype(BF16)
    w_out_bf = w_out.astype(BF16)

    new_ak, new_av, new_bk, new_bv = [], [], [], []
    row = lambda v: v.reshape(1, -1)
    for l in range(DEPTH):
        lambda_init = 0.8 - 0.6 * math.exp(-0.3 * l)
        z, h = _in_projection(
            x_ctx, x_lat, mods, l, row(g_pre1[l]), w_in_bf,
            row(jnp.tile(a_q_gain[l], A_HEADS)), row(jnp.tile(a_k_gain[l], A_KV_HEADS)), gmat, rope)

        zc = z[:N_CTX]
        new_ak.append(zc[:, COL_KA:COL_VA].reshape(BATCH, SEQ, A_KV_HEADS, A_HEAD_DIM))
        new_av.append(zc[:, COL_VA:COL_QB].reshape(BATCH, SEQ, A_KV_HEADS, A_HEAD_DIM))
        new_bk.append(zc[:, COL_KB:COL_VB].reshape(BATCH, SEQ, B_HEADS, 2, B_HALF_DIM))
        new_bv.append(zc[:, COL_VB:COL_ZC].reshape(BATCH, SEQ, B_HEADS, B_V_DIM))

        subg = row(b_subln_gain[l])
        oa_c, ob_c = _attention_ctx(z, b_lambda[l], subg, lambda_init)
        oa_l, ob_l = _attention_lat(z, caches, l, b_lambda[l], subg, lambda_init)
        conv_args = (c_dw_w[l], row(c_dw_b[l]), row(c_ln_g[l]), row(c_ln_b[l]),
                     _block_diag(d_w_group[l]).astype(BF16), row(d_scale[l]))
        oc_c, od_c = _conv_pool(z, SEQ, BATCH, 0, *conv_args)
        oc_l, od_l = _conv_pool(z, DEC_SEQ, DEC_BATCH, N_CTX, *conv_args)

        x1, h2, eid, rank, wts, cnt = _mix(
            x_ctx, x_lat, h, (oa_c, ob_c, oc_c, od_c), (oa_l, ob_l, oc_l, od_l), mods, l,
            w_in_bf, w_branch_bf, w_out_bf,
            row(g_post1[l]), row(g_pre2[l]), r_w[l].T, r_b[l].reshape(N_EXPERTS, 1))
        wts = wts[:TOP_K].T

        pos, tile_expert, tile_rows, tile_slot, tile_next, starts, padded = _routing_tables(
            eid, rank, cnt)
        xs = _dispatch(starts, padded, pos, h2)
        ys = _expert_ffn(tile_expert, tile_rows, tile_slot, tile_next, xs, l,
                         e_w1, e_b1, e_w2, e_b2)
        x_ctx, x_lat = _combine(pos, ys, wts, x1, mods, l, row(g_post2[l]))

    y_prompt = x_ctx.reshape(BATCH, SEQ, D_MODEL)
    y_sample = x_lat.reshape(DEC_BATCH, DEC_SEQ, D_MODEL)
    return (y_prompt, y_sample, jnp.stack(new_ak, axis=1), jnp.stack(new_av, axis=1),
            jnp.stack(new_bk, axis=1), jnp.stack(new_bv, axis=1))
```

```python
import functools
import math

import jax
import jax.numpy as jnp
import numpy as np
from jax import lax
from jax.experimental import pallas as pl
from jax.experimental.pallas import tpu as pltpu

F32 = jnp.float32
BF16 = jnp.bfloat16

D_MODEL = 1024
BATCH = 16
SEQ = 256
DEPTH = 2
DEC_BATCH = 4
DEC_SEQ = 2048
PAST_LEN = 256
GRID_W = 64
ROPE_THETA = 10000.0
N_BRANCH = 4
BRANCH_W = D_MODEL // 4
A_HEADS = 4
A_KV_HEADS = 2
A_HEAD_DIM = 64
B_HEADS = 4
B_HALF_DIM = 32
B_V_DIM = 2 * B_HALF_DIM
C_CONV_WIDTH = 31
D_GROUPS = 4
D_GROUP_W = BRANCH_W // D_GROUPS
POOL_WINDOWS = (2, 4, 8, 16)
N_EXPERTS = 32
TOP_K = 4
D_FF = D_MODEL
SWIGLU_LIMIT = 7.0
SWIGLU_ALPHA = 1.702
EPS = 1e-6

N_CTX = BATCH * SEQ
N_LAT = DEC_BATCH * DEC_SEQ
N_TOK = N_CTX + N_LAT

COL_QA, COL_KA, COL_VA = 0, 256, 384
COL_QB, COL_KB, COL_VB = 512, 768, 1024
COL_ZC, COL_ZD = 1280, 1792
N_SMALL = 2048
N_GATE = N_BRANCH * D_MODEL

TM = 256
N_TILES = N_TOK // TM
CTX_TILES = N_CTX // TM
LAT_TILES_PER_SEQ = DEC_SEQ // TM
MOD_ROWS = 8

TE = 512
N_SLOTS = N_TOK * TOP_K
P_MAX = N_SLOTS + N_EXPERTS * TE
E_TILES = P_MAX // TE

TQ = 512
CONV_PAD = 16
CONV_CHUNK = 128

VMEM_LIMIT = 56 * 1024 * 1024


def _sigmoid(x):
    return 1.0 / (1.0 + jnp.exp(-x))


def _split_bf16(a):
    hi = a.astype(BF16)
    lo = (a - hi.astype(F32)).astype(BF16)
    return hi, lo


def _dot(a, b):
    return jnp.dot(a, b, preferred_element_type=F32)


def _dot_nt(a, b):
    return lax.dot_general(a, b, (((1,), (1,)), ((), ())), preferred_element_type=F32)


def _dot3(a, b):
    ah, al = _split_bf16(a)
    bh, bl = _split_bf16(b)
    return _dot(ah, bh) + _dot(ah, bl) + _dot(al, bh)


def _rms(x, g):
    return x * lax.rsqrt(jnp.mean(x * x, axis=-1, keepdims=True) + EPS) * g


def _mod_row(i, ctx_tiles=CTX_TILES, tiles_per_seq=LAT_TILES_PER_SEQ):
    return jnp.where(i < ctx_tiles, 0, 1 + (i - ctx_tiles) // tiles_per_seq)


LANES = 128
ROW_TILES = D_MODEL // LANES


def _store_token_major(ref, x):
    n = x.shape[0]
    for c in range(ROW_TILES):
        ref[pl.ds(c, n, stride=ROW_TILES), :] = x[:, c * LANES:(c + 1) * LANES]


def _load_token_major(ref):
    n = ref.shape[0] // ROW_TILES
    return jnp.concatenate(
        [ref[pl.ds(c, n, stride=ROW_TILES), :] for c in range(ROW_TILES)], axis=1)


def _ctx_rows(width, tile=TM):
    last = N_CTX // tile - 1
    return pl.BlockSpec((tile, width), lambda i: (jnp.minimum(i, last), 0))


def _lat_rows(width, tile=TM):
    first = N_CTX // tile
    return pl.BlockSpec((tile, width), lambda i: (jnp.maximum(i - first, 0), 0))


MOD_TN = 1536


def _mod_kernel(cond_ref, w_ref, b_ref, o_ref):
    c = cond_ref[...]
    s = c * _sigmoid(c)
    o_ref[0] = _dot3(s, w_ref[0]) + b_ref[0]


def _modulation(cond8, w_mod, b_mod):
    L = w_mod.shape[0]
    return pl.pallas_call(
        _mod_kernel,
        name="modulation",
        grid=(L, 6 * D_MODEL // MOD_TN),
        in_specs=[
            pl.BlockSpec((MOD_ROWS, D_MODEL), lambda l, j: (0, 0)),
            pl.BlockSpec((1, D_MODEL, MOD_TN), lambda l, j: (l, 0, j)),
            pl.BlockSpec((1, 1, MOD_TN), lambda l, j: (l, 0, j)),
        ],
        out_specs=pl.BlockSpec((1, MOD_ROWS, MOD_TN), lambda l, j: (l, 0, j)),
        out_shape=jax.ShapeDtypeStruct((L, MOD_ROWS, 6 * D_MODEL), F32),
        compiler_params=pltpu.CompilerParams(
            dimension_semantics=("arbitrary", "arbitrary"), vmem_limit_bytes=VMEM_LIMIT),
    )(cond8, w_mod, b_mod.reshape(L, 1, 6 * D_MODEL))


def _group_mean(sq, gmat):
    hi, lo = _split_bf16(sq)
    return _dot(hi, gmat) + _dot(lo, gmat)


def _swap_halves(x, half):
    w = x.shape[-1]
    lane = lax.broadcasted_iota(jnp.int32, x.shape, 1)
    first = (lane % (2 * half)) < half
    return jnp.where(first, pltpu.roll(x, w - half, 1), pltpu.roll(x, half, 1))


def _inproj_kernel(xc_ref, xl_ref, mod_ref, gpre_ref, w_ref, gq_ref, gk_ref, gmat_ref,
                   cos_a_ref, sin_a_ref, cos_b_ref, sin_b_ref, z_ref, h_ref):
    i = pl.program_id(0)
    m = mod_ref[0]
    sh1 = m[:, 0:D_MODEL]
    sc1 = m[:, D_MODEL:2 * D_MODEL]
    x = jnp.where(i < CTX_TILES, xc_ref[...], xl_ref[...])
    h = _rms(x, gpre_ref[...]) * (1.0 + sc1) + sh1
    hb = h.astype(BF16)
    h_ref[...] = hb
    z = _dot(hb, w_ref[0])

    gmat = gmat_ref[...]
    qa = z[:, COL_QA:COL_KA]
    ka = z[:, COL_KA:COL_VA]
    qa = qa * lax.rsqrt(_group_mean(qa * qa, gmat) + EPS) * gq_ref[...]
    ka = ka * lax.rsqrt(_group_mean(ka * ka, gmat[0:128, 0:128]) + EPS) * gk_ref[...]
    qb = z[:, COL_QB:COL_KB]
    kb = z[:, COL_KB:COL_VB]
    z_ref[:, COL_VA:COL_QB] = z[:, COL_VA:COL_QB]
    z_ref[:, COL_VB:N_SMALL] = z[:, COL_VB:N_SMALL]

    @pl.when(i < CTX_TILES)
    def _():
        z_ref[:, COL_QA:COL_KA] = qa
        z_ref[:, COL_KA:COL_VA] = ka
        z_ref[:, COL_QB:COL_KB] = qb
        z_ref[:, COL_KB:COL_VB] = kb

    @pl.when(i >= CTX_TILES)
    def _():
        cos_a = cos_a_ref[...]
        sin_a = sin_a_ref[...]
        cos_b = cos_b_ref[...]
        sin_b = sin_b_ref[...]
        ha, hb_ = A_HEAD_DIM // 2, B_HALF_DIM // 2
        z_ref[:, COL_QA:COL_KA] = qa * cos_a + _swap_halves(qa, ha) * sin_a
        z_ref[:, COL_KA:COL_VA] = ka * cos_a[:, 0:128] + _swap_halves(ka, ha) * sin_a[:, 0:128]
        z_ref[:, COL_QB:COL_KB] = qb * cos_b + _swap_halves(qb, hb_) * sin_b
        z_ref[:, COL_KB:COL_VB] = kb * cos_b + _swap_halves(kb, hb_) * sin_b


def _rope_block(i):
    return jnp.where(i < CTX_TILES, 0, (i - CTX_TILES) % LAT_TILES_PER_SEQ)


def _in_projection(x_ctx, x_lat, mods, l, g_pre, w_in_bf, gq, gk, gmat, rope):
    const = lambda i: (0, 0)
    rope_spec = pl.BlockSpec((TM, 256), lambda i: (_rope_block(i), 0))
    return pl.pallas_call(
        _inproj_kernel,
        name="in_projection",
        grid=(N_TILES,),
        in_specs=[
            _ctx_rows(D_MODEL), _lat_rows(D_MODEL),
            pl.BlockSpec((1, 1, 6 * D_MODEL), lambda i: (l * MOD_ROWS + _mod_row(i), 0, 0)),
            pl.BlockSpec((1, D_MODEL), const),
            pl.BlockSpec((1, D_MODEL, N_SMALL), lambda i: (l, 0, 0)),
            pl.BlockSpec((1, 256), const),
            pl.BlockSpec((1, 128), const),
            pl.BlockSpec((256, 256), const),
            rope_spec, rope_spec, rope_spec, rope_spec,
        ],
        out_specs=[
            pl.BlockSpec((TM, N_SMALL), lambda i: (i, 0)),
            pl.BlockSpec((TM, D_MODEL), lambda i: (i, 0)),
        ],
        out_shape=[
            jax.ShapeDtypeStruct((N_TOK, N_SMALL), F32),
            jax.ShapeDtypeStruct((N_TOK, D_MODEL), BF16),
        ],
        compiler_params=pltpu.CompilerParams(
            dimension_semantics=("arbitrary",), vmem_limit_bytes=VMEM_LIMIT),
    )(x_ctx, x_lat, mods, g_pre, w_in_bf, gq, gk, gmat, *rope)


LOG2E = 1.4426950408889634
HALF_LANES = 64


def _attend(q, ks, vexts):
    ss = [_dot_nt(q, k) for k in ks]
    m = ss[0].max(axis=-1, keepdims=True)
    for s in ss[1:]:
        m = jnp.maximum(m, s.max(axis=-1, keepdims=True))
    o = None
    for s, v in zip(ss, vexts):
        t = _dot(jnp.exp2(s - m).astype(BF16), v)
        o = t if o is None else o + t
    return o


def _value_ext(v128, upper):
    lane = lax.broadcasted_iota(jnp.int32, v128.shape, 1)
    keep = (lane >= HALF_LANES) if upper else (lane < HALF_LANES)
    return jnp.where(keep, v128, 1.0).astype(BF16)


def _normalised(o, upper):
    if upper:
        return o[:, HALF_LANES:] / o[:, 0:1]
    return o[:, :HALF_LANES] / o[:, HALF_LANES:HALF_LANES + 1]


def _attn_kernel(*refs, has_cache, lambda_init, n_seq=1):
    if n_seq > 1:
        rows = refs[0].shape[0] // n_seq
        for s in range(n_seq):
            sub = [r.at[pl.ds(s * rows, rows), :] for r in refs[:6]]
            outs = [r.at[pl.ds(s * rows, rows), :] for r in refs[-2:]]
            _attn_kernel(*sub, *refs[6:-2], *outs, has_cache=has_cache, lambda_init=lambda_init)
        return
    if has_cache:
        (qa_ref, ka_ref, va_ref, qb_ref, kb_ref, vb_ref,
         cka_ref, cva_ref, ckb_ref, cvb_ref, lam_ref, subg_ref, oa_ref, ob_ref) = refs
    else:
        (qa_ref, ka_ref, va_ref, qb_ref, kb_ref, vb_ref,
         lam_ref, subg_ref, oa_ref, ob_ref) = refs

    def cols(ref, lo, hi):
        return ref[:, lo:hi].astype(BF16)

    def ccols(ref, lo, hi):
        return ref[0, 0, :, lo:hi].astype(BF16)

    scale_a = A_HEAD_DIM ** -0.5 * LOG2E
    group = A_HEADS // A_KV_HEADS
    for g in range(A_KV_HEADS):
        lo, hi = g * A_HEAD_DIM, (g + 1) * A_HEAD_DIM
        upper = g % 2 == 1
        ks = [cols(ka_ref, lo, hi)]
        vexts = [_value_ext(va_ref[...], upper)]
        if has_cache:
            ks.append(ccols(cka_ref, lo, hi))
            vexts.append(_value_ext(cva_ref[0, 0], upper))
        for hd in range(g * group, (g + 1) * group):
            q = (qa_ref[:, hd * A_HEAD_DIM:(hd + 1) * A_HEAD_DIM] * scale_a).astype(BF16)
            o = _attend(q, ks, vexts)
            oa_ref[:, hd * A_HEAD_DIM:(hd + 1) * A_HEAD_DIM] = _normalised(o, upper)

    bl = lam_ref[...]
    lam = (jnp.exp(jnp.sum(bl[0:1] * bl[1:2], axis=-1, keepdims=True))
           - jnp.exp(jnp.sum(bl[2:3] * bl[3:4], axis=-1, keepdims=True)) + lambda_init)
    scale_b = B_HALF_DIM ** -0.5 * LOG2E
    subg = subg_ref[...]
    for hd in range(B_HEADS):
        base = hd * 2 * B_HALF_DIM
        vlo, vhi = hd * B_V_DIM, (hd + 1) * B_V_DIM
        blk = (hd // 2) * 2 * B_V_DIM
        upper = hd % 2 == 1
        vexts = [_value_ext(vb_ref[:, blk:blk + 2 * B_V_DIM], upper)]
        if has_cache:
            vexts.append(_value_ext(cvb_ref[0, 0, :, blk:blk + 2 * B_V_DIM], upper))
        outs = []
        for c in range(2):
            lo, hi = base + c * B_HALF_DIM, base + (c + 1) * B_HALF_DIM
            q = (qb_ref[:, lo:hi] * scale_b).astype(BF16)
            ks = [cols(kb_ref, lo, hi)]
            if has_cache:
                ks.append(ccols(ckb_ref, lo, hi))
            outs.append(_normalised(_attend(q, ks, vexts), upper))
        o = outs[0] - lam * outs[1]
        o = _rms(o, subg) * (1.0 - lambda_init)
        ob_ref[:, vlo:vhi] = o


CTX_SEQ_PER_STEP = 1


def _attention_ctx(z, b_lambda, subg, lambda_init):
    rows = CTX_SEQ_PER_STEP * SEQ

    def zspec(width, col):
        return pl.BlockSpec((rows, width), lambda b: (b, col // width))
    out_spec = pl.BlockSpec((rows, BRANCH_W), lambda b: (b, 0))
    return pl.pallas_call(
        functools.partial(_attn_kernel, has_cache=False, lambda_init=lambda_init,
                          n_seq=CTX_SEQ_PER_STEP),
        name="attention_ctx",
        grid=(BATCH // CTX_SEQ_PER_STEP,),
        in_specs=[
            zspec(256, COL_QA), zspec(128, COL_KA), zspec(128, COL_VA),
            zspec(256, COL_QB), zspec(256, COL_KB), zspec(256, COL_VB),
            pl.BlockSpec((4, B_HALF_DIM), lambda b: (0, 0)),
            pl.BlockSpec((1, B_V_DIM), lambda b: (0, 0)),
        ],
        out_specs=[out_spec, out_spec],
        out_shape=[jax.ShapeDtypeStruct((N_CTX, BRANCH_W), F32)] * 2,
        compiler_params=pltpu.CompilerParams(
            dimension_semantics=("arbitrary",), vmem_limit_bytes=VMEM_LIMIT),
    )(z, z, z, z, z, z, b_lambda, subg)


def _attention_lat(z, caches, l, b_lambda, subg, lambda_init):
    q_tiles = DEC_SEQ // TQ
    seq0 = N_CTX // DEC_SEQ

    def qspec(col):
        return pl.BlockSpec((TQ, 256), lambda b, t: (N_CTX // TQ + b * q_tiles + t, col // 256))

    def kspec(width, col):
        return pl.BlockSpec((DEC_SEQ, width), lambda b, t: (seq0 + b, col // width))

    def cspec(width):
        return pl.BlockSpec((1, 1, PAST_LEN, width), lambda b, t: (b, l, 0, 0))

    out_spec = pl.BlockSpec((TQ, BRANCH_W), lambda b, t: (b * q_tiles + t, 0))
    cka, cva, ckb, cvb = caches
    return pl.pallas_call(
        functools.partial(_attn_kernel, has_cache=True, lambda_init=lambda_init),
        name="attention_lat",
        grid=(DEC_BATCH, q_tiles),
        in_specs=[
            qspec(COL_QA), kspec(128, COL_KA), kspec(128, COL_VA),
            qspec(COL_QB), kspec(256, COL_KB), kspec(256, COL_VB),
            cspec(128), cspec(128), cspec(256), cspec(256),
            pl.BlockSpec((4, B_HALF_DIM), lambda b, t: (0, 0)),
            pl.BlockSpec((1, B_V_DIM), lambda b, t: (0, 0)),
        ],
        out_specs=[out_spec, out_spec],
        out_shape=[jax.ShapeDtypeStruct((N_LAT, BRANCH_W), F32)] * 2,
        compiler_params=pltpu.CompilerParams(
            dimension_semantics=("arbitrary", "arbitrary"), vmem_limit_bytes=VMEM_LIMIT),
    )(z, z, z, z, z, z, cka, cva, ckb, cvb, b_lambda, subg)


SUBLANES = 8


def _row_shifter(win):
    span = win.shape[0] - SUBLANES
    shifted = {}

    def rows(off):
        s = off % SUBLANES
        if s not in shifted:
            shifted[s] = win[s:s + span, :]
        base = off - s
        return shifted[s][base:base + CONV_CHUNK, :]

    return rows


def _convpool_kernel(a_ref, b_ref, d_ref, dww_ref, dwb_ref, lng_ref, lnb_ref, wbd_ref, dsc_ref,
                     oc_ref, od_ref, hpad, upad, *, seq_len):
    n_chunks = seq_len // CONV_CHUNK
    zeros = jnp.zeros((CONV_PAD, BRANCH_W), F32)
    hpad[0:CONV_PAD, :] = zeros
    hpad[seq_len + CONV_PAD:seq_len + 2 * CONV_PAD, :] = zeros
    upad[0:CONV_PAD, :] = zeros
    upad[seq_len + CONV_PAD:seq_len + 2 * CONV_PAD, :] = zeros

    def fill(c, carry):
        r = pl.multiple_of(c * CONV_CHUNK, CONV_CHUNK)
        a = a_ref[pl.ds(r, CONV_CHUNK), :]
        b = b_ref[pl.ds(r, CONV_CHUNK), :]
        hpad[pl.ds(r + CONV_PAD, CONV_CHUNK), :] = a * _sigmoid(b)
        upad[pl.ds(r + CONV_PAD, CONV_CHUNK), :] = d_ref[pl.ds(r, CONV_CHUNK), :]
        return carry

    lax.fori_loop(0, n_chunks, fill, 0)

    lane = lax.broadcasted_iota(jnp.int32, (CONV_CHUNK, BRANCH_W), 1)
    half = C_CONV_WIDTH // 2

    def body(c, carry):
        r = pl.multiple_of(c * CONV_CHUNK, CONV_CHUNK)
        hrows = _row_shifter(hpad[pl.ds(r, CONV_CHUNK + 2 * CONV_PAD), :])
        acc = jnp.zeros((CONV_CHUNK, BRANCH_W), F32)
        for k in range(C_CONV_WIDTH):
            acc = acc + dww_ref[k:k + 1, :] * hrows(CONV_PAD - half + k)
        acc = acc + dwb_ref[...]
        mu = jnp.mean(acc, axis=-1, keepdims=True)
        cen = acc - mu
        var = jnp.mean(cen * cen, axis=-1, keepdims=True)
        y = cen * lax.rsqrt(var + EPS) * lng_ref[...] + lnb_ref[...]
        oc_ref[pl.ds(r, CONV_CHUNK), :] = y * _sigmoid(y)

        urows = _row_shifter(upad[pl.ds(r, CONV_CHUNK + 2 * CONV_PAD), :])

        def ld(d):
            return urows(CONV_PAD + d)

        u = ld(0)
        sums = {}
        s = u + ld(-1)
        sums[2] = s
        s = s + ld(-2) + ld(1)
        sums[4] = s
        s = s + ld(-4) + ld(-3) + ld(2) + ld(3)
        sums[8] = s
        s = s + ld(-8) + ld(-7) + ld(-6) + ld(-5) + ld(4) + ld(5) + ld(6) + ld(7)
        sums[16] = s
        t = r + lax.broadcasted_iota(jnp.int32, (CONV_CHUNK, 1), 0)
        pooled = None
        for g, w in reversed(list(enumerate(POOL_WINDOWS))):
            lo = jnp.maximum(t - w // 2, 0)
            hi = jnp.minimum(t + (w - 1 - w // 2), seq_len - 1)
            mean = sums[w] / (hi - lo + 1).astype(F32)
            pooled = mean if pooled is None else jnp.where(lane < (g + 1) * D_GROUP_W, mean, pooled)
        dlt = (pooled - u).astype(BF16)
        od_ref[pl.ds(r, CONV_CHUNK), :] = _dot(dlt, wbd_ref[...]) * dsc_ref[...]
        return carry

    lax.fori_loop(0, n_chunks, body, 0)


def _conv_pool(z, seq_len, n_seq, row0, dww, dwb, lng, lnb, wbd, dsc):
    blk0 = row0 // seq_len

    def zspec(col):
        return pl.BlockSpec((seq_len, 256), lambda b: (blk0 + b, col // 256))

    def const(shape):
        return pl.BlockSpec(shape, lambda b: (0, 0))

    out_spec = pl.BlockSpec((seq_len, BRANCH_W), lambda b: (b, 0))
    return pl.pallas_call(
        functools.partial(_convpool_kernel, seq_len=seq_len),
        name=f"conv_pool_{seq_len}",
        grid=(n_seq,),
        in_specs=[
            zspec(COL_ZC), zspec(COL_ZC + BRANCH_W), zspec(COL_ZD),
            const((C_CONV_WIDTH, BRANCH_W)), const((1, BRANCH_W)), const((1, BRANCH_W)),
            const((1, BRANCH_W)), const((BRANCH_W, BRANCH_W)), const((1, BRANCH_W)),
        ],
        out_specs=[out_spec, out_spec],
        out_shape=[jax.ShapeDtypeStruct((n_seq * seq_len, BRANCH_W), F32)] * 2,
        scratch_shapes=[pltpu.VMEM((seq_len + 2 * CONV_PAD, BRANCH_W), F32)] * 2,
        compiler_params=pltpu.CompilerParams(
            dimension_semantics=("arbitrary",), vmem_limit_bytes=VMEM_LIMIT),
    )(z, z, z, dww, dwb, lng, lnb, wbd, dsc)


MIX_TM = 512
MIX_TILES = N_TOK // MIX_TM
MIX_CTX_TILES = N_CTX // MIX_TM
MIX_LAT_PER_SEQ = DEC_SEQ // MIX_TM
ROUTE_ROWS = 8


def _mix_kernel(xc_ref, xl_ref, h_ref, oa_c, ob_c, oc_c, od_c, oa_l, ob_l, oc_l, od_l,
                mod_ref, wg01_ref, wg23_ref, wbr_ref, wout_ref, gpost_ref, gpre2_ref, rwt_ref, rb_ref,
                x1_ref, h2_ref, eid_ref, rank_ref, wts_ref, cnt_ref, carry):
    i = pl.program_id(0)
    is_ctx = i < MIX_CTX_TILES

    @pl.when(i == 0)
    def _():
        carry[...] = jnp.zeros_like(carry)

    m = mod_ref[0]
    g1 = m[:, 2 * D_MODEL:3 * D_MODEL]
    sh2 = m[:, 3 * D_MODEL:4 * D_MODEL]
    sc2 = m[:, 4 * D_MODEL:5 * D_MODEL]

    hb = h_ref[...]
    merged = None
    for n, (c_ref, l_ref) in enumerate(((oa_c, oa_l), (ob_c, ob_l), (oc_c, oc_l), (od_c, od_l))):
        wg_ref = wg01_ref if n < 2 else wg23_ref
        gate = _sigmoid(_dot(hb, wg_ref[0, :, (n % 2) * D_MODEL:(n % 2 + 1) * D_MODEL]))
        o = jnp.where(is_ctx, c_ref[...], l_ref[...])
        br = _dot(o.astype(BF16), wbr_ref[0, n])
        merged = gate * br if merged is None else merged + gate * br
    y = _dot(merged.astype(BF16), wout_ref[0])
    x = jnp.where(is_ctx, xc_ref[...], xl_ref[...])
    x1 = x + g1 * _rms(y, gpost_ref[...])
    x1_ref[...] = x1
    h2 = _rms(x1, gpre2_ref[...]) * (1.0 + sc2) + sh2
    _store_token_major(h2_ref, h2)

    rh, rl = _split_bf16(rwt_ref[...])
    hh, hl = _split_bf16(h2)
    logits = _dot_nt(rh, hh) + _dot_nt(rh, hl) + _dot_nt(rl, hh) + rb_ref[...]

    iota_e = lax.broadcasted_iota(jnp.int32, (N_EXPERTS, MIX_TM), 0)
    rem = logits
    vals, idxs = [], []
    for _ in range(TOP_K):
        mx = jnp.max(rem, axis=0, keepdims=True)
        idx = jnp.min(jnp.where(rem == mx, iota_e, N_EXPERTS), axis=0, keepdims=True)
        vals.append(mx)
        idxs.append(idx)
        rem = jnp.where(iota_e == idx, -jnp.inf, rem)
    exps = [jnp.exp(v - vals[0]) for v in vals]
    den = exps[0]
    for e in exps[1:]:
        den = den + e

    sel = [iota_e == idx for idx in idxs]
    member = jnp.zeros((N_EXPERTS, MIX_TM), F32)
    for s in sel:
        member = member + jnp.where(s, 1.0, 0.0)
    row = lax.broadcasted_iota(jnp.int32, (MIX_TM, MIX_TM), 0)
    col = lax.broadcasted_iota(jnp.int32, (MIX_TM, MIX_TM), 1)
    earlier = jnp.where(row < col, 1.0, 0.0).astype(BF16)
    seen = _dot(member.astype(BF16), earlier) + carry[...]
    carry[...] = carry[...] + jnp.sum(member, axis=1, keepdims=True)
    cnt_ref[...] = carry[...]

    sub = lax.broadcasted_iota(jnp.int32, (ROUTE_ROWS, MIX_TM), 0)
    eid = jnp.zeros((ROUTE_ROWS, MIX_TM), jnp.int32)
    rank = jnp.zeros((ROUTE_ROWS, MIX_TM), jnp.int32)
    wts = jnp.zeros((ROUTE_ROWS, MIX_TM), F32)
    for k in range(TOP_K):
        rk = jnp.sum(jnp.where(sel[k], seen, 0.0), axis=0, keepdims=True).astype(jnp.int32)
        eid = jnp.where(sub == k, idxs[k], eid)
        rank = jnp.where(sub == k, rk, rank)
        wts = jnp.where(sub == k, exps[k] / den, wts)
    eid_ref[...] = eid
    rank_ref[...] = rank
    wts_ref[...] = wts


def _mix(x_ctx, x_lat, h, o_ctx, o_lat, mods, l, w_in_bf, w_branch_bf, w_out_bf,
         g_post, g_pre2, r_wt, r_b):
    const2 = lambda i: (0, 0)
    tile = lambda width: pl.BlockSpec((MIX_TM, width), lambda i: (i, 0))
    route = pl.BlockSpec((ROUTE_ROWS, MIX_TM), lambda i: (0, i))
    ctx_tile = _ctx_rows(BRANCH_W, MIX_TM)
    lat_tile = _lat_rows(BRANCH_W, MIX_TM)
    mod_row = lambda i: _mod_row(i, MIX_CTX_TILES, MIX_LAT_PER_SEQ)
    single = pl.Buffered(1)
    return pl.pallas_call(
        _mix_kernel,
        name="mix_router",
        grid=(MIX_TILES,),
        in_specs=[
            _ctx_rows(D_MODEL, MIX_TM), _lat_rows(D_MODEL, MIX_TM), tile(D_MODEL),
            ctx_tile, ctx_tile, ctx_tile, ctx_tile, lat_tile, lat_tile, lat_tile, lat_tile,
            pl.BlockSpec((1, 1, 6 * D_MODEL), lambda i: (l * MOD_ROWS + mod_row(i), 0, 0)),
            pl.BlockSpec((1, D_MODEL, N_SMALL), lambda i: (l, 0, 1), pipeline_mode=single),
            pl.BlockSpec((1, D_MODEL, N_SMALL), lambda i: (l, 0, 2), pipeline_mode=single),
            pl.BlockSpec((1, N_BRANCH, BRANCH_W, D_MODEL), lambda i: (l, 0, 0, 0),
                         pipeline_mode=single),
            pl.BlockSpec((1, D_MODEL, D_MODEL), lambda i: (l, 0, 0), pipeline_mode=single),
            pl.BlockSpec((1, D_MODEL), const2),
            pl.BlockSpec((1, D_MODEL), const2),
            pl.BlockSpec((N_EXPERTS, D_MODEL), const2),
            pl.BlockSpec((N_EXPERTS, 1), const2),
        ],
        out_specs=[
            tile(D_MODEL), pl.BlockSpec((MIX_TM * ROW_TILES, LANES), lambda i: (i, 0)),
            route, route, route,
            pl.BlockSpec((N_EXPERTS, 1), const2),
        ],
        out_shape=[
            jax.ShapeDtypeStruct((N_TOK, D_MODEL), F32),
            jax.ShapeDtypeStruct((N_TOK * ROW_TILES, LANES), F32),
            jax.ShapeDtypeStruct((ROUTE_ROWS, N_TOK), jnp.int32),
            jax.ShapeDtypeStruct((ROUTE_ROWS, N_TOK), jnp.int32),
            jax.ShapeDtypeStruct((ROUTE_ROWS, N_TOK), F32),
            jax.ShapeDtypeStruct((N_EXPERTS, 1), F32),
        ],
        scratch_shapes=[pltpu.VMEM((N_EXPERTS, 1), F32)],
        compiler_params=pltpu.CompilerParams(
            dimension_semantics=("arbitrary",), vmem_limit_bytes=VMEM_LIMIT),
    )(x_ctx, x_lat, h, *o_ctx, *o_lat, mods, w_in_bf, w_in_bf, w_branch_bf, w_out_bf,
      g_post, g_pre2, r_wt, r_b)


def _row_copy(src_ref, src_row, dst_ref, dst_row, sem):
    return pltpu.make_async_copy(src_ref.at[pl.ds(src_row, 1), :], dst_ref.at[pl.ds(dst_row, 1), :], sem)


def _token_rows(t, n=1):
    return pl.ds(pl.multiple_of(t * ROW_TILES, ROW_TILES), n * ROW_TILES)


def _token_copy(src_ref, src_tok, dst_ref, dst_tok, sem):
    return pltpu.make_async_copy(src_ref.at[_token_rows(src_tok), :], dst_ref.at[_token_rows(dst_tok), :], sem)


DISPATCH_ROWS = 512
DISPATCH_STEPS = N_TOK // DISPATCH_ROWS
DMA_UNROLL = 8


def _dispatch_kernel(starts_ref, padded_ref, pos_ref, h2_ref, xs_ref, zbuf, sem, zsem, tsem):
    i = pl.program_id(0)

    def pad_copy(e):
        last = pl.multiple_of(starts_ref[e] + padded_ref[e] - TE, TE)
        return pltpu.make_async_copy(zbuf, xs_ref.at[_token_rows(last, TE), :], zsem)

    first_tail = (starts_ref[N_EXPERTS - 1] + padded_ref[N_EXPERTS - 1]) // TE

    def tail_copy(j):
        return pltpu.make_async_copy(zbuf, xs_ref.at[_token_rows(j * TE, TE), :], tsem)

    def tail_start(j, carry):
        tail_copy(j).start()
        return carry

    def tail_wait(j, carry):
        tail_copy(j).wait()
        return carry

    @pl.when(i == 0)
    def _():
        zbuf[...] = jnp.zeros_like(zbuf)
        for e in range(N_EXPERTS):
            @pl.when(padded_ref[e] > 0)
            def _():
                pad_copy(e).start()
        lax.fori_loop(first_tail, E_TILES, tail_start, 0)
        for e in range(N_EXPERTS):
            @pl.when(padded_ref[e] > 0)
            def _():
                pad_copy(e).wait()

    @pl.when(i == DISPATCH_STEPS - 1)
    def _():
        lax.fori_loop(first_tail, E_TILES, tail_wait, 0)

    def issue(t, carry):
        for k in range(TOP_K):
            _token_copy(h2_ref, t, xs_ref, pos_ref[t * TOP_K + k], sem).start(priority=k % 2)
        return carry

    lax.fori_loop(0, DISPATCH_ROWS, issue, 0, unroll=DMA_UNROLL)

    for _ in range(TOP_K):
        pltpu.make_async_copy(h2_ref, xs_ref.at[_token_rows(0, DISPATCH_ROWS), :], sem).wait()


def _dispatch(starts, padded, pos, h2):
    grid_spec = pltpu.PrefetchScalarGridSpec(
        num_scalar_prefetch=2,
        grid=(DISPATCH_STEPS,),
        in_specs=[
            pl.BlockSpec((DISPATCH_ROWS * TOP_K,), lambda i, st, pd: (i,), memory_space=pltpu.SMEM),
            pl.BlockSpec((DISPATCH_ROWS * ROW_TILES, LANES), lambda i, st, pd: (i, 0)),
        ],
        out_specs=pl.BlockSpec(memory_space=pl.ANY),
        scratch_shapes=[pltpu.VMEM((TE * ROW_TILES, LANES), F32),
                        pltpu.SemaphoreType.DMA(()), pltpu.SemaphoreType.DMA(()),
                        pltpu.SemaphoreType.DMA(())],
    )
    return pl.pallas_call(
        _dispatch_kernel,
        name="dispatch",
        grid_spec=grid_spec,
        out_shape=jax.ShapeDtypeStruct((P_MAX * ROW_TILES, LANES), F32),
        compiler_params=pltpu.CompilerParams(
            dimension_semantics=("arbitrary",), vmem_limit_bytes=VMEM_LIMIT),
    )(starts, padded, pos, h2)


W_CAST_ROWS = 128


def _moe_kernel(te_ref, rows_ref, slot_ref, next_ref, x_ref, w1_hbm, b1_ref, w2_hbm, b2_ref, y_ref,
                w1f, w2f, w1b, w2b, sem, *, layer):
    j = pl.program_id(0)
    e = te_ref[j]
    prev = te_ref[jnp.maximum(j - 1, 0)]
    valid = rows_ref[j] > 0
    first = jnp.logical_and(valid, jnp.logical_or(j == 0, e != prev))
    s = slot_ref[j]

    def weight_copies(ex, sl):
        return (pltpu.make_async_copy(w1_hbm.at[layer, ex], w1f.at[sl], sem.at[0, sl]),
                pltpu.make_async_copy(w2_hbm.at[layer, ex], w2f.at[sl], sem.at[1, sl]))

    @pl.when(j == 0)
    def _():
        for cp in weight_copies(e, s):
            cp.start()

    @pl.when(first)
    def _():
        for cp in weight_copies(e, s):
            cp.wait()
        nxt = next_ref[j]

        @pl.when(nxt >= 0)
        def _():
            for cp in weight_copies(nxt, 1 - s):
                cp.start()

        def cast(c, carry):
            r = pl.multiple_of(c * W_CAST_ROWS, W_CAST_ROWS)
            w1b[pl.ds(r, W_CAST_ROWS), :] = w1f[s, pl.ds(r, W_CAST_ROWS), :].astype(BF16)
            w2b[pl.ds(r, W_CAST_ROWS), :] = w2f[s, pl.ds(r, W_CAST_ROWS), :].astype(BF16)
            return carry
        lax.fori_loop(0, D_MODEL // W_CAST_ROWS, cast, 0)

    @pl.when(valid)
    def _():
        x = _load_token_major(x_ref).astype(BF16)
        u = _dot(x, w1b[...]) + b1_ref[0, 0]
        xg = jnp.minimum(u[:, :D_FF], SWIGLU_LIMIT)
        xl = jnp.clip(u[:, D_FF:], -SWIGLU_LIMIT, SWIGLU_LIMIT)
        act = (xl + 1.0) * (xg * _sigmoid(SWIGLU_ALPHA * xg))
        y_ref[...] = _dot(act.astype(BF16), w2b[...]) + b2_ref[0, 0]

    @pl.when(jnp.logical_not(valid))
    def _():
        y_ref[...] = jnp.zeros_like(y_ref)


def _expert_ffn(tile_expert, tile_rows, tile_slot, tile_next, xs, l, e_w1, e_b1, e_w2, e_b2):
    L = e_w1.shape[0]
    grid_spec = pltpu.PrefetchScalarGridSpec(
        num_scalar_prefetch=4,
        grid=(E_TILES,),
        in_specs=[
            pl.BlockSpec((TE * ROW_TILES, LANES),
                         lambda j, te, nr, sl, nx: (jnp.where(nr[j] > 0, j, 0), 0)),
            pl.BlockSpec(memory_space=pl.ANY),
            pl.BlockSpec((1, 1, 1, 2 * D_FF), lambda j, te, nr, sl, nx: (l, te[j], 0, 0)),
            pl.BlockSpec(memory_space=pl.ANY),
            pl.BlockSpec((1, 1, 1, D_MODEL), lambda j, te, nr, sl, nx: (l, te[j], 0, 0)),
        ],
        out_specs=pl.BlockSpec((TE, D_MODEL), lambda j, te, nr, sl, nx: (j, 0)),
        scratch_shapes=[pltpu.VMEM((2, D_MODEL, 2 * D_FF), F32), pltpu.VMEM((2, D_FF, D_MODEL), F32),
                        pltpu.VMEM((D_MODEL, 2 * D_FF), BF16), pltpu.VMEM((D_FF, D_MODEL), BF16),
                        pltpu.SemaphoreType.DMA((2, 2))],
    )
    return pl.pallas_call(
        functools.partial(_moe_kernel, layer=l),
        name="expert_ffn",
        grid_spec=grid_spec,
        out_shape=jax.ShapeDtypeStruct((P_MAX, D_MODEL), F32),
        compiler_params=pltpu.CompilerParams(
            dimension_semantics=("arbitrary",), vmem_limit_bytes=VMEM_LIMIT),
    )(tile_expert, tile_rows, tile_slot, tile_next, xs, e_w1,
      e_b1.reshape(L, N_EXPERTS, 1, 2 * D_FF), e_w2, e_b2.reshape(L, N_EXPERTS, 1, D_MODEL))


def _combine_kernel(pos_ref, pos_next_ref, ys_ref, wts_ref, x1_ref, mod_ref, gpost_ref,
                    out_c_ref, out_l_ref, buf, sem):
    i = pl.program_id(0)
    slot = i % 2

    def gather(p_ref, s):
        def issue(t, carry):
            for k in range(TOP_K):
                _row_copy(ys_ref, p_ref[t * TOP_K + k], buf.at[s, k], t, sem.at[s]).start(priority=k % 2)
            return carry
        lax.fori_loop(0, TM, issue, 0, unroll=DMA_UNROLL)

    @pl.when(i == 0)
    def _():
        gather(pos_ref, 0)

    @pl.when(i + 1 < N_TILES)
    def _():
        gather(pos_next_ref, 1 - slot)

    for k in range(TOP_K):
        pltpu.make_async_copy(ys_ref.at[pl.ds(0, TM), :], buf.at[slot, k], sem.at[slot]).wait()

    w = wts_ref[...]
    y = w[:, 0:1] * buf[slot, 0]
    for k in range(1, TOP_K):
        y = y + w[:, k:k + 1] * buf[slot, k]
    g2 = mod_ref[0][:, 5 * D_MODEL:6 * D_MODEL]
    res = x1_ref[...] + g2 * _rms(y, gpost_ref[...])

    @pl.when(i < CTX_TILES)
    def _():
        out_c_ref[...] = res

    @pl.when(i >= CTX_TILES)
    def _():
        out_l_ref[...] = res


def _combine(pos, ys, wts, x1, mods, l, g_post2):
    return pl.pallas_call(
        _combine_kernel,
        name="combine",
        grid=(N_TILES,),
        in_specs=[
            pl.BlockSpec((TM * TOP_K,), lambda i: (i,), memory_space=pltpu.SMEM),
            pl.BlockSpec((TM * TOP_K,), lambda i: (jnp.minimum(i + 1, N_TILES - 1),),
                         memory_space=pltpu.SMEM),
            pl.BlockSpec(memory_space=pl.ANY),
            pl.BlockSpec((TM, TOP_K), lambda i: (i, 0)),
            pl.BlockSpec((TM, D_MODEL), lambda i: (i, 0)),
            pl.BlockSpec((1, 1, 6 * D_MODEL), lambda i: (l * MOD_ROWS + _mod_row(i), 0, 0)),
            pl.BlockSpec((1, D_MODEL), lambda i: (0, 0)),
        ],
        out_specs=[_ctx_rows(D_MODEL), _lat_rows(D_MODEL)],
        out_shape=[jax.ShapeDtypeStruct((N_CTX, D_MODEL), F32),
                   jax.ShapeDtypeStruct((N_LAT, D_MODEL), F32)],
        scratch_shapes=[pltpu.VMEM((2, TOP_K, TM, D_MODEL), F32), pltpu.SemaphoreType.DMA((2,))],
        compiler_params=pltpu.CompilerParams(
            dimension_semantics=("arbitrary",), vmem_limit_bytes=VMEM_LIMIT),
    )(pos, pos, ys, wts, x1, mods, g_post2)


def _rope_tables():
    t = np.arange(DEC_SEQ)
    r = (t // GRID_W).astype(np.float32)
    c = (t % GRID_W).astype(np.float32)

    def table(dim, reps):
        n_axis = dim // 4
        inv = (ROPE_THETA ** (-np.arange(n_axis, dtype=np.float32) / n_axis)).astype(np.float32)
        ang = np.concatenate([r[:, None] * inv, c[:, None] * inv], axis=-1).astype(np.float32)
        cos, sin = np.cos(ang), np.sin(ang)
        return (np.tile(np.concatenate([cos, cos], -1), (1, reps)),
                np.tile(np.concatenate([-sin, sin], -1), (1, reps)))

    cos_a, sin_a = table(A_HEAD_DIM, 256 // A_HEAD_DIM)
    cos_b, sin_b = table(B_HALF_DIM, 256 // B_HALF_DIM)
    return tuple(jnp.asarray(a, F32) for a in (cos_a, sin_a, cos_b, sin_b))


def _group_mean_matrix():
    idx = np.arange(256) // A_HEAD_DIM
    return jnp.asarray((idx[:, None] == idx[None, :]).astype(np.float32) / A_HEAD_DIM, BF16)


def _block_diag(w):
    out = jnp.zeros((BRANCH_W, BRANCH_W), w.dtype)
    for g in range(D_GROUPS):
        out = out.at[g * D_GROUP_W:(g + 1) * D_GROUP_W, g * D_GROUP_W:(g + 1) * D_GROUP_W].set(w[g])
    return out


def _routing_tables(eid, rank, cnt):
    cnt = cnt.reshape(N_EXPERTS).astype(jnp.int32)
    padded = ((cnt + TE - 1) // TE) * TE
    ends = jnp.cumsum(padded)
    starts = ends - padded
    experts = jnp.arange(N_EXPERTS, dtype=jnp.int32)
    start_of = jnp.sum(jnp.where(eid[:TOP_K, :, None] == experts, starts, 0), axis=-1)
    pos = (start_of + rank[:TOP_K]).T.reshape(N_SLOTS)
    tile_row = jnp.arange(E_TILES, dtype=jnp.int32) * TE
    tile_expert = jnp.sum((tile_row[:, None] >= ends[None, :]).astype(jnp.int32), axis=-1)
    tile_expert = jnp.minimum(tile_expert, N_EXPERTS - 1)
    used_end = jnp.sum(jnp.where(tile_expert[:, None] == experts, starts + cnt, 0), axis=-1)
    tile_rows = jnp.where(tile_row < ends[-1], jnp.clip(used_end - tile_row, 0, TE), 0)
    valid = tile_rows > 0
    changed = jnp.concatenate([jnp.ones((1,), bool), tile_expert[1:] != tile_expert[:-1]])
    tile_slot = (jnp.cumsum(jnp.logical_and(valid, changed).astype(jnp.int32)) + 1) % 2
    group_end = jnp.sum(jnp.where(tile_expert[:, None] == experts, ends, 0), axis=-1) // TE
    tiles = jnp.arange(E_TILES, dtype=jnp.int32)
    expert_at = jnp.where(valid, tile_expert, -1)
    tile_next = jnp.sum(jnp.where(group_end[:, None] == tiles, expert_at + 1, 0), axis=-1) - 1
    as_i32 = lambda a: a.astype(jnp.int32)
    return (as_i32(pos), as_i32(tile_expert), as_i32(tile_rows), as_i32(tile_slot), as_i32(tile_next),
            as_i32(starts), as_i32(padded))


def kernel(x_prompt, x_sample, cache_a_k, cache_a_v, cache_b_k, cache_b_v, c, c_ctx, w_mod, b_mod, g_pre1, g_post1, g_pre2, g_post2, w_in, a_q_gain, a_k_gain, b_lambda, b_subln_gain, c_dw_w, c_dw_b, c_ln_g, c_ln_b, d_w_group, d_scale, w_branch, w_out, r_w, r_b, e_w1, e_b1, e_w2, e_b2):
    x_ctx = x_prompt.reshape(N_CTX, D_MODEL)
    x_lat = x_sample.reshape(N_LAT, D_MODEL)
    cond8 = jnp.concatenate(
        [c_ctx[None, :], c, jnp.zeros((MOD_ROWS - 1 - DEC_BATCH, D_MODEL), F32)], axis=0)
    mods = _modulation(cond8, w_mod, b_mod).reshape(DEPTH * MOD_ROWS, 1, 6 * D_MODEL)

    rope = _rope_tables()
    gmat = _group_mean_matrix()
    caches = (
        cache_a_k.reshape(DEC_BATCH, DEPTH, PAST_LEN, A_KV_HEADS * A_HEAD_DIM),
        cache_a_v.reshape(DEC_BATCH, DEPTH, PAST_LEN, A_KV_HEADS * A_HEAD_DIM),
        cache_b_k.reshape(DEC_BATCH, DEPTH, PAST_LEN, B_HEADS * 2 * B_HALF_DIM),
        cache_b_v.reshape(DEC_BATCH, DEPTH, PAST_LEN, B_HEADS * B_V_DIM),
    )

    w_in_bf = w_in.astype(BF16)
    w_branch_bf = w_branch.astype(BF16)
    w_out_bf = w_out.astype(BF16)

    new_ak, new_av, new_bk, new_bv = [], [], [], []
    row = lambda v: v.reshape(1, -1)
    for l in range(DEPTH):
        lambda_init = 0.8 - 0.6 * math.exp(-0.3 * l)
        z, h = _in_projection(
            x_ctx, x_lat, mods, l, row(g_pre1[l]), w_in_bf,
            row(jnp.tile(a_q_gain[l], A_HEADS)), row(jnp.tile(a_k_gain[l], A_KV_HEADS)), gmat, rope)

        zc = z[:N_CTX]
        new_ak.append(zc[:, COL_KA:COL_VA].reshape(BATCH, SEQ, A_KV_HEADS, A_HEAD_DIM))
        new_av.append(zc[:, COL_VA:COL_QB].reshape(BATCH, SEQ, A_KV_HEADS, A_HEAD_DIM))
        new_bk.append(zc[:, COL_KB:COL_VB].reshape(BATCH, SEQ, B_HEADS, 2, B_HALF_DIM))
        new_bv.append(zc[:, COL_VB:COL_ZC].reshape(BATCH, SEQ, B_HEADS, B_V_DIM))

        subg = row(b_subln_gain[l])
        oa_c, ob_c = _attention_ctx(z, b_lambda[l], subg, lambda_init)
        oa_l, ob_l = _attention_lat(z, caches, l, b_lambda[l], subg, lambda_init)
        conv_args = (c_dw_w[l], row(c_dw_b[l]), row(c_ln_g[l]), row(c_ln_b[l]),
                     _block_diag(d_w_group[l]).astype(BF16), row(d_scale[l]))
        oc_c, od_c = _conv_pool(z, SEQ, BATCH, 0, *conv_args)
        oc_l, od_l = _conv_pool(z, DEC_SEQ, DEC_BATCH, N_CTX, *conv_args)

        x1, h2, eid, rank, wts, cnt = _mix(
            x_ctx, x_lat, h, (oa_c, ob_c, oc_c, od_c), (oa_l, ob_l, oc_l, od_l), mods, l,
            w_in_bf, w_branch_bf, w_out_bf,
            row(g_post1[l]), row(g_pre2[l]), r_w[l].T, r_b[l].reshape(N_EXPERTS, 1))
        wts = wts[:TOP_K].T

        pos, tile_expert, tile_rows, tile_slot, tile_next, starts, padded = _routing_tables(
            eid, rank, cnt)
        xs = _dispatch(starts, padded, pos, h2)
        ys = _expert_ffn(tile_expert, tile_rows, tile_slot, tile_next, xs, l,
                         e_w1, e_b1, e_w2, e_b2)
        x_ctx, x_lat = _combine(pos, ys, wts, x1, mods, l, row(g_post2[l]))

    y_prompt = x_ctx.reshape(BATCH, SEQ, D_MODEL)
    y_sample = x_lat.reshape(DEC_BATCH, DEC_SEQ, D_MODEL)
    return (y_prompt, y_sample, jnp.stack(new_ak, axis=1), jnp.stack(new_av, axis=1),
            jnp.stack(new_bk, axis=1), jnp.stack(new_bv, axis=1))
```

```python
import functools
import math

import jax
import jax.numpy as jnp
import numpy as np
from jax import lax
from jax.experimental import pallas as pl
from jax.experimental.pallas import tpu as pltpu

F32 = jnp.float32
BF16 = jnp.bfloat16

D_MODEL = 1024
BATCH = 16
SEQ = 256
DEPTH = 2
DEC_BATCH = 4
DEC_SEQ = 2048
PAST_LEN = 256
GRID_W = 64
ROPE_THETA = 10000.0
N_BRANCH = 4
BRANCH_W = D_MODEL // 4
A_HEADS = 4
A_KV_HEADS = 2
A_HEAD_DIM = 64
B_HEADS = 4
B_HALF_DIM = 32
B_V_DIM = 2 * B_HALF_DIM
C_CONV_WIDTH = 31
D_GROUPS = 4
D_GROUP_W = BRANCH_W // D_GROUPS
POOL_WINDOWS = (2, 4, 8, 16)
N_EXPERTS = 32
TOP_K = 4
D_FF = D_MODEL
SWIGLU_LIMIT = 7.0
SWIGLU_ALPHA = 1.702
EPS = 1e-6

N_CTX = BATCH * SEQ
N_LAT = DEC_BATCH * DEC_SEQ
N_TOK = N_CTX + N_LAT

COL_QA, COL_KA, COL_VA = 0, 256, 384
COL_QB, COL_KB, COL_VB = 512, 768, 1024
COL_ZC, COL_ZD = 1280, 1792
N_SMALL = 2048
N_GATE = N_BRANCH * D_MODEL

TM = 256
N_TILES = N_TOK // TM
CTX_TILES = N_CTX // TM
LAT_TILES_PER_SEQ = DEC_SEQ // TM
MOD_ROWS = 8

TE = 512
N_SLOTS = N_TOK * TOP_K
P_MAX = N_SLOTS + N_EXPERTS * TE
E_TILES = P_MAX // TE

TQ = 512
CONV_PAD = 16
CONV_CHUNK = 128

VMEM_LIMIT = 56 * 1024 * 1024


def _sigmoid(x):
    return 1.0 / (1.0 + jnp.exp(-x))


def _split_bf16(a):
    hi = a.astype(BF16)
    lo = (a - hi.astype(F32)).astype(BF16)
    return hi, lo


def _dot(a, b):
    return jnp.dot(a, b, preferred_element_type=F32)


def _dot_nt(a, b):
    return lax.dot_general(a, b, (((1,), (1,)), ((), ())), preferred_element_type=F32)


def _dot3(a, b):
    ah, al = _split_bf16(a)
    bh, bl = _split_bf16(b)
    return _dot(ah, bh) + _dot(ah, bl) + _dot(al, bh)


def _rms(x, g):
    return x * lax.rsqrt(jnp.mean(x * x, axis=-1, keepdims=True) + EPS) * g


def _mod_row(i, ctx_tiles=CTX_TILES, tiles_per_seq=LAT_TILES_PER_SEQ):
    return jnp.where(i < ctx_tiles, 0, 1 + (i - ctx_tiles) // tiles_per_seq)


LANES = 128
ROW_TILES = D_MODEL // LANES


def _store_token_major(ref, x):
    n = x.shape[0]
    for c in range(ROW_TILES):
        ref[pl.ds(c, n, stride=ROW_TILES), :] = x[:, c * LANES:(c + 1) * LANES]


def _load_token_major(ref):
    n = ref.shape[0] // ROW_TILES
    return jnp.concatenate(
        [ref[pl.ds(c, n, stride=ROW_TILES), :] for c in range(ROW_TILES)], axis=1)


def _ctx_rows(width, tile=TM):
    last = N_CTX // tile - 1
    return pl.BlockSpec((tile, width), lambda i: (jnp.minimum(i, last), 0))


def _lat_rows(width, tile=TM):
    first = N_CTX // tile
    return pl.BlockSpec((tile, width), lambda i: (jnp.maximum(i - first, 0), 0))


MOD_TN = 1536


def _mod_kernel(cond_ref, w_ref, b_ref, o_ref):
    c = cond_ref[...]
    s = c * _sigmoid(c)
    o_ref[0] = _dot3(s, w_ref[0]) + b_ref[0]


def _modulation(cond8, w_mod, b_mod):
    L = w_mod.shape[0]
    return pl.pallas_call(
        _mod_kernel,
        name="modulation",
        grid=(L, 6 * D_MODEL // MOD_TN),
        in_specs=[
            pl.BlockSpec((MOD_ROWS, D_MODEL), lambda l, j: (0, 0)),
            pl.BlockSpec((1, D_MODEL, MOD_TN), lambda l, j: (l, 0, j)),
            pl.BlockSpec((1, 1, MOD_TN), lambda l, j: (l, 0, j)),
        ],
        out_specs=pl.BlockSpec((1, MOD_ROWS, MOD_TN), lambda l, j: (l, 0, j)),
        out_shape=jax.ShapeDtypeStruct((L, MOD_ROWS, 6 * D_MODEL), F32),
        compiler_params=pltpu.CompilerParams(
            dimension_semantics=("arbitrary", "arbitrary"), vmem_limit_bytes=VMEM_LIMIT),
    )(cond8, w_mod, b_mod.reshape(L, 1, 6 * D_MODEL))


def _group_mean(sq, gmat):
    hi, lo = _split_bf16(sq)
    return _dot(hi, gmat) + _dot(lo, gmat)


def _swap_halves(x, half):
    w = x.shape[-1]
    lane = lax.broadcasted_iota(jnp.int32, x.shape, 1)
    first = (lane % (2 * half)) < half
    return jnp.where(first, pltpu.roll(x, w - half, 1), pltpu.roll(x, half, 1))


def _inproj_kernel(xc_ref, xl_ref, mod_ref, gpre_ref, w_ref, gq_ref, gk_ref, gmat_ref,
                   cos_a_ref, sin_a_ref, cos_b_ref, sin_b_ref, z_ref, h_ref):
    i = pl.program_id(0)
    m = mod_ref[0]
    sh1 = m[:, 0:D_MODEL]
    sc1 = m[:, D_MODEL:2 * D_MODEL]
    x = jnp.where(i < CTX_TILES, xc_ref[...], xl_ref[...])
    h = _rms(x, gpre_ref[...]) * (1.0 + sc1) + sh1
    hb = h.astype(BF16)
    h_ref[...] = hb
    z = _dot(hb, w_ref[0])

    gmat = gmat_ref[...]
    qa = z[:, COL_QA:COL_KA]
    ka = z[:, COL_KA:COL_VA]
    qa = qa * lax.rsqrt(_group_mean(qa * qa, gmat) + EPS) * gq_ref[...]
    ka = ka * lax.rsqrt(_group_mean(ka * ka, gmat[0:128, 0:128]) + EPS) * gk_ref[...]
    qb = z[:, COL_QB:COL_KB]
    kb = z[:, COL_KB:COL_VB]
    z_ref[:, COL_VA:COL_QB] = z[:, COL_VA:COL_QB]
    z_ref[:, COL_VB:N_SMALL] = z[:, COL_VB:N_SMALL]

    @pl.when(i < CTX_TILES)
    def _():
        z_ref[:, COL_QA:COL_KA] = qa
        z_ref[:, COL_KA:COL_VA] = ka
        z_ref[:, COL_QB:COL_KB] = qb
        z_ref[:, COL_KB:COL_VB] = kb

    @pl.when(i >= CTX_TILES)
    def _():
        cos_a = cos_a_ref[...]
        sin_a = sin_a_ref[...]
        cos_b = cos_b_ref[...]
        sin_b = sin_b_ref[...]
        ha, hb_ = A_HEAD_DIM // 2, B_HALF_DIM // 2
        z_ref[:, COL_QA:COL_KA] = qa * cos_a + _swap_halves(qa, ha) * sin_a
        z_ref[:, COL_KA:COL_VA] = ka * cos_a[:, 0:128] + _swap_halves(ka, ha) * sin_a[:, 0:128]
        z_ref[:, COL_QB:COL_KB] = qb * cos_b + _swap_halves(qb, hb_) * sin_b
        z_ref[:, COL_KB:COL_VB] = kb * cos_b + _swap_halves(kb, hb_) * sin_b


def _rope_block(i):
    return jnp.where(i < CTX_TILES, 0, (i - CTX_TILES) % LAT_TILES_PER_SEQ)


def _in_projection(x_ctx, x_lat, mods, l, g_pre, w_in_bf, gq, gk, gmat, rope):
    const = lambda i: (0, 0)
    rope_spec = pl.BlockSpec((TM, 256), lambda i: (_rope_block(i), 0))
    return pl.pallas_call(
        _inproj_kernel,
        name="in_projection",
        grid=(N_TILES,),
        in_specs=[
            _ctx_rows(D_MODEL), _lat_rows(D_MODEL),
            pl.BlockSpec((1, 1, 6 * D_MODEL), lambda i: (l * MOD_ROWS + _mod_row(i), 0, 0)),
            pl.BlockSpec((1, D_MODEL), const),
            pl.BlockSpec((1, D_MODEL, N_SMALL), lambda i: (l, 0, 0)),
            pl.BlockSpec((1, 256), const),
            pl.BlockSpec((1, 128), const),
            pl.BlockSpec((256, 256), const),
            rope_spec, rope_spec, rope_spec, rope_spec,
        ],
        out_specs=[
            pl.BlockSpec((TM, N_SMALL), lambda i: (i, 0)),
            pl.BlockSpec((TM, D_MODEL), lambda i: (i, 0)),
        ],
        out_shape=[
            jax.ShapeDtypeStruct((N_TOK, N_SMALL), F32),
            jax.ShapeDtypeStruct((N_TOK, D_MODEL), BF16),
        ],
        compiler_params=pltpu.CompilerParams(
            dimension_semantics=("arbitrary",), vmem_limit_bytes=VMEM_LIMIT),
    )(x_ctx, x_lat, mods, g_pre, w_in_bf, gq, gk, gmat, *rope)


LOG2E = 1.4426950408889634
HALF_LANES = 64


def _attend(q, ks, vexts):
    ss = [_dot_nt(q, k) for k in ks]
    m = ss[0].max(axis=-1, keepdims=True)
    for s in ss[1:]:
        m = jnp.maximum(m, s.max(axis=-1, keepdims=True))
    o = None
    for s, v in zip(ss, vexts):
        t = _dot(jnp.exp2(s - m).astype(BF16), v)
        o = t if o is None else o + t
    return o


def _value_ext(v128, upper):
    lane = lax.broadcasted_iota(jnp.int32, v128.shape, 1)
    keep = (lane >= HALF_LANES) if upper else (lane < HALF_LANES)
    return jnp.where(keep, v128, 1.0).astype(BF16)


def _normalised(o, upper):
    if upper:
        return o[:, HALF_LANES:] / o[:, 0:1]
    return o[:, :HALF_LANES] / o[:, HALF_LANES:HALF_LANES + 1]


def _attn_kernel(*refs, has_cache, lambda_init, n_seq=1):
    if n_seq > 1:
        rows = refs[0].shape[0] // n_seq
        for s in range(n_seq):
            sub = [r.at[pl.ds(s * rows, rows), :] for r in refs[:6]]
            outs = [r.at[pl.ds(s * rows, rows), :] for r in refs[-2:]]
            _attn_kernel(*sub, *refs[6:-2], *outs, has_cache=has_cache, lambda_init=lambda_init)
        return
    if has_cache:
        (qa_ref, ka_ref, va_ref, qb_ref, kb_ref, vb_ref,
         cka_ref, cva_ref, ckb_ref, cvb_ref, lam_ref, subg_ref, oa_ref, ob_ref) = refs
    else:
        (qa_ref, ka_ref, va_ref, qb_ref, kb_ref, vb_ref,
         lam_ref, subg_ref, oa_ref, ob_ref) = refs

    def cols(ref, lo, hi):
        return ref[:, lo:hi].astype(BF16)

    def ccols(ref, lo, hi):
        return ref[0, 0, :, lo:hi].astype(BF16)

    scale_a = A_HEAD_DIM ** -0.5 * LOG2E
    group = A_HEADS // A_KV_HEADS
    for g in range(A_KV_HEADS):
        lo, hi = g * A_HEAD_DIM, (g + 1) * A_HEAD_DIM
        upper = g % 2 == 1
        ks = [cols(ka_ref, lo, hi)]
        vexts = [_value_ext(va_ref[...], upper)]
        if has_cache:
            ks.append(ccols(cka_ref, lo, hi))
            vexts.append(_value_ext(cva_ref[0, 0], upper))
        for hd in range(g * group, (g + 1) * group):
            q = (qa_ref[:, hd * A_HEAD_DIM:(hd + 1) * A_HEAD_DIM] * scale_a).astype(BF16)
            o = _attend(q, ks, vexts)
            oa_ref[:, hd * A_HEAD_DIM:(hd + 1) * A_HEAD_DIM] = _normalised(o, upper)

    bl = lam_ref[...]
    lam = (jnp.exp(jnp.sum(bl[0:1] * bl[1:2], axis=-1, keepdims=True))
           - jnp.exp(jnp.sum(bl[2:3] * bl[3:4], axis=-1, keepdims=True)) + lambda_init)
    scale_b = B_HALF_DIM ** -0.5 * LOG2E
    subg = subg_ref[...]
    for hd in range(B_HEADS):
        base = hd * 2 * B_HALF_DIM
        vlo, vhi = hd * B_V_DIM, (hd + 1) * B_V_DIM
        blk = (hd // 2) * 2 * B_V_DIM
        upper = hd % 2 == 1
        vexts = [_value_ext(vb_ref[:, blk:blk + 2 * B_V_DIM], upper)]
        if has_cache:
            vexts.append(_value_ext(cvb_ref[0, 0, :, blk:blk + 2 * B_V_DIM], upper))
        outs = []
        for c in range(2):
            lo, hi = base + c * B_HALF_DIM, base + (c + 1) * B_HALF_DIM
            q = (qb_ref[:, lo:hi] * scale_b).astype(BF16)
            ks = [cols(kb_ref, lo, hi)]
            if has_cache:
                ks.append(ccols(ckb_ref, lo, hi))
            outs.append(_normalised(_attend(q, ks, vexts), upper))
        o = outs[0] - lam * outs[1]
        o = _rms(o, subg) * (1.0 - lambda_init)
        ob_ref[:, vlo:vhi] = o


CTX_SEQ_PER_STEP = 1


def _attention_ctx(z, b_lambda, subg, lambda_init):
    rows = CTX_SEQ_PER_STEP * SEQ

    def zspec(width, col):
        return pl.BlockSpec((rows, width), lambda b: (b, col // width))
    out_spec = pl.BlockSpec((rows, BRANCH_W), lambda b: (b, 0))
    return pl.pallas_call(
        functools.partial(_attn_kernel, has_cache=False, lambda_init=lambda_init,
                          n_seq=CTX_SEQ_PER_STEP),
        name="attention_ctx",
        grid=(BATCH // CTX_SEQ_PER_STEP,),
        in_specs=[
            zspec(256, COL_QA), zspec(128, COL_KA), zspec(128, COL_VA),
            zspec(256, COL_QB), zspec(256, COL_KB), zspec(256, COL_VB),
            pl.BlockSpec((4, B_HALF_DIM), lambda b: (0, 0)),
            pl.BlockSpec((1, B_V_DIM), lambda b: (0, 0)),
        ],
        out_specs=[out_spec, out_spec],
        out_shape=[jax.ShapeDtypeStruct((N_CTX, BRANCH_W), F32)] * 2,
        compiler_params=pltpu.CompilerParams(
            dimension_semantics=("arbitrary",), vmem_limit_bytes=VMEM_LIMIT),
    )(z, z, z, z, z, z, b_lambda, subg)


def _attention_lat(z, caches, l, b_lambda, subg, lambda_init):
    q_tiles = DEC_SEQ // TQ
    seq0 = N_CTX // DEC_SEQ

    def qspec(col):
        return pl.BlockSpec((TQ, 256), lambda b, t: (N_CTX // TQ + b * q_tiles + t, col // 256))

    def kspec(width, col):
        return pl.BlockSpec((DEC_SEQ, width), lambda b, t: (seq0 + b, col // width))

    def cspec(width):
        return pl.BlockSpec((1, 1, PAST_LEN, width), lambda b, t: (b, l, 0, 0))

    out_spec = pl.BlockSpec((TQ, BRANCH_W), lambda b, t: (b * q_tiles + t, 0))
    cka, cva, ckb, cvb = caches
    return pl.pallas_call(
        functools.partial(_attn_kernel, has_cache=True, lambda_init=lambda_init),
        name="attention_lat",
        grid=(DEC_BATCH, q_tiles),
        in_specs=[
            qspec(COL_QA), kspec(128, COL_KA), kspec(128, COL_VA),
            qspec(COL_QB), kspec(256, COL_KB), kspec(256, COL_VB),
            cspec(128), cspec(128), cspec(256), cspec(256),
            pl.BlockSpec((4, B_HALF_DIM), lambda b, t: (0, 0)),
            pl.BlockSpec((1, B_V_DIM), lambda b, t: (0, 0)),
        ],
        out_specs=[out_spec, out_spec],
        out_shape=[jax.ShapeDtypeStruct((N_LAT, BRANCH_W), F32)] * 2,
        compiler_params=pltpu.CompilerParams(
            dimension_semantics=("arbitrary", "arbitrary"), vmem_limit_bytes=VMEM_LIMIT),
    )(z, z, z, z, z, z, cka, cva, ckb, cvb, b_lambda, subg)


SUBLANES = 8


def _row_shifter(win):
    span = win.shape[0] - SUBLANES
    shifted = {}

    def rows(off):
        s = off % SUBLANES
        if s not in shifted:
            shifted[s] = win[s:s + span, :]
        base = off - s
        return shifted[s][base:base + CONV_CHUNK, :]

    return rows


def _convpool_kernel(a_ref, b_ref, d_ref, dww_ref, dwb_ref, lng_ref, lnb_ref, wbd_ref, dsc_ref,
                     oc_ref, od_ref, hpad, upad, *, seq_len):
    n_chunks = seq_len // CONV_CHUNK
    zeros = jnp.zeros((CONV_PAD, BRANCH_W), F32)
    hpad[0:CONV_PAD, :] = zeros
    hpad[seq_len + CONV_PAD:seq_len + 2 * CONV_PAD, :] = zeros
    upad[0:CONV_PAD, :] = zeros
    upad[seq_len + CONV_PAD:seq_len + 2 * CONV_PAD, :] = zeros

    def fill(c, carry):
        r = pl.multiple_of(c * CONV_CHUNK, CONV_CHUNK)
        a = a_ref[pl.ds(r, CONV_CHUNK), :]
        b = b_ref[pl.ds(r, CONV_CHUNK), :]
        hpad[pl.ds(r + CONV_PAD, CONV_CHUNK), :] = a * _sigmoid(b)
        upad[pl.ds(r + CONV_PAD, CONV_CHUNK), :] = d_ref[pl.ds(r, CONV_CHUNK), :]
        return carry

    lax.fori_loop(0, n_chunks, fill, 0)

    lane = lax.broadcasted_iota(jnp.int32, (CONV_CHUNK, BRANCH_W), 1)
    half = C_CONV_WIDTH // 2

    def body(c, carry):
        r = pl.multiple_of(c * CONV_CHUNK, CONV_CHUNK)
        hrows = _row_shifter(hpad[pl.ds(r, CONV_CHUNK + 2 * CONV_PAD), :])
        acc = jnp.zeros((CONV_CHUNK, BRANCH_W), F32)
        for k in range(C_CONV_WIDTH):
            acc = acc + dww_ref[k:k + 1, :] * hrows(CONV_PAD - half + k)
        acc = acc + dwb_ref[...]
        mu = jnp.mean(acc, axis=-1, keepdims=True)
        cen = acc - mu
        var = jnp.mean(cen * cen, axis=-1, keepdims=True)
        y = cen * lax.rsqrt(var + EPS) * lng_ref[...] + lnb_ref[...]
        oc_ref[pl.ds(r, CONV_CHUNK), :] = y * _sigmoid(y)

        urows = _row_shifter(upad[pl.ds(r, CONV_CHUNK + 2 * CONV_PAD), :])

        def ld(d):
            return urows(CONV_PAD + d)

        u = ld(0)
        sums = {}
        s = u + ld(-1)
        sums[2] = s
        s = s + ld(-2) + ld(1)
        sums[4] = s
        s = s + ld(-4) + ld(-3) + ld(2) + ld(3)
        sums[8] = s
        s = s + ld(-8) + ld(-7) + ld(-6) + ld(-5) + ld(4) + ld(5) + ld(6) + ld(7)
        sums[16] = s
        t = r + lax.broadcasted_iota(jnp.int32, (CONV_CHUNK, 1), 0)
        pooled = None
        for g, w in reversed(list(enumerate(POOL_WINDOWS))):
            lo = jnp.maximum(t - w // 2, 0)
            hi = jnp.minimum(t + (w - 1 - w // 2), seq_len - 1)
            mean = sums[w] / (hi - lo + 1).astype(F32)
            pooled = mean if pooled is None else jnp.where(lane < (g + 1) * D_GROUP_W, mean, pooled)
        dlt = (pooled - u).astype(BF16)
        od_ref[pl.ds(r, CONV_CHUNK), :] = _dot(dlt, wbd_ref[...]) * dsc_ref[...]
        return carry

    lax.fori_loop(0, n_chunks, body, 0)


def _conv_pool(z, seq_len, n_seq, row0, dww, dwb, lng, lnb, wbd, dsc):
    blk0 = row0 // seq_len

    def zspec(col):
        return pl.BlockSpec((seq_len, 256), lambda b: (blk0 + b, col // 256))

    def const(shape):
        return pl.BlockSpec(shape, lambda b: (0, 0))

    out_spec = pl.BlockSpec((seq_len, BRANCH_W), lambda b: (b, 0))
    return pl.pallas_call(
        functools.partial(_convpool_kernel, seq_len=seq_len),
        name=f"conv_pool_{seq_len}",
        grid=(n_seq,),
        in_specs=[
            zspec(COL_ZC), zspec(COL_ZC + BRANCH_W), zspec(COL_ZD),
            const((C_CONV_WIDTH, BRANCH_W)), const((1, BRANCH_W)), const((1, BRANCH_W)),
            const((1, BRANCH_W)), const((BRANCH_W, BRANCH_W)), const((1, BRANCH_W)),
        ],
        out_specs=[out_spec, out_spec],
        out_shape=[jax.ShapeDtypeStruct((n_seq * seq_len, BRANCH_W), F32)] * 2,
        scratch_shapes=[pltpu.VMEM((seq_len + 2 * CONV_PAD, BRANCH_W), F32)] * 2,
        compiler_params=pltpu.CompilerParams(
            dimension_semantics=("arbitrary",), vmem_limit_bytes=VMEM_LIMIT),
    )(z, z, z, dww, dwb, lng, lnb, wbd, dsc)


MIX_TM = 512
MIX_TILES = N_TOK // MIX_TM
MIX_CTX_TILES = N_CTX // MIX_TM
MIX_LAT_PER_SEQ = DEC_SEQ // MIX_TM
ROUTE_ROWS = 8


def _mix_kernel(xc_ref, xl_ref, h_ref, oa_c, ob_c, oc_c, od_c, oa_l, ob_l, oc_l, od_l,
                mod_ref, wg01_ref, wg23_ref, wbr_ref, wout_ref, gpost_ref, gpre2_ref, rwt_ref, rb_ref,
                x1_ref, h2_ref, eid_ref, rank_ref, wts_ref, cnt_ref, carry):
    i = pl.program_id(0)
    is_ctx = i < MIX_CTX_TILES

    @pl.when(i == 0)
    def _():
        carry[...] = jnp.zeros_like(carry)

    m = mod_ref[0]
    g1 = m[:, 2 * D_MODEL:3 * D_MODEL]
    sh2 = m[:, 3 * D_MODEL:4 * D_MODEL]
    sc2 = m[:, 4 * D_MODEL:5 * D_MODEL]

    hb = h_ref[...]
    merged = None
    for n, (c_ref, l_ref) in enumerate(((oa_c, oa_l), (ob_c, ob_l), (oc_c, oc_l), (od_c, od_l))):
        wg_ref = wg01_ref if n < 2 else wg23_ref
        gate = _sigmoid(_dot(hb, wg_ref[0, :, (n % 2) * D_MODEL:(n % 2 + 1) * D_MODEL]))
        o = jnp.where(is_ctx, c_ref[...], l_ref[...])
        br = _dot(o.astype(BF16), wbr_ref[0, n])
        merged = gate * br if merged is None else merged + gate * br
    y = _dot(merged.astype(BF16), wout_ref[0])
    x = jnp.where(is_ctx, xc_ref[...], xl_ref[...])
    x1 = x + g1 * _rms(y, gpost_ref[...])
    x1_ref[...] = x1
    h2 = _rms(x1, gpre2_ref[...]) * (1.0 + sc2) + sh2
    _store_token_major(h2_ref, h2)

    rh, rl = _split_bf16(rwt_ref[...])
    hh, hl = _split_bf16(h2)
    logits = _dot_nt(rh, hh) + _dot_nt(rh, hl) + _dot_nt(rl, hh) + rb_ref[...]

    iota_e = lax.broadcasted_iota(jnp.int32, (N_EXPERTS, MIX_TM), 0)
    rem = logits
    vals, idxs = [], []
    for _ in range(TOP_K):
        mx = jnp.max(rem, axis=0, keepdims=True)
        idx = jnp.min(jnp.where(rem == mx, iota_e, N_EXPERTS), axis=0, keepdims=True)
        vals.append(mx)
        idxs.append(idx)
        rem = jnp.where(iota_e == idx, -jnp.inf, rem)
    exps = [jnp.exp(v - vals[0]) for v in vals]
    den = exps[0]
    for e in exps[1:]:
        den = den + e

    sel = [iota_e == idx for idx in idxs]
    member = jnp.zeros((N_EXPERTS, MIX_TM), F32)
    for s in sel:
        member = member + jnp.where(s, 1.0, 0.0)
    row = lax.broadcasted_iota(jnp.int32, (MIX_TM, MIX_TM), 0)
    col = lax.broadcasted_iota(jnp.int32, (MIX_TM, MIX_TM), 1)
    earlier = jnp.where(row < col, 1.0, 0.0).astype(BF16)
    seen = _dot(member.astype(BF16), earlier) + carry[...]
    carry[...] = carry[...] + jnp.sum(member, axis=1, keepdims=True)
    cnt_ref[...] = carry[...]

    sub = lax.broadcasted_iota(jnp.int32, (ROUTE_ROWS, MIX_TM), 0)
    eid = jnp.zeros((ROUTE_ROWS, MIX_TM), jnp.int32)
    rank = jnp.zeros((ROUTE_ROWS, MIX_TM), jnp.int32)
    wts = jnp.zeros((ROUTE_ROWS, MIX_TM), F32)
    for k in range(TOP_K):
        rk = jnp.sum(jnp.where(sel[k], seen, 0.0), axis=0, keepdims=True).astype(jnp.int32)
        eid = jnp.where(sub == k, idxs[k], eid)
        rank = jnp.where(sub == k, rk, rank)
        wts = jnp.where(sub == k, exps[k] / den, wts)
    eid_ref[...] = eid
    rank_ref[...] = rank
    wts_ref[...] = wts


def _mix(x_ctx, x_lat, h, o_ctx, o_lat, mods, l, w_in_bf, w_branch_bf, w_out_bf,
         g_post, g_pre2, r_wt, r_b):
    const2 = lambda i: (0, 0)
    tile = lambda width: pl.BlockSpec((MIX_TM, width), lambda i: (i, 0))
    route = pl.BlockSpec((ROUTE_ROWS, MIX_TM), lambda i: (0, i))
    ctx_tile = _ctx_rows(BRANCH_W, MIX_TM)
    lat_tile = _lat_rows(BRANCH_W, MIX_TM)
    mod_row = lambda i: _mod_row(i, MIX_CTX_TILES, MIX_LAT_PER_SEQ)
    single = pl.Buffered(1)
    return pl.pallas_call(
        _mix_kernel,
        name="mix_router",
        grid=(MIX_TILES,),
        in_specs=[
            _ctx_rows(D_MODEL, MIX_TM), _lat_rows(D_MODEL, MIX_TM), tile(D_MODEL),
            ctx_tile, ctx_tile, ctx_tile, ctx_tile, lat_tile, lat_tile, lat_tile, lat_tile,
            pl.BlockSpec((1, 1, 6 * D_MODEL), lambda i: (l * MOD_ROWS + mod_row(i), 0, 0)),
            pl.BlockSpec((1, D_MODEL, N_SMALL), lambda i: (l, 0, 1), pipeline_mode=single),
            pl.BlockSpec((1, D_MODEL, N_SMALL), lambda i: (l, 0, 2), pipeline_mode=single),
            pl.BlockSpec((1, N_BRANCH, BRANCH_W, D_MODEL), lambda i: (l, 0, 0, 0),
                         pipeline_mode=single),
            pl.BlockSpec((1, D_MODEL, D_MODEL), lambda i: (l, 0, 0), pipeline_mode=single),
            pl.BlockSpec((1, D_MODEL), const2),
            pl.BlockSpec((1, D_MODEL), const2),
            pl.BlockSpec((N_EXPERTS, D_MODEL), const2),
            pl.BlockSpec((N_EXPERTS, 1), const2),
        ],
        out_specs=[
            tile(D_MODEL), pl.BlockSpec((MIX_TM * ROW_TILES, LANES), lambda i: (i, 0)),
            route, route, route,
            pl.BlockSpec((N_EXPERTS, 1), const2),
        ],
        out_shape=[
            jax.ShapeDtypeStruct((N_TOK, D_MODEL), F32),
            jax.ShapeDtypeStruct((N_TOK * ROW_TILES, LANES), F32),
            jax.ShapeDtypeStruct((ROUTE_ROWS, N_TOK), jnp.int32),
            jax.ShapeDtypeStruct((ROUTE_ROWS, N_TOK), jnp.int32),
            jax.ShapeDtypeStruct((ROUTE_ROWS, N_TOK), F32),
            jax.ShapeDtypeStruct((N_EXPERTS, 1), F32),
        ],
        scratch_shapes=[pltpu.VMEM((N_EXPERTS, 1), F32)],
        compiler_params=pltpu.CompilerParams(
            dimension_semantics=("arbitrary",), vmem_limit_bytes=VMEM_LIMIT),
    )(x_ctx, x_lat, h, *o_ctx, *o_lat, mods, w_in_bf, w_in_bf, w_branch_bf, w_out_bf,
      g_post, g_pre2, r_wt, r_b)


def _token_rows(t, n=1):
    return pl.ds(pl.multiple_of(t * ROW_TILES, ROW_TILES), n * ROW_TILES)


def _token_copy(src_ref, src_tok, dst_ref, dst_tok, sem):
    return pltpu.make_async_copy(src_ref.at[_token_rows(src_tok), :], dst_ref.at[_token_rows(dst_tok), :], sem)


DISPATCH_ROWS = 512
DISPATCH_STEPS = N_TOK // DISPATCH_ROWS
DMA_UNROLL = 8


def _dispatch_kernel(starts_ref, padded_ref, pos_ref, h2_ref, xs_ref, zbuf, sem, zsem, tsem):
    i = pl.program_id(0)

    def pad_copy(e):
        last = pl.multiple_of(starts_ref[e] + padded_ref[e] - TE, TE)
        return pltpu.make_async_copy(zbuf, xs_ref.at[_token_rows(last, TE), :], zsem)

    first_tail = (starts_ref[N_EXPERTS - 1] + padded_ref[N_EXPERTS - 1]) // TE

    def tail_copy(j):
        return pltpu.make_async_copy(zbuf, xs_ref.at[_token_rows(j * TE, TE), :], tsem)

    def tail_start(j, carry):
        tail_copy(j).start()
        return carry

    def tail_wait(j, carry):
        tail_copy(j).wait()
        return carry

    @pl.when(i == 0)
    def _():
        zbuf[...] = jnp.zeros_like(zbuf)
        for e in range(N_EXPERTS):
            @pl.when(padded_ref[e] > 0)
            def _():
                pad_copy(e).start()
        lax.fori_loop(first_tail, E_TILES, tail_start, 0)
        for e in range(N_EXPERTS):
            @pl.when(padded_ref[e] > 0)
            def _():
                pad_copy(e).wait()

    @pl.when(i == DISPATCH_STEPS - 1)
    def _():
        lax.fori_loop(first_tail, E_TILES, tail_wait, 0)

    def issue(t, carry):
        for k in range(TOP_K):
            _token_copy(h2_ref, t, xs_ref, pos_ref[t * TOP_K + k], sem).start(priority=k % 2)
        return carry

    lax.fori_loop(0, DISPATCH_ROWS, issue, 0, unroll=DMA_UNROLL)

    for _ in range(TOP_K):
        pltpu.make_async_copy(h2_ref, xs_ref.at[_token_rows(0, DISPATCH_ROWS), :], sem).wait()


def _dispatch(starts, padded, pos, h2):
    grid_spec = pltpu.PrefetchScalarGridSpec(
        num_scalar_prefetch=2,
        grid=(DISPATCH_STEPS,),
        in_specs=[
            pl.BlockSpec((DISPATCH_ROWS * TOP_K,), lambda i, st, pd: (i,), memory_space=pltpu.SMEM),
            pl.BlockSpec((DISPATCH_ROWS * ROW_TILES, LANES), lambda i, st, pd: (i, 0)),
        ],
        out_specs=pl.BlockSpec(memory_space=pl.ANY),
        scratch_shapes=[pltpu.VMEM((TE * ROW_TILES, LANES), F32),
                        pltpu.SemaphoreType.DMA(()), pltpu.SemaphoreType.DMA(()),
                        pltpu.SemaphoreType.DMA(())],
    )
    return pl.pallas_call(
        _dispatch_kernel,
        name="dispatch",
        grid_spec=grid_spec,
        out_shape=jax.ShapeDtypeStruct((P_MAX * ROW_TILES, LANES), F32),
        compiler_params=pltpu.CompilerParams(
            dimension_semantics=("arbitrary",), vmem_limit_bytes=VMEM_LIMIT),
    )(starts, padded, pos, h2)


W_CAST_ROWS = 128


def _moe_kernel(te_ref, rows_ref, slot_ref, next_ref, x_ref, w1_hbm, b1_ref, w2_hbm, b2_ref, y_ref,
                w1f, w2f, w1b, w2b, sem, *, layer):
    j = pl.program_id(0)
    e = te_ref[j]
    prev = te_ref[jnp.maximum(j - 1, 0)]
    valid = rows_ref[j] > 0
    first = jnp.logical_and(valid, jnp.logical_or(j == 0, e != prev))
    s = slot_ref[j]

    def weight_copies(ex, sl):
        return (pltpu.make_async_copy(w1_hbm.at[layer, ex], w1f.at[sl], sem.at[0, sl]),
                pltpu.make_async_copy(w2_hbm.at[layer, ex], w2f.at[sl], sem.at[1, sl]))

    @pl.when(j == 0)
    def _():
        for cp in weight_copies(e, s):
            cp.start()

    @pl.when(first)
    def _():
        for cp in weight_copies(e, s):
            cp.wait()
        nxt = next_ref[j]

        @pl.when(nxt >= 0)
        def _():
            for cp in weight_copies(nxt, 1 - s):
                cp.start()

        def cast(c, carry):
            r = pl.multiple_of(c * W_CAST_ROWS, W_CAST_ROWS)
            w1b[pl.ds(r, W_CAST_ROWS), :] = w1f[s, pl.ds(r, W_CAST_ROWS), :].astype(BF16)
            w2b[pl.ds(r, W_CAST_ROWS), :] = w2f[s, pl.ds(r, W_CAST_ROWS), :].astype(BF16)
            return carry
        lax.fori_loop(0, D_MODEL // W_CAST_ROWS, cast, 0)

    @pl.when(valid)
    def _():
        x = _load_token_major(x_ref).astype(BF16)
        u = _dot(x, w1b[...]) + b1_ref[0, 0]
        xg = jnp.minimum(u[:, :D_FF], SWIGLU_LIMIT)
        xl = jnp.clip(u[:, D_FF:], -SWIGLU_LIMIT, SWIGLU_LIMIT)
        act = (xl + 1.0) * (xg * _sigmoid(SWIGLU_ALPHA * xg))
        _store_token_major(y_ref, _dot(act.astype(BF16), w2b[...]) + b2_ref[0, 0])

    @pl.when(jnp.logical_not(valid))
    def _():
        y_ref[...] = jnp.zeros_like(y_ref)


def _expert_ffn(tile_expert, tile_rows, tile_slot, tile_next, xs, l, e_w1, e_b1, e_w2, e_b2):
    L = e_w1.shape[0]
    grid_spec = pltpu.PrefetchScalarGridSpec(
        num_scalar_prefetch=4,
        grid=(E_TILES,),
        in_specs=[
            pl.BlockSpec((TE * ROW_TILES, LANES),
                         lambda j, te, nr, sl, nx: (jnp.where(nr[j] > 0, j, 0), 0)),
            pl.BlockSpec(memory_space=pl.ANY),
            pl.BlockSpec((1, 1, 1, 2 * D_FF), lambda j, te, nr, sl, nx: (l, te[j], 0, 0)),
            pl.BlockSpec(memory_space=pl.ANY),
            pl.BlockSpec((1, 1, 1, D_MODEL), lambda j, te, nr, sl, nx: (l, te[j], 0, 0)),
        ],
        out_specs=pl.BlockSpec((TE * ROW_TILES, LANES), lambda j, te, nr, sl, nx: (j, 0)),
        scratch_shapes=[pltpu.VMEM((2, D_MODEL, 2 * D_FF), F32), pltpu.VMEM((2, D_FF, D_MODEL), F32),
                        pltpu.VMEM((D_MODEL, 2 * D_FF), BF16), pltpu.VMEM((D_FF, D_MODEL), BF16),
                        pltpu.SemaphoreType.DMA((2, 2))],
    )
    return pl.pallas_call(
        functools.partial(_moe_kernel, layer=l),
        name="expert_ffn",
        grid_spec=grid_spec,
        out_shape=jax.ShapeDtypeStruct((P_MAX * ROW_TILES, LANES), F32),
        compiler_params=pltpu.CompilerParams(
            dimension_semantics=("arbitrary",), vmem_limit_bytes=VMEM_LIMIT),
    )(tile_expert, tile_rows, tile_slot, tile_next, xs, e_w1,
      e_b1.reshape(L, N_EXPERTS, 1, 2 * D_FF), e_w2, e_b2.reshape(L, N_EXPERTS, 1, D_MODEL))


def _combine_kernel(pos_ref, pos_next_ref, ys_ref, wts_ref, x1_ref, mod_ref, gpost_ref,
                    out_c_ref, out_l_ref, buf, sem):
    i = pl.program_id(0)
    slot = i % 2

    def gather(p_ref, s):
        def issue(t, carry):
            for k in range(TOP_K):
                _token_copy(ys_ref, p_ref[t * TOP_K + k], buf.at[s, k], t, sem.at[s]).start(priority=k % 2)
            return carry
        lax.fori_loop(0, TM, issue, 0, unroll=DMA_UNROLL)

    @pl.when(i == 0)
    def _():
        gather(pos_ref, 0)

    @pl.when(i + 1 < N_TILES)
    def _():
        gather(pos_next_ref, 1 - slot)

    for k in range(TOP_K):
        pltpu.make_async_copy(ys_ref.at[_token_rows(0, TM), :], buf.at[slot, k], sem.at[slot]).wait()

    w = wts_ref[...]
    y = w[:, 0:1] * _load_token_major(buf.at[slot, 0])
    for k in range(1, TOP_K):
        y = y + w[:, k:k + 1] * _load_token_major(buf.at[slot, k])
    g2 = mod_ref[0][:, 5 * D_MODEL:6 * D_MODEL]
    res = x1_ref[...] + g2 * _rms(y, gpost_ref[...])

    @pl.when(i < CTX_TILES)
    def _():
        out_c_ref[...] = res

    @pl.when(i >= CTX_TILES)
    def _():
        out_l_ref[...] = res


def _combine(pos, ys, wts, x1, mods, l, g_post2):
    return pl.pallas_call(
        _combine_kernel,
        name="combine",
        grid=(N_TILES,),
        in_specs=[
            pl.BlockSpec((TM * TOP_K,), lambda i: (i,), memory_space=pltpu.SMEM),
            pl.BlockSpec((TM * TOP_K,), lambda i: (jnp.minimum(i + 1, N_TILES - 1),),
                         memory_space=pltpu.SMEM),
            pl.BlockSpec(memory_space=pl.ANY),
            pl.BlockSpec((TM, TOP_K), lambda i: (i, 0)),
            pl.BlockSpec((TM, D_MODEL), lambda i: (i, 0)),
            pl.BlockSpec((1, 1, 6 * D_MODEL), lambda i: (l * MOD_ROWS + _mod_row(i), 0, 0)),
            pl.BlockSpec((1, D_MODEL), lambda i: (0, 0)),
        ],
        out_specs=[_ctx_rows(D_MODEL), _lat_rows(D_MODEL)],
        out_shape=[jax.ShapeDtypeStruct((N_CTX, D_MODEL), F32),
                   jax.ShapeDtypeStruct((N_LAT, D_MODEL), F32)],
        scratch_shapes=[pltpu.VMEM((2, TOP_K, TM * ROW_TILES, LANES), F32),
                        pltpu.SemaphoreType.DMA((2,))],
        compiler_params=pltpu.CompilerParams(
            dimension_semantics=("arbitrary",), vmem_limit_bytes=VMEM_LIMIT),
    )(pos, pos, ys, wts, x1, mods, g_post2)


def _rope_tables():
    t = np.arange(DEC_SEQ)
    r = (t // GRID_W).astype(np.float32)
    c = (t % GRID_W).astype(np.float32)

    def table(dim, reps):
        n_axis = dim // 4
        inv = (ROPE_THETA ** (-np.arange(n_axis, dtype=np.float32) / n_axis)).astype(np.float32)
        ang = np.concatenate([r[:, None] * inv, c[:, None] * inv], axis=-1).astype(np.float32)
        cos, sin = np.cos(ang), np.sin(ang)
        return (np.tile(np.concatenate([cos, cos], -1), (1, reps)),
                np.tile(np.concatenate([-sin, sin], -1), (1, reps)))

    cos_a, sin_a = table(A_HEAD_DIM, 256 // A_HEAD_DIM)
    cos_b, sin_b = table(B_HALF_DIM, 256 // B_HALF_DIM)
    return tuple(jnp.asarray(a, F32) for a in (cos_a, sin_a, cos_b, sin_b))


def _group_mean_matrix():
    idx = np.arange(256) // A_HEAD_DIM
    return jnp.asarray((idx[:, None] == idx[None, :]).astype(np.float32) / A_HEAD_DIM, BF16)


def _block_diag(w):
    out = jnp.zeros((BRANCH_W, BRANCH_W), w.dtype)
    for g in range(D_GROUPS):
        out = out.at[g * D_GROUP_W:(g + 1) * D_GROUP_W, g * D_GROUP_W:(g + 1) * D_GROUP_W].set(w[g])
    return out


def _routing_tables(eid, rank, cnt):
    cnt = cnt.reshape(N_EXPERTS).astype(jnp.int32)
    padded = ((cnt + TE - 1) // TE) * TE
    ends = jnp.cumsum(padded)
    starts = ends - padded
    experts = jnp.arange(N_EXPERTS, dtype=jnp.int32)
    start_of = jnp.sum(jnp.where(eid[:TOP_K, :, None] == experts, starts, 0), axis=-1)
    pos = (start_of + rank[:TOP_K]).T.reshape(N_SLOTS)
    tile_row = jnp.arange(E_TILES, dtype=jnp.int32) * TE
    tile_expert = jnp.sum((tile_row[:, None] >= ends[None, :]).astype(jnp.int32), axis=-1)
    tile_expert = jnp.minimum(tile_expert, N_EXPERTS - 1)
    used_end = jnp.sum(jnp.where(tile_expert[:, None] == experts, starts + cnt, 0), axis=-1)
    tile_rows = jnp.where(tile_row < ends[-1], jnp.clip(used_end - tile_row, 0, TE), 0)
    valid = tile_rows > 0
    changed = jnp.concatenate([jnp.ones((1,), bool), tile_expert[1:] != tile_expert[:-1]])
    tile_slot = (jnp.cumsum(jnp.logical_and(valid, changed).astype(jnp.int32)) + 1) % 2
    group_end = jnp.sum(jnp.where(tile_expert[:, None] == experts, ends, 0), axis=-1) // TE
    tiles = jnp.arange(E_TILES, dtype=jnp.int32)
    expert_at = jnp.where(valid, tile_expert, -1)
    tile_next = jnp.sum(jnp.where(group_end[:, None] == tiles, expert_at + 1, 0), axis=-1) - 1
    as_i32 = lambda a: a.astype(jnp.int32)
    return (as_i32(pos), as_i32(tile_expert), as_i32(tile_rows), as_i32(tile_slot), as_i32(tile_next),
            as_i32(starts), as_i32(padded))


def kernel(x_prompt, x_sample, cache_a_k, cache_a_v, cache_b_k, cache_b_v, c, c_ctx, w_mod, b_mod, g_pre1, g_post1, g_pre2, g_post2, w_in, a_q_gain, a_k_gain, b_lambda, b_subln_gain, c_dw_w, c_dw_b, c_ln_g, c_ln_b, d_w_group, d_scale, w_branch, w_out, r_w, r_b, e_w1, e_b1, e_w2, e_b2):
    x_ctx = x_prompt.reshape(N_CTX, D_MODEL)
    x_lat = x_sample.reshape(N_LAT, D_MODEL)
    cond8 = jnp.concatenate(
        [c_ctx[None, :], c, jnp.zeros((MOD_ROWS - 1 - DEC_BATCH, D_MODEL), F32)], axis=0)
    mods = _modulation(cond8, w_mod, b_mod).reshape(DEPTH * MOD_ROWS, 1, 6 * D_MODEL)

    rope = _rope_tables()
    gmat = _group_mean_matrix()
    caches = (
        cache_a_k.reshape(DEC_BATCH, DEPTH, PAST_LEN, A_KV_HEADS * A_HEAD_DIM),
        cache_a_v.reshape(DEC_BATCH, DEPTH, PAST_LEN, A_KV_HEADS * A_HEAD_DIM),
        cache_b_k.reshape(DEC_BATCH, DEPTH, PAST_LEN, B_HEADS * 2 * B_HALF_DIM),
        cache_b_v.reshape(DEC_BATCH, DEPTH, PAST_LEN, B_HEADS * B_V_DIM),
    )

    w_in_bf = w_in.astype(BF16)
    w_branch_bf = w_branch.astype(BF16)
    w_out_bf = w_out.astype(BF16)

    new_ak, new_av, new_bk, new_bv = [], [], [], []
    row = lambda v: v.reshape(1, -1)
    for l in range(DEPTH):
        lambda_init = 0.8 - 0.6 * math.exp(-0.3 * l)
        z, h = _in_projection(
            x_ctx, x_lat, mods, l, row(g_pre1[l]), w_in_bf,
            row(jnp.tile(a_q_gain[l], A_HEADS)), row(jnp.tile(a_k_gain[l], A_KV_HEADS)), gmat, rope)

        zc = z[:N_CTX]
        new_ak.append(zc[:, COL_KA:COL_VA].reshape(BATCH, SEQ, A_KV_HEADS, A_HEAD_DIM))
        new_av.append(zc[:, COL_VA:COL_QB].reshape(BATCH, SEQ, A_KV_HEADS, A_HEAD_DIM))
        new_bk.append(zc[:, COL_KB:COL_VB].reshape(BATCH, SEQ, B_HEADS, 2, B_HALF_DIM))
        new_bv.append(zc[:, COL_VB:COL_ZC].reshape(BATCH, SEQ, B_HEADS, B_V_DIM))

        subg = row(b_subln_gain[l])
        oa_c, ob_c = _attention_ctx(z, b_lambda[l], subg, lambda_init)
        oa_l, ob_l = _attention_lat(z, caches, l, b_lambda[l], subg, lambda_init)
        conv_args = (c_dw_w[l], row(c_dw_b[l]), row(c_ln_g[l]), row(c_ln_b[l]),
                     _block_diag(d_w_group[l]).astype(BF16), row(d_scale[l]))
        oc_c, od_c = _conv_pool(z, SEQ, BATCH, 0, *conv_args)
        oc_l, od_l = _conv_pool(z, DEC_SEQ, DEC_BATCH, N_CTX, *conv_args)

        x1, h2, eid, rank, wts, cnt = _mix(
            x_ctx, x_lat, h, (oa_c, ob_c, oc_c, od_c), (oa_l, ob_l, oc_l, od_l), mods, l,
            w_in_bf, w_branch_bf, w_out_bf,
            row(g_post1[l]), row(g_pre2[l]), r_w[l].T, r_b[l].reshape(N_EXPERTS, 1))
        wts = wts[:TOP_K].T

        pos, tile_expert, tile_rows, tile_slot, tile_next, starts, padded = _routing_tables(
            eid, rank, cnt)
        xs = _dispatch(starts, padded, pos, h2)
        ys = _expert_ffn(tile_expert, tile_rows, tile_slot, tile_next, xs, l,
                         e_w1, e_b1, e_w2, e_b2)
        x_ctx, x_lat = _combine(pos, ys, wts, x1, mods, l, row(g_post2[l]))

    y_prompt = x_ctx.reshape(BATCH, SEQ, D_MODEL)
    y_sample = x_lat.reshape(DEC_BATCH, DEC_SEQ, D_MODEL)
    return (y_prompt, y_sample, jnp.stack(new_ak, axis=1), jnp.stack(new_av, axis=1),
            jnp.stack(new_bk, axis=1), jnp.stack(new_bv, axis=1))
```

```python
import functools
import math

import jax
import jax.numpy as jnp
import numpy as np
from jax import lax
from jax.experimental import pallas as pl
from jax.experimental.pallas import tpu as pltpu

F32 = jnp.float32
BF16 = jnp.bfloat16

D_MODEL = 1024
BATCH = 16
SEQ = 256
DEPTH = 2
DEC_BATCH = 4
DEC_SEQ = 2048
PAST_LEN = 256
GRID_W = 64
ROPE_THETA = 10000.0
N_BRANCH = 4
BRANCH_W = D_MODEL // 4
A_HEADS = 4
A_KV_HEADS = 2
A_HEAD_DIM = 64
B_HEADS = 4
B_HALF_DIM = 32
B_V_DIM = 2 * B_HALF_DIM
C_CONV_WIDTH = 31
D_GROUPS = 4
D_GROUP_W = BRANCH_W // D_GROUPS
POOL_WINDOWS = (2, 4, 8, 16)
N_EXPERTS = 32
TOP_K = 4
D_FF = D_MODEL
SWIGLU_LIMIT = 7.0
SWIGLU_ALPHA = 1.702
EPS = 1e-6

N_CTX = BATCH * SEQ
N_LAT = DEC_BATCH * DEC_SEQ
N_TOK = N_CTX + N_LAT

COL_QA, COL_KA, COL_VA = 0, 256, 384
COL_QB, COL_KB, COL_VB = 512, 768, 1024
COL_ZC, COL_ZD = 1280, 1792
N_SMALL = 2048
N_GATE = N_BRANCH * D_MODEL

TM = 256
N_TILES = N_TOK // TM
CTX_TILES = N_CTX // TM
LAT_TILES_PER_SEQ = DEC_SEQ // TM
MOD_ROWS = 8

TE = 512
N_SLOTS = N_TOK * TOP_K
P_MAX = N_SLOTS + N_EXPERTS * TE
E_TILES = P_MAX // TE

TQ = 512
CONV_PAD = 16
CONV_CHUNK = 128

VMEM_LIMIT = 56 * 1024 * 1024


def _sigmoid(x):
    return 1.0 / (1.0 + jnp.exp(-x))


def _split_bf16(a):
    hi = a.astype(BF16)
    lo = (a - hi.astype(F32)).astype(BF16)
    return hi, lo


def _dot(a, b):
    return jnp.dot(a, b, preferred_element_type=F32)


def _dot_nt(a, b):
    return lax.dot_general(a, b, (((1,), (1,)), ((), ())), preferred_element_type=F32)


def _dot3(a, b):
    ah, al = _split_bf16(a)
    bh, bl = _split_bf16(b)
    return _dot(ah, bh) + _dot(ah, bl) + _dot(al, bh)


def _rms(x, g):
    return x * lax.rsqrt(jnp.mean(x * x, axis=-1, keepdims=True) + EPS) * g


def _mod_row(i, ctx_tiles=CTX_TILES, tiles_per_seq=LAT_TILES_PER_SEQ):
    return jnp.where(i < ctx_tiles, 0, 1 + (i - ctx_tiles) // tiles_per_seq)


LANES = 128
ROW_TILES = D_MODEL // LANES


def _store_token_major(ref, x):
    n = x.shape[0]
    for c in range(ROW_TILES):
        ref[pl.ds(c, n, stride=ROW_TILES), :] = x[:, c * LANES:(c + 1) * LANES]


def _load_token_major(ref):
    n = ref.shape[0] // ROW_TILES
    return jnp.concatenate(
        [ref[pl.ds(c, n, stride=ROW_TILES), :] for c in range(ROW_TILES)], axis=1)


def _ctx_rows(width, tile=TM):
    last = N_CTX // tile - 1
    return pl.BlockSpec((tile, width), lambda i: (jnp.minimum(i, last), 0))


def _lat_rows(width, tile=TM):
    first = N_CTX // tile
    return pl.BlockSpec((tile, width), lambda i: (jnp.maximum(i - first, 0), 0))


MOD_TN = 1536


def _mod_kernel(cond_ref, w_ref, b_ref, o_ref):
    c = cond_ref[...]
    s = c * _sigmoid(c)
    o_ref[0] = _dot3(s, w_ref[0]) + b_ref[0]


def _modulation(cond8, w_mod, b_mod):
    L = w_mod.shape[0]
    return pl.pallas_call(
        _mod_kernel,
        name="modulation",
        grid=(L, 6 * D_MODEL // MOD_TN),
        in_specs=[
            pl.BlockSpec((MOD_ROWS, D_MODEL), lambda l, j: (0, 0)),
            pl.BlockSpec((1, D_MODEL, MOD_TN), lambda l, j: (l, 0, j)),
            pl.BlockSpec((1, 1, MOD_TN), lambda l, j: (l, 0, j)),
        ],
        out_specs=pl.BlockSpec((1, MOD_ROWS, MOD_TN), lambda l, j: (l, 0, j)),
        out_shape=jax.ShapeDtypeStruct((L, MOD_ROWS, 6 * D_MODEL), F32),
        compiler_params=pltpu.CompilerParams(
            dimension_semantics=("arbitrary", "arbitrary"), vmem_limit_bytes=VMEM_LIMIT),
    )(cond8, w_mod, b_mod.reshape(L, 1, 6 * D_MODEL))


def _group_mean(sq, gmat):
    hi, lo = _split_bf16(sq)
    return _dot(hi, gmat) + _dot(lo, gmat)


def _swap_halves(x, half):
    w = x.shape[-1]
    lane = lax.broadcasted_iota(jnp.int32, x.shape, 1)
    first = (lane % (2 * half)) < half
    return jnp.where(first, pltpu.roll(x, w - half, 1), pltpu.roll(x, half, 1))


IN_TM = 512
IN_CTX_TILES = N_CTX // IN_TM
IN_LAT_PER_SEQ = DEC_SEQ // IN_TM


def _inproj_kernel(xc_ref, xl_ref, mod_ref, gpre_ref, w_ref, gq_ref, gk_ref, gmat_ref,
                   cos_a_ref, sin_a_ref, cos_b_ref, sin_b_ref, z_ref, h_ref):
    i = pl.program_id(0)
    m = mod_ref[0]
    sh1 = m[:, 0:D_MODEL]
    sc1 = m[:, D_MODEL:2 * D_MODEL]
    x = jnp.where(i < IN_CTX_TILES, xc_ref[...], xl_ref[...])
    h = _rms(x, gpre_ref[...]) * (1.0 + sc1) + sh1
    hb = h.astype(BF16)
    h_ref[...] = hb
    z = _dot(hb, w_ref[0])

    gmat = gmat_ref[...]
    qa = z[:, COL_QA:COL_KA]
    ka = z[:, COL_KA:COL_VA]
    qa = qa * lax.rsqrt(_group_mean(qa * qa, gmat) + EPS) * gq_ref[...]
    ka = ka * lax.rsqrt(_group_mean(ka * ka, gmat[0:128, 0:128]) + EPS) * gk_ref[...]
    qb = z[:, COL_QB:COL_KB]
    kb = z[:, COL_KB:COL_VB]
    z_ref[:, COL_VA:COL_QB] = z[:, COL_VA:COL_QB]
    z_ref[:, COL_VB:N_SMALL] = z[:, COL_VB:N_SMALL]

    @pl.when(i < IN_CTX_TILES)
    def _():
        z_ref[:, COL_QA:COL_KA] = qa
        z_ref[:, COL_KA:COL_VA] = ka
        z_ref[:, COL_QB:COL_KB] = qb
        z_ref[:, COL_KB:COL_VB] = kb

    @pl.when(i >= IN_CTX_TILES)
    def _():
        cos_a = cos_a_ref[...]
        sin_a = sin_a_ref[...]
        cos_b = cos_b_ref[...]
        sin_b = sin_b_ref[...]
        ha, hb_ = A_HEAD_DIM // 2, B_HALF_DIM // 2
        z_ref[:, COL_QA:COL_KA] = qa * cos_a + _swap_halves(qa, ha) * sin_a
        z_ref[:, COL_KA:COL_VA] = ka * cos_a[:, 0:128] + _swap_halves(ka, ha) * sin_a[:, 0:128]
        z_ref[:, COL_QB:COL_KB] = qb * cos_b + _swap_halves(qb, hb_) * sin_b
        z_ref[:, COL_KB:COL_VB] = kb * cos_b + _swap_halves(kb, hb_) * sin_b


def _rope_block(i):
    return jnp.where(i < IN_CTX_TILES, 0, (i - IN_CTX_TILES) % IN_LAT_PER_SEQ)


def _in_projection(x_ctx, x_lat, mods, l, g_pre, w_in_bf, gq, gk, gmat, rope):
    const = lambda i: (0, 0)
    rope_spec = pl.BlockSpec((IN_TM, 256), lambda i: (_rope_block(i), 0))
    mod_row = lambda i: _mod_row(i, IN_CTX_TILES, IN_LAT_PER_SEQ)
    return pl.pallas_call(
        _inproj_kernel,
        name="in_projection",
        grid=(N_TOK // IN_TM,),
        in_specs=[
            _ctx_rows(D_MODEL, IN_TM), _lat_rows(D_MODEL, IN_TM),
            pl.BlockSpec((1, 1, 6 * D_MODEL), lambda i: (l * MOD_ROWS + mod_row(i), 0, 0)),
            pl.BlockSpec((1, D_MODEL), const),
            pl.BlockSpec((1, D_MODEL, N_SMALL), lambda i: (l, 0, 0)),
            pl.BlockSpec((1, 256), const),
            pl.BlockSpec((1, 128), const),
            pl.BlockSpec((256, 256), const),
            rope_spec, rope_spec, rope_spec, rope_spec,
        ],
        out_specs=[
            pl.BlockSpec((IN_TM, N_SMALL), lambda i: (i, 0)),
            pl.BlockSpec((IN_TM, D_MODEL), lambda i: (i, 0)),
        ],
        out_shape=[
            jax.ShapeDtypeStruct((N_TOK, N_SMALL), F32),
            jax.ShapeDtypeStruct((N_TOK, D_MODEL), BF16),
        ],
        compiler_params=pltpu.CompilerParams(
            dimension_semantics=("arbitrary",), vmem_limit_bytes=VMEM_LIMIT),
    )(x_ctx, x_lat, mods, g_pre, w_in_bf, gq, gk, gmat, *rope)


LOG2E = 1.4426950408889634
HALF_LANES = 64


def _attend(q, ks, vexts):
    ss = [_dot_nt(q, k) for k in ks]
    m = ss[0].max(axis=-1, keepdims=True)
    for s in ss[1:]:
        m = jnp.maximum(m, s.max(axis=-1, keepdims=True))
    o = None
    for s, v in zip(ss, vexts):
        t = _dot(jnp.exp2(s - m).astype(BF16), v)
        o = t if o is None else o + t
    return o


def _value_ext(v128, upper):
    lane = lax.broadcasted_iota(jnp.int32, v128.shape, 1)
    keep = (lane >= HALF_LANES) if upper else (lane < HALF_LANES)
    return jnp.where(keep, v128, 1.0).astype(BF16)


def _normalised(o, upper):
    if upper:
        return o[:, HALF_LANES:] / o[:, 0:1]
    return o[:, :HALF_LANES] / o[:, HALF_LANES:HALF_LANES + 1]


def _attn_kernel(*refs, has_cache, lambda_init, after_unit=None):
    if has_cache:
        (qa_ref, ka_ref, va_ref, qb_ref, kb_ref, vb_ref,
         cka_ref, cva_ref, ckb_ref, cvb_ref, lam_ref, subg_ref, oa_ref, ob_ref) = refs
    else:
        (qa_ref, ka_ref, va_ref, qb_ref, kb_ref, vb_ref,
         lam_ref, subg_ref, oa_ref, ob_ref) = refs

    def cols(ref, lo, hi):
        return ref[:, lo:hi].astype(BF16)

    def ccols(ref, lo, hi):
        return ref[0, 0, :, lo:hi].astype(BF16)

    scale_a = A_HEAD_DIM ** -0.5 * LOG2E
    group = A_HEADS // A_KV_HEADS
    for g in range(A_KV_HEADS):
        lo, hi = g * A_HEAD_DIM, (g + 1) * A_HEAD_DIM
        upper = g % 2 == 1
        ks = [cols(ka_ref, lo, hi)]
        vexts = [_value_ext(va_ref[...], upper)]
        if has_cache:
            ks.append(ccols(cka_ref, lo, hi))
            vexts.append(_value_ext(cva_ref[0, 0], upper))
        for hd in range(g * group, (g + 1) * group):
            q = (qa_ref[:, hd * A_HEAD_DIM:(hd + 1) * A_HEAD_DIM] * scale_a).astype(BF16)
            o = _attend(q, ks, vexts)
            oa_ref[:, hd * A_HEAD_DIM:(hd + 1) * A_HEAD_DIM] = _normalised(o, upper)
        if after_unit is not None:
            after_unit(g)

    bl = lam_ref[...]
    lam = (jnp.exp(jnp.sum(bl[0:1] * bl[1:2], axis=-1, keepdims=True))
           - jnp.exp(jnp.sum(bl[2:3] * bl[3:4], axis=-1, keepdims=True)) + lambda_init)
    scale_b = B_HALF_DIM ** -0.5 * LOG2E
    subg = subg_ref[...]
    for hd in range(B_HEADS):
        base = hd * 2 * B_HALF_DIM
        vlo, vhi = hd * B_V_DIM, (hd + 1) * B_V_DIM
        blk = (hd // 2) * 2 * B_V_DIM
        upper = hd % 2 == 1
        vexts = [_value_ext(vb_ref[:, blk:blk + 2 * B_V_DIM], upper)]
        if has_cache:
            vexts.append(_value_ext(cvb_ref[0, 0, :, blk:blk + 2 * B_V_DIM], upper))
        outs = []
        for c in range(2):
            lo, hi = base + c * B_HALF_DIM, base + (c + 1) * B_HALF_DIM
            q = (qb_ref[:, lo:hi] * scale_b).astype(BF16)
            ks = [cols(kb_ref, lo, hi)]
            if has_cache:
                ks.append(ccols(ckb_ref, lo, hi))
            outs.append(_normalised(_attend(q, ks, vexts), upper))
        o = outs[0] - lam * outs[1]
        o = _rms(o, subg) * (1.0 - lambda_init)
        ob_ref[:, vlo:vhi] = o
        if after_unit is not None:
            after_unit(A_KV_HEADS + hd)


SUBLANES = 8


def _row_shifter(win):
    span = win.shape[0] - SUBLANES
    shifted = {}

    def rows(off):
        s = off % SUBLANES
        if s not in shifted:
            shifted[s] = win[s:s + span, :]
        base = off - s
        return shifted[s][base:base + CONV_CHUNK, :]

    return rows


def _conv_fill(a_ref, b_ref, d_ref, hpad, upad, seq_len):
    zeros = jnp.zeros((CONV_PAD, BRANCH_W), F32)
    hpad[0:CONV_PAD, :] = zeros
    hpad[seq_len + CONV_PAD:seq_len + 2 * CONV_PAD, :] = zeros
    upad[0:CONV_PAD, :] = zeros
    upad[seq_len + CONV_PAD:seq_len + 2 * CONV_PAD, :] = zeros

    def fill(c, carry):
        r = pl.multiple_of(c * CONV_CHUNK, CONV_CHUNK)
        a = a_ref[pl.ds(r, CONV_CHUNK), :]
        b = b_ref[pl.ds(r, CONV_CHUNK), :]
        hpad[pl.ds(r + CONV_PAD, CONV_CHUNK), :] = a * _sigmoid(b)
        upad[pl.ds(r + CONV_PAD, CONV_CHUNK), :] = d_ref[pl.ds(r, CONV_CHUNK), :]
        return carry

    lax.fori_loop(0, seq_len // CONV_CHUNK, fill, 0)


def _conv_chunk(r, out_r, conv_refs, hpad, upad, oc_ref, od_ref, seq_len):
    dww_ref, dwb_ref, lng_ref, lnb_ref, wbd_ref, dsc_ref = conv_refs
    lane = lax.broadcasted_iota(jnp.int32, (CONV_CHUNK, BRANCH_W), 1)
    half = C_CONV_WIDTH // 2

    hrows = _row_shifter(hpad[pl.ds(r, CONV_CHUNK + 2 * CONV_PAD), :])
    acc = jnp.zeros((CONV_CHUNK, BRANCH_W), F32)
    for k in range(C_CONV_WIDTH):
        acc = acc + dww_ref[k:k + 1, :] * hrows(CONV_PAD - half + k)
    acc = acc + dwb_ref[...]
    mu = jnp.mean(acc, axis=-1, keepdims=True)
    cen = acc - mu
    var = jnp.mean(cen * cen, axis=-1, keepdims=True)
    y = cen * lax.rsqrt(var + EPS) * lng_ref[...] + lnb_ref[...]
    oc_ref[pl.ds(out_r, CONV_CHUNK), :] = y * _sigmoid(y)

    urows = _row_shifter(upad[pl.ds(r, CONV_CHUNK + 2 * CONV_PAD), :])

    def ld(d):
        return urows(CONV_PAD + d)

    u = ld(0)
    sums = {}
    s = u + ld(-1)
    sums[2] = s
    s = s + ld(-2) + ld(1)
    sums[4] = s
    s = s + ld(-4) + ld(-3) + ld(2) + ld(3)
    sums[8] = s
    s = s + ld(-8) + ld(-7) + ld(-6) + ld(-5) + ld(4) + ld(5) + ld(6) + ld(7)
    sums[16] = s
    t = r + lax.broadcasted_iota(jnp.int32, (CONV_CHUNK, 1), 0)
    pooled = None
    for g, w in reversed(list(enumerate(POOL_WINDOWS))):
        lo = jnp.maximum(t - w // 2, 0)
        hi = jnp.minimum(t + (w - 1 - w // 2), seq_len - 1)
        mean = sums[w] / (hi - lo + 1).astype(F32)
        pooled = mean if pooled is None else jnp.where(lane < (g + 1) * D_GROUP_W, mean, pooled)
    dlt = (pooled - u).astype(BF16)
    od_ref[pl.ds(out_r, CONV_CHUNK), :] = _dot(dlt, wbd_ref[...]) * dsc_ref[...]


N_CONV_REFS = 6
N_ATTN_UNITS = A_KV_HEADS + B_HEADS


def _mixers_kernel(*refs, has_cache, lambda_init, seq_len, step_rows):
    n_attn = 12 if has_cache else 8
    attn_in = refs[:n_attn]
    a_ref, b_ref, d_ref = refs[n_attn:n_attn + 3]
    conv_refs = refs[n_attn + 3:n_attn + 3 + N_CONV_REFS]
    oa_ref, ob_ref, oc_ref, od_ref, hpad, upad = refs[n_attn + 3 + N_CONV_REFS:]

    if seq_len == step_rows:
        _conv_fill(a_ref, b_ref, d_ref, hpad, upad, seq_len)
        row0 = 0
    else:
        t = pl.program_id(1)

        @pl.when(t == 0)
        def _():
            _conv_fill(a_ref, b_ref, d_ref, hpad, upad, seq_len)
        row0 = t * step_rows

    n_chunks = step_rows // CONV_CHUNK

    def conv(c):
        r = row0 + c * CONV_CHUNK
        if not isinstance(r, int):
            r = pl.multiple_of(r, CONV_CHUNK)
        _conv_chunk(r, c * CONV_CHUNK, conv_refs, hpad, upad, oc_ref, od_ref, seq_len)

    def after_unit(k):
        for c in range(k * n_chunks // N_ATTN_UNITS, (k + 1) * n_chunks // N_ATTN_UNITS):
            conv(c)

    _attn_kernel(*attn_in, oa_ref, ob_ref, has_cache=has_cache, lambda_init=lambda_init,
                 after_unit=after_unit)


def _conv_specs(index_map):
    const = lambda shape: pl.BlockSpec(shape, index_map)
    return [const((C_CONV_WIDTH, BRANCH_W)), const((1, BRANCH_W)), const((1, BRANCH_W)),
            const((1, BRANCH_W)), const((BRANCH_W, BRANCH_W)), const((1, BRANCH_W))]


def _conv_scratch(seq_len):
    return [pltpu.VMEM((seq_len + 2 * CONV_PAD, BRANCH_W), F32)] * 2


def _mixers_ctx(z, b_lambda, subg, lambda_init, conv_args):
    def zspec(width, col):
        return pl.BlockSpec((SEQ, width), lambda b: (b, col // width))
    out_spec = pl.BlockSpec((SEQ, BRANCH_W), lambda b: (b, 0))
    return pl.pallas_call(
        functools.partial(_mixers_kernel, has_cache=False, lambda_init=lambda_init,
                          seq_len=SEQ, step_rows=SEQ),
        name="mixers_ctx",
        grid=(BATCH,),
        in_specs=[
            zspec(256, COL_QA), zspec(128, COL_KA), zspec(128, COL_VA),
            zspec(256, COL_QB), zspec(256, COL_KB), zspec(256, COL_VB),
            pl.BlockSpec((4, B_HALF_DIM), lambda b: (0, 0)),
            pl.BlockSpec((1, B_V_DIM), lambda b: (0, 0)),
            zspec(256, COL_ZC), zspec(256, COL_ZC + BRANCH_W), zspec(256, COL_ZD),
            *_conv_specs(lambda b: (0, 0)),
        ],
        out_specs=[out_spec] * 4,
        out_shape=[jax.ShapeDtypeStruct((N_CTX, BRANCH_W), F32)] * 4,
        scratch_shapes=_conv_scratch(SEQ),
        compiler_params=pltpu.CompilerParams(
            dimension_semantics=("arbitrary",), vmem_limit_bytes=VMEM_LIMIT),
    )(z, z, z, z, z, z, b_lambda, subg, z, z, z, *conv_args)


def _mixers_lat(z, caches, l, b_lambda, subg, lambda_init, conv_args):
    q_tiles = DEC_SEQ // TQ
    seq0 = N_CTX // DEC_SEQ

    def qspec(col):
        return pl.BlockSpec((TQ, 256), lambda b, t: (N_CTX // TQ + b * q_tiles + t, col // 256))

    def kspec(width, col):
        return pl.BlockSpec((DEC_SEQ, width), lambda b, t: (seq0 + b, col // width))

    def cspec(width):
        return pl.BlockSpec((1, 1, PAST_LEN, width), lambda b, t: (b, l, 0, 0))

    out_spec = pl.BlockSpec((TQ, BRANCH_W), lambda b, t: (b * q_tiles + t, 0))
    cka, cva, ckb, cvb = caches
    return pl.pallas_call(
        functools.partial(_mixers_kernel, has_cache=True, lambda_init=lambda_init,
                          seq_len=DEC_SEQ, step_rows=TQ),
        name="mixers_lat",
        grid=(DEC_BATCH, q_tiles),
        in_specs=[
            qspec(COL_QA), kspec(128, COL_KA), kspec(128, COL_VA),
            qspec(COL_QB), kspec(256, COL_KB), kspec(256, COL_VB),
            cspec(128), cspec(128), cspec(256), cspec(256),
            pl.BlockSpec((4, B_HALF_DIM), lambda b, t: (0, 0)),
            pl.BlockSpec((1, B_V_DIM), lambda b, t: (0, 0)),
            kspec(256, COL_ZC), kspec(256, COL_ZC + BRANCH_W), kspec(256, COL_ZD),
            *_conv_specs(lambda b, t: (0, 0)),
        ],
        out_specs=[out_spec] * 4,
        out_shape=[jax.ShapeDtypeStruct((N_LAT, BRANCH_W), F32)] * 4,
        scratch_shapes=_conv_scratch(DEC_SEQ),
        compiler_params=pltpu.CompilerParams(
            dimension_semantics=("arbitrary", "arbitrary"), vmem_limit_bytes=VMEM_LIMIT),
    )(z, z, z, z, z, z, cka, cva, ckb, cvb, b_lambda, subg, z, z, z, *conv_args)


MIX_TM = 512
MIX_TILES = N_TOK // MIX_TM
MIX_CTX_TILES = N_CTX // MIX_TM
MIX_LAT_PER_SEQ = DEC_SEQ // MIX_TM
ROUTE_ROWS = 8


def _mix_kernel(xc_ref, xl_ref, h_ref, oa_c, ob_c, oc_c, od_c, oa_l, ob_l, oc_l, od_l,
                mod_ref, wg01_ref, wg23_ref, wbr_ref, wout_ref, gpost_ref, gpre2_ref, rwt_ref, rb_ref,
                x1_ref, h2_ref, eid_ref, rank_ref, wts_ref, cnt_ref, carry):
    i = pl.program_id(0)
    is_ctx = i < MIX_CTX_TILES

    @pl.when(i == 0)
    def _():
        carry[...] = jnp.zeros_like(carry)

    m = mod_ref[0]
    g1 = m[:, 2 * D_MODEL:3 * D_MODEL]
    sh2 = m[:, 3 * D_MODEL:4 * D_MODEL]
    sc2 = m[:, 4 * D_MODEL:5 * D_MODEL]

    hb = h_ref[...]
    merged = None
    for n, (c_ref, l_ref) in enumerate(((oa_c, oa_l), (ob_c, ob_l), (oc_c, oc_l), (od_c, od_l))):
        wg_ref = wg01_ref if n < 2 else wg23_ref
        gate = _sigmoid(_dot(hb, wg_ref[0, :, (n % 2) * D_MODEL:(n % 2 + 1) * D_MODEL]))
        o = jnp.where(is_ctx, c_ref[...], l_ref[...])
        br = _dot(o.astype(BF16), wbr_ref[0, n])
        merged = gate * br if merged is None else merged + gate * br
    y = _dot(merged.astype(BF16), wout_ref[0])
    x = jnp.where(is_ctx, xc_ref[...], xl_ref[...])
    x1 = x + g1 * _rms(y, gpost_ref[...])
    x1_ref[...] = x1
    h2 = _rms(x1, gpre2_ref[...]) * (1.0 + sc2) + sh2
    _store_token_major(h2_ref, h2)

    rh, rl = _split_bf16(rwt_ref[...])
    hh, hl = _split_bf16(h2)
    logits = _dot_nt(rh, hh) + _dot_nt(rh, hl) + _dot_nt(rl, hh) + rb_ref[...]

    iota_e = lax.broadcasted_iota(jnp.int32, (N_EXPERTS, MIX_TM), 0)
    rem = logits
    vals, idxs = [], []
    for _ in range(TOP_K):
        mx = jnp.max(rem, axis=0, keepdims=True)
        idx = jnp.min(jnp.where(rem == mx, iota_e, N_EXPERTS), axis=0, keepdims=True)
        vals.append(mx)
        idxs.append(idx)
        rem = jnp.where(iota_e == idx, -jnp.inf, rem)
    exps = [jnp.exp(v - vals[0]) for v in vals]
    den = exps[0]
    for e in exps[1:]:
        den = den + e

    sel = [iota_e == idx for idx in idxs]
    member = jnp.zeros((N_EXPERTS, MIX_TM), F32)
    for s in sel:
        member = member + jnp.where(s, 1.0, 0.0)
    row = lax.broadcasted_iota(jnp.int32, (MIX_TM, MIX_TM), 0)
    col = lax.broadcasted_iota(jnp.int32, (MIX_TM, MIX_TM), 1)
    earlier = jnp.where(row < col, 1.0, 0.0).astype(BF16)
    seen = _dot(member.astype(BF16), earlier) + carry[...]
    carry[...] = carry[...] + jnp.sum(member, axis=1, keepdims=True)
    cnt_ref[...] = carry[...]

    sub = lax.broadcasted_iota(jnp.int32, (ROUTE_ROWS, MIX_TM), 0)
    eid = jnp.zeros((ROUTE_ROWS, MIX_TM), jnp.int32)
    rank = jnp.zeros((ROUTE_ROWS, MIX_TM), jnp.int32)
    wts = jnp.zeros((ROUTE_ROWS, MIX_TM), F32)
    for k in range(TOP_K):
        rk = jnp.sum(jnp.where(sel[k], seen, 0.0), axis=0, keepdims=True).astype(jnp.int32)
        eid = jnp.where(sub == k, idxs[k], eid)
        rank = jnp.where(sub == k, rk, rank)
        wts = jnp.where(sub == k, exps[k] / den, wts)
    eid_ref[...] = eid
    rank_ref[...] = rank
    wts_ref[...] = wts


def _mix(x_ctx, x_lat, h, o_ctx, o_lat, mods, l, w_in_bf, w_branch_bf, w_out_bf,
         g_post, g_pre2, r_wt, r_b):
    const2 = lambda i: (0, 0)
    tile = lambda width: pl.BlockSpec((MIX_TM, width), lambda i: (i, 0))
    route = pl.BlockSpec((ROUTE_ROWS, MIX_TM), lambda i: (0, i))
    ctx_tile = _ctx_rows(BRANCH_W, MIX_TM)
    lat_tile = _lat_rows(BRANCH_W, MIX_TM)
    mod_row = lambda i: _mod_row(i, MIX_CTX_TILES, MIX_LAT_PER_SEQ)
    single = pl.Buffered(1)
    return pl.pallas_call(
        _mix_kernel,
        name="mix_router",
        grid=(MIX_TILES,),
        in_specs=[
            _ctx_rows(D_MODEL, MIX_TM), _lat_rows(D_MODEL, MIX_TM), tile(D_MODEL),
            ctx_tile, ctx_tile, ctx_tile, ctx_tile, lat_tile, lat_tile, lat_tile, lat_tile,
            pl.BlockSpec((1, 1, 6 * D_MODEL), lambda i: (l * MOD_ROWS + mod_row(i), 0, 0)),
            pl.BlockSpec((1, D_MODEL, N_SMALL), lambda i: (l, 0, 1), pipeline_mode=single),
            pl.BlockSpec((1, D_MODEL, N_SMALL), lambda i: (l, 0, 2), pipeline_mode=single),
            pl.BlockSpec((1, N_BRANCH, BRANCH_W, D_MODEL), lambda i: (l, 0, 0, 0),
                         pipeline_mode=single),
            pl.BlockSpec((1, D_MODEL, D_MODEL), lambda i: (l, 0, 0), pipeline_mode=single),
            pl.BlockSpec((1, D_MODEL), const2),
            pl.BlockSpec((1, D_MODEL), const2),
            pl.BlockSpec((N_EXPERTS, D_MODEL), const2),
            pl.BlockSpec((N_EXPERTS, 1), const2),
        ],
        out_specs=[
            tile(D_MODEL), pl.BlockSpec((MIX_TM * ROW_TILES, LANES), lambda i: (i, 0)),
            route, route, route,
            pl.BlockSpec((N_EXPERTS, 1), const2),
        ],
        out_shape=[
            jax.ShapeDtypeStruct((N_TOK, D_MODEL), F32),
            jax.ShapeDtypeStruct((N_TOK * ROW_TILES, LANES), F32),
            jax.ShapeDtypeStruct((ROUTE_ROWS, N_TOK), jnp.int32),
            jax.ShapeDtypeStruct((ROUTE_ROWS, N_TOK), jnp.int32),
            jax.ShapeDtypeStruct((ROUTE_ROWS, N_TOK), F32),
            jax.ShapeDtypeStruct((N_EXPERTS, 1), F32),
        ],
        scratch_shapes=[pltpu.VMEM((N_EXPERTS, 1), F32)],
        compiler_params=pltpu.CompilerParams(
            dimension_semantics=("arbitrary",), vmem_limit_bytes=VMEM_LIMIT),
    )(x_ctx, x_lat, h, *o_ctx, *o_lat, mods, w_in_bf, w_in_bf, w_branch_bf, w_out_bf,
      g_post, g_pre2, r_wt, r_b)


def _token_rows(t, n=1):
    return pl.ds(pl.multiple_of(t * ROW_TILES, ROW_TILES), n * ROW_TILES)


def _token_copy(src_ref, src_tok, dst_ref, dst_tok, sem):
    return pltpu.make_async_copy(src_ref.at[_token_rows(src_tok), :], dst_ref.at[_token_rows(dst_tok), :], sem)


DISPATCH_ROWS = 512
DISPATCH_STEPS = N_TOK // DISPATCH_ROWS
DMA_UNROLL = 8


def _dispatch_kernel(starts_ref, padded_ref, pos_ref, h2_ref, xs_ref, zbuf, sem, zsem, tsem):
    i = pl.program_id(0)

    def pad_copy(e):
        last = pl.multiple_of(starts_ref[e] + padded_ref[e] - TE, TE)
        return pltpu.make_async_copy(zbuf, xs_ref.at[_token_rows(last, TE), :], zsem)

    first_tail = (starts_ref[N_EXPERTS - 1] + padded_ref[N_EXPERTS - 1]) // TE

    def tail_copy(j):
        return pltpu.make_async_copy(zbuf, xs_ref.at[_token_rows(j * TE, TE), :], tsem)

    def tail_start(j, carry):
        tail_copy(j).start()
        return carry

    def tail_wait(j, carry):
        tail_copy(j).wait()
        return carry

    @pl.when(i == 0)
    def _():
        zbuf[...] = jnp.zeros_like(zbuf)
        for e in range(N_EXPERTS):
            @pl.when(padded_ref[e] > 0)
            def _():
                pad_copy(e).start()
        lax.fori_loop(first_tail, E_TILES, tail_start, 0)
        for e in range(N_EXPERTS):
            @pl.when(padded_ref[e] > 0)
            def _():
                pad_copy(e).wait()

    @pl.when(i == DISPATCH_STEPS - 1)
    def _():
        lax.fori_loop(first_tail, E_TILES, tail_wait, 0)

    def issue(t, carry):
        for k in range(TOP_K):
            _token_copy(h2_ref, t, xs_ref, pos_ref[t * TOP_K + k], sem).start(priority=k % 2)
        return carry

    lax.fori_loop(0, DISPATCH_ROWS, issue, 0, unroll=DMA_UNROLL)

    for _ in range(TOP_K):
        pltpu.make_async_copy(h2_ref, xs_ref.at[_token_rows(0, DISPATCH_ROWS), :], sem).wait()


def _dispatch(starts, padded, pos, h2):
    grid_spec = pltpu.PrefetchScalarGridSpec(
        num_scalar_prefetch=2,
        grid=(DISPATCH_STEPS,),
        in_specs=[
            pl.BlockSpec((DISPATCH_ROWS * TOP_K,), lambda i, st, pd: (i,), memory_space=pltpu.SMEM),
            pl.BlockSpec((DISPATCH_ROWS * ROW_TILES, LANES), lambda i, st, pd: (i, 0)),
        ],
        out_specs=pl.BlockSpec(memory_space=pl.ANY),
        scratch_shapes=[pltpu.VMEM((TE * ROW_TILES, LANES), F32),
                        pltpu.SemaphoreType.DMA(()), pltpu.SemaphoreType.DMA(()),
                        pltpu.SemaphoreType.DMA(())],
    )
    return pl.pallas_call(
        _dispatch_kernel,
        name="dispatch",
        grid_spec=grid_spec,
        out_shape=jax.ShapeDtypeStruct((P_MAX * ROW_TILES, LANES), F32),
        compiler_params=pltpu.CompilerParams(
            dimension_semantics=("arbitrary",), vmem_limit_bytes=VMEM_LIMIT),
    )(starts, padded, pos, h2)


W_CAST_ROWS = 128


def _moe_kernel(te_ref, rows_ref, slot_ref, next_ref, x_ref, w1_hbm, b1_ref, w2_hbm, b2_ref, y_ref,
                w1f, w2f, w1b, w2b, sem, *, layer):
    j = pl.program_id(0)
    e = te_ref[j]
    prev = te_ref[jnp.maximum(j - 1, 0)]
    valid = rows_ref[j] > 0
    first = jnp.logical_and(valid, jnp.logical_or(j == 0, e != prev))
    s = slot_ref[j]

    def weight_copies(ex, sl):
        return (pltpu.make_async_copy(w1_hbm.at[layer, ex], w1f.at[sl], sem.at[0, sl]),
                pltpu.make_async_copy(w2_hbm.at[layer, ex], w2f.at[sl], sem.at[1, sl]))

    @pl.when(j == 0)
    def _():
        for cp in weight_copies(e, s):
            cp.start()

    @pl.when(first)
    def _():
        for cp in weight_copies(e, s):
            cp.wait()
        nxt = next_ref[j]

        @pl.when(nxt >= 0)
        def _():
            for cp in weight_copies(nxt, 1 - s):
                cp.start()

        def cast(c, carry):
            r = pl.multiple_of(c * W_CAST_ROWS, W_CAST_ROWS)
            w1b[pl.ds(r, W_CAST_ROWS), :] = w1f[s, pl.ds(r, W_CAST_ROWS), :].astype(BF16)
            w2b[pl.ds(r, W_CAST_ROWS), :] = w2f[s, pl.ds(r, W_CAST_ROWS), :].astype(BF16)
            return carry
        lax.fori_loop(0, D_MODEL // W_CAST_ROWS, cast, 0)

    @pl.when(valid)
    def _():
        x = _load_token_major(x_ref).astype(BF16)
        u = _dot(x, w1b[...]) + b1_ref[0, 0]
        xg = jnp.minimum(u[:, :D_FF], SWIGLU_LIMIT)
        xl = jnp.clip(u[:, D_FF:], -SWIGLU_LIMIT, SWIGLU_LIMIT)
        act = (xl + 1.0) * (xg * _sigmoid(SWIGLU_ALPHA * xg))
        _store_token_major(y_ref, _dot(act.astype(BF16), w2b[...]) + b2_ref[0, 0])

    @pl.when(jnp.logical_not(valid))
    def _():
        y_ref[...] = jnp.zeros_like(y_ref)


def _expert_ffn(tile_expert, tile_rows, tile_slot, tile_next, xs, l, e_w1, e_b1, e_w2, e_b2):
    L = e_w1.shape[0]
    grid_spec = pltpu.PrefetchScalarGridSpec(
        num_scalar_prefetch=4,
        grid=(E_TILES,),
        in_specs=[
            pl.BlockSpec((TE * ROW_TILES, LANES),
                         lambda j, te, nr, sl, nx: (jnp.where(nr[j] > 0, j, 0), 0)),
            pl.BlockSpec(memory_space=pl.ANY),
            pl.BlockSpec((1, 1, 1, 2 * D_FF), lambda j, te, nr, sl, nx: (l, te[j], 0, 0)),
            pl.BlockSpec(memory_space=pl.ANY),
            pl.BlockSpec((1, 1, 1, D_MODEL), lambda j, te, nr, sl, nx: (l, te[j], 0, 0)),
        ],
        out_specs=pl.BlockSpec((TE * ROW_TILES, LANES), lambda j, te, nr, sl, nx: (j, 0)),
        scratch_shapes=[pltpu.VMEM((2, D_MODEL, 2 * D_FF), F32), pltpu.VMEM((2, D_FF, D_MODEL), F32),
                        pltpu.VMEM((D_MODEL, 2 * D_FF), BF16), pltpu.VMEM((D_FF, D_MODEL), BF16),
                        pltpu.SemaphoreType.DMA((2, 2))],
    )
    return pl.pallas_call(
        functools.partial(_moe_kernel, layer=l),
        name="expert_ffn",
        grid_spec=grid_spec,
        out_shape=jax.ShapeDtypeStruct((P_MAX * ROW_TILES, LANES), F32),
        compiler_params=pltpu.CompilerParams(
            dimension_semantics=("arbitrary",), vmem_limit_bytes=VMEM_LIMIT),
    )(tile_expert, tile_rows, tile_slot, tile_next, xs, e_w1,
      e_b1.reshape(L, N_EXPERTS, 1, 2 * D_FF), e_w2, e_b2.reshape(L, N_EXPERTS, 1, D_MODEL))


def _combine_kernel(pos_ref, pos_next_ref, ys_ref, wts_ref, x1_ref, mod_ref, gpost_ref,
                    out_c_ref, out_l_ref, buf, sem):
    i = pl.program_id(0)
    slot = i % 2

    def gather(p_ref, s):
        def issue(t, carry):
            for k in range(TOP_K):
                _token_copy(ys_ref, p_ref[t * TOP_K + k], buf.at[s, k], t, sem.at[s]).start(priority=k % 2)
            return carry
        lax.fori_loop(0, TM, issue, 0, unroll=DMA_UNROLL)

    @pl.when(i == 0)
    def _():
        gather(pos_ref, 0)

    @pl.when(i + 1 < N_TILES)
    def _():
        gather(pos_next_ref, 1 - slot)

    for k in range(TOP_K):
        pltpu.make_async_copy(ys_ref.at[_token_rows(0, TM), :], buf.at[slot, k], sem.at[slot]).wait()

    w = wts_ref[...]
    y = w[:, 0:1] * _load_token_major(buf.at[slot, 0])
    for k in range(1, TOP_K):
        y = y + w[:, k:k + 1] * _load_token_major(buf.at[slot, k])
    g2 = mod_ref[0][:, 5 * D_MODEL:6 * D_MODEL]
    res = x1_ref[...] + g2 * _rms(y, gpost_ref[...])

    @pl.when(i < CTX_TILES)
    def _():
        out_c_ref[...] = res

    @pl.when(i >= CTX_TILES)
    def _():
        out_l_ref[...] = res


def _combine(pos, ys, wts, x1, mods, l, g_post2):
    return pl.pallas_call(
        _combine_kernel,
        name="combine",
        grid=(N_TILES,),
        in_specs=[
            pl.BlockSpec((TM * TOP_K,), lambda i: (i,), memory_space=pltpu.SMEM),
            pl.BlockSpec((TM * TOP_K,), lambda i: (jnp.minimum(i + 1, N_TILES - 1),),
                         memory_space=pltpu.SMEM),
            pl.BlockSpec(memory_space=pl.ANY),
            pl.BlockSpec((TM, TOP_K), lambda i: (i, 0)),
            pl.BlockSpec((TM, D_MODEL), lambda i: (i, 0)),
            pl.BlockSpec((1, 1, 6 * D_MODEL), lambda i: (l * MOD_ROWS + _mod_row(i), 0, 0)),
            pl.BlockSpec((1, D_MODEL), lambda i: (0, 0)),
        ],
        out_specs=[_ctx_rows(D_MODEL), _lat_rows(D_MODEL)],
        out_shape=[jax.ShapeDtypeStruct((N_CTX, D_MODEL), F32),
                   jax.ShapeDtypeStruct((N_LAT, D_MODEL), F32)],
        scratch_shapes=[pltpu.VMEM((2, TOP_K, TM * ROW_TILES, LANES), F32),
                        pltpu.SemaphoreType.DMA((2,))],
        compiler_params=pltpu.CompilerParams(
            dimension_semantics=("arbitrary",), vmem_limit_bytes=VMEM_LIMIT),
    )(pos, pos, ys, wts, x1, mods, g_post2)


def _rope_tables():
    t = np.arange(DEC_SEQ)
    r = (t // GRID_W).astype(np.float32)
    c = (t % GRID_W).astype(np.float32)

    def table(dim, reps):
        n_axis = dim // 4
        inv = (ROPE_THETA ** (-np.arange(n_axis, dtype=np.float32) / n_axis)).astype(np.float32)
        ang = np.concatenate([r[:, None] * inv, c[:, None] * inv], axis=-1).astype(np.float32)
        cos, sin = np.cos(ang), np.sin(ang)
        return (np.tile(np.concatenate([cos, cos], -1), (1, reps)),
                np.tile(np.concatenate([-sin, sin], -1), (1, reps)))

    cos_a, sin_a = table(A_HEAD_DIM, 256 // A_HEAD_DIM)
    cos_b, sin_b = table(B_HALF_DIM, 256 // B_HALF_DIM)
    return tuple(jnp.asarray(a, F32) for a in (cos_a, sin_a, cos_b, sin_b))


def _group_mean_matrix():
    idx = np.arange(256) // A_HEAD_DIM
    return jnp.asarray((idx[:, None] == idx[None, :]).astype(np.float32) / A_HEAD_DIM, BF16)


def _block_diag(w):
    out = jnp.zeros((BRANCH_W, BRANCH_W), w.dtype)
    for g in range(D_GROUPS):
        out = out.at[g * D_GROUP_W:(g + 1) * D_GROUP_W, g * D_GROUP_W:(g + 1) * D_GROUP_W].set(w[g])
    return out


def _routing_tables(eid, rank, cnt):
    cnt = cnt.reshape(N_EXPERTS).astype(jnp.int32)
    padded = ((cnt + TE - 1) // TE) * TE
    ends = jnp.cumsum(padded)
    starts = ends - padded
    experts = jnp.arange(N_EXPERTS, dtype=jnp.int32)
    start_of = jnp.sum(jnp.where(eid[:TOP_K, :, None] == experts, starts, 0), axis=-1)
    pos = (start_of + rank[:TOP_K]).T.reshape(N_SLOTS)
    tile_row = jnp.arange(E_TILES, dtype=jnp.int32) * TE
    tile_expert = jnp.sum((tile_row[:, None] >= ends[None, :]).astype(jnp.int32), axis=-1)
    tile_expert = jnp.minimum(tile_expert, N_EXPERTS - 1)
    used_end = jnp.sum(jnp.where(tile_expert[:, None] == experts, starts + cnt, 0), axis=-1)
    tile_rows = jnp.where(tile_row < ends[-1], jnp.clip(used_end - tile_row, 0, TE), 0)
    valid = tile_rows > 0
    changed = jnp.concatenate([jnp.ones((1,), bool), tile_expert[1:] != tile_expert[:-1]])
    tile_slot = (jnp.cumsum(jnp.logical_and(valid, changed).astype(jnp.int32)) + 1) % 2
    group_end = jnp.sum(jnp.where(tile_expert[:, None] == experts, ends, 0), axis=-1) // TE
    tiles = jnp.arange(E_TILES, dtype=jnp.int32)
    expert_at = jnp.where(valid, tile_expert, -1)
    tile_next = jnp.sum(jnp.where(group_end[:, None] == tiles, expert_at + 1, 0), axis=-1) - 1
    as_i32 = lambda a: a.astype(jnp.int32)
    return (as_i32(pos), as_i32(tile_expert), as_i32(tile_rows), as_i32(tile_slot), as_i32(tile_next),
            as_i32(starts), as_i32(padded))


def kernel(x_prompt, x_sample, cache_a_k, cache_a_v, cache_b_k, cache_b_v, c, c_ctx, w_mod, b_mod, g_pre1, g_post1, g_pre2, g_post2, w_in, a_q_gain, a_k_gain, b_lambda, b_subln_gain, c_dw_w, c_dw_b, c_ln_g, c_ln_b, d_w_group, d_scale, w_branch, w_out, r_w, r_b, e_w1, e_b1, e_w2, e_b2):
    x_ctx = x_prompt.reshape(N_CTX, D_MODEL)
    x_lat = x_sample.reshape(N_LAT, D_MODEL)
    cond8 = jnp.concatenate(
        [c_ctx[None, :], c, jnp.zeros((MOD_ROWS - 1 - DEC_BATCH, D_MODEL), F32)], axis=0)
    mods = _modulation(cond8, w_mod, b_mod).reshape(DEPTH * MOD_ROWS, 1, 6 * D_MODEL)

    rope = _rope_tables()
    gmat = _group_mean_matrix()
    caches = (
        cache_a_k.reshape(DEC_BATCH, DEPTH, PAST_LEN, A_KV_HEADS * A_HEAD_DIM),
        cache_a_v.reshape(DEC_BATCH, DEPTH, PAST_LEN, A_KV_HEADS * A_HEAD_DIM),
        cache_b_k.reshape(DEC_BATCH, DEPTH, PAST_LEN, B_HEADS * 2 * B_HALF_DIM),
        cache_b_v.reshape(DEC_BATCH, DEPTH, PAST_LEN, B_HEADS * B_V_DIM),
    )

    w_in_bf = w_in.astype(BF16)
    w_branch_bf = w_branch.astype(BF16)
    w_out_bf = w_out.astype(BF16)

    new_ak, new_av, new_bk, new_bv = [], [], [], []
    row = lambda v: v.reshape(1, -1)
    for l in range(DEPTH):
        lambda_init = 0.8 - 0.6 * math.exp(-0.3 * l)
        z, h = _in_projection(
            x_ctx, x_lat, mods, l, row(g_pre1[l]), w_in_bf,
            row(jnp.tile(a_q_gain[l], A_HEADS)), row(jnp.tile(a_k_gain[l], A_KV_HEADS)), gmat, rope)

        zc = z[:N_CTX]
        new_ak.append(zc[:, COL_KA:COL_VA].reshape(BATCH, SEQ, A_KV_HEADS, A_HEAD_DIM))
        new_av.append(zc[:, COL_VA:COL_QB].reshape(BATCH, SEQ, A_KV_HEADS, A_HEAD_DIM))
        new_bk.append(zc[:, COL_KB:COL_VB].reshape(BATCH, SEQ, B_HEADS, 2, B_HALF_DIM))
        new_bv.append(zc[:, COL_VB:COL_ZC].reshape(BATCH, SEQ, B_HEADS, B_V_DIM))

        subg = row(b_subln_gain[l])
        conv_args = (c_dw_w[l], row(c_dw_b[l]), row(c_ln_g[l]), row(c_ln_b[l]),
                     _block_diag(d_w_group[l]).astype(BF16), row(d_scale[l]))
        o_ctx = _mixers_ctx(z, b_lambda[l], subg, lambda_init, conv_args)
        o_lat = _mixers_lat(z, caches, l, b_lambda[l], subg, lambda_init, conv_args)

        x1, h2, eid, rank, wts, cnt = _mix(
            x_ctx, x_lat, h, o_ctx, o_lat, mods, l,
            w_in_bf, w_branch_bf, w_out_bf,
            row(g_post1[l]), row(g_pre2[l]), r_w[l].T, r_b[l].reshape(N_EXPERTS, 1))
        wts = wts[:TOP_K].T

        pos, tile_expert, tile_rows, tile_slot, tile_next, starts, padded = _routing_tables(
            eid, rank, cnt)
        xs = _dispatch(starts, padded, pos, h2)
        ys = _expert_ffn(tile_expert, tile_rows, tile_slot, tile_next, xs, l,
                         e_w1, e_b1, e_w2, e_b2)
        x_ctx, x_lat = _combine(pos, ys, wts, x1, mods, l, row(g_post2[l]))

    y_prompt = x_ctx.reshape(BATCH, SEQ, D_MODEL)
    y_sample = x_lat.reshape(DEC_BATCH, DEC_SEQ, D_MODEL)
    return (y_prompt, y_sample, jnp.stack(new_ak, axis=1), jnp.stack(new_av, axis=1),
            jnp.stack(new_bk, axis=1), jnp.stack(new_bv, axis=1))
```

```python
import functools
import math

import jax
import jax.numpy as jnp
import numpy as np
from jax import lax
from jax.experimental import pallas as pl
from jax.experimental.pallas import tpu as pltpu

F32 = jnp.float32
BF16 = jnp.bfloat16

D_MODEL = 1024
BATCH = 16
SEQ = 256
DEPTH = 2
DEC_BATCH = 4
DEC_SEQ = 2048
PAST_LEN = 256
GRID_W = 64
ROPE_THETA = 10000.0
N_BRANCH = 4
BRANCH_W = D_MODEL // 4
A_HEADS = 4
A_KV_HEADS = 2
A_HEAD_DIM = 64
B_HEADS = 4
B_HALF_DIM = 32
B_V_DIM = 2 * B_HALF_DIM
C_CONV_WIDTH = 31
D_GROUPS = 4
D_GROUP_W = BRANCH_W // D_GROUPS
POOL_WINDOWS = (2, 4, 8, 16)
N_EXPERTS = 32
TOP_K = 4
D_FF = D_MODEL
SWIGLU_LIMIT = 7.0
SWIGLU_ALPHA = 1.702
EPS = 1e-6

N_CTX = BATCH * SEQ
N_LAT = DEC_BATCH * DEC_SEQ
N_TOK = N_CTX + N_LAT

COL_QA, COL_KA, COL_VA = 0, 256, 384
COL_QB, COL_KB, COL_VB = 512, 768, 1024
COL_ZC, COL_ZD = 1280, 1792
N_SMALL = 2048
N_GATE = N_BRANCH * D_MODEL

TM = 256
N_TILES = N_TOK // TM
CTX_TILES = N_CTX // TM
LAT_TILES_PER_SEQ = DEC_SEQ // TM
MOD_ROWS = 8

TE = 512
N_SLOTS = N_TOK * TOP_K
P_MAX = N_SLOTS + N_EXPERTS * TE
E_TILES = P_MAX // TE

TQ = 512
CONV_PAD = 16
CONV_CHUNK = 128

VMEM_LIMIT = 56 * 1024 * 1024


def _sigmoid(x):
    return 1.0 / (1.0 + jnp.exp(-x))


def _split_bf16(a):
    hi = a.astype(BF16)
    lo = (a - hi.astype(F32)).astype(BF16)
    return hi, lo


def _dot(a, b):
    return jnp.dot(a, b, preferred_element_type=F32)


def _dot_nt(a, b):
    return lax.dot_general(a, b, (((1,), (1,)), ((), ())), preferred_element_type=F32)


def _dot3(a, b):
    ah, al = _split_bf16(a)
    bh, bl = _split_bf16(b)
    return _dot(ah, bh) + _dot(ah, bl) + _dot(al, bh)


def _rms(x, g):
    return x * lax.rsqrt(jnp.mean(x * x, axis=-1, keepdims=True) + EPS) * g


def _mod_row(i, ctx_tiles=CTX_TILES, tiles_per_seq=LAT_TILES_PER_SEQ):
    return jnp.where(i < ctx_tiles, 0, 1 + (i - ctx_tiles) // tiles_per_seq)


LANES = 128
ROW_TILES = D_MODEL // LANES


def _store_token_major(ref, x):
    n = x.shape[0]
    for c in range(ROW_TILES):
        ref[pl.ds(c, n, stride=ROW_TILES), :] = x[:, c * LANES:(c + 1) * LANES]


def _load_token_major(ref):
    n = ref.shape[0] // ROW_TILES
    return jnp.concatenate(
        [ref[pl.ds(c, n, stride=ROW_TILES), :] for c in range(ROW_TILES)], axis=1)


def _ctx_rows(width, tile=TM):
    last = N_CTX // tile - 1
    return pl.BlockSpec((tile, width), lambda i: (jnp.minimum(i, last), 0))


def _lat_rows(width, tile=TM):
    first = N_CTX // tile
    return pl.BlockSpec((tile, width), lambda i: (jnp.maximum(i - first, 0), 0))


MOD_TN = 1536


def _mod_kernel(cond_ref, w_ref, b_ref, o_ref):
    c = cond_ref[...]
    s = c * _sigmoid(c)
    o_ref[0] = _dot3(s, w_ref[0]) + b_ref[0]


def _modulation(cond8, w_mod, b_mod):
    L = w_mod.shape[0]
    return pl.pallas_call(
        _mod_kernel,
        name="modulation",
        grid=(L, 6 * D_MODEL // MOD_TN),
        in_specs=[
            pl.BlockSpec((MOD_ROWS, D_MODEL), lambda l, j: (0, 0)),
            pl.BlockSpec((1, D_MODEL, MOD_TN), lambda l, j: (l, 0, j)),
            pl.BlockSpec((1, 1, MOD_TN), lambda l, j: (l, 0, j)),
        ],
        out_specs=pl.BlockSpec((1, MOD_ROWS, MOD_TN), lambda l, j: (l, 0, j)),
        out_shape=jax.ShapeDtypeStruct((L, MOD_ROWS, 6 * D_MODEL), F32),
        compiler_params=pltpu.CompilerParams(
            dimension_semantics=("arbitrary", "arbitrary"), vmem_limit_bytes=VMEM_LIMIT),
    )(cond8, w_mod, b_mod.reshape(L, 1, 6 * D_MODEL))


def _group_mean(sq, gmat):
    hi, lo = _split_bf16(sq)
    return _dot(hi, gmat) + _dot(lo, gmat)


def _swap_halves(x, half):
    w = x.shape[-1]
    lane = lax.broadcasted_iota(jnp.int32, x.shape, 1)
    first = (lane % (2 * half)) < half
    return jnp.where(first, pltpu.roll(x, w - half, 1), pltpu.roll(x, half, 1))


IN_TM = 512
IN_CTX_TILES = N_CTX // IN_TM
IN_LAT_PER_SEQ = DEC_SEQ // IN_TM


def _inproj_kernel(xc_ref, xl_ref, mod_ref, gpre_ref, w_ref, gq_ref, gk_ref, gmat_ref,
                   cos_a_ref, sin_a_ref, cos_b_ref, sin_b_ref, z_ref, h_ref):
    i = pl.program_id(0)
    m = mod_ref[0]
    sh1 = m[:, 0:D_MODEL]
    sc1 = m[:, D_MODEL:2 * D_MODEL]
    x = jnp.where(i < IN_CTX_TILES, xc_ref[...], xl_ref[...])
    h = _rms(x, gpre_ref[...]) * (1.0 + sc1) + sh1
    hb = h.astype(BF16)
    h_ref[...] = hb
    z = _dot(hb, w_ref[0])

    gmat = gmat_ref[...]
    qa = z[:, COL_QA:COL_KA]
    ka = z[:, COL_KA:COL_VA]
    qa = qa * lax.rsqrt(_group_mean(qa * qa, gmat) + EPS) * gq_ref[...]
    ka = ka * lax.rsqrt(_group_mean(ka * ka, gmat[0:128, 0:128]) + EPS) * gk_ref[...]
    qb = z[:, COL_QB:COL_KB]
    kb = z[:, COL_KB:COL_VB]
    z_ref[:, COL_VA:COL_QB] = z[:, COL_VA:COL_QB]
    z_ref[:, COL_VB:N_SMALL] = z[:, COL_VB:N_SMALL]

    @pl.when(i < IN_CTX_TILES)
    def _():
        z_ref[:, COL_QA:COL_KA] = qa
        z_ref[:, COL_KA:COL_VA] = ka
        z_ref[:, COL_QB:COL_KB] = qb
        z_ref[:, COL_KB:COL_VB] = kb

    @pl.when(i >= IN_CTX_TILES)
    def _():
        cos_a = cos_a_ref[...]
        sin_a = sin_a_ref[...]
        cos_b = cos_b_ref[...]
        sin_b = sin_b_ref[...]
        ha, hb_ = A_HEAD_DIM // 2, B_HALF_DIM // 2
        z_ref[:, COL_QA:COL_KA] = qa * cos_a + _swap_halves(qa, ha) * sin_a
        z_ref[:, COL_KA:COL_VA] = ka * cos_a[:, 0:128] + _swap_halves(ka, ha) * sin_a[:, 0:128]
        z_ref[:, COL_QB:COL_KB] = qb * cos_b + _swap_halves(qb, hb_) * sin_b
        z_ref[:, COL_KB:COL_VB] = kb * cos_b + _swap_halves(kb, hb_) * sin_b


def _rope_block(i):
    return jnp.where(i < IN_CTX_TILES, 0, (i - IN_CTX_TILES) % IN_LAT_PER_SEQ)


def _in_projection(x_ctx, x_lat, mods, l, g_pre, w_in_bf, gq, gk, gmat, rope):
    const = lambda i: (0, 0)
    rope_spec = pl.BlockSpec((IN_TM, 256), lambda i: (_rope_block(i), 0))
    mod_row = lambda i: _mod_row(i, IN_CTX_TILES, IN_LAT_PER_SEQ)
    return pl.pallas_call(
        _inproj_kernel,
        name="in_projection",
        grid=(N_TOK // IN_TM,),
        in_specs=[
            _ctx_rows(D_MODEL, IN_TM), _lat_rows(D_MODEL, IN_TM),
            pl.BlockSpec((1, 1, 6 * D_MODEL), lambda i: (l * MOD_ROWS + mod_row(i), 0, 0)),
            pl.BlockSpec((1, D_MODEL), const),
            pl.BlockSpec((1, D_MODEL, N_SMALL), lambda i: (l, 0, 0)),
            pl.BlockSpec((1, 256), const),
            pl.BlockSpec((1, 128), const),
            pl.BlockSpec((256, 256), const),
            rope_spec, rope_spec, rope_spec, rope_spec,
        ],
        out_specs=[
            pl.BlockSpec((IN_TM, N_SMALL), lambda i: (i, 0)),
            pl.BlockSpec((IN_TM, D_MODEL), lambda i: (i, 0)),
        ],
        out_shape=[
            jax.ShapeDtypeStruct((N_TOK, N_SMALL), F32),
            jax.ShapeDtypeStruct((N_TOK, D_MODEL), BF16),
        ],
        compiler_params=pltpu.CompilerParams(
            dimension_semantics=("arbitrary",), vmem_limit_bytes=VMEM_LIMIT),
    )(x_ctx, x_lat, mods, g_pre, w_in_bf, gq, gk, gmat, *rope)


LOG2E = 1.4426950408889634
HALF_LANES = 64


def _attend(q, ks, vexts):
    ss = [_dot_nt(q, k) for k in ks]
    m = ss[0].max(axis=-1, keepdims=True)
    for s in ss[1:]:
        m = jnp.maximum(m, s.max(axis=-1, keepdims=True))
    o = None
    for s, v in zip(ss, vexts):
        t = _dot(jnp.exp2(s - m).astype(BF16), v)
        o = t if o is None else o + t
    return o


def _value_ext(v128, upper):
    lane = lax.broadcasted_iota(jnp.int32, v128.shape, 1)
    keep = (lane >= HALF_LANES) if upper else (lane < HALF_LANES)
    return jnp.where(keep, v128, 1.0).astype(BF16)


def _normalised(o, upper):
    if upper:
        return o[:, HALF_LANES:] / o[:, 0:1]
    return o[:, :HALF_LANES] / o[:, HALF_LANES:HALF_LANES + 1]


def _attn_kernel(*refs, has_cache, lambda_init, after_unit=None):
    if has_cache:
        (qa_ref, ka_ref, va_ref, qb_ref, kb_ref, vb_ref,
         cka_ref, cva_ref, ckb_ref, cvb_ref, lam_ref, subg_ref, oa_ref, ob_ref) = refs
    else:
        (qa_ref, ka_ref, va_ref, qb_ref, kb_ref, vb_ref,
         lam_ref, subg_ref, oa_ref, ob_ref) = refs

    def cols(ref, lo, hi):
        return ref[:, lo:hi].astype(BF16)

    def ccols(ref, lo, hi):
        return ref[0, 0, :, lo:hi].astype(BF16)

    scale_a = A_HEAD_DIM ** -0.5 * LOG2E
    group = A_HEADS // A_KV_HEADS
    for g in range(A_KV_HEADS):
        lo, hi = g * A_HEAD_DIM, (g + 1) * A_HEAD_DIM
        upper = g % 2 == 1
        ks = [cols(ka_ref, lo, hi)]
        vexts = [_value_ext(va_ref[...], upper)]
        if has_cache:
            ks.append(ccols(cka_ref, lo, hi))
            vexts.append(_value_ext(cva_ref[0, 0], upper))
        for hd in range(g * group, (g + 1) * group):
            q = (qa_ref[:, hd * A_HEAD_DIM:(hd + 1) * A_HEAD_DIM] * scale_a).astype(BF16)
            o = _attend(q, ks, vexts)
            oa_ref[:, hd * A_HEAD_DIM:(hd + 1) * A_HEAD_DIM] = _normalised(o, upper)
        if after_unit is not None:
            after_unit(g)

    bl = lam_ref[...]
    lam = (jnp.exp(jnp.sum(bl[0:1] * bl[1:2], axis=-1, keepdims=True))
           - jnp.exp(jnp.sum(bl[2:3] * bl[3:4], axis=-1, keepdims=True)) + lambda_init)
    scale_b = B_HALF_DIM ** -0.5 * LOG2E
    subg = subg_ref[...]
    for hd in range(B_HEADS):
        base = hd * 2 * B_HALF_DIM
        vlo, vhi = hd * B_V_DIM, (hd + 1) * B_V_DIM
        blk = (hd // 2) * 2 * B_V_DIM
        upper = hd % 2 == 1
        vexts = [_value_ext(vb_ref[:, blk:blk + 2 * B_V_DIM], upper)]
        if has_cache:
            vexts.append(_value_ext(cvb_ref[0, 0, :, blk:blk + 2 * B_V_DIM], upper))
        outs = []
        for c in range(2):
            lo, hi = base + c * B_HALF_DIM, base + (c + 1) * B_HALF_DIM
            q = (qb_ref[:, lo:hi] * scale_b).astype(BF16)
            ks = [cols(kb_ref, lo, hi)]
            if has_cache:
                ks.append(ccols(ckb_ref, lo, hi))
            outs.append(_normalised(_attend(q, ks, vexts), upper))
        o = outs[0] - lam * outs[1]
        o = _rms(o, subg) * (1.0 - lambda_init)
        ob_ref[:, vlo:vhi] = o
        if after_unit is not None:
            after_unit(A_KV_HEADS + hd)


SUBLANES = 8


def _row_shifter(win):
    span = win.shape[0] - SUBLANES
    shifted = {}

    def rows(off):
        s = off % SUBLANES
        if s not in shifted:
            shifted[s] = win[s:s + span, :]
        base = off - s
        return shifted[s][base:base + CONV_CHUNK, :]

    return rows


def _conv_fill(a_ref, b_ref, d_ref, hpad, upad, seq_len):
    zeros = jnp.zeros((CONV_PAD, BRANCH_W), F32)
    hpad[0:CONV_PAD, :] = zeros
    hpad[seq_len + CONV_PAD:seq_len + 2 * CONV_PAD, :] = zeros
    upad[0:CONV_PAD, :] = zeros
    upad[seq_len + CONV_PAD:seq_len + 2 * CONV_PAD, :] = zeros

    def fill(c, carry):
        r = pl.multiple_of(c * CONV_CHUNK, CONV_CHUNK)
        a = a_ref[pl.ds(r, CONV_CHUNK), :]
        b = b_ref[pl.ds(r, CONV_CHUNK), :]
        hpad[pl.ds(r + CONV_PAD, CONV_CHUNK), :] = a * _sigmoid(b)
        upad[pl.ds(r + CONV_PAD, CONV_CHUNK), :] = d_ref[pl.ds(r, CONV_CHUNK), :]
        return carry

    lax.fori_loop(0, seq_len // CONV_CHUNK, fill, 0)


def _conv_chunk(r, out_r, conv_refs, hpad, upad, oc_ref, od_ref, seq_len):
    dww_ref, dwb_ref, lng_ref, lnb_ref, wbd_ref, dsc_ref = conv_refs
    lane = lax.broadcasted_iota(jnp.int32, (CONV_CHUNK, BRANCH_W), 1)
    half = C_CONV_WIDTH // 2

    hrows = _row_shifter(hpad[pl.ds(r, CONV_CHUNK + 2 * CONV_PAD), :])
    acc = jnp.zeros((CONV_CHUNK, BRANCH_W), F32)
    for k in range(C_CONV_WIDTH):
        acc = acc + dww_ref[k:k + 1, :] * hrows(CONV_PAD - half + k)
    acc = acc + dwb_ref[...]
    mu = jnp.mean(acc, axis=-1, keepdims=True)
    cen = acc - mu
    var = jnp.mean(cen * cen, axis=-1, keepdims=True)
    y = cen * lax.rsqrt(var + EPS) * lng_ref[...] + lnb_ref[...]
    oc_ref[pl.ds(out_r, CONV_CHUNK), :] = y * _sigmoid(y)

    urows = _row_shifter(upad[pl.ds(r, CONV_CHUNK + 2 * CONV_PAD), :])

    def ld(d):
        return urows(CONV_PAD + d)

    u = ld(0)
    sums = {}
    s = u + ld(-1)
    sums[2] = s
    s = s + ld(-2) + ld(1)
    sums[4] = s
    s = s + ld(-4) + ld(-3) + ld(2) + ld(3)
    sums[8] = s
    s = s + ld(-8) + ld(-7) + ld(-6) + ld(-5) + ld(4) + ld(5) + ld(6) + ld(7)
    sums[16] = s
    t = r + lax.broadcasted_iota(jnp.int32, (CONV_CHUNK, 1), 0)
    pooled = None
    for g, w in reversed(list(enumerate(POOL_WINDOWS))):
        lo = jnp.maximum(t - w // 2, 0)
        hi = jnp.minimum(t + (w - 1 - w // 2), seq_len - 1)
        mean = sums[w] / (hi - lo + 1).astype(F32)
        pooled = mean if pooled is None else jnp.where(lane < (g + 1) * D_GROUP_W, mean, pooled)
    dlt = (pooled - u).astype(BF16)
    od_ref[pl.ds(out_r, CONV_CHUNK), :] = _dot(dlt, wbd_ref[...]) * dsc_ref[...]


N_CONV_REFS = 6
N_ATTN_UNITS = A_KV_HEADS + B_HEADS


def _mixers_kernel(*refs, has_cache, lambda_init, seq_len, step_rows):
    n_attn = 12 if has_cache else 8
    attn_in = refs[:n_attn]
    a_ref, b_ref, d_ref = refs[n_attn:n_attn + 3]
    conv_refs = refs[n_attn + 3:n_attn + 3 + N_CONV_REFS]
    oa_ref, ob_ref, oc_ref, od_ref, hpad, upad = refs[n_attn + 3 + N_CONV_REFS:]

    if seq_len == step_rows:
        _conv_fill(a_ref, b_ref, d_ref, hpad, upad, seq_len)
        row0 = 0
    else:
        t = pl.program_id(1)

        @pl.when(t == 0)
        def _():
            _conv_fill(a_ref, b_ref, d_ref, hpad, upad, seq_len)
        row0 = t * step_rows

    n_chunks = step_rows // CONV_CHUNK

    def conv(c):
        r = row0 + c * CONV_CHUNK
        if not isinstance(r, int):
            r = pl.multiple_of(r, CONV_CHUNK)
        _conv_chunk(r, c * CONV_CHUNK, conv_refs, hpad, upad, oc_ref, od_ref, seq_len)

    def after_unit(k):
        for c in range(k * n_chunks // N_ATTN_UNITS, (k + 1) * n_chunks // N_ATTN_UNITS):
            conv(c)

    _attn_kernel(*attn_in, oa_ref, ob_ref, has_cache=has_cache, lambda_init=lambda_init,
                 after_unit=after_unit)


def _conv_specs(index_map):
    const = lambda shape: pl.BlockSpec(shape, index_map)
    return [const((C_CONV_WIDTH, BRANCH_W)), const((1, BRANCH_W)), const((1, BRANCH_W)),
            const((1, BRANCH_W)), const((BRANCH_W, BRANCH_W)), const((1, BRANCH_W))]


def _conv_scratch(seq_len):
    return [pltpu.VMEM((seq_len + 2 * CONV_PAD, BRANCH_W), F32)] * 2


def _mixers_ctx(z, b_lambda, subg, lambda_init, conv_args):
    def zspec(width, col):
        return pl.BlockSpec((SEQ, width), lambda b: (b, col // width))
    out_spec = pl.BlockSpec((SEQ, BRANCH_W), lambda b: (b, 0))
    return pl.pallas_call(
        functools.partial(_mixers_kernel, has_cache=False, lambda_init=lambda_init,
                          seq_len=SEQ, step_rows=SEQ),
        name="mixers_ctx",
        grid=(BATCH,),
        in_specs=[
            zspec(256, COL_QA), zspec(128, COL_KA), zspec(128, COL_VA),
            zspec(256, COL_QB), zspec(256, COL_KB), zspec(256, COL_VB),
            pl.BlockSpec((4, B_HALF_DIM), lambda b: (0, 0)),
            pl.BlockSpec((1, B_V_DIM), lambda b: (0, 0)),
            zspec(256, COL_ZC), zspec(256, COL_ZC + BRANCH_W), zspec(256, COL_ZD),
            *_conv_specs(lambda b: (0, 0)),
        ],
        out_specs=[out_spec] * 4,
        out_shape=[jax.ShapeDtypeStruct((N_CTX, BRANCH_W), F32)] * 4,
        scratch_shapes=_conv_scratch(SEQ),
        compiler_params=pltpu.CompilerParams(
            dimension_semantics=("arbitrary",), vmem_limit_bytes=VMEM_LIMIT),
    )(z, z, z, z, z, z, b_lambda, subg, z, z, z, *conv_args)


def _mixers_lat(z, caches, l, b_lambda, subg, lambda_init, conv_args):
    q_tiles = DEC_SEQ // TQ
    seq0 = N_CTX // DEC_SEQ

    def qspec(col):
        return pl.BlockSpec((TQ, 256), lambda b, t: (N_CTX // TQ + b * q_tiles + t, col // 256))

    def kspec(width, col):
        return pl.BlockSpec((DEC_SEQ, width), lambda b, t: (seq0 + b, col // width))

    def cspec(width):
        return pl.BlockSpec((1, 1, PAST_LEN, width), lambda b, t: (b, l, 0, 0))

    out_spec = pl.BlockSpec((TQ, BRANCH_W), lambda b, t: (b * q_tiles + t, 0))
    cka, cva, ckb, cvb = caches
    return pl.pallas_call(
        functools.partial(_mixers_kernel, has_cache=True, lambda_init=lambda_init,
                          seq_len=DEC_SEQ, step_rows=TQ),
        name="mixers_lat",
        grid=(DEC_BATCH, q_tiles),
        in_specs=[
            qspec(COL_QA), kspec(128, COL_KA), kspec(128, COL_VA),
            qspec(COL_QB), kspec(256, COL_KB), kspec(256, COL_VB),
            cspec(128), cspec(128), cspec(256), cspec(256),
            pl.BlockSpec((4, B_HALF_DIM), lambda b, t: (0, 0)),
            pl.BlockSpec((1, B_V_DIM), lambda b, t: (0, 0)),
            kspec(256, COL_ZC), kspec(256, COL_ZC + BRANCH_W), kspec(256, COL_ZD),
            *_conv_specs(lambda b, t: (0, 0)),
        ],
        out_specs=[out_spec] * 4,
        out_shape=[jax.ShapeDtypeStruct((N_LAT, BRANCH_W), F32)] * 4,
        scratch_shapes=_conv_scratch(DEC_SEQ),
        compiler_params=pltpu.CompilerParams(
            dimension_semantics=("arbitrary", "arbitrary"), vmem_limit_bytes=VMEM_LIMIT),
    )(z, z, z, z, z, z, cka, cva, ckb, cvb, b_lambda, subg, z, z, z, *conv_args)


MIX_TM = 512
MIX_TILES = N_TOK // MIX_TM
MIX_CTX_TILES = N_CTX // MIX_TM
MIX_LAT_PER_SEQ = DEC_SEQ // MIX_TM
ROUTE_ROWS = 8


def _mix_kernel(xc_ref, xl_ref, h_ref, oa_c, ob_c, oc_c, od_c, oa_l, ob_l, oc_l, od_l,
                mod_ref, wg01_ref, wg23_ref, wbr_ref, wout_ref, gpost_ref, gpre2_ref, rwt_ref, rb_ref,
                x1_ref, h2_ref, eid_ref, rank_ref, wts_ref, cnt_ref, carry):
    i = pl.program_id(0)
    is_ctx = i < MIX_CTX_TILES

    @pl.when(i == 0)
    def _():
        carry[...] = jnp.zeros_like(carry)

    m = mod_ref[0]
    g1 = m[:, 2 * D_MODEL:3 * D_MODEL]
    sh2 = m[:, 3 * D_MODEL:4 * D_MODEL]
    sc2 = m[:, 4 * D_MODEL:5 * D_MODEL]

    hb = h_ref[...]
    merged = None
    for n, (c_ref, l_ref) in enumerate(((oa_c, oa_l), (ob_c, ob_l), (oc_c, oc_l), (od_c, od_l))):
        wg_ref = wg01_ref if n < 2 else wg23_ref
        gate = _sigmoid(_dot(hb, wg_ref[0, :, (n % 2) * D_MODEL:(n % 2 + 1) * D_MODEL]))
        o = jnp.where(is_ctx, c_ref[...], l_ref[...])
        br = _dot(o.astype(BF16), wbr_ref[0, n])
        merged = gate * br if merged is None else merged + gate * br
    y = _dot(merged.astype(BF16), wout_ref[0])
    x = jnp.where(is_ctx, xc_ref[...], xl_ref[...])
    x1 = x + g1 * _rms(y, gpost_ref[...])
    x1_ref[...] = x1
    h2 = _rms(x1, gpre2_ref[...]) * (1.0 + sc2) + sh2
    _store_token_major(h2_ref, h2)

    rh, rl = _split_bf16(rwt_ref[...])
    hh, hl = _split_bf16(h2)
    logits = _dot_nt(rh, hh) + _dot_nt(rh, hl) + _dot_nt(rl, hh) + rb_ref[...]

    iota_e = lax.broadcasted_iota(jnp.int32, (N_EXPERTS, MIX_TM), 0)
    rem = logits
    vals, idxs = [], []
    for _ in range(TOP_K):
        mx = jnp.max(rem, axis=0, keepdims=True)
        idx = jnp.min(jnp.where(rem == mx, iota_e, N_EXPERTS), axis=0, keepdims=True)
        vals.append(mx)
        idxs.append(idx)
        rem = jnp.where(iota_e == idx, -jnp.inf, rem)
    exps = [jnp.exp(v - vals[0]) for v in vals]
    den = exps[0]
    for e in exps[1:]:
        den = den + e

    sel = [iota_e == idx for idx in idxs]
    member = jnp.zeros((N_EXPERTS, MIX_TM), F32)
    for s in sel:
        member = member + jnp.where(s, 1.0, 0.0)
    row = lax.broadcasted_iota(jnp.int32, (MIX_TM, MIX_TM), 0)
    col = lax.broadcasted_iota(jnp.int32, (MIX_TM, MIX_TM), 1)
    earlier = jnp.where(row < col, 1.0, 0.0).astype(BF16)
    seen = _dot(member.astype(BF16), earlier) + carry[...]
    carry[...] = carry[...] + jnp.sum(member, axis=1, keepdims=True)
    cnt_ref[...] = carry[...]

    sub = lax.broadcasted_iota(jnp.int32, (ROUTE_ROWS, MIX_TM), 0)
    eid = jnp.zeros((ROUTE_ROWS, MIX_TM), jnp.int32)
    rank = jnp.zeros((ROUTE_ROWS, MIX_TM), jnp.int32)
    wts = jnp.zeros((ROUTE_ROWS, MIX_TM), F32)
    for k in range(TOP_K):
        rk = jnp.sum(jnp.where(sel[k], seen, 0.0), axis=0, keepdims=True).astype(jnp.int32)
        eid = jnp.where(sub == k, idxs[k], eid)
        rank = jnp.where(sub == k, rk, rank)
        wts = jnp.where(sub == k, exps[k] / den, wts)
    eid_ref[...] = eid
    rank_ref[...] = rank
    wts_ref[...] = wts


def _mix(x_ctx, x_lat, h, o_ctx, o_lat, mods, l, w_in_bf, w_branch_bf, w_out_bf,
         g_post, g_pre2, r_wt, r_b):
    const2 = lambda i: (0, 0)
    tile = lambda width: pl.BlockSpec((MIX_TM, width), lambda i: (i, 0))
    route = pl.BlockSpec((ROUTE_ROWS, MIX_TM), lambda i: (0, i))
    ctx_tile = _ctx_rows(BRANCH_W, MIX_TM)
    lat_tile = _lat_rows(BRANCH_W, MIX_TM)
    mod_row = lambda i: _mod_row(i, MIX_CTX_TILES, MIX_LAT_PER_SEQ)
    single = pl.Buffered(1)
    return pl.pallas_call(
        _mix_kernel,
        name="mix_router",
        grid=(MIX_TILES,),
        in_specs=[
            _ctx_rows(D_MODEL, MIX_TM), _lat_rows(D_MODEL, MIX_TM), tile(D_MODEL),
            ctx_tile, ctx_tile, ctx_tile, ctx_tile, lat_tile, lat_tile, lat_tile, lat_tile,
            pl.BlockSpec((1, 1, 6 * D_MODEL), lambda i: (l * MOD_ROWS + mod_row(i), 0, 0)),
            pl.BlockSpec((1, D_MODEL, N_SMALL), lambda i: (l, 0, 1), pipeline_mode=single),
            pl.BlockSpec((1, D_MODEL, N_SMALL), lambda i: (l, 0, 2), pipeline_mode=single),
            pl.BlockSpec((1, N_BRANCH, BRANCH_W, D_MODEL), lambda i: (l, 0, 0, 0),
                         pipeline_mode=single),
            pl.BlockSpec((1, D_MODEL, D_MODEL), lambda i: (l, 0, 0), pipeline_mode=single),
            pl.BlockSpec((1, D_MODEL), const2),
            pl.BlockSpec((1, D_MODEL), const2),
            pl.BlockSpec((N_EXPERTS, D_MODEL), const2),
            pl.BlockSpec((N_EXPERTS, 1), const2),
        ],
        out_specs=[
            tile(D_MODEL), pl.BlockSpec((MIX_TM * ROW_TILES, LANES), lambda i: (i, 0)),
            route, route, route,
            pl.BlockSpec((N_EXPERTS, 1), const2),
        ],
        out_shape=[
            jax.ShapeDtypeStruct((N_TOK, D_MODEL), F32),
            jax.ShapeDtypeStruct((N_TOK * ROW_TILES, LANES), F32),
            jax.ShapeDtypeStruct((ROUTE_ROWS, N_TOK), jnp.int32),
            jax.ShapeDtypeStruct((ROUTE_ROWS, N_TOK), jnp.int32),
            jax.ShapeDtypeStruct((ROUTE_ROWS, N_TOK), F32),
            jax.ShapeDtypeStruct((N_EXPERTS, 1), F32),
        ],
        scratch_shapes=[pltpu.VMEM((N_EXPERTS, 1), F32)],
        compiler_params=pltpu.CompilerParams(
            dimension_semantics=("arbitrary",), vmem_limit_bytes=VMEM_LIMIT),
    )(x_ctx, x_lat, h, *o_ctx, *o_lat, mods, w_in_bf, w_in_bf, w_branch_bf, w_out_bf,
      g_post, g_pre2, r_wt, r_b)


def _token_rows(t, n=1):
    return pl.ds(pl.multiple_of(t * ROW_TILES, ROW_TILES), n * ROW_TILES)


def _token_copy(src_ref, src_tok, dst_ref, dst_tok, sem):
    return pltpu.make_async_copy(src_ref.at[_token_rows(src_tok), :], dst_ref.at[_token_rows(dst_tok), :], sem)


DISPATCH_ROWS = 512
DISPATCH_STEPS = N_TOK // DISPATCH_ROWS
DMA_UNROLL = 8


def _dispatch_kernel(starts_ref, padded_ref, pos_ref, h2_ref, xs_ref, zbuf, sem, zsem, tsem):
    i = pl.program_id(0)

    def pad_copy(e):
        last = pl.multiple_of(starts_ref[e] + padded_ref[e] - TE, TE)
        return pltpu.make_async_copy(zbuf, xs_ref.at[_token_rows(last, TE), :], zsem)

    first_tail = (starts_ref[N_EXPERTS - 1] + padded_ref[N_EXPERTS - 1]) // TE

    def tail_copy(j):
        return pltpu.make_async_copy(zbuf, xs_ref.at[_token_rows(j * TE, TE), :], tsem)

    def tail_start(j, carry):
        tail_copy(j).start()
        return carry

    def tail_wait(j, carry):
        tail_copy(j).wait()
        return carry

    @pl.when(i == 0)
    def _():
        zbuf[...] = jnp.zeros_like(zbuf)
        for e in range(N_EXPERTS):
            @pl.when(padded_ref[e] > 0)
            def _():
                pad_copy(e).start()
        lax.fori_loop(first_tail, E_TILES, tail_start, 0)
        for e in range(N_EXPERTS):
            @pl.when(padded_ref[e] > 0)
            def _():
                pad_copy(e).wait()

    @pl.when(i == DISPATCH_STEPS - 1)
    def _():
        lax.fori_loop(first_tail, E_TILES, tail_wait, 0)

    def issue(t, carry):
        for k in range(TOP_K):
            _token_copy(h2_ref, t, xs_ref, pos_ref[t * TOP_K + k], sem).start(priority=k % 2)
        return carry

    lax.fori_loop(0, DISPATCH_ROWS, issue, 0, unroll=DMA_UNROLL)

    for _ in range(TOP_K):
        pltpu.make_async_copy(h2_ref, xs_ref.at[_token_rows(0, DISPATCH_ROWS), :], sem).wait()


def _dispatch(starts, padded, pos, h2):
    grid_spec = pltpu.PrefetchScalarGridSpec(
        num_scalar_prefetch=2,
        grid=(DISPATCH_STEPS,),
        in_specs=[
            pl.BlockSpec((DISPATCH_ROWS * TOP_K,), lambda i, st, pd: (i,), memory_space=pltpu.SMEM),
            pl.BlockSpec((DISPATCH_ROWS * ROW_TILES, LANES), lambda i, st, pd: (i, 0)),
        ],
        out_specs=pl.BlockSpec(memory_space=pl.ANY),
        scratch_shapes=[pltpu.VMEM((TE * ROW_TILES, LANES), F32),
                        pltpu.SemaphoreType.DMA(()), pltpu.SemaphoreType.DMA(()),
                        pltpu.SemaphoreType.DMA(())],
    )
    return pl.pallas_call(
        _dispatch_kernel,
        name="dispatch",
        grid_spec=grid_spec,
        out_shape=jax.ShapeDtypeStruct((P_MAX * ROW_TILES, LANES), F32),
        compiler_params=pltpu.CompilerParams(
            dimension_semantics=("arbitrary",), vmem_limit_bytes=VMEM_LIMIT),
    )(starts, padded, pos, h2)


W_CAST_ROWS = 128


def _moe_kernel(te_ref, rows_ref, slot_ref, next_ref, x_ref, w1_hbm, b1_ref, w2_hbm, b2_ref, y_ref,
                w1f, w2f, w1b, w2b, sem, *, layer):
    j = pl.program_id(0)
    e = te_ref[j]
    prev = te_ref[jnp.maximum(j - 1, 0)]
    valid = rows_ref[j] > 0
    first = jnp.logical_and(valid, jnp.logical_or(j == 0, e != prev))
    s = slot_ref[j]

    def weight_copies(ex, sl):
        return (pltpu.make_async_copy(w1_hbm.at[layer, ex], w1f.at[sl], sem.at[0, sl]),
                pltpu.make_async_copy(w2_hbm.at[layer, ex], w2f.at[sl], sem.at[1, sl]))

    @pl.when(j == 0)
    def _():
        for cp in weight_copies(e, s):
            cp.start()

    @pl.when(first)
    def _():
        for cp in weight_copies(e, s):
            cp.wait()
        nxt = next_ref[j]

        @pl.when(nxt >= 0)
        def _():
            for cp in weight_copies(nxt, 1 - s):
                cp.start(priority=1)

        def cast(c, carry):
            r = pl.multiple_of(c * W_CAST_ROWS, W_CAST_ROWS)
            w1b[pl.ds(r, W_CAST_ROWS), :] = w1f[s, pl.ds(r, W_CAST_ROWS), :].astype(BF16)
            w2b[pl.ds(r, W_CAST_ROWS), :] = w2f[s, pl.ds(r, W_CAST_ROWS), :].astype(BF16)
            return carry
        lax.fori_loop(0, D_MODEL // W_CAST_ROWS, cast, 0)

    @pl.when(valid)
    def _():
        x = _load_token_major(x_ref).astype(BF16)
        u = _dot(x, w1b[...]) + b1_ref[0, 0]
        xg = jnp.minimum(u[:, :D_FF], SWIGLU_LIMIT)
        xl = jnp.clip(u[:, D_FF:], -SWIGLU_LIMIT, SWIGLU_LIMIT)
        act = (xl + 1.0) * (xg * _sigmoid(SWIGLU_ALPHA * xg))
        _store_token_major(y_ref, _dot(act.astype(BF16), w2b[...]) + b2_ref[0, 0])

    @pl.when(jnp.logical_not(valid))
    def _():
        y_ref[...] = jnp.zeros_like(y_ref)


def _expert_ffn(tile_expert, tile_rows, tile_slot, tile_next, xs, l, e_w1, e_b1, e_w2, e_b2):
    L = e_w1.shape[0]
    grid_spec = pltpu.PrefetchScalarGridSpec(
        num_scalar_prefetch=4,
        grid=(E_TILES,),
        in_specs=[
            pl.BlockSpec((TE * ROW_TILES, LANES),
                         lambda j, te, nr, sl, nx: (jnp.where(nr[j] > 0, j, 0), 0)),
            pl.BlockSpec(memory_space=pl.ANY),
            pl.BlockSpec((1, 1, 1, 2 * D_FF), lambda j, te, nr, sl, nx: (l, te[j], 0, 0)),
            pl.BlockSpec(memory_space=pl.ANY),
            pl.BlockSpec((1, 1, 1, D_MODEL), lambda j, te, nr, sl, nx: (l, te[j], 0, 0)),
        ],
        out_specs=pl.BlockSpec((TE * ROW_TILES, LANES), lambda j, te, nr, sl, nx: (j, 0)),
        scratch_shapes=[pltpu.VMEM((2, D_MODEL, 2 * D_FF), F32), pltpu.VMEM((2, D_FF, D_MODEL), F32),
                        pltpu.VMEM((D_MODEL, 2 * D_FF), BF16), pltpu.VMEM((D_FF, D_MODEL), BF16),
                        pltpu.SemaphoreType.DMA((2, 2))],
    )
    return pl.pallas_call(
        functools.partial(_moe_kernel, layer=l),
        name="expert_ffn",
        grid_spec=grid_spec,
        out_shape=jax.ShapeDtypeStruct((P_MAX * ROW_TILES, LANES), F32),
        compiler_params=pltpu.CompilerParams(
            dimension_semantics=("arbitrary",), vmem_limit_bytes=VMEM_LIMIT),
    )(tile_expert, tile_rows, tile_slot, tile_next, xs, e_w1,
      e_b1.reshape(L, N_EXPERTS, 1, 2 * D_FF), e_w2, e_b2.reshape(L, N_EXPERTS, 1, D_MODEL))


def _combine_kernel(pos_ref, pos_next_ref, ys_ref, wts_ref, x1_ref, mod_ref, gpost_ref,
                    out_c_ref, out_l_ref, buf, sem):
    i = pl.program_id(0)
    slot = i % 2

    def gather(p_ref, s):
        def issue(t, carry):
            for k in range(TOP_K):
                _token_copy(ys_ref, p_ref[t * TOP_K + k], buf.at[s, k], t, sem.at[s]).start(priority=1)
            return carry
        lax.fori_loop(0, TM, issue, 0, unroll=DMA_UNROLL)

    @pl.when(i == 0)
    def _():
        gather(pos_ref, 0)

    @pl.when(i + 1 < N_TILES)
    def _():
        gather(pos_next_ref, 1 - slot)

    for k in range(TOP_K):
        pltpu.make_async_copy(ys_ref.at[_token_rows(0, TM), :], buf.at[slot, k], sem.at[slot]).wait()

    w = wts_ref[...]
    y = w[:, 0:1] * _load_token_major(buf.at[slot, 0])
    for k in range(1, TOP_K):
        y = y + w[:, k:k + 1] * _load_token_major(buf.at[slot, k])
    g2 = mod_ref[0][:, 5 * D_MODEL:6 * D_MODEL]
    res = x1_ref[...] + g2 * _rms(y, gpost_ref[...])

    @pl.when(i < CTX_TILES)
    def _():
        out_c_ref[...] = res

    @pl.when(i >= CTX_TILES)
    def _():
        out_l_ref[...] = res


def _combine(pos, ys, wts, x1, mods, l, g_post2):
    return pl.pallas_call(
        _combine_kernel,
        name="combine",
        grid=(N_TILES,),
        in_specs=[
            pl.BlockSpec((TM * TOP_K,), lambda i: (i,), memory_space=pltpu.SMEM),
            pl.BlockSpec((TM * TOP_K,), lambda i: (jnp.minimum(i + 1, N_TILES - 1),),
                         memory_space=pltpu.SMEM),
            pl.BlockSpec(memory_space=pl.ANY),
            pl.BlockSpec((TM, TOP_K), lambda i: (i, 0)),
            pl.BlockSpec((TM, D_MODEL), lambda i: (i, 0)),
            pl.BlockSpec((1, 1, 6 * D_MODEL), lambda i: (l * MOD_ROWS + _mod_row(i), 0, 0)),
            pl.BlockSpec((1, D_MODEL), lambda i: (0, 0)),
        ],
        out_specs=[_ctx_rows(D_MODEL), _lat_rows(D_MODEL)],
        out_shape=[jax.ShapeDtypeStruct((N_CTX, D_MODEL), F32),
                   jax.ShapeDtypeStruct((N_LAT, D_MODEL), F32)],
        scratch_shapes=[pltpu.VMEM((2, TOP_K, TM * ROW_TILES, LANES), F32),
                        pltpu.SemaphoreType.DMA((2,))],
        compiler_params=pltpu.CompilerParams(
            dimension_semantics=("arbitrary",), vmem_limit_bytes=VMEM_LIMIT),
    )(pos, pos, ys, wts, x1, mods, g_post2)


def _rope_tables():
    t = np.arange(DEC_SEQ)
    r = (t // GRID_W).astype(np.float32)
    c = (t % GRID_W).astype(np.float32)

    def table(dim, reps):
        n_axis = dim // 4
        inv = (ROPE_THETA ** (-np.arange(n_axis, dtype=np.float32) / n_axis)).astype(np.float32)
        ang = np.concatenate([r[:, None] * inv, c[:, None] * inv], axis=-1).astype(np.float32)
        cos, sin = np.cos(ang), np.sin(ang)
        return (np.tile(np.concatenate([cos, cos], -1), (1, reps)),
                np.tile(np.concatenate([-sin, sin], -1), (1, reps)))

    cos_a, sin_a = table(A_HEAD_DIM, 256 // A_HEAD_DIM)
    cos_b, sin_b = table(B_HALF_DIM, 256 // B_HALF_DIM)
    return tuple(jnp.asarray(a, F32) for a in (cos_a, sin_a, cos_b, sin_b))


def _group_mean_matrix():
    idx = np.arange(256) // A_HEAD_DIM
    return jnp.asarray((idx[:, None] == idx[None, :]).astype(np.float32) / A_HEAD_DIM, BF16)


def _block_diag(w):
    out = jnp.zeros((BRANCH_W, BRANCH_W), w.dtype)
    for g in range(D_GROUPS):
        out = out.at[g * D_GROUP_W:(g + 1) * D_GROUP_W, g * D_GROUP_W:(g + 1) * D_GROUP_W].set(w[g])
    return out


def _routing_tables(eid, rank, cnt):
    cnt = cnt.reshape(N_EXPERTS).astype(jnp.int32)
    padded = ((cnt + TE - 1) // TE) * TE
    ends = jnp.cumsum(padded)
    starts = ends - padded
    experts = jnp.arange(N_EXPERTS, dtype=jnp.int32)
    start_of = jnp.sum(jnp.where(eid[:TOP_K, :, None] == experts, starts, 0), axis=-1)
    pos = (start_of + rank[:TOP_K]).T.reshape(N_SLOTS)
    tile_row = jnp.arange(E_TILES, dtype=jnp.int32) * TE
    tile_expert = jnp.sum((tile_row[:, None] >= ends[None, :]).astype(jnp.int32), axis=-1)
    tile_expert = jnp.minimum(tile_expert, N_EXPERTS - 1)
    used_end = jnp.sum(jnp.where(tile_expert[:, None] == experts, starts + cnt, 0), axis=-1)
    tile_rows = jnp.where(tile_row < ends[-1], jnp.clip(used_end - tile_row, 0, TE), 0)
    valid = tile_rows > 0
    changed = jnp.concatenate([jnp.ones((1,), bool), tile_expert[1:] != tile_expert[:-1]])
    tile_slot = (jnp.cumsum(jnp.logical_and(valid, changed).astype(jnp.int32)) + 1) % 2
    group_end = jnp.sum(jnp.where(tile_expert[:, None] == experts, ends, 0), axis=-1) // TE
    tiles = jnp.arange(E_TILES, dtype=jnp.int32)
    expert_at = jnp.where(valid, tile_expert, -1)
    tile_next = jnp.sum(jnp.where(group_end[:, None] == tiles, expert_at + 1, 0), axis=-1) - 1
    as_i32 = lambda a: a.astype(jnp.int32)
    return (as_i32(pos), as_i32(tile_expert), as_i32(tile_rows), as_i32(tile_slot), as_i32(tile_next),
            as_i32(starts), as_i32(padded))


def kernel(x_prompt, x_sample, cache_a_k, cache_a_v, cache_b_k, cache_b_v, c, c_ctx, w_mod, b_mod, g_pre1, g_post1, g_pre2, g_post2, w_in, a_q_gain, a_k_gain, b_lambda, b_subln_gain, c_dw_w, c_dw_b, c_ln_g, c_ln_b, d_w_group, d_scale, w_branch, w_out, r_w, r_b, e_w1, e_b1, e_w2, e_b2):
    x_ctx = x_prompt.reshape(N_CTX, D_MODEL)
    x_lat = x_sample.reshape(N_LAT, D_MODEL)
    cond8 = jnp.concatenate(
        [c_ctx[None, :], c, jnp.zeros((MOD_ROWS - 1 - DEC_BATCH, D_MODEL), F32)], axis=0)
    mods = _modulation(cond8, w_mod, b_mod).reshape(DEPTH * MOD_ROWS, 1, 6 * D_MODEL)

    rope = _rope_tables()
    gmat = _group_mean_matrix()
    caches = (
        cache_a_k.reshape(DEC_BATCH, DEPTH, PAST_LEN, A_KV_HEADS * A_HEAD_DIM),
        cache_a_v.reshape(DEC_BATCH, DEPTH, PAST_LEN, A_KV_HEADS * A_HEAD_DIM),
        cache_b_k.reshape(DEC_BATCH, DEPTH, PAST_LEN, B_HEADS * 2 * B_HALF_DIM),
        cache_b_v.reshape(DEC_BATCH, DEPTH, PAST_LEN, B_HEADS * B_V_DIM),
    )

    w_in_bf = w_in.astype(BF16)
    w_branch_bf = w_branch.astype(BF16)
    w_out_bf = w_out.astype(BF16)

    new_ak, new_av, new_bk, new_bv = [], [], [], []
    row = lambda v: v.reshape(1, -1)
    for l in range(DEPTH):
        lambda_init = 0.8 - 0.6 * math.exp(-0.3 * l)
        z, h = _in_projection(
            x_ctx, x_lat, mods, l, row(g_pre1[l]), w_in_bf,
            row(jnp.tile(a_q_gain[l], A_HEADS)), row(jnp.tile(a_k_gain[l], A_KV_HEADS)), gmat, rope)

        zc = z[:N_CTX]
        new_ak.append(zc[:, COL_KA:COL_VA].reshape(BATCH, SEQ, A_KV_HEADS, A_HEAD_DIM))
        new_av.append(zc[:, COL_VA:COL_QB].reshape(BATCH, SEQ, A_KV_HEADS, A_HEAD_DIM))
        new_bk.append(zc[:, COL_KB:COL_VB].reshape(BATCH, SEQ, B_HEADS, 2, B_HALF_DIM))
        new_bv.append(zc[:, COL_VB:COL_ZC].reshape(BATCH, SEQ, B_HEADS, B_V_DIM))

        subg = row(b_subln_gain[l])
        conv_args = (c_dw_w[l], row(c_dw_b[l]), row(c_ln_g[l]), row(c_ln_b[l]),
                     _block_diag(d_w_group[l]).astype(BF16), row(d_scale[l]))
        o_ctx = _mixers_ctx(z, b_lambda[l], subg, lambda_init, conv_args)
        o_lat = _mixers_lat(z, caches, l, b_lambda[l], subg, lambda_init, conv_args)

        x1, h2, eid, rank, wts, cnt = _mix(
            x_ctx, x_lat, h, o_ctx, o_lat, mods, l,
            w_in_bf, w_branch_bf, w_out_bf,
            row(g_post1[l]), row(g_pre2[l]), r_w[l].T, r_b[l].reshape(N_EXPERTS, 1))
        wts = wts[:TOP_K].T

        pos, tile_expert, tile_rows, tile_slot, tile_next, starts, padded = _routing_tables(
            eid, rank, cnt)
        xs = _dispatch(starts, padded, pos, h2)
        ys = _expert_ffn(tile_expert, tile_rows, tile_slot, tile_next, xs, l,
                         e_w1, e_b1, e_w2, e_b2)
        x_ctx, x_lat = _combine(pos, ys, wts, x1, mods, l, row(g_post2[l]))

    y_prompt = x_ctx.reshape(BATCH, SEQ, D_MODEL)
    y_sample = x_lat.reshape(DEC_BATCH, DEC_SEQ, D_MODEL)
    return (y_prompt, y_sample, jnp.stack(new_ak, axis=1), jnp.stack(new_av, axis=1),
            jnp.stack(new_bk, axis=1), jnp.stack(new_bv, axis=1))
```

```python
import functools
import math

import jax
import jax.numpy as jnp
import numpy as np
from jax import lax
from jax.experimental import pallas as pl
from jax.experimental.pallas import tpu as pltpu

F32 = jnp.float32
BF16 = jnp.bfloat16

D_MODEL = 1024
BATCH = 16
SEQ = 256
DEPTH = 2
DEC_BATCH = 4
DEC_SEQ = 2048
PAST_LEN = 256
GRID_W = 64
ROPE_THETA = 10000.0
N_BRANCH = 4
BRANCH_W = D_MODEL // 4
A_HEADS = 4
A_KV_HEADS = 2
A_HEAD_DIM = 64
B_HEADS = 4
B_HALF_DIM = 32
B_V_DIM = 2 * B_HALF_DIM
C_CONV_WIDTH = 31
D_GROUPS = 4
D_GROUP_W = BRANCH_W // D_GROUPS
POOL_WINDOWS = (2, 4, 8, 16)
N_EXPERTS = 32
TOP_K = 4
D_FF = D_MODEL
SWIGLU_LIMIT = 7.0
SWIGLU_ALPHA = 1.702
EPS = 1e-6

N_CTX = BATCH * SEQ
N_LAT = DEC_BATCH * DEC_SEQ
N_TOK = N_CTX + N_LAT

COL_QA, COL_KA, COL_VA = 0, 256, 384
COL_QB, COL_KB, COL_VB = 512, 768, 1024
COL_ZC, COL_ZD = 1280, 1792
N_SMALL = 2048
N_GATE = N_BRANCH * D_MODEL

TM = 256
N_TILES = N_TOK // TM
CTX_TILES = N_CTX // TM
LAT_TILES_PER_SEQ = DEC_SEQ // TM
MOD_ROWS = 8

TE = 512
N_SLOTS = N_TOK * TOP_K
P_MAX = N_SLOTS + N_EXPERTS * TE
E_TILES = P_MAX // TE

TQ = 512
CONV_PAD = 16
CONV_CHUNK = 128

VMEM_LIMIT = 56 * 1024 * 1024


def _sigmoid(x):
    return 1.0 / (1.0 + jnp.exp(-x))


def _split_bf16(a):
    hi = a.astype(BF16)
    lo = (a - hi.astype(F32)).astype(BF16)
    return hi, lo


def _dot(a, b):
    return jnp.dot(a, b, preferred_element_type=F32)


def _dot_nt(a, b):
    return lax.dot_general(a, b, (((1,), (1,)), ((), ())), preferred_element_type=F32)


def _dot3(a, b):
    ah, al = _split_bf16(a)
    bh, bl = _split_bf16(b)
    return _dot(ah, bh) + _dot(ah, bl) + _dot(al, bh)


def _rms(x, g):
    return x * lax.rsqrt(jnp.mean(x * x, axis=-1, keepdims=True) + EPS) * g


def _mod_row(i, ctx_tiles=CTX_TILES, tiles_per_seq=LAT_TILES_PER_SEQ):
    return jnp.where(i < ctx_tiles, 0, 1 + (i - ctx_tiles) // tiles_per_seq)


LANES = 128
ROW_TILES = D_MODEL // LANES


def _store_token_major(ref, x):
    n = x.shape[0]
    for c in range(ROW_TILES):
        ref[pl.ds(c, n, stride=ROW_TILES), :] = x[:, c * LANES:(c + 1) * LANES]


def _load_token_major(ref):
    n = ref.shape[0] // ROW_TILES
    return jnp.concatenate(
        [ref[pl.ds(c, n, stride=ROW_TILES), :] for c in range(ROW_TILES)], axis=1)


def _ctx_rows(width, tile=TM):
    last = N_CTX // tile - 1
    return pl.BlockSpec((tile, width), lambda i: (jnp.minimum(i, last), 0))


def _lat_rows(width, tile=TM):
    first = N_CTX // tile
    return pl.BlockSpec((tile, width), lambda i: (jnp.maximum(i - first, 0), 0))


MOD_TN = 1536


def _mod_kernel(cond_ref, w_ref, b_ref, o_ref):
    c = cond_ref[...]
    s = c * _sigmoid(c)
    o_ref[0] = _dot3(s, w_ref[0]) + b_ref[0]


def _modulation(cond8, w_mod, b_mod):
    L = w_mod.shape[0]
    return pl.pallas_call(
        _mod_kernel,
        name="modulation",
        grid=(L, 6 * D_MODEL // MOD_TN),
        in_specs=[
            pl.BlockSpec((MOD_ROWS, D_MODEL), lambda l, j: (0, 0)),
            pl.BlockSpec((1, D_MODEL, MOD_TN), lambda l, j: (l, 0, j)),
            pl.BlockSpec((1, 1, MOD_TN), lambda l, j: (l, 0, j)),
        ],
        out_specs=pl.BlockSpec((1, MOD_ROWS, MOD_TN), lambda l, j: (l, 0, j)),
        out_shape=jax.ShapeDtypeStruct((L, MOD_ROWS, 6 * D_MODEL), F32),
        compiler_params=pltpu.CompilerParams(
            dimension_semantics=("arbitrary", "arbitrary"), vmem_limit_bytes=VMEM_LIMIT),
    )(cond8, w_mod, b_mod.reshape(L, 1, 6 * D_MODEL))


def _group_mean(sq, gmat):
    hi, lo = _split_bf16(sq)
    return _dot(hi, gmat) + _dot(lo, gmat)


def _swap_halves(x, half):
    w = x.shape[-1]
    lane = lax.broadcasted_iota(jnp.int32, x.shape, 1)
    first = (lane % (2 * half)) < half
    return jnp.where(first, pltpu.roll(x, w - half, 1), pltpu.roll(x, half, 1))


IN_TM = 512
IN_CTX_TILES = N_CTX // IN_TM
IN_LAT_PER_SEQ = DEC_SEQ // IN_TM


def _inproj_kernel(xc_ref, xl_ref, mod_ref, gpre_ref, w_ref, gq_ref, gk_ref, gmat_ref,
                   cos_a_ref, sin_a_ref, cos_b_ref, sin_b_ref, z_ref, h_ref):
    i = pl.program_id(0)
    m = mod_ref[0]
    sh1 = m[:, 0:D_MODEL]
    sc1 = m[:, D_MODEL:2 * D_MODEL]
    x = jnp.where(i < IN_CTX_TILES, xc_ref[...], xl_ref[...])
    h = _rms(x, gpre_ref[...]) * (1.0 + sc1) + sh1
    hb = h.astype(BF16)
    h_ref[...] = hb
    z = _dot(hb, w_ref[0])

    gmat = gmat_ref[...]
    qa = z[:, COL_QA:COL_KA]
    ka = z[:, COL_KA:COL_VA]
    qa = qa * lax.rsqrt(_group_mean(qa * qa, gmat) + EPS) * gq_ref[...]
    ka = ka * lax.rsqrt(_group_mean(ka * ka, gmat[0:128, 0:128]) + EPS) * gk_ref[...]
    qb = z[:, COL_QB:COL_KB]
    kb = z[:, COL_KB:COL_VB]
    z_ref[:, COL_VA:COL_QB] = z[:, COL_VA:COL_QB]
    z_ref[:, COL_VB:N_SMALL] = z[:, COL_VB:N_SMALL]

    @pl.when(i < IN_CTX_TILES)
    def _():
        z_ref[:, COL_QA:COL_KA] = qa
        z_ref[:, COL_KA:COL_VA] = ka
        z_ref[:, COL_QB:COL_KB] = qb
        z_ref[:, COL_KB:COL_VB] = kb

    @pl.when(i >= IN_CTX_TILES)
    def _():
        cos_a = cos_a_ref[...]
        sin_a = sin_a_ref[...]
        cos_b = cos_b_ref[...]
        sin_b = sin_b_ref[...]
        ha, hb_ = A_HEAD_DIM // 2, B_HALF_DIM // 2
        z_ref[:, COL_QA:COL_KA] = qa * cos_a + _swap_halves(qa, ha) * sin_a
        z_ref[:, COL_KA:COL_VA] = ka * cos_a[:, 0:128] + _swap_halves(ka, ha) * sin_a[:, 0:128]
        z_ref[:, COL_QB:COL_KB] = qb * cos_b + _swap_halves(qb, hb_) * sin_b
        z_ref[:, COL_KB:COL_VB] = kb * cos_b + _swap_halves(kb, hb_) * sin_b


def _rope_block(i):
    return jnp.where(i < IN_CTX_TILES, 0, (i - IN_CTX_TILES) % IN_LAT_PER_SEQ)


def _in_projection(x_ctx, x_lat, mods, l, g_pre, w_in_bf, gq, gk, gmat, rope):
    const = lambda i: (0, 0)
    rope_spec = pl.BlockSpec((IN_TM, 256), lambda i: (_rope_block(i), 0))
    mod_row = lambda i: _mod_row(i, IN_CTX_TILES, IN_LAT_PER_SEQ)
    return pl.pallas_call(
        _inproj_kernel,
        name="in_projection",
        grid=(N_TOK // IN_TM,),
        in_specs=[
            _ctx_rows(D_MODEL, IN_TM), _lat_rows(D_MODEL, IN_TM),
            pl.BlockSpec((1, 1, 6 * D_MODEL), lambda i: (l * MOD_ROWS + mod_row(i), 0, 0)),
            pl.BlockSpec((1, D_MODEL), const),
            pl.BlockSpec((1, D_MODEL, N_SMALL), lambda i: (l, 0, 0)),
            pl.BlockSpec((1, 256), const),
            pl.BlockSpec((1, 128), const),
            pl.BlockSpec((256, 256), const),
            rope_spec, rope_spec, rope_spec, rope_spec,
        ],
        out_specs=[
            pl.BlockSpec((IN_TM, N_SMALL), lambda i: (i, 0)),
            pl.BlockSpec((IN_TM, D_MODEL), lambda i: (i, 0)),
        ],
        out_shape=[
            jax.ShapeDtypeStruct((N_TOK, N_SMALL), F32),
            jax.ShapeDtypeStruct((N_TOK, D_MODEL), BF16),
        ],
        compiler_params=pltpu.CompilerParams(
            dimension_semantics=("arbitrary",), vmem_limit_bytes=VMEM_LIMIT),
    )(x_ctx, x_lat, mods, g_pre, w_in_bf, gq, gk, gmat, *rope)


LOG2E = 1.4426950408889634
HALF_LANES = 64


def _attend(q, ks, vexts):
    ss = [_dot_nt(q, k) for k in ks]
    m = ss[0].max(axis=-1, keepdims=True)
    for s in ss[1:]:
        m = jnp.maximum(m, s.max(axis=-1, keepdims=True))
    o = None
    for s, v in zip(ss, vexts):
        t = _dot(jnp.exp2(s - m).astype(BF16), v)
        o = t if o is None else o + t
    return o


def _value_ext(v128, upper):
    lane = lax.broadcasted_iota(jnp.int32, v128.shape, 1)
    keep = (lane >= HALF_LANES) if upper else (lane < HALF_LANES)
    return jnp.where(keep, v128, 1.0).astype(BF16)


def _normalised(o, upper):
    if upper:
        return o[:, HALF_LANES:] / o[:, 0:1]
    return o[:, :HALF_LANES] / o[:, HALF_LANES:HALF_LANES + 1]


def _attn_kernel(*refs, has_cache, lambda_init, after_unit=None):
    if has_cache:
        (qa_ref, ka_ref, va_ref, qb_ref, kb_ref, vb_ref,
         cka_ref, cva_ref, ckb_ref, cvb_ref, lam_ref, subg_ref, oa_ref, ob_ref) = refs
    else:
        (qa_ref, ka_ref, va_ref, qb_ref, kb_ref, vb_ref,
         lam_ref, subg_ref, oa_ref, ob_ref) = refs

    def cols(ref, lo, hi):
        return ref[:, lo:hi].astype(BF16)

    def ccols(ref, lo, hi):
        return ref[0, 0, :, lo:hi].astype(BF16)

    scale_a = A_HEAD_DIM ** -0.5 * LOG2E
    group = A_HEADS // A_KV_HEADS
    for g in range(A_KV_HEADS):
        lo, hi = g * A_HEAD_DIM, (g + 1) * A_HEAD_DIM
        upper = g % 2 == 1
        ks = [cols(ka_ref, lo, hi)]
        vexts = [_value_ext(va_ref[...], upper)]
        if has_cache:
            ks.append(ccols(cka_ref, lo, hi))
            vexts.append(_value_ext(cva_ref[0, 0], upper))
        for hd in range(g * group, (g + 1) * group):
            q = (qa_ref[:, hd * A_HEAD_DIM:(hd + 1) * A_HEAD_DIM] * scale_a).astype(BF16)
            o = _attend(q, ks, vexts)
            oa_ref[:, hd * A_HEAD_DIM:(hd + 1) * A_HEAD_DIM] = _normalised(o, upper)
        if after_unit is not None:
            after_unit(g)

    bl = lam_ref[...]
    lam = (jnp.exp(jnp.sum(bl[0:1] * bl[1:2], axis=-1, keepdims=True))
           - jnp.exp(jnp.sum(bl[2:3] * bl[3:4], axis=-1, keepdims=True)) + lambda_init)
    scale_b = B_HALF_DIM ** -0.5 * LOG2E
    subg = subg_ref[...]
    for hd in range(B_HEADS):
        base = hd * 2 * B_HALF_DIM
        vlo, vhi = hd * B_V_DIM, (hd + 1) * B_V_DIM
        blk = (hd // 2) * 2 * B_V_DIM
        upper = hd % 2 == 1
        vexts = [_value_ext(vb_ref[:, blk:blk + 2 * B_V_DIM], upper)]
        if has_cache:
            vexts.append(_value_ext(cvb_ref[0, 0, :, blk:blk + 2 * B_V_DIM], upper))
        outs = []
        for c in range(2):
            lo, hi = base + c * B_HALF_DIM, base + (c + 1) * B_HALF_DIM
            q = (qb_ref[:, lo:hi] * scale_b).astype(BF16)
            ks = [cols(kb_ref, lo, hi)]
            if has_cache:
                ks.append(ccols(ckb_ref, lo, hi))
            outs.append(_normalised(_attend(q, ks, vexts), upper))
        o = outs[0] - lam * outs[1]
        o = _rms(o, subg) * (1.0 - lambda_init)
        ob_ref[:, vlo:vhi] = o
        if after_unit is not None:
            after_unit(A_KV_HEADS + hd)


SUBLANES = 8


def _row_shifter(win):
    span = win.shape[0] - SUBLANES
    shifted = {}

    def rows(off):
        s = off % SUBLANES
        if s not in shifted:
            shifted[s] = win[s:s + span, :]
        base = off - s
        return shifted[s][base:base + CONV_CHUNK, :]

    return rows


def _conv_fill(a_ref, b_ref, d_ref, hpad, upad, seq_len):
    zeros = jnp.zeros((CONV_PAD, BRANCH_W), F32)
    hpad[0:CONV_PAD, :] = zeros
    hpad[seq_len + CONV_PAD:seq_len + 2 * CONV_PAD, :] = zeros
    upad[0:CONV_PAD, :] = zeros
    upad[seq_len + CONV_PAD:seq_len + 2 * CONV_PAD, :] = zeros

    def fill(c, carry):
        r = pl.multiple_of(c * CONV_CHUNK, CONV_CHUNK)
        a = a_ref[pl.ds(r, CONV_CHUNK), :]
        b = b_ref[pl.ds(r, CONV_CHUNK), :]
        hpad[pl.ds(r + CONV_PAD, CONV_CHUNK), :] = a * _sigmoid(b)
        upad[pl.ds(r + CONV_PAD, CONV_CHUNK), :] = d_ref[pl.ds(r, CONV_CHUNK), :]
        return carry

    lax.fori_loop(0, seq_len // CONV_CHUNK, fill, 0)


def _conv_chunk(r, out_r, conv_refs, hpad, upad, oc_ref, od_ref, seq_len):
    dww_ref, dwb_ref, lng_ref, lnb_ref, wbd_ref, dsc_ref = conv_refs
    lane = lax.broadcasted_iota(jnp.int32, (CONV_CHUNK, BRANCH_W), 1)
    half = C_CONV_WIDTH // 2

    hrows = _row_shifter(hpad[pl.ds(r, CONV_CHUNK + 2 * CONV_PAD), :])
    acc = jnp.zeros((CONV_CHUNK, BRANCH_W), F32)
    for k in range(C_CONV_WIDTH):
        acc = acc + dww_ref[k:k + 1, :] * hrows(CONV_PAD - half + k)
    acc = acc + dwb_ref[...]
    mu = jnp.mean(acc, axis=-1, keepdims=True)
    cen = acc - mu
    var = jnp.mean(cen * cen, axis=-1, keepdims=True)
    y = cen * lax.rsqrt(var + EPS) * lng_ref[...] + lnb_ref[...]
    oc_ref[pl.ds(out_r, CONV_CHUNK), :] = y * _sigmoid(y)

    urows = _row_shifter(upad[pl.ds(r, CONV_CHUNK + 2 * CONV_PAD), :])

    def ld(d):
        return urows(CONV_PAD + d)

    u = ld(0)
    sums = {}
    s = u + ld(-1)
    sums[2] = s
    s = s + ld(-2) + ld(1)
    sums[4] = s
    s = s + ld(-4) + ld(-3) + ld(2) + ld(3)
    sums[8] = s
    s = s + ld(-8) + ld(-7) + ld(-6) + ld(-5) + ld(4) + ld(5) + ld(6) + ld(7)
    sums[16] = s
    t = r + lax.broadcasted_iota(jnp.int32, (CONV_CHUNK, 1), 0)
    pooled = None
    for g, w in reversed(list(enumerate(POOL_WINDOWS))):
        lo = jnp.maximum(t - w // 2, 0)
        hi = jnp.minimum(t + (w - 1 - w // 2), seq_len - 1)
        mean = sums[w] / (hi - lo + 1).astype(F32)
        pooled = mean if pooled is None else jnp.where(lane < (g + 1) * D_GROUP_W, mean, pooled)
    dlt = (pooled - u).astype(BF16)
    od_ref[pl.ds(out_r, CONV_CHUNK), :] = _dot(dlt, wbd_ref[...]) * dsc_ref[...]


N_CONV_REFS = 6
N_ATTN_UNITS = A_KV_HEADS + B_HEADS


def _mixers_kernel(*refs, has_cache, lambda_init, seq_len, step_rows):
    n_attn = 12 if has_cache else 8
    attn_in = refs[:n_attn]
    a_ref, b_ref, d_ref = refs[n_attn:n_attn + 3]
    conv_refs = refs[n_attn + 3:n_attn + 3 + N_CONV_REFS]
    oa_ref, ob_ref, oc_ref, od_ref, hpad, upad = refs[n_attn + 3 + N_CONV_REFS:]

    if seq_len == step_rows:
        _conv_fill(a_ref, b_ref, d_ref, hpad, upad, seq_len)
        row0 = 0
    else:
        t = pl.program_id(1)

        @pl.when(t == 0)
        def _():
            _conv_fill(a_ref, b_ref, d_ref, hpad, upad, seq_len)
        row0 = t * step_rows

    n_chunks = step_rows // CONV_CHUNK

    def conv(c):
        r = row0 + c * CONV_CHUNK
        if not isinstance(r, int):
            r = pl.multiple_of(r, CONV_CHUNK)
        _conv_chunk(r, c * CONV_CHUNK, conv_refs, hpad, upad, oc_ref, od_ref, seq_len)

    def after_unit(k):
        for c in range(k * n_chunks // N_ATTN_UNITS, (k + 1) * n_chunks // N_ATTN_UNITS):
            conv(c)

    _attn_kernel(*attn_in, oa_ref, ob_ref, has_cache=has_cache, lambda_init=lambda_init,
                 after_unit=after_unit)


def _conv_specs(index_map):
    const = lambda shape: pl.BlockSpec(shape, index_map)
    return [const((C_CONV_WIDTH, BRANCH_W)), const((1, BRANCH_W)), const((1, BRANCH_W)),
            const((1, BRANCH_W)), const((BRANCH_W, BRANCH_W)), const((1, BRANCH_W))]


def _conv_scratch(seq_len):
    return [pltpu.VMEM((seq_len + 2 * CONV_PAD, BRANCH_W), F32)] * 2


def _mixers_ctx(z, b_lambda, subg, lambda_init, conv_args):
    def zspec(width, col):
        return pl.BlockSpec((SEQ, width), lambda b: (b, col // width))
    out_spec = pl.BlockSpec((SEQ, BRANCH_W), lambda b: (b, 0))
    return pl.pallas_call(
        functools.partial(_mixers_kernel, has_cache=False, lambda_init=lambda_init,
                          seq_len=SEQ, step_rows=SEQ),
        name="mixers_ctx",
        grid=(BATCH,),
        in_specs=[
            zspec(256, COL_QA), zspec(128, COL_KA), zspec(128, COL_VA),
            zspec(256, COL_QB), zspec(256, COL_KB), zspec(256, COL_VB),
            pl.BlockSpec((4, B_HALF_DIM), lambda b: (0, 0)),
            pl.BlockSpec((1, B_V_DIM), lambda b: (0, 0)),
            zspec(256, COL_ZC), zspec(256, COL_ZC + BRANCH_W), zspec(256, COL_ZD),
            *_conv_specs(lambda b: (0, 0)),
        ],
        out_specs=[out_spec] * 4,
        out_shape=[jax.ShapeDtypeStruct((N_CTX, BRANCH_W), F32)] * 4,
        scratch_shapes=_conv_scratch(SEQ),
        compiler_params=pltpu.CompilerParams(
            dimension_semantics=("arbitrary",), vmem_limit_bytes=VMEM_LIMIT),
    )(z, z, z, z, z, z, b_lambda, subg, z, z, z, *conv_args)


def _mixers_lat(z, caches, l, b_lambda, subg, lambda_init, conv_args):
    q_tiles = DEC_SEQ // TQ
    seq0 = N_CTX // DEC_SEQ

    def qspec(col):
        return pl.BlockSpec((TQ, 256), lambda b, t: (N_CTX // TQ + b * q_tiles + t, col // 256))

    def kspec(width, col):
        return pl.BlockSpec((DEC_SEQ, width), lambda b, t: (seq0 + b, col // width))

    def cspec(width):
        return pl.BlockSpec((1, 1, PAST_LEN, width), lambda b, t: (b, l, 0, 0))

    out_spec = pl.BlockSpec((TQ, BRANCH_W), lambda b, t: (b * q_tiles + t, 0))
    cka, cva, ckb, cvb = caches
    return pl.pallas_call(
        functools.partial(_mixers_kernel, has_cache=True, lambda_init=lambda_init,
                          seq_len=DEC_SEQ, step_rows=TQ),
        name="mixers_lat",
        grid=(DEC_BATCH, q_tiles),
        in_specs=[
            qspec(COL_QA), kspec(128, COL_KA), kspec(128, COL_VA),
            qspec(COL_QB), kspec(256, COL_KB), kspec(256, COL_VB),
            cspec(128), cspec(128), cspec(256), cspec(256),
            pl.BlockSpec((4, B_HALF_DIM), lambda b, t: (0, 0)),
            pl.BlockSpec((1, B_V_DIM), lambda b, t: (0, 0)),
            kspec(256, COL_ZC), kspec(256, COL_ZC + BRANCH_W), kspec(256, COL_ZD),
            *_conv_specs(lambda b, t: (0, 0)),
        ],
        out_specs=[out_spec] * 4,
        out_shape=[jax.ShapeDtypeStruct((N_LAT, BRANCH_W), F32)] * 4,
        scratch_shapes=_conv_scratch(DEC_SEQ),
        compiler_params=pltpu.CompilerParams(
            dimension_semantics=("arbitrary", "arbitrary"), vmem_limit_bytes=VMEM_LIMIT),
    )(z, z, z, z, z, z, cka, cva, ckb, cvb, b_lambda, subg, z, z, z, *conv_args)


MIX_TM = 512
MIX_TILES = N_TOK // MIX_TM
MIX_CTX_TILES = N_CTX // MIX_TM
MIX_LAT_PER_SEQ = DEC_SEQ // MIX_TM
ROUTE_ROWS = 8


def _mix_kernel(xc_ref, xl_ref, h_ref, oa_c, ob_c, oc_c, od_c, oa_l, ob_l, oc_l, od_l,
                mod_ref, wg01_ref, wg23_ref, wbr_ref, wout_ref, gpost_ref, gpre2_ref, rwt_ref, rb_ref,
                x1_ref, h2_ref, eid_ref, rank_ref, wts_ref, cnt_ref, carry):
    i = pl.program_id(0)
    is_ctx = i < MIX_CTX_TILES

    @pl.when(i == 0)
    def _():
        carry[...] = jnp.zeros_like(carry)

    m = mod_ref[0]
    g1 = m[:, 2 * D_MODEL:3 * D_MODEL]
    sh2 = m[:, 3 * D_MODEL:4 * D_MODEL]
    sc2 = m[:, 4 * D_MODEL:5 * D_MODEL]

    hb = h_ref[...]
    merged = None
    for n, (c_ref, l_ref) in enumerate(((oa_c, oa_l), (ob_c, ob_l), (oc_c, oc_l), (od_c, od_l))):
        wg_ref = wg01_ref if n < 2 else wg23_ref
        gate = _sigmoid(_dot(hb, wg_ref[0, :, (n % 2) * D_MODEL:(n % 2 + 1) * D_MODEL]))
        o = jnp.where(is_ctx, c_ref[...], l_ref[...])
        br = _dot(o.astype(BF16), wbr_ref[0, n])
        merged = gate * br if merged is None else merged + gate * br
    y = _dot(merged.astype(BF16), wout_ref[0])
    x = jnp.where(is_ctx, xc_ref[...], xl_ref[...])
    x1 = x + g1 * _rms(y, gpost_ref[...])
    x1_ref[...] = x1
    h2 = _rms(x1, gpre2_ref[...]) * (1.0 + sc2) + sh2
    _store_token_major(h2_ref, h2)

    rh, rl = _split_bf16(rwt_ref[...])
    hh, hl = _split_bf16(h2)
    logits = _dot_nt(rh, hh) + _dot_nt(rh, hl) + _dot_nt(rl, hh) + rb_ref[...]

    iota_e = lax.broadcasted_iota(jnp.int32, (N_EXPERTS, MIX_TM), 0)
    rem = logits
    vals, idxs = [], []
    for _ in range(TOP_K):
        mx = jnp.max(rem, axis=0, keepdims=True)
        idx = jnp.min(jnp.where(rem == mx, iota_e, N_EXPERTS), axis=0, keepdims=True)
        vals.append(mx)
        idxs.append(idx)
        rem = jnp.where(iota_e == idx, -jnp.inf, rem)
    exps = [jnp.exp(v - vals[0]) for v in vals]
    den = exps[0]
    for e in exps[1:]:
        den = den + e

    sel = [iota_e == idx for idx in idxs]
    member = jnp.zeros((N_EXPERTS, MIX_TM), F32)
    for s in sel:
        member = member + jnp.where(s, 1.0, 0.0)
    row = lax.broadcasted_iota(jnp.int32, (MIX_TM, MIX_TM), 0)
    col = lax.broadcasted_iota(jnp.int32, (MIX_TM, MIX_TM), 1)
    earlier = jnp.where(row < col, 1.0, 0.0).astype(BF16)
    seen = _dot(member.astype(BF16), earlier) + carry[...]
    carry[...] = carry[...] + jnp.sum(member, axis=1, keepdims=True)
    cnt_ref[...] = carry[...]

    sub = lax.broadcasted_iota(jnp.int32, (ROUTE_ROWS, MIX_TM), 0)
    eid = jnp.zeros((ROUTE_ROWS, MIX_TM), jnp.int32)
    rank = jnp.zeros((ROUTE_ROWS, MIX_TM), jnp.int32)
    wts = jnp.zeros((ROUTE_ROWS, MIX_TM), F32)
    for k in range(TOP_K):
        rk = jnp.sum(jnp.where(sel[k], seen, 0.0), axis=0, keepdims=True).astype(jnp.int32)
        eid = jnp.where(sub == k, idxs[k], eid)
        rank = jnp.where(sub == k, rk, rank)
        wts = jnp.where(sub == k, exps[k] / den, wts)
    eid_ref[...] = eid
    rank_ref[...] = rank
    wts_ref[...] = wts


def _mix(x_ctx, x_lat, h, o_ctx, o_lat, mods, l, w_in_bf, w_branch_bf, w_out_bf,
         g_post, g_pre2, r_wt, r_b):
    const2 = lambda i: (0, 0)
    tile = lambda width: pl.BlockSpec((MIX_TM, width), lambda i: (i, 0))
    route = pl.BlockSpec((ROUTE_ROWS, MIX_TM), lambda i: (0, i))
    ctx_tile = _ctx_rows(BRANCH_W, MIX_TM)
    lat_tile = _lat_rows(BRANCH_W, MIX_TM)
    mod_row = lambda i: _mod_row(i, MIX_CTX_TILES, MIX_LAT_PER_SEQ)
    single = pl.Buffered(1)
    return pl.pallas_call(
        _mix_kernel,
        name="mix_router",
        grid=(MIX_TILES,),
        in_specs=[
            _ctx_rows(D_MODEL, MIX_TM), _lat_rows(D_MODEL, MIX_TM), tile(D_MODEL),
            ctx_tile, ctx_tile, ctx_tile, ctx_tile, lat_tile, lat_tile, lat_tile, lat_tile,
            pl.BlockSpec((1, 1, 6 * D_MODEL), lambda i: (l * MOD_ROWS + mod_row(i), 0, 0)),
            pl.BlockSpec((1, D_MODEL, N_SMALL), lambda i: (l, 0, 1), pipeline_mode=single),
            pl.BlockSpec((1, D_MODEL, N_SMALL), lambda i: (l, 0, 2), pipeline_mode=single),
            pl.BlockSpec((1, N_BRANCH, BRANCH_W, D_MODEL), lambda i: (l, 0, 0, 0),
                         pipeline_mode=single),
            pl.BlockSpec((1, D_MODEL, D_MODEL), lambda i: (l, 0, 0), pipeline_mode=single),
            pl.BlockSpec((1, D_MODEL), const2),
            pl.BlockSpec((1, D_MODEL), const2),
            pl.BlockSpec((N_EXPERTS, D_MODEL), const2),
            pl.BlockSpec((N_EXPERTS, 1), const2),
        ],
        out_specs=[
            tile(D_MODEL), pl.BlockSpec((MIX_TM * ROW_TILES, LANES), lambda i: (i, 0)),
            route, route, route,
            pl.BlockSpec((N_EXPERTS, 1), const2),
        ],
        out_shape=[
            jax.ShapeDtypeStruct((N_TOK, D_MODEL), F32),
            jax.ShapeDtypeStruct((N_TOK * ROW_TILES, LANES), F32),
            jax.ShapeDtypeStruct((ROUTE_ROWS, N_TOK), jnp.int32),
            jax.ShapeDtypeStruct((ROUTE_ROWS, N_TOK), jnp.int32),
            jax.ShapeDtypeStruct((ROUTE_ROWS, N_TOK), F32),
            jax.ShapeDtypeStruct((N_EXPERTS, 1), F32),
        ],
        scratch_shapes=[pltpu.VMEM((N_EXPERTS, 1), F32)],
        compiler_params=pltpu.CompilerParams(
            dimension_semantics=("arbitrary",), vmem_limit_bytes=VMEM_LIMIT),
    )(x_ctx, x_lat, h, *o_ctx, *o_lat, mods, w_in_bf, w_in_bf, w_branch_bf, w_out_bf,
      g_post, g_pre2, r_wt, r_b)


def _token_rows(t, n=1):
    return pl.ds(pl.multiple_of(t * ROW_TILES, ROW_TILES), n * ROW_TILES)


def _token_copy(src_ref, src_tok, dst_ref, dst_tok, sem):
    return pltpu.make_async_copy(src_ref.at[_token_rows(src_tok), :], dst_ref.at[_token_rows(dst_tok), :], sem)


DISPATCH_ROWS = 512
DISPATCH_STEPS = N_TOK // DISPATCH_ROWS
DMA_UNROLL = 8


def _dispatch_kernel(starts_ref, padded_ref, pos_ref, h2_ref, xs_ref, zbuf, sem, zsem, tsem):
    i = pl.program_id(0)

    def pad_copy(e):
        last = pl.multiple_of(starts_ref[e] + padded_ref[e] - TE, TE)
        return pltpu.make_async_copy(zbuf, xs_ref.at[_token_rows(last, TE), :], zsem)

    first_tail = (starts_ref[N_EXPERTS - 1] + padded_ref[N_EXPERTS - 1]) // TE

    def tail_copy(j):
        return pltpu.make_async_copy(zbuf, xs_ref.at[_token_rows(j * TE, TE), :], tsem)

    def tail_start(j, carry):
        tail_copy(j).start()
        return carry

    def tail_wait(j, carry):
        tail_copy(j).wait()
        return carry

    @pl.when(i == 0)
    def _():
        zbuf[...] = jnp.zeros_like(zbuf)
        for e in range(N_EXPERTS):
            @pl.when(padded_ref[e] > 0)
            def _():
                pad_copy(e).start()
        lax.fori_loop(first_tail, E_TILES, tail_start, 0)
        for e in range(N_EXPERTS):
            @pl.when(padded_ref[e] > 0)
            def _():
                pad_copy(e).wait()

    @pl.when(i == DISPATCH_STEPS - 1)
    def _():
        lax.fori_loop(first_tail, E_TILES, tail_wait, 0)

    def issue(t, carry):
        for k in range(TOP_K):
            _token_copy(h2_ref, t, xs_ref, pos_ref[t * TOP_K + k], sem).start(priority=k % 2)
        return carry

    lax.fori_loop(0, DISPATCH_ROWS, issue, 0, unroll=DMA_UNROLL)

    for _ in range(TOP_K):
        pltpu.make_async_copy(h2_ref, xs_ref.at[_token_rows(0, DISPATCH_ROWS), :], sem).wait()


def _dispatch(starts, padded, pos, h2):
    grid_spec = pltpu.PrefetchScalarGridSpec(
        num_scalar_prefetch=2,
        grid=(DISPATCH_STEPS,),
        in_specs=[
            pl.BlockSpec((DISPATCH_ROWS * TOP_K,), lambda i, st, pd: (i,), memory_space=pltpu.SMEM),
            pl.BlockSpec((DISPATCH_ROWS * ROW_TILES, LANES), lambda i, st, pd: (i, 0)),
        ],
        out_specs=pl.BlockSpec(memory_space=pl.ANY),
        scratch_shapes=[pltpu.VMEM((TE * ROW_TILES, LANES), F32),
                        pltpu.SemaphoreType.DMA(()), pltpu.SemaphoreType.DMA(()),
                        pltpu.SemaphoreType.DMA(())],
    )
    return pl.pallas_call(
        _dispatch_kernel,
        name="dispatch",
        grid_spec=grid_spec,
        out_shape=jax.ShapeDtypeStruct((P_MAX * ROW_TILES, LANES), F32),
        compiler_params=pltpu.CompilerParams(
            dimension_semantics=("arbitrary",), vmem_limit_bytes=VMEM_LIMIT),
    )(starts, padded, pos, h2)


W_CAST_ROWS = 128


def _moe_kernel(te_ref, rows_ref, slot_ref, next_ref, x_ref, w1_hbm, b1_ref, w2_hbm, b2_ref, y_ref,
                w1f, w2f, w1b, w2b, sem, *, layer):
    j = pl.program_id(0)
    e = te_ref[j]
    prev = te_ref[jnp.maximum(j - 1, 0)]
    valid = rows_ref[j] > 0
    first = jnp.logical_and(valid, jnp.logical_or(j == 0, e != prev))
    s = slot_ref[j]

    def weight_copies(ex, sl):
        return (pltpu.make_async_copy(w1_hbm.at[layer, ex], w1f.at[sl], sem.at[0, sl]),
                pltpu.make_async_copy(w2_hbm.at[layer, ex], w2f.at[sl], sem.at[1, sl]))

    @pl.when(j == 0)
    def _():
        for cp in weight_copies(e, s):
            cp.start()

    @pl.when(first)
    def _():
        for cp in weight_copies(e, s):
            cp.wait()
        nxt = next_ref[j]

        @pl.when(nxt >= 0)
        def _():
            for cp in weight_copies(nxt, 1 - s):
                cp.start()

        def cast(c, carry):
            r = pl.multiple_of(c * W_CAST_ROWS, W_CAST_ROWS)
            w1b[pl.ds(r, W_CAST_ROWS), :] = w1f[s, pl.ds(r, W_CAST_ROWS), :].astype(BF16)
            w2b[pl.ds(r, W_CAST_ROWS), :] = w2f[s, pl.ds(r, W_CAST_ROWS), :].astype(BF16)
            return carry
        lax.fori_loop(0, D_MODEL // W_CAST_ROWS, cast, 0)

    def ffn(x_rows, y_rows):
        x = _load_token_major(x_rows).astype(BF16)
        u = _dot(x, w1b[...]) + b1_ref[0, 0]
        xg = jnp.minimum(u[:, :D_FF], SWIGLU_LIMIT)
        xl = jnp.clip(u[:, D_FF:], -SWIGLU_LIMIT, SWIGLU_LIMIT)
        act = (xl + 1.0) * (xg * _sigmoid(SWIGLU_ALPHA * xg))
        _store_token_major(y_rows, _dot(act.astype(BF16), w2b[...]) + b2_ref[0, 0])

    n_rows = rows_ref[j]
    half = pl.ds(0, TE // 2 * ROW_TILES)
    rest = pl.ds(TE // 2 * ROW_TILES, TE // 2 * ROW_TILES)

    @pl.when(n_rows > TE // 2)
    def _():
        ffn(x_ref, y_ref)

    @pl.when(jnp.logical_and(valid, n_rows <= TE // 2))
    def _():
        ffn(x_ref.at[half, :], y_ref.at[half, :])
        y_ref[rest, :] = jnp.zeros((TE // 2 * ROW_TILES, LANES), F32)

    @pl.when(jnp.logical_not(valid))
    def _():
        y_ref[...] = jnp.zeros_like(y_ref)


def _expert_ffn(tile_expert, tile_rows, tile_slot, tile_next, xs, l, e_w1, e_b1, e_w2, e_b2):
    L = e_w1.shape[0]
    grid_spec = pltpu.PrefetchScalarGridSpec(
        num_scalar_prefetch=4,
        grid=(E_TILES,),
        in_specs=[
            pl.BlockSpec((TE * ROW_TILES, LANES),
                         lambda j, te, nr, sl, nx: (jnp.where(nr[j] > 0, j, 0), 0)),
            pl.BlockSpec(memory_space=pl.ANY),
            pl.BlockSpec((1, 1, 1, 2 * D_FF), lambda j, te, nr, sl, nx: (l, te[j], 0, 0)),
            pl.BlockSpec(memory_space=pl.ANY),
            pl.BlockSpec((1, 1, 1, D_MODEL), lambda j, te, nr, sl, nx: (l, te[j], 0, 0)),
        ],
        out_specs=pl.BlockSpec((TE * ROW_TILES, LANES), lambda j, te, nr, sl, nx: (j, 0)),
        scratch_shapes=[pltpu.VMEM((2, D_MODEL, 2 * D_FF), F32), pltpu.VMEM((2, D_FF, D_MODEL), F32),
                        pltpu.VMEM((D_MODEL, 2 * D_FF), BF16), pltpu.VMEM((D_FF, D_MODEL), BF16),
                        pltpu.SemaphoreType.DMA((2, 2))],
    )
    return pl.pallas_call(
        functools.partial(_moe_kernel, layer=l),
        name="expert_ffn",
        grid_spec=grid_spec,
        out_shape=jax.ShapeDtypeStruct((P_MAX * ROW_TILES, LANES), F32),
        compiler_params=pltpu.CompilerParams(
            dimension_semantics=("arbitrary",), vmem_limit_bytes=VMEM_LIMIT),
    )(tile_expert, tile_rows, tile_slot, tile_next, xs, e_w1,
      e_b1.reshape(L, N_EXPERTS, 1, 2 * D_FF), e_w2, e_b2.reshape(L, N_EXPERTS, 1, D_MODEL))


def _combine_kernel(pos_ref, pos_next_ref, ys_ref, wts_ref, x1_ref, mod_ref, gpost_ref,
                    out_c_ref, out_l_ref, buf, sem):
    i = pl.program_id(0)
    slot = i % 2

    def gather(p_ref, s):
        def issue(t, carry):
            for k in range(TOP_K):
                _token_copy(ys_ref, p_ref[t * TOP_K + k], buf.at[s, k], t, sem.at[s]).start(priority=k % 2)
            return carry
        lax.fori_loop(0, TM, issue, 0, unroll=DMA_UNROLL)

    @pl.when(i == 0)
    def _():
        gather(pos_ref, 0)

    @pl.when(i + 1 < N_TILES)
    def _():
        gather(pos_next_ref, 1 - slot)

    for k in range(TOP_K):
        pltpu.make_async_copy(ys_ref.at[_token_rows(0, TM), :], buf.at[slot, k], sem.at[slot]).wait()

    w = wts_ref[...]
    y = w[:, 0:1] * _load_token_major(buf.at[slot, 0])
    for k in range(1, TOP_K):
        y = y + w[:, k:k + 1] * _load_token_major(buf.at[slot, k])
    g2 = mod_ref[0][:, 5 * D_MODEL:6 * D_MODEL]
    res = x1_ref[...] + g2 * _rms(y, gpost_ref[...])

    @pl.when(i < CTX_TILES)
    def _():
        out_c_ref[...] = res

    @pl.when(i >= CTX_TILES)
    def _():
        out_l_ref[...] = res


def _combine(pos, ys, wts, x1, mods, l, g_post2):
    return pl.pallas_call(
        _combine_kernel,
        name="combine",
        grid=(N_TILES,),
        in_specs=[
            pl.BlockSpec((TM * TOP_K,), lambda i: (i,), memory_space=pltpu.SMEM),
            pl.BlockSpec((TM * TOP_K,), lambda i: (jnp.minimum(i + 1, N_TILES - 1),),
                         memory_space=pltpu.SMEM),
            pl.BlockSpec(memory_space=pl.ANY),
            pl.BlockSpec((TM, TOP_K), lambda i: (i, 0)),
            pl.BlockSpec((TM, D_MODEL), lambda i: (i, 0)),
            pl.BlockSpec((1, 1, 6 * D_MODEL), lambda i: (l * MOD_ROWS + _mod_row(i), 0, 0)),
            pl.BlockSpec((1, D_MODEL), lambda i: (0, 0)),
        ],
        out_specs=[_ctx_rows(D_MODEL), _lat_rows(D_MODEL)],
        out_shape=[jax.ShapeDtypeStruct((N_CTX, D_MODEL), F32),
                   jax.ShapeDtypeStruct((N_LAT, D_MODEL), F32)],
        scratch_shapes=[pltpu.VMEM((2, TOP_K, TM * ROW_TILES, LANES), F32),
                        pltpu.SemaphoreType.DMA((2,))],
        compiler_params=pltpu.CompilerParams(
            dimension_semantics=("arbitrary",), vmem_limit_bytes=VMEM_LIMIT),
    )(pos, pos, ys, wts, x1, mods, g_post2)


def _rope_tables():
    t = np.arange(DEC_SEQ)
    r = (t // GRID_W).astype(np.float32)
    c = (t % GRID_W).astype(np.float32)

    def table(dim, reps):
        n_axis = dim // 4
        inv = (ROPE_THETA ** (-np.arange(n_axis, dtype=np.float32) / n_axis)).astype(np.float32)
        ang = np.concatenate([r[:, None] * inv, c[:, None] * inv], axis=-1).astype(np.float32)
        cos, sin = np.cos(ang), np.sin(ang)
        return (np.tile(np.concatenate([cos, cos], -1), (1, reps)),
                np.tile(np.concatenate([-sin, sin], -1), (1, reps)))

    cos_a, sin_a = table(A_HEAD_DIM, 256 // A_HEAD_DIM)
    cos_b, sin_b = table(B_HALF_DIM, 256 // B_HALF_DIM)
    return tuple(jnp.asarray(a, F32) for a in (cos_a, sin_a, cos_b, sin_b))


def _group_mean_matrix():
    idx = np.arange(256) // A_HEAD_DIM
    return jnp.asarray((idx[:, None] == idx[None, :]).astype(np.float32) / A_HEAD_DIM, BF16)


def _block_diag(w):
    out = jnp.zeros((BRANCH_W, BRANCH_W), w.dtype)
    for g in range(D_GROUPS):
        out = out.at[g * D_GROUP_W:(g + 1) * D_GROUP_W, g * D_GROUP_W:(g + 1) * D_GROUP_W].set(w[g])
    return out


def _routing_tables(eid, rank, cnt):
    cnt = cnt.reshape(N_EXPERTS).astype(jnp.int32)
    padded = ((cnt + TE - 1) // TE) * TE
    ends = jnp.cumsum(padded)
    starts = ends - padded
    experts = jnp.arange(N_EXPERTS, dtype=jnp.int32)
    start_of = jnp.sum(jnp.where(eid[:TOP_K, :, None] == experts, starts, 0), axis=-1)
    pos = (start_of + rank[:TOP_K]).T.reshape(N_SLOTS)
    tile_row = jnp.arange(E_TILES, dtype=jnp.int32) * TE
    tile_expert = jnp.sum((tile_row[:, None] >= ends[None, :]).astype(jnp.int32), axis=-1)
    tile_expert = jnp.minimum(tile_expert, N_EXPERTS - 1)
    used_end = jnp.sum(jnp.where(tile_expert[:, None] == experts, starts + cnt, 0), axis=-1)
    tile_rows = jnp.where(tile_row < ends[-1], jnp.clip(used_end - tile_row, 0, TE), 0)
    valid = tile_rows > 0
    changed = jnp.concatenate([jnp.ones((1,), bool), tile_expert[1:] != tile_expert[:-1]])
    tile_slot = (jnp.cumsum(jnp.logical_and(valid, changed).astype(jnp.int32)) + 1) % 2
    group_end = jnp.sum(jnp.where(tile_expert[:, None] == experts, ends, 0), axis=-1) // TE
    tiles = jnp.arange(E_TILES, dtype=jnp.int32)
    expert_at = jnp.where(valid, tile_expert, -1)
    tile_next = jnp.sum(jnp.where(group_end[:, None] == tiles, expert_at + 1, 0), axis=-1) - 1
    as_i32 = lambda a: a.astype(jnp.int32)
    return (as_i32(pos), as_i32(tile_expert), as_i32(tile_rows), as_i32(tile_slot), as_i32(tile_next),
            as_i32(starts), as_i32(padded))


def kernel(x_prompt, x_sample, cache_a_k, cache_a_v, cache_b_k, cache_b_v, c, c_ctx, w_mod, b_mod, g_pre1, g_post1, g_pre2, g_post2, w_in, a_q_gain, a_k_gain, b_lambda, b_subln_gain, c_dw_w, c_dw_b, c_ln_g, c_ln_b, d_w_group, d_scale, w_branch, w_out, r_w, r_b, e_w1, e_b1, e_w2, e_b2):
    x_ctx = x_prompt.reshape(N_CTX, D_MODEL)
    x_lat = x_sample.reshape(N_LAT, D_MODEL)
    cond8 = jnp.concatenate(
        [c_ctx[None, :], c, jnp.zeros((MOD_ROWS - 1 - DEC_BATCH, D_MODEL), F32)], axis=0)
    mods = _modulation(cond8, w_mod, b_mod).reshape(DEPTH * MOD_ROWS, 1, 6 * D_MODEL)

    rope = _rope_tables()
    gmat = _group_mean_matrix()
    caches = (
        cache_a_k.reshape(DEC_BATCH, DEPTH, PAST_LEN, A_KV_HEADS * A_HEAD_DIM),
        cache_a_v.reshape(DEC_BATCH, DEPTH, PAST_LEN, A_KV_HEADS * A_HEAD_DIM),
        cache_b_k.reshape(DEC_BATCH, DEPTH, PAST_LEN, B_HEADS * 2 * B_HALF_DIM),
        cache_b_v.reshape(DEC_BATCH, DEPTH, PAST_LEN, B_HEADS * B_V_DIM),
    )

    w_in_bf = w_in.astype(BF16)
    w_branch_bf = w_branch.astype(BF16)
    w_out_bf = w_out.astype(BF16)

    new_ak, new_av, new_bk, new_bv = [], [], [], []
    row = lambda v: v.reshape(1, -1)
    for l in range(DEPTH):
        lambda_init = 0.8 - 0.6 * math.exp(-0.3 * l)
        z, h = _in_projection(
            x_ctx, x_lat, mods, l, row(g_pre1[l]), w_in_bf,
            row(jnp.tile(a_q_gain[l], A_HEADS)), row(jnp.tile(a_k_gain[l], A_KV_HEADS)), gmat, rope)

        zc = z[:N_CTX]
        new_ak.append(zc[:, COL_KA:COL_VA].reshape(BATCH, SEQ, A_KV_HEADS, A_HEAD_DIM))
        new_av.append(zc[:, COL_VA:COL_QB].reshape(BATCH, SEQ, A_KV_HEADS, A_HEAD_DIM))
        new_bk.append(zc[:, COL_KB:COL_VB].reshape(BATCH, SEQ, B_HEADS, 2, B_HALF_DIM))
        new_bv.append(zc[:, COL_VB:COL_ZC].reshape(BATCH, SEQ, B_HEADS, B_V_DIM))

        subg = row(b_subln_gain[l])
        conv_args = (c_dw_w[l], row(c_dw_b[l]), row(c_ln_g[l]), row(c_ln_b[l]),
                     _block_diag(d_w_group[l]).astype(BF16), row(d_scale[l]))
        o_ctx = _mixers_ctx(z, b_lambda[l], subg, lambda_init, conv_args)
        o_lat = _mixers_lat(z, caches, l, b_lambda[l], subg, lambda_init, conv_args)

        x1, h2, eid, rank, wts, cnt = _mix(
            x_ctx, x_lat, h, o_ctx, o_lat, mods, l,
            w_in_bf, w_branch_bf, w_out_bf,
            row(g_post1[l]), row(g_pre2[l]), r_w[l].T, r_b[l].reshape(N_EXPERTS, 1))
        wts = wts[:TOP_K].T

        pos, tile_expert, tile_rows, tile_slot, tile_next, starts, padded = _routing_tables(
            eid, rank, cnt)
        xs = _dispatch(starts, padded, pos, h2)
        ys = _expert_ffn(tile_expert, tile_rows, tile_slot, tile_next, xs, l,
                         e_w1, e_b1, e_w2, e_b2)
        x_ctx, x_lat = _combine(pos, ys, wts, x1, mods, l, row(g_post2[l]))

    y_prompt = x_ctx.reshape(BATCH, SEQ, D_MODEL)
    y_sample = x_lat.reshape(DEC_BATCH, DEC_SEQ, D_MODEL)
    return (y_prompt, y_sample, jnp.stack(new_ak, axis=1), jnp.stack(new_av, axis=1),
            jnp.stack(new_bk, axis=1), jnp.stack(new_bv, axis=1))
```

```python
import functools
import math

import jax
import jax.numpy as jnp
import numpy as np
from jax import lax
from jax.experimental import pallas as pl
from jax.experimental.pallas import tpu as pltpu

F32 = jnp.float32
BF16 = jnp.bfloat16

D_MODEL = 1024
BATCH = 16
SEQ = 256
DEPTH = 2
DEC_BATCH = 4
DEC_SEQ = 2048
PAST_LEN = 256
GRID_W = 64
ROPE_THETA = 10000.0
N_BRANCH = 4
BRANCH_W = D_MODEL // 4
A_HEADS = 4
A_KV_HEADS = 2
A_HEAD_DIM = 64
B_HEADS = 4
B_HALF_DIM = 32
B_V_DIM = 2 * B_HALF_DIM
C_CONV_WIDTH = 31
D_GROUPS = 4
D_GROUP_W = BRANCH_W // D_GROUPS
POOL_WINDOWS = (2, 4, 8, 16)
N_EXPERTS = 32
TOP_K = 4
D_FF = D_MODEL
SWIGLU_LIMIT = 7.0
SWIGLU_ALPHA = 1.702
EPS = 1e-6

N_CTX = BATCH * SEQ
N_LAT = DEC_BATCH * DEC_SEQ
N_TOK = N_CTX + N_LAT

COL_QA, COL_KA, COL_VA = 0, 256, 384
COL_QB, COL_KB, COL_VB = 512, 768, 1024
COL_ZC, COL_ZD = 1280, 1792
N_SMALL = 2048
N_GATE = N_BRANCH * D_MODEL

TM = 256
N_TILES = N_TOK // TM
CTX_TILES = N_CTX // TM
LAT_TILES_PER_SEQ = DEC_SEQ // TM
MOD_ROWS = 8

TE = 512
N_SLOTS = N_TOK * TOP_K
P_MAX = N_SLOTS + N_EXPERTS * TE
E_TILES = P_MAX // TE

TQ = 512
CONV_PAD = 16
CONV_CHUNK = 128

VMEM_LIMIT = 56 * 1024 * 1024


def _sigmoid(x):
    return 1.0 / (1.0 + jnp.exp(-x))


def _split_bf16(a):
    hi = a.astype(BF16)
    lo = (a - hi.astype(F32)).astype(BF16)
    return hi, lo


def _dot(a, b):
    return jnp.dot(a, b, preferred_element_type=F32)


def _dot_nt(a, b):
    return lax.dot_general(a, b, (((1,), (1,)), ((), ())), preferred_element_type=F32)


def _dot3(a, b):
    ah, al = _split_bf16(a)
    bh, bl = _split_bf16(b)
    return _dot(ah, bh) + _dot(ah, bl) + _dot(al, bh)


def _rms(x, g):
    return x * lax.rsqrt(jnp.mean(x * x, axis=-1, keepdims=True) + EPS) * g


def _mod_row(i, ctx_tiles=CTX_TILES, tiles_per_seq=LAT_TILES_PER_SEQ):
    return jnp.where(i < ctx_tiles, 0, 1 + (i - ctx_tiles) // tiles_per_seq)


LANES = 128
ROW_TILES = D_MODEL // LANES


def _store_token_major(ref, x):
    n = x.shape[0]
    for c in range(ROW_TILES):
        ref[pl.ds(c, n, stride=ROW_TILES), :] = x[:, c * LANES:(c + 1) * LANES]


def _load_token_major(ref):
    n = ref.shape[0] // ROW_TILES
    return jnp.concatenate(
        [ref[pl.ds(c, n, stride=ROW_TILES), :] for c in range(ROW_TILES)], axis=1)


def _ctx_rows(width, tile=TM):
    last = N_CTX // tile - 1
    return pl.BlockSpec((tile, width), lambda i: (jnp.minimum(i, last), 0))


def _lat_rows(width, tile=TM):
    first = N_CTX // tile
    return pl.BlockSpec((tile, width), lambda i: (jnp.maximum(i - first, 0), 0))


MOD_TN = 1536


def _mod_kernel(cond_ref, w_ref, b_ref, o_ref):
    c = cond_ref[...]
    s = c * _sigmoid(c)
    o_ref[0] = _dot3(s, w_ref[0]) + b_ref[0]


def _modulation(cond8, w_mod, b_mod):
    L = w_mod.shape[0]
    return pl.pallas_call(
        _mod_kernel,
        name="modulation",
        grid=(L, 6 * D_MODEL // MOD_TN),
        in_specs=[
            pl.BlockSpec((MOD_ROWS, D_MODEL), lambda l, j: (0, 0)),
            pl.BlockSpec((1, D_MODEL, MOD_TN), lambda l, j: (l, 0, j)),
            pl.BlockSpec((1, 1, MOD_TN), lambda l, j: (l, 0, j)),
        ],
        out_specs=pl.BlockSpec((1, MOD_ROWS, MOD_TN), lambda l, j: (l, 0, j)),
        out_shape=jax.ShapeDtypeStruct((L, MOD_ROWS, 6 * D_MODEL), F32),
        compiler_params=pltpu.CompilerParams(
            dimension_semantics=("arbitrary", "arbitrary"), vmem_limit_bytes=VMEM_LIMIT),
    )(cond8, w_mod, b_mod.reshape(L, 1, 6 * D_MODEL))


def _group_mean(sq, gmat):
    hi, lo = _split_bf16(sq)
    return _dot(hi, gmat) + _dot(lo, gmat)


def _swap_halves(x, half):
    w = x.shape[-1]
    lane = lax.broadcasted_iota(jnp.int32, x.shape, 1)
    first = (lane % (2 * half)) < half
    return jnp.where(first, pltpu.roll(x, w - half, 1), pltpu.roll(x, half, 1))


IN_TM = 512
IN_CTX_TILES = N_CTX // IN_TM
IN_LAT_PER_SEQ = DEC_SEQ // IN_TM


def _inproj_kernel(xc_ref, xl_ref, mod_ref, gpre_ref, w_ref, gq_ref, gk_ref, gmat_ref,
                   cos_a_ref, sin_a_ref, cos_b_ref, sin_b_ref, z_ref, h_ref):
    i = pl.program_id(0)
    m = mod_ref[0]
    sh1 = m[:, 0:D_MODEL]
    sc1 = m[:, D_MODEL:2 * D_MODEL]
    x = jnp.where(i < IN_CTX_TILES, xc_ref[...], xl_ref[...])
    h = _rms(x, gpre_ref[...]) * (1.0 + sc1) + sh1
    hb = h.astype(BF16)
    h_ref[...] = hb
    z = _dot(hb, w_ref[0])

    gmat = gmat_ref[...]
    qa = z[:, COL_QA:COL_KA]
    ka = z[:, COL_KA:COL_VA]
    qa = qa * lax.rsqrt(_group_mean(qa * qa, gmat) + EPS) * gq_ref[...]
    ka = ka * lax.rsqrt(_group_mean(ka * ka, gmat[0:128, 0:128]) + EPS) * gk_ref[...]
    qb = z[:, COL_QB:COL_KB]
    kb = z[:, COL_KB:COL_VB]
    z_ref[:, COL_VA:COL_QB] = z[:, COL_VA:COL_QB]
    z_ref[:, COL_VB:N_SMALL] = z[:, COL_VB:N_SMALL]

    @pl.when(i < IN_CTX_TILES)
    def _():
        z_ref[:, COL_QA:COL_KA] = qa
        z_ref[:, COL_KA:COL_VA] = ka
        z_ref[:, COL_QB:COL_KB] = qb
        z_ref[:, COL_KB:COL_VB] = kb

    @pl.when(i >= IN_CTX_TILES)
    def _():
        cos_a = cos_a_ref[...]
        sin_a = sin_a_ref[...]
        cos_b = cos_b_ref[...]
        sin_b = sin_b_ref[...]
        ha, hb_ = A_HEAD_DIM // 2, B_HALF_DIM // 2
        z_ref[:, COL_QA:COL_KA] = qa * cos_a + _swap_halves(qa, ha) * sin_a
        z_ref[:, COL_KA:COL_VA] = ka * cos_a[:, 0:128] + _swap_halves(ka, ha) * sin_a[:, 0:128]
        z_ref[:, COL_QB:COL_KB] = qb * cos_b + _swap_halves(qb, hb_) * sin_b
        z_ref[:, COL_KB:COL_VB] = kb * cos_b + _swap_halves(kb, hb_) * sin_b


def _rope_block(i):
    return jnp.where(i < IN_CTX_TILES, 0, (i - IN_CTX_TILES) % IN_LAT_PER_SEQ)


def _in_projection(x_ctx, x_lat, mods, l, g_pre, w_in_bf, gq, gk, gmat, rope):
    const = lambda i: (0, 0)
    rope_spec = pl.BlockSpec((IN_TM, 256), lambda i: (_rope_block(i), 0))
    mod_row = lambda i: _mod_row(i, IN_CTX_TILES, IN_LAT_PER_SEQ)
    return pl.pallas_call(
        _inproj_kernel,
        name="in_projection",
        grid=(N_TOK // IN_TM,),
        in_specs=[
            _ctx_rows(D_MODEL, IN_TM), _lat_rows(D_MODEL, IN_TM),
            pl.BlockSpec((1, 1, 6 * D_MODEL), lambda i: (l * MOD_ROWS + mod_row(i), 0, 0)),
            pl.BlockSpec((1, D_MODEL), const),
            pl.BlockSpec((1, D_MODEL, N_SMALL), lambda i: (l, 0, 0)),
            pl.BlockSpec((1, 256), const),
            pl.BlockSpec((1, 128), const),
            pl.BlockSpec((256, 256), const),
            rope_spec, rope_spec, rope_spec, rope_spec,
        ],
        out_specs=[
            pl.BlockSpec((IN_TM, N_SMALL), lambda i: (i, 0)),
            pl.BlockSpec((IN_TM, D_MODEL), lambda i: (i, 0)),
        ],
        out_shape=[
            jax.ShapeDtypeStruct((N_TOK, N_SMALL), F32),
            jax.ShapeDtypeStruct((N_TOK, D_MODEL), BF16),
        ],
        compiler_params=pltpu.CompilerParams(
            dimension_semantics=("arbitrary",), vmem_limit_bytes=VMEM_LIMIT),
    )(x_ctx, x_lat, mods, g_pre, w_in_bf, gq, gk, gmat, *rope)


LOG2E = 1.4426950408889634
HALF_LANES = 64


def _attend(q, ks, vexts):
    ss = [_dot_nt(q, k) for k in ks]
    m = ss[0].max(axis=-1, keepdims=True)
    for s in ss[1:]:
        m = jnp.maximum(m, s.max(axis=-1, keepdims=True))
    o = None
    for s, v in zip(ss, vexts):
        t = _dot(jnp.exp2(s - m).astype(BF16), v)
        o = t if o is None else o + t
    return o


def _value_ext(v128, upper):
    lane = lax.broadcasted_iota(jnp.int32, v128.shape, 1)
    keep = (lane >= HALF_LANES) if upper else (lane < HALF_LANES)
    return jnp.where(keep, v128, 1.0).astype(BF16)


def _normalised(o, upper):
    if upper:
        return o[:, HALF_LANES:] / o[:, 0:1]
    return o[:, :HALF_LANES] / o[:, HALF_LANES:HALF_LANES + 1]


def _attn_kernel(*refs, has_cache, lambda_init, after_unit=None):
    if has_cache:
        (qa_ref, ka_ref, va_ref, qb_ref, kb_ref, vb_ref,
         cka_ref, cva_ref, ckb_ref, cvb_ref, lam_ref, subg_ref, oa_ref, ob_ref) = refs
    else:
        (qa_ref, ka_ref, va_ref, qb_ref, kb_ref, vb_ref,
         lam_ref, subg_ref, oa_ref, ob_ref) = refs

    def cols(ref, lo, hi):
        return ref[:, lo:hi].astype(BF16)

    def ccols(ref, lo, hi):
        return ref[0, 0, :, lo:hi].astype(BF16)

    scale_a = A_HEAD_DIM ** -0.5 * LOG2E
    group = A_HEADS // A_KV_HEADS
    for g in range(A_KV_HEADS):
        lo, hi = g * A_HEAD_DIM, (g + 1) * A_HEAD_DIM
        upper = g % 2 == 1
        ks = [cols(ka_ref, lo, hi)]
        vexts = [_value_ext(va_ref[...], upper)]
        if has_cache:
            ks.append(ccols(cka_ref, lo, hi))
            vexts.append(_value_ext(cva_ref[0, 0], upper))
        heads = range(g * group, (g + 1) * group)
        if has_cache:
            rows = qa_ref.shape[0]
            q = jnp.concatenate(
                [qa_ref[:, hd * A_HEAD_DIM:(hd + 1) * A_HEAD_DIM] for hd in heads], axis=0)
            o = _normalised(_attend((q * scale_a).astype(BF16), ks, vexts), upper)
            for n, hd in enumerate(heads):
                oa_ref[:, hd * A_HEAD_DIM:(hd + 1) * A_HEAD_DIM] = o[n * rows:(n + 1) * rows]
        else:
            for hd in heads:
                q = (qa_ref[:, hd * A_HEAD_DIM:(hd + 1) * A_HEAD_DIM] * scale_a).astype(BF16)
                o = _attend(q, ks, vexts)
                oa_ref[:, hd * A_HEAD_DIM:(hd + 1) * A_HEAD_DIM] = _normalised(o, upper)
        if after_unit is not None:
            after_unit(g)

    bl = lam_ref[...]
    lam = (jnp.exp(jnp.sum(bl[0:1] * bl[1:2], axis=-1, keepdims=True))
           - jnp.exp(jnp.sum(bl[2:3] * bl[3:4], axis=-1, keepdims=True)) + lambda_init)
    scale_b = B_HALF_DIM ** -0.5 * LOG2E
    subg = subg_ref[...]
    for hd in range(B_HEADS):
        base = hd * 2 * B_HALF_DIM
        vlo, vhi = hd * B_V_DIM, (hd + 1) * B_V_DIM
        blk = (hd // 2) * 2 * B_V_DIM
        upper = hd % 2 == 1
        vexts = [_value_ext(vb_ref[:, blk:blk + 2 * B_V_DIM], upper)]
        if has_cache:
            vexts.append(_value_ext(cvb_ref[0, 0, :, blk:blk + 2 * B_V_DIM], upper))
        outs = []
        for c in range(2):
            lo, hi = base + c * B_HALF_DIM, base + (c + 1) * B_HALF_DIM
            q = (qb_ref[:, lo:hi] * scale_b).astype(BF16)
            ks = [cols(kb_ref, lo, hi)]
            if has_cache:
                ks.append(ccols(ckb_ref, lo, hi))
            outs.append(_normalised(_attend(q, ks, vexts), upper))
        o = outs[0] - lam * outs[1]
        o = _rms(o, subg) * (1.0 - lambda_init)
        ob_ref[:, vlo:vhi] = o
        if after_unit is not None:
            after_unit(A_KV_HEADS + hd)


SUBLANES = 8


def _row_shifter(win):
    span = win.shape[0] - SUBLANES
    shifted = {}

    def rows(off):
        s = off % SUBLANES
        if s not in shifted:
            shifted[s] = win[s:s + span, :]
        base = off - s
        return shifted[s][base:base + CONV_CHUNK, :]

    return rows


def _conv_fill(a_ref, b_ref, d_ref, hpad, upad, seq_len):
    zeros = jnp.zeros((CONV_PAD, BRANCH_W), F32)
    hpad[0:CONV_PAD, :] = zeros
    hpad[seq_len + CONV_PAD:seq_len + 2 * CONV_PAD, :] = zeros
    upad[0:CONV_PAD, :] = zeros
    upad[seq_len + CONV_PAD:seq_len + 2 * CONV_PAD, :] = zeros

    def fill(c, carry):
        r = pl.multiple_of(c * CONV_CHUNK, CONV_CHUNK)
        a = a_ref[pl.ds(r, CONV_CHUNK), :]
        b = b_ref[pl.ds(r, CONV_CHUNK), :]
        hpad[pl.ds(r + CONV_PAD, CONV_CHUNK), :] = a * _sigmoid(b)
        upad[pl.ds(r + CONV_PAD, CONV_CHUNK), :] = d_ref[pl.ds(r, CONV_CHUNK), :]
        return carry

    lax.fori_loop(0, seq_len // CONV_CHUNK, fill, 0)


def _conv_chunk(r, out_r, conv_refs, hpad, upad, oc_ref, od_ref, seq_len):
    dww_ref, dwb_ref, lng_ref, lnb_ref, wbd_ref, dsc_ref = conv_refs
    lane = lax.broadcasted_iota(jnp.int32, (CONV_CHUNK, BRANCH_W), 1)
    half = C_CONV_WIDTH // 2

    hrows = _row_shifter(hpad[pl.ds(r, CONV_CHUNK + 2 * CONV_PAD), :])
    acc = jnp.zeros((CONV_CHUNK, BRANCH_W), F32)
    for k in range(C_CONV_WIDTH):
        acc = acc + dww_ref[k:k + 1, :] * hrows(CONV_PAD - half + k)
    acc = acc + dwb_ref[...]
    mu = jnp.mean(acc, axis=-1, keepdims=True)
    cen = acc - mu
    var = jnp.mean(cen * cen, axis=-1, keepdims=True)
    y = cen * lax.rsqrt(var + EPS) * lng_ref[...] + lnb_ref[...]
    oc_ref[pl.ds(out_r, CONV_CHUNK), :] = y * _sigmoid(y)

    urows = _row_shifter(upad[pl.ds(r, CONV_CHUNK + 2 * CONV_PAD), :])

    def ld(d):
        return urows(CONV_PAD + d)

    u = ld(0)
    sums = {}
    s = u + ld(-1)
    sums[2] = s
    s = s + ld(-2) + ld(1)
    sums[4] = s
    s = s + ld(-4) + ld(-3) + ld(2) + ld(3)
    sums[8] = s
    s = s + ld(-8) + ld(-7) + ld(-6) + ld(-5) + ld(4) + ld(5) + ld(6) + ld(7)
    sums[16] = s
    t = r + lax.broadcasted_iota(jnp.int32, (CONV_CHUNK, 1), 0)
    pooled = None
    for g, w in reversed(list(enumerate(POOL_WINDOWS))):
        lo = jnp.maximum(t - w // 2, 0)
        hi = jnp.minimum(t + (w - 1 - w // 2), seq_len - 1)
        mean = sums[w] / (hi - lo + 1).astype(F32)
        pooled = mean if pooled is None else jnp.where(lane < (g + 1) * D_GROUP_W, mean, pooled)
    dlt = (pooled - u).astype(BF16)
    od_ref[pl.ds(out_r, CONV_CHUNK), :] = _dot(dlt, wbd_ref[...]) * dsc_ref[...]


N_CONV_REFS = 6
N_ATTN_UNITS = A_KV_HEADS + B_HEADS


def _mixers_kernel(*refs, has_cache, lambda_init, seq_len, step_rows):
    n_attn = 12 if has_cache else 8
    attn_in = refs[:n_attn]
    a_ref, b_ref, d_ref = refs[n_attn:n_attn + 3]
    conv_refs = refs[n_attn + 3:n_attn + 3 + N_CONV_REFS]
    oa_ref, ob_ref, oc_ref, od_ref, hpad, upad = refs[n_attn + 3 + N_CONV_REFS:]

    if seq_len == step_rows:
        _conv_fill(a_ref, b_ref, d_ref, hpad, upad, seq_len)
        row0 = 0
    else:
        t = pl.program_id(1)

        @pl.when(t == 0)
        def _():
            _conv_fill(a_ref, b_ref, d_ref, hpad, upad, seq_len)
        row0 = t * step_rows

    n_chunks = step_rows // CONV_CHUNK

    def conv(c):
        r = row0 + c * CONV_CHUNK
        if not isinstance(r, int):
            r = pl.multiple_of(r, CONV_CHUNK)
        _conv_chunk(r, c * CONV_CHUNK, conv_refs, hpad, upad, oc_ref, od_ref, seq_len)

    def after_unit(k):
        for c in range(k * n_chunks // N_ATTN_UNITS, (k + 1) * n_chunks // N_ATTN_UNITS):
            conv(c)

    _attn_kernel(*attn_in, oa_ref, ob_ref, has_cache=has_cache, lambda_init=lambda_init,
                 after_unit=after_unit)


def _conv_specs(index_map):
    const = lambda shape: pl.BlockSpec(shape, index_map)
    return [const((C_CONV_WIDTH, BRANCH_W)), const((1, BRANCH_W)), const((1, BRANCH_W)),
            const((1, BRANCH_W)), const((BRANCH_W, BRANCH_W)), const((1, BRANCH_W))]


def _conv_scratch(seq_len):
    return [pltpu.VMEM((seq_len + 2 * CONV_PAD, BRANCH_W), F32)] * 2


def _mixers_ctx(z, b_lambda, subg, lambda_init, conv_args):
    def zspec(width, col):
        return pl.BlockSpec((SEQ, width), lambda b: (b, col // width))
    out_spec = pl.BlockSpec((SEQ, BRANCH_W), lambda b: (b, 0))
    return pl.pallas_call(
        functools.partial(_mixers_kernel, has_cache=False, lambda_init=lambda_init,
                          seq_len=SEQ, step_rows=SEQ),
        name="mixers_ctx",
        grid=(BATCH,),
        in_specs=[
            zspec(256, COL_QA), zspec(128, COL_KA), zspec(128, COL_VA),
            zspec(256, COL_QB), zspec(256, COL_KB), zspec(256, COL_VB),
            pl.BlockSpec((4, B_HALF_DIM), lambda b: (0, 0)),
            pl.BlockSpec((1, B_V_DIM), lambda b: (0, 0)),
            zspec(256, COL_ZC), zspec(256, COL_ZC + BRANCH_W), zspec(256, COL_ZD),
            *_conv_specs(lambda b: (0, 0)),
        ],
        out_specs=[out_spec] * 4,
        out_shape=[jax.ShapeDtypeStruct((N_CTX, BRANCH_W), F32)] * 4,
        scratch_shapes=_conv_scratch(SEQ),
        compiler_params=pltpu.CompilerParams(
            dimension_semantics=("arbitrary",), vmem_limit_bytes=VMEM_LIMIT),
    )(z, z, z, z, z, z, b_lambda, subg, z, z, z, *conv_args)


def _mixers_lat(z, caches, l, b_lambda, subg, lambda_init, conv_args):
    q_tiles = DEC_SEQ // TQ
    seq0 = N_CTX // DEC_SEQ

    def qspec(col):
        return pl.BlockSpec((TQ, 256), lambda b, t: (N_CTX // TQ + b * q_tiles + t, col // 256))

    def kspec(width, col):
        return pl.BlockSpec((DEC_SEQ, width), lambda b, t: (seq0 + b, col // width))

    def cspec(width):
        return pl.BlockSpec((1, 1, PAST_LEN, width), lambda b, t: (b, l, 0, 0))

    out_spec = pl.BlockSpec((TQ, BRANCH_W), lambda b, t: (b * q_tiles + t, 0))
    cka, cva, ckb, cvb = caches
    return pl.pallas_call(
        functools.partial(_mixers_kernel, has_cache=True, lambda_init=lambda_init,
                          seq_len=DEC_SEQ, step_rows=TQ),
        name="mixers_lat",
        grid=(DEC_BATCH, q_tiles),
        in_specs=[
            qspec(COL_QA), kspec(128, COL_KA), kspec(128, COL_VA),
            qspec(COL_QB), kspec(256, COL_KB), kspec(256, COL_VB),
            cspec(128), cspec(128), cspec(256), cspec(256),
            pl.BlockSpec((4, B_HALF_DIM), lambda b, t: (0, 0)),
            pl.BlockSpec((1, B_V_DIM), lambda b, t: (0, 0)),
            kspec(256, COL_ZC), kspec(256, COL_ZC + BRANCH_W), kspec(256, COL_ZD),
            *_conv_specs(lambda b, t: (0, 0)),
        ],
        out_specs=[out_spec] * 4,
        out_shape=[jax.ShapeDtypeStruct((N_LAT, BRANCH_W), F32)] * 4,
        scratch_shapes=_conv_scratch(DEC_SEQ),
        compiler_params=pltpu.CompilerParams(
            dimension_semantics=("arbitrary", "arbitrary"), vmem_limit_bytes=VMEM_LIMIT),
    )(z, z, z, z, z, z, cka, cva, ckb, cvb, b_lambda, subg, z, z, z, *conv_args)


MIX_TM = 512
MIX_TILES = N_TOK // MIX_TM
MIX_CTX_TILES = N_CTX // MIX_TM
MIX_LAT_PER_SEQ = DEC_SEQ // MIX_TM
ROUTE_ROWS = 8


def _mix_kernel(xc_ref, xl_ref, h_ref, oa_c, ob_c, oc_c, od_c, oa_l, ob_l, oc_l, od_l,
                mod_ref, wg01_ref, wg23_ref, wbr_ref, wout_ref, gpost_ref, gpre2_ref, rwt_ref, rb_ref,
                x1_ref, h2_ref, eid_ref, rank_ref, wts_ref, cnt_ref, carry):
    i = pl.program_id(0)
    is_ctx = i < MIX_CTX_TILES

    @pl.when(i == 0)
    def _():
        carry[...] = jnp.zeros_like(carry)

    m = mod_ref[0]
    g1 = m[:, 2 * D_MODEL:3 * D_MODEL]
    sh2 = m[:, 3 * D_MODEL:4 * D_MODEL]
    sc2 = m[:, 4 * D_MODEL:5 * D_MODEL]

    hb = h_ref[...]
    merged = None
    for n, (c_ref, l_ref) in enumerate(((oa_c, oa_l), (ob_c, ob_l), (oc_c, oc_l), (od_c, od_l))):
        wg_ref = wg01_ref if n < 2 else wg23_ref
        gate = _sigmoid(_dot(hb, wg_ref[0, :, (n % 2) * D_MODEL:(n % 2 + 1) * D_MODEL]))
        o = jnp.where(is_ctx, c_ref[...], l_ref[...])
        br = _dot(o.astype(BF16), wbr_ref[0, n])
        merged = gate * br if merged is None else merged + gate * br
    y = _dot(merged.astype(BF16), wout_ref[0])
    x = jnp.where(is_ctx, xc_ref[...], xl_ref[...])
    x1 = x + g1 * _rms(y, gpost_ref[...])
    x1_ref[...] = x1
    h2 = _rms(x1, gpre2_ref[...]) * (1.0 + sc2) + sh2
    _store_token_major(h2_ref, h2)

    rh, rl = _split_bf16(rwt_ref[...])
    hh, hl = _split_bf16(h2)
    logits = _dot_nt(rh, hh) + _dot_nt(rh, hl) + _dot_nt(rl, hh) + rb_ref[...]

    iota_e = lax.broadcasted_iota(jnp.int32, (N_EXPERTS, MIX_TM), 0)
    rem = logits
    vals, idxs = [], []
    for _ in range(TOP_K):
        mx = jnp.max(rem, axis=0, keepdims=True)
        idx = jnp.min(jnp.where(rem == mx, iota_e, N_EXPERTS), axis=0, keepdims=True)
        vals.append(mx)
        idxs.append(idx)
        rem = jnp.where(iota_e == idx, -jnp.inf, rem)
    exps = [jnp.exp(v - vals[0]) for v in vals]
    den = exps[0]
    for e in exps[1:]:
        den = den + e

    sel = [iota_e == idx for idx in idxs]
    member = jnp.zeros((N_EXPERTS, MIX_TM), F32)
    for s in sel:
        member = member + jnp.where(s, 1.0, 0.0)
    row = lax.broadcasted_iota(jnp.int32, (MIX_TM, MIX_TM), 0)
    col = lax.broadcasted_iota(jnp.int32, (MIX_TM, MIX_TM), 1)
    earlier = jnp.where(row < col, 1.0, 0.0).astype(BF16)
    seen = _dot(member.astype(BF16), earlier) + carry[...]
    carry[...] = carry[...] + jnp.sum(member, axis=1, keepdims=True)
    cnt_ref[...] = carry[...]

    sub = lax.broadcasted_iota(jnp.int32, (ROUTE_ROWS, MIX_TM), 0)
    eid = jnp.zeros((ROUTE_ROWS, MIX_TM), jnp.int32)
    rank = jnp.zeros((ROUTE_ROWS, MIX_TM), jnp.int32)
    wts = jnp.zeros((ROUTE_ROWS, MIX_TM), F32)
    for k in range(TOP_K):
        rk = jnp.sum(jnp.where(sel[k], seen, 0.0), axis=0, keepdims=True).astype(jnp.int32)
        eid = jnp.where(sub == k, idxs[k], eid)
        rank = jnp.where(sub == k, rk, rank)
        wts = jnp.where(sub == k, exps[k] / den, wts)
    eid_ref[...] = eid
    rank_ref[...] = rank
    wts_ref[...] = wts


def _mix(x_ctx, x_lat, h, o_ctx, o_lat, mods, l, w_in_bf, w_branch_bf, w_out_bf,
         g_post, g_pre2, r_wt, r_b):
    const2 = lambda i: (0, 0)
    tile = lambda width: pl.BlockSpec((MIX_TM, width), lambda i: (i, 0))
    route = pl.BlockSpec((ROUTE_ROWS, MIX_TM), lambda i: (0, i))
    ctx_tile = _ctx_rows(BRANCH_W, MIX_TM)
    lat_tile = _lat_rows(BRANCH_W, MIX_TM)
    mod_row = lambda i: _mod_row(i, MIX_CTX_TILES, MIX_LAT_PER_SEQ)
    single = pl.Buffered(1)
    return pl.pallas_call(
        _mix_kernel,
        name="mix_router",
        grid=(MIX_TILES,),
        in_specs=[
            _ctx_rows(D_MODEL, MIX_TM), _lat_rows(D_MODEL, MIX_TM), tile(D_MODEL),
            ctx_tile, ctx_tile, ctx_tile, ctx_tile, lat_tile, lat_tile, lat_tile, lat_tile,
            pl.BlockSpec((1, 1, 6 * D_MODEL), lambda i: (l * MOD_ROWS + mod_row(i), 0, 0)),
            pl.BlockSpec((1, D_MODEL, N_SMALL), lambda i: (l, 0, 1), pipeline_mode=single),
            pl.BlockSpec((1, D_MODEL, N_SMALL), lambda i: (l, 0, 2), pipeline_mode=single),
            pl.BlockSpec((1, N_BRANCH, BRANCH_W, D_MODEL), lambda i: (l, 0, 0, 0),
                         pipeline_mode=single),
            pl.BlockSpec((1, D_MODEL, D_MODEL), lambda i: (l, 0, 0), pipeline_mode=single),
            pl.BlockSpec((1, D_MODEL), const2),
            pl.BlockSpec((1, D_MODEL), const2),
            pl.BlockSpec((N_EXPERTS, D_MODEL), const2),
            pl.BlockSpec((N_EXPERTS, 1), const2),
        ],
        out_specs=[
            tile(D_MODEL), pl.BlockSpec((MIX_TM * ROW_TILES, LANES), lambda i: (i, 0)),
            route, route, route,
            pl.BlockSpec((N_EXPERTS, 1), const2),
        ],
        out_shape=[
            jax.ShapeDtypeStruct((N_TOK, D_MODEL), F32),
            jax.ShapeDtypeStruct((N_TOK * ROW_TILES, LANES), F32),
            jax.ShapeDtypeStruct((ROUTE_ROWS, N_TOK), jnp.int32),
            jax.ShapeDtypeStruct((ROUTE_ROWS, N_TOK), jnp.int32),
            jax.ShapeDtypeStruct((ROUTE_ROWS, N_TOK), F32),
            jax.ShapeDtypeStruct((N_EXPERTS, 1), F32),
        ],
        scratch_shapes=[pltpu.VMEM((N_EXPERTS, 1), F32)],
        compiler_params=pltpu.CompilerParams(
            dimension_semantics=("arbitrary",), vmem_limit_bytes=VMEM_LIMIT),
    )(x_ctx, x_lat, h, *o_ctx, *o_lat, mods, w_in_bf, w_in_bf, w_branch_bf, w_out_bf,
      g_post, g_pre2, r_wt, r_b)


def _token_rows(t, n=1):
    return pl.ds(pl.multiple_of(t * ROW_TILES, ROW_TILES), n * ROW_TILES)


def _token_copy(src_ref, src_tok, dst_ref, dst_tok, sem):
    return pltpu.make_async_copy(src_ref.at[_token_rows(src_tok), :], dst_ref.at[_token_rows(dst_tok), :], sem)


DISPATCH_ROWS = 512
DISPATCH_STEPS = N_TOK // DISPATCH_ROWS
DMA_UNROLL = 8


def _dispatch_kernel(starts_ref, padded_ref, pos_ref, h2_ref, xs_ref, zbuf, sem, zsem, tsem):
    i = pl.program_id(0)

    def pad_copy(e):
        last = pl.multiple_of(starts_ref[e] + padded_ref[e] - TE, TE)
        return pltpu.make_async_copy(zbuf, xs_ref.at[_token_rows(last, TE), :], zsem)

    first_tail = (starts_ref[N_EXPERTS - 1] + padded_ref[N_EXPERTS - 1]) // TE

    def tail_copy(j):
        return pltpu.make_async_copy(zbuf, xs_ref.at[_token_rows(j * TE, TE), :], tsem)

    def tail_start(j, carry):
        tail_copy(j).start()
        return carry

    def tail_wait(j, carry):
        tail_copy(j).wait()
        return carry

    @pl.when(i == 0)
    def _():
        zbuf[...] = jnp.zeros_like(zbuf)
        for e in range(N_EXPERTS):
            @pl.when(padded_ref[e] > 0)
            def _():
                pad_copy(e).start()
        lax.fori_loop(first_tail, E_TILES, tail_start, 0)
        for e in range(N_EXPERTS):
            @pl.when(padded_ref[e] > 0)
            def _():
                pad_copy(e).wait()

    @pl.when(i == DISPATCH_STEPS - 1)
    def _():
        lax.fori_loop(first_tail, E_TILES, tail_wait, 0)

    def issue(t, carry):
        for k in range(TOP_K):
            _token_copy(h2_ref, t, xs_ref, pos_ref[t * TOP_K + k], sem).start(priority=k % 2)
        return carry

    lax.fori_loop(0, DISPATCH_ROWS, issue, 0, unroll=DMA_UNROLL)

    for _ in range(TOP_K):
        pltpu.make_async_copy(h2_ref, xs_ref.at[_token_rows(0, DISPATCH_ROWS), :], sem).wait()


def _dispatch(starts, padded, pos, h2):
    grid_spec = pltpu.PrefetchScalarGridSpec(
        num_scalar_prefetch=2,
        grid=(DISPATCH_STEPS,),
        in_specs=[
            pl.BlockSpec((DISPATCH_ROWS * TOP_K,), lambda i, st, pd: (i,), memory_space=pltpu.SMEM),
            pl.BlockSpec((DISPATCH_ROWS * ROW_TILES, LANES), lambda i, st, pd: (i, 0)),
        ],
        out_specs=pl.BlockSpec(memory_space=pl.ANY),
        scratch_shapes=[pltpu.VMEM((TE * ROW_TILES, LANES), F32),
                        pltpu.SemaphoreType.DMA(()), pltpu.SemaphoreType.DMA(()),
                        pltpu.SemaphoreType.DMA(())],
    )
    return pl.pallas_call(
        _dispatch_kernel,
        name="dispatch",
        grid_spec=grid_spec,
        out_shape=jax.ShapeDtypeStruct((P_MAX * ROW_TILES, LANES), F32),
        compiler_params=pltpu.CompilerParams(
            dimension_semantics=("arbitrary",), vmem_limit_bytes=VMEM_LIMIT),
    )(starts, padded, pos, h2)


W_CAST_ROWS = 128


def _moe_kernel(te_ref, rows_ref, slot_ref, next_ref, x_ref, w1_hbm, b1_ref, w2_hbm, b2_ref, y_ref,
                w1f, w2f, w1b, w2b, sem, *, layer):
    j = pl.program_id(0)
    e = te_ref[j]
    prev = te_ref[jnp.maximum(j - 1, 0)]
    valid = rows_ref[j] > 0
    first = jnp.logical_and(valid, jnp.logical_or(j == 0, e != prev))
    s = slot_ref[j]

    def weight_copies(ex, sl):
        return (pltpu.make_async_copy(w1_hbm.at[layer, ex], w1f.at[sl], sem.at[0, sl]),
                pltpu.make_async_copy(w2_hbm.at[layer, ex], w2f.at[sl], sem.at[1, sl]))

    @pl.when(j == 0)
    def _():
        for cp in weight_copies(e, s):
            cp.start()

    @pl.when(first)
    def _():
        for cp in weight_copies(e, s):
            cp.wait()
        nxt = next_ref[j]

        @pl.when(nxt >= 0)
        def _():
            for cp in weight_copies(nxt, 1 - s):
                cp.start()

        def cast(c, carry):
            r = pl.multiple_of(c * W_CAST_ROWS, W_CAST_ROWS)
            w1b[pl.ds(r, W_CAST_ROWS), :] = w1f[s, pl.ds(r, W_CAST_ROWS), :].astype(BF16)
            w2b[pl.ds(r, W_CAST_ROWS), :] = w2f[s, pl.ds(r, W_CAST_ROWS), :].astype(BF16)
            return carry
        lax.fori_loop(0, D_MODEL // W_CAST_ROWS, cast, 0)

    def ffn(x_rows, y_rows):
        x = _load_token_major(x_rows).astype(BF16)
        u = _dot(x, w1b[...]) + b1_ref[0, 0]
        xg = jnp.minimum(u[:, :D_FF], SWIGLU_LIMIT)
        xl = jnp.clip(u[:, D_FF:], -SWIGLU_LIMIT, SWIGLU_LIMIT)
        act = (xl + 1.0) * (xg * _sigmoid(SWIGLU_ALPHA * xg))
        _store_token_major(y_rows, _dot(act.astype(BF16), w2b[...]) + b2_ref[0, 0])

    n_rows = rows_ref[j]

    @pl.when(n_rows > TE // 2)
    def _():
        ffn(x_ref, y_ref)

    def partial(rows):
        used = pl.ds(0, rows * ROW_TILES)
        rest = pl.ds(rows * ROW_TILES, (TE - rows) * ROW_TILES)
        ffn(x_ref.at[used, :], y_ref.at[used, :])
        y_ref[rest, :] = jnp.zeros(((TE - rows) * ROW_TILES, LANES), F32)

    @pl.when(jnp.logical_and(n_rows > TE // 4, n_rows <= TE // 2))
    def _():
        partial(TE // 2)

    @pl.when(jnp.logical_and(valid, n_rows <= TE // 4))
    def _():
        partial(TE // 4)

    @pl.when(jnp.logical_not(valid))
    def _():
        y_ref[...] = jnp.zeros_like(y_ref)


def _expert_ffn(tile_expert, tile_rows, tile_slot, tile_next, xs, l, e_w1, e_b1, e_w2, e_b2):
    L = e_w1.shape[0]
    grid_spec = pltpu.PrefetchScalarGridSpec(
        num_scalar_prefetch=4,
        grid=(E_TILES,),
        in_specs=[
            pl.BlockSpec((TE * ROW_TILES, LANES),
                         lambda j, te, nr, sl, nx: (jnp.where(nr[j] > 0, j, 0), 0)),
            pl.BlockSpec(memory_space=pl.ANY),
            pl.BlockSpec((1, 1, 1, 2 * D_FF), lambda j, te, nr, sl, nx: (l, te[j], 0, 0)),
            pl.BlockSpec(memory_space=pl.ANY),
            pl.BlockSpec((1, 1, 1, D_MODEL), lambda j, te, nr, sl, nx: (l, te[j], 0, 0)),
        ],
        out_specs=pl.BlockSpec((TE * ROW_TILES, LANES), lambda j, te, nr, sl, nx: (j, 0)),
        scratch_shapes=[pltpu.VMEM((2, D_MODEL, 2 * D_FF), F32), pltpu.VMEM((2, D_FF, D_MODEL), F32),
                        pltpu.VMEM((D_MODEL, 2 * D_FF), BF16), pltpu.VMEM((D_FF, D_MODEL), BF16),
                        pltpu.SemaphoreType.DMA((2, 2))],
    )
    return pl.pallas_call(
        functools.partial(_moe_kernel, layer=l),
        name="expert_ffn",
        grid_spec=grid_spec,
        out_shape=jax.ShapeDtypeStruct((P_MAX * ROW_TILES, LANES), F32),
        compiler_params=pltpu.CompilerParams(
            dimension_semantics=("arbitrary",), vmem_limit_bytes=VMEM_LIMIT),
    )(tile_expert, tile_rows, tile_slot, tile_next, xs, e_w1,
      e_b1.reshape(L, N_EXPERTS, 1, 2 * D_FF), e_w2, e_b2.reshape(L, N_EXPERTS, 1, D_MODEL))


def _combine_kernel(pos_ref, pos_next_ref, ys_ref, wts_ref, x1_ref, mod_ref, gpost_ref,
                    out_c_ref, out_l_ref, buf, sem):
    i = pl.program_id(0)
    slot = i % 2

    def gather(p_ref, s):
        def issue(t, carry):
            for k in range(TOP_K):
                _token_copy(ys_ref, p_ref[t * TOP_K + k], buf.at[s, k], t, sem.at[s]).start(priority=k % 2)
            return carry
        lax.fori_loop(0, TM, issue, 0, unroll=DMA_UNROLL)

    @pl.when(i == 0)
    def _():
        gather(pos_ref, 0)

    @pl.when(i + 1 < N_TILES)
    def _():
        gather(pos_next_ref, 1 - slot)

    for k in range(TOP_K):
        pltpu.make_async_copy(ys_ref.at[_token_rows(0, TM), :], buf.at[slot, k], sem.at[slot]).wait()

    w = wts_ref[...]
    y = w[:, 0:1] * _load_token_major(buf.at[slot, 0])
    for k in range(1, TOP_K):
        y = y + w[:, k:k + 1] * _load_token_major(buf.at[slot, k])
    g2 = mod_ref[0][:, 5 * D_MODEL:6 * D_MODEL]
    res = x1_ref[...] + g2 * _rms(y, gpost_ref[...])

    @pl.when(i < CTX_TILES)
    def _():
        out_c_ref[...] = res

    @pl.when(i >= CTX_TILES)
    def _():
        out_l_ref[...] = res


def _combine(pos, ys, wts, x1, mods, l, g_post2):
    return pl.pallas_call(
        _combine_kernel,
        name="combine",
        grid=(N_TILES,),
        in_specs=[
            pl.BlockSpec((TM * TOP_K,), lambda i: (i,), memory_space=pltpu.SMEM),
            pl.BlockSpec((TM * TOP_K,), lambda i: (jnp.minimum(i + 1, N_TILES - 1),),
                         memory_space=pltpu.SMEM),
            pl.BlockSpec(memory_space=pl.ANY),
            pl.BlockSpec((TM, TOP_K), lambda i: (i, 0)),
            pl.BlockSpec((TM, D_MODEL), lambda i: (i, 0)),
            pl.BlockSpec((1, 1, 6 * D_MODEL), lambda i: (l * MOD_ROWS + _mod_row(i), 0, 0)),
            pl.BlockSpec((1, D_MODEL), lambda i: (0, 0)),
        ],
        out_specs=[_ctx_rows(D_MODEL), _lat_rows(D_MODEL)],
        out_shape=[jax.ShapeDtypeStruct((N_CTX, D_MODEL), F32),
                   jax.ShapeDtypeStruct((N_LAT, D_MODEL), F32)],
        scratch_shapes=[pltpu.VMEM((2, TOP_K, TM * ROW_TILES, LANES), F32),
                        pltpu.SemaphoreType.DMA((2,))],
        compiler_params=pltpu.CompilerParams(
            dimension_semantics=("arbitrary",), vmem_limit_bytes=VMEM_LIMIT),
    )(pos, pos, ys, wts, x1, mods, g_post2)


def _rope_tables():
    t = np.arange(DEC_SEQ)
    r = (t // GRID_W).astype(np.float32)
    c = (t % GRID_W).astype(np.float32)

    def table(dim, reps):
        n_axis = dim // 4
        inv = (ROPE_THETA ** (-np.arange(n_axis, dtype=np.float32) / n_axis)).astype(np.float32)
        ang = np.concatenate([r[:, None] * inv, c[:, None] * inv], axis=-1).astype(np.float32)
        cos, sin = np.cos(ang), np.sin(ang)
        return (np.tile(np.concatenate([cos, cos], -1), (1, reps)),
                np.tile(np.concatenate([-sin, sin], -1), (1, reps)))

    cos_a, sin_a = table(A_HEAD_DIM, 256 // A_HEAD_DIM)
    cos_b, sin_b = table(B_HALF_DIM, 256 // B_HALF_DIM)
    return tuple(jnp.asarray(a, F32) for a in (cos_a, sin_a, cos_b, sin_b))


def _group_mean_matrix():
    idx = np.arange(256) // A_HEAD_DIM
    return jnp.asarray((idx[:, None] == idx[None, :]).astype(np.float32) / A_HEAD_DIM, BF16)


def _block_diag(w):
    out = jnp.zeros((BRANCH_W, BRANCH_W), w.dtype)
    for g in range(D_GROUPS):
        out = out.at[g * D_GROUP_W:(g + 1) * D_GROUP_W, g * D_GROUP_W:(g + 1) * D_GROUP_W].set(w[g])
    return out


def _routing_tables(eid, rank, cnt):
    cnt = cnt.reshape(N_EXPERTS).astype(jnp.int32)
    padded = ((cnt + TE - 1) // TE) * TE
    ends = jnp.cumsum(padded)
    starts = ends - padded
    experts = jnp.arange(N_EXPERTS, dtype=jnp.int32)
    start_of = jnp.sum(jnp.where(eid[:TOP_K, :, None] == experts, starts, 0), axis=-1)
    pos = (start_of + rank[:TOP_K]).T.reshape(N_SLOTS)
    tile_row = jnp.arange(E_TILES, dtype=jnp.int32) * TE
    tile_expert = jnp.sum((tile_row[:, None] >= ends[None, :]).astype(jnp.int32), axis=-1)
    tile_expert = jnp.minimum(tile_expert, N_EXPERTS - 1)
    used_end = jnp.sum(jnp.where(tile_expert[:, None] == experts, starts + cnt, 0), axis=-1)
    tile_rows = jnp.where(tile_row < ends[-1], jnp.clip(used_end - tile_row, 0, TE), 0)
    valid = tile_rows > 0
    changed = jnp.concatenate([jnp.ones((1,), bool), tile_expert[1:] != tile_expert[:-1]])
    tile_slot = (jnp.cumsum(jnp.logical_and(valid, changed).astype(jnp.int32)) + 1) % 2
    group_end = jnp.sum(jnp.where(tile_expert[:, None] == experts, ends, 0), axis=-1) // TE
    tiles = jnp.arange(E_TILES, dtype=jnp.int32)
    expert_at = jnp.where(valid, tile_expert, -1)
    tile_next = jnp.sum(jnp.where(group_end[:, None] == tiles, expert_at + 1, 0), axis=-1) - 1
    as_i32 = lambda a: a.astype(jnp.int32)
    return (as_i32(pos), as_i32(tile_expert), as_i32(tile_rows), as_i32(tile_slot), as_i32(tile_next),
            as_i32(starts), as_i32(padded))


def kernel(x_prompt, x_sample, cache_a_k, cache_a_v, cache_b_k, cache_b_v, c, c_ctx, w_mod, b_mod, g_pre1, g_post1, g_pre2, g_post2, w_in, a_q_gain, a_k_gain, b_lambda, b_subln_gain, c_dw_w, c_dw_b, c_ln_g, c_ln_b, d_w_group, d_scale, w_branch, w_out, r_w, r_b, e_w1, e_b1, e_w2, e_b2):
    x_ctx = x_prompt.reshape(N_CTX, D_MODEL)
    x_lat = x_sample.reshape(N_LAT, D_MODEL)
    cond8 = jnp.concatenate(
        [c_ctx[None, :], c, jnp.zeros((MOD_ROWS - 1 - DEC_BATCH, D_MODEL), F32)], axis=0)
    mods = _modulation(cond8, w_mod, b_mod).reshape(DEPTH * MOD_ROWS, 1, 6 * D_MODEL)

    rope = _rope_tables()
    gmat = _group_mean_matrix()
    caches = (
        cache_a_k.reshape(DEC_BATCH, DEPTH, PAST_LEN, A_KV_HEADS * A_HEAD_DIM),
        cache_a_v.reshape(DEC_BATCH, DEPTH, PAST_LEN, A_KV_HEADS * A_HEAD_DIM),
        cache_b_k.reshape(DEC_BATCH, DEPTH, PAST_LEN, B_HEADS * 2 * B_HALF_DIM),
        cache_b_v.reshape(DEC_BATCH, DEPTH, PAST_LEN, B_HEADS * B_V_DIM),
    )

    w_in_bf = w_in.astype(BF16)
    w_branch_bf = w_branch.astype(BF16)
    w_out_bf = w_out.astype(BF16)

    new_ak, new_av, new_bk, new_bv = [], [], [], []
    row = lambda v: v.reshape(1, -1)
    for l in range(DEPTH):
        lambda_init = 0.8 - 0.6 * math.exp(-0.3 * l)
        z, h = _in_projection(
            x_ctx, x_lat, mods, l, row(g_pre1[l]), w_in_bf,
            row(jnp.tile(a_q_gain[l], A_HEADS)), row(jnp.tile(a_k_gain[l], A_KV_HEADS)), gmat, rope)

        zc = z[:N_CTX]
        new_ak.append(zc[:, COL_KA:COL_VA].reshape(BATCH, SEQ, A_KV_HEADS, A_HEAD_DIM))
        new_av.append(zc[:, COL_VA:COL_QB].reshape(BATCH, SEQ, A_KV_HEADS, A_HEAD_DIM))
        new_bk.append(zc[:, COL_KB:COL_VB].reshape(BATCH, SEQ, B_HEADS, 2, B_HALF_DIM))
        new_bv.append(zc[:, COL_VB:COL_ZC].reshape(BATCH, SEQ, B_HEADS, B_V_DIM))

        subg = row(b_subln_gain[l])
        conv_args = (c_dw_w[l], row(c_dw_b[l]), row(c_ln_g[l]), row(c_ln_b[l]),
                     _block_diag(d_w_group[l]).astype(BF16), row(d_scale[l]))
        o_ctx = _mixers_ctx(z, b_lambda[l], subg, lambda_init, conv_args)
        o_lat = _mixers_lat(z, caches, l, b_lambda[l], subg, lambda_init, conv_args)

        x1, h2, eid, rank, wts, cnt = _mix(
            x_ctx, x_lat, h, o_ctx, o_lat, mods, l,
            w_in_bf, w_branch_bf, w_out_bf,
            row(g_post1[l]), row(g_pre2[l]), r_w[l].T, r_b[l].reshape(N_EXPERTS, 1))
        wts = wts[:TOP_K].T

        pos, tile_expert, tile_rows, tile_slot, tile_next, starts, padded = _routing_tables(
            eid, rank, cnt)
        xs = _dispatch(starts, padded, pos, h2)
        ys = _expert_ffn(tile_expert, tile_rows, tile_slot, tile_next, xs, l,
                         e_w1, e_b1, e_w2, e_b2)
        x_ctx, x_lat = _combine(pos, ys, wts, x1, mods, l, row(g_post2[l]))

    y_prompt = x_ctx.reshape(BATCH, SEQ, D_MODEL)
    y_sample = x_lat.reshape(DEC_BATCH, DEC_SEQ, D_MODEL)
    return (y_prompt, y_sample, jnp.stack(new_ak, axis=1), jnp.stack(new_av, axis=1),
            jnp.stack(new_bk, axis=1), jnp.stack(new_bv, axis=1))
```
